```python
import jax, jax.numpy as jnp
from jax import lax
import numpy as np

D_MODEL = 2048
BATCH = 1
SEQ = 8192
DEPTH = 2

GRID_W = 64
HEAD_DIM = 128
NA_HEADS = 4
NA_WIN_ROWS = 8
NA_WIN_COLS = 16
MLA_HEADS = 6
MLA_Q_RANK = 512
MLA_KV_RANK = 256
MLA_NOPE = 128
MLA_ROPE = 64
MLA_V = 128
MLA_Q_BLOCK = 128
ROPE_THETA = 10000.0
SWA_HEADS = 6
SWA_KV_HEADS = 2
SWA_WINDOW = 128
SWA_BLOCK = 128

D_A = NA_HEADS * HEAD_DIM
D_B = MLA_HEADS * MLA_V
D_C = SWA_HEADS * HEAD_DIM
D_MIX = D_A + D_B + D_C
IN_A = 3 * D_A
IN_B = MLA_Q_RANK + MLA_KV_RANK + MLA_ROPE
IN_C = (SWA_HEADS + 2 * SWA_KV_HEADS) * HEAD_DIM
D_IN = IN_A + IN_B + IN_C
D_FF = -(-8 * D_MODEL // (3 * 256)) * 256

DEEPNORM_ALPHA = (2 * DEPTH) ** 0.25
DEEPNORM_BETA = (8 * DEPTH) ** -0.25
LN_EPS = 1e-5
RMS_EPS = 1e-6
NEG_INF = -1e30

kernel_name = "hybrid_parallel_heads_encoder"


def layer_norm(x, g, b):
    xf = x.astype(jnp.float32)
    mu = xf.mean(-1, keepdims=True)
    var = jnp.square(xf - mu).mean(-1, keepdims=True)
    return ((xf - mu) * lax.rsqrt(var + LN_EPS) * g.astype(jnp.float32) + b.astype(jnp.float32)).astype(x.dtype)


def rms_norm(x, g):
    xf = x.astype(jnp.float32)
    ms = jnp.square(xf).mean(-1, keepdims=True)
    return (xf * lax.rsqrt(ms + RMS_EPS) * g.astype(jnp.float32)).astype(x.dtype)


def rope(x, pos):
    half = x.shape[-1] // 2
    inv = ROPE_THETA ** (-jnp.arange(half, dtype=jnp.float32) / half)
    ang = pos.astype(jnp.float32)[:, None] * inv[None, :]
    cos = jnp.cos(ang)[:, None, :]
    sin = jnp.sin(ang)[:, None, :]
    x1 = x[..., :half].astype(jnp.float32)
    x2 = x[..., half:].astype(jnp.float32)
    return jnp.concatenate([x1 * cos - x2 * sin, x2 * cos + x1 * sin], axis=-1).astype(x.dtype)


def neighbourhood_attention(q, k, v, rpb):
    B, S, H, d = q.shape
    rows = S // GRID_W
    kh = min(NA_WIN_ROWS, rows)
    r = jnp.arange(rows)
    r0 = jnp.clip(r - kh // 2, 0, rows - kh)
    row_idx = r0[:, None] + jnp.arange(kh)[None, :]
    col = jnp.arange(GRID_W)
    c0 = jnp.clip(col - NA_WIN_COLS // 2, 0, GRID_W - NA_WIN_COLS)
    col_in = (col[None, :] >= c0[:, None]) & (col[None, :] < c0[:, None] + NA_WIN_COLS)
    qg = q.reshape(B, rows, GRID_W, H, d)
    kg = k.reshape(B, rows, GRID_W, H, d)[:, row_idx]
    vg = v.reshape(B, rows, GRID_W, H, d)[:, row_idx]
    s = jnp.einsum('brqhd,brkwhd->bhrqkw', qg, kg).astype(jnp.float32) * (d ** -0.5)
    roff = row_idx - r[:, None] + (NA_WIN_ROWS - 1)
    coff = jnp.clip(col[None, :] - col[:, None], -(NA_WIN_COLS - 1), NA_WIN_COLS - 1) + (NA_WIN_COLS - 1)
    bias = rpb.astype(jnp.float32)[:, roff[:, None, :, None], coff[None, :, None, :]]
    s = jnp.where(col_in[None, None, None, :, None, :], s + bias[None], NEG_INF)
    p = jax.nn.softmax(s.reshape(B, H, rows, GRID_W, kh * GRID_W), axis=-1)
    p = p.reshape(B, H, rows, GRID_W, kh, GRID_W).astype(v.dtype)
    o = jnp.einsum('bhrqkw,brkwhd->brqhd', p, vg)
    return o.reshape(B, S, H * d)


def latent_attention(c_q, c_kv, k_rope, q_norm_g, kv_norm_g, w_uq, w_ukv, pos):
    B, S, _ = c_q.shape
    H = MLA_HEADS
    q = (rms_norm(c_q, q_norm_g) @ w_uq).reshape(B, S, H, MLA_NOPE + MLA_ROPE)
    q_nope = q[..., :MLA_NOPE]
    q_pe = rope(q[..., MLA_NOPE:], pos)
    kv = (rms_norm(c_kv, kv_norm_g) @ w_ukv).reshape(B, S, H, MLA_NOPE + MLA_V)
    k_nope = kv[..., :MLA_NOPE]
    v = kv[..., MLA_NOPE:]
    k_pe = rope(k_rope[:, :, None, :], pos)[:, :, 0]
    scale = (MLA_NOPE + MLA_ROPE) ** -0.5
    nb = S // MLA_Q_BLOCK
    qn_blocks = q_nope.reshape(B, nb, MLA_Q_BLOCK, H, MLA_NOPE).transpose(1, 0, 2, 3, 4)
    qp_blocks = q_pe.reshape(B, nb, MLA_Q_BLOCK, H, MLA_ROPE).transpose(1, 0, 2, 3, 4)

    def one_block(blk):
        qn, qp = blk
        s = jnp.einsum('bqhd,bkhd->bhqk', qn, k_nope) + jnp.einsum('bqhr,bkr->bhqk', qp, k_pe)
        p = jax.nn.softmax(s.astype(jnp.float32) * scale, axis=-1).astype(v.dtype)
        return jnp.einsum('bhqk,bkhd->bqhd', p, v)

    o = lax.map(one_block, (qn_blocks, qp_blocks))
    return o.transpose(1, 0, 2, 3, 4).reshape(B, S, H * MLA_V)


def windowed_gqa(q, k, v, sink):
    B, S, H, d = q.shape
    hkv = k.shape[2]
    G = H // hkv
    T = SWA_BLOCK
    nb = S // T
    qb = q.reshape(B, nb, T, hkv, G, d)

    def band(x):
        xp = jnp.pad(x.reshape(B, nb, T, hkv, d), ((0, 0), (1, 1), (0, 0), (0, 0), (0, 0)))
        return jnp.concatenate([xp[:, :-2], xp[:, 1:-1], xp[:, 2:]], axis=2)

    kb = band(k)
    vb = band(v)
    s = jnp.einsum('bnqhgd,bnshd->bhgnqs', qb, kb).astype(jnp.float32) * (d ** -0.5)
    blk = jnp.arange(nb)[:, None]
    qpos = blk * T + jnp.arange(T)[None, :]
    kpos = (blk - 1) * T + jnp.arange(3 * T)[None, :]
    dist = jnp.abs(qpos[:, :, None] - kpos[:, None, :])
    valid = (dist <= SWA_WINDOW) & (kpos[:, None, :] >= 0) & (kpos[:, None, :] < S)
    slopes = jnp.asarray(np.array([2.0 ** (-8.0 * (i + 1) / H) for i in range(H)], dtype=np.float32))
    s = s - slopes.reshape(hkv, G)[None, :, :, None, None, None] * dist.astype(jnp.float32)[None, None, None]
    s = jnp.where(valid[None, None, None], s, NEG_INF)
    sink_l = jnp.broadcast_to(sink.astype(jnp.float32).reshape(1, hkv, G, 1, 1, 1), s.shape[:-1] + (1,))
    p = jax.nn.softmax(jnp.concatenate([s, sink_l], axis=-1), axis=-1)[..., :-1].astype(v.dtype)
    o = jnp.einsum('bhgnqs,bnshd->bnqhgd', p, vb)
    return o.reshape(B, S, H * d)


def setup_inputs(seed: int = 0) -> dict:
    key = jax.random.key(seed)
    ks = jax.random.split(key, 20)
    L, D = DEPTH, D_MODEL
    nrm = jax.random.normal
    beta = DEEPNORM_BETA
    col_scale = np.ones((D_IN,), dtype=np.float32)
    col_scale[2 * D_A:3 * D_A] = beta
    v_c0 = IN_A + IN_B + (SWA_HEADS + SWA_KV_HEADS) * HEAD_DIM
    col_scale[v_c0:v_c0 + SWA_KV_HEADS * HEAD_DIM] = beta
    ukv_scale = np.ones((MLA_HEADS, MLA_NOPE + MLA_V), dtype=np.float32)
    ukv_scale[:, MLA_NOPE:] = beta
    ukv_scale = ukv_scale.reshape(-1)
    return {
        "x": nrm(ks[0], (BATCH, SEQ, D), jnp.float32),
        "c": nrm(ks[1], (BATCH, D), jnp.float32),
        "w_ada": nrm(ks[2], (L, D, 6 * D), jnp.float32) * (0.1 * D ** -0.5),
        "b_ada": nrm(ks[3], (L, 6 * D), jnp.float32) * 0.01,
        "w_in": nrm(ks[4], (L, D, D_IN), jnp.float32) * (D ** -0.5) * jnp.asarray(col_scale),
        "na_rpb": nrm(ks[5], (L, NA_HEADS, 2 * NA_WIN_ROWS - 1, 2 * NA_WIN_COLS - 1), jnp.float32) * 0.1,
        "mla_q_norm": 1.0 + 0.02 * nrm(ks[6], (L, MLA_Q_RANK), jnp.float32),
        "mla_kv_norm": 1.0 + 0.02 * nrm(ks[7], (L, MLA_KV_RANK), jnp.float32),
        "mla_w_uq": nrm(ks[8], (L, MLA_Q_RANK, MLA_HEADS * (MLA_NOPE + MLA_ROPE)), jnp.float32) * (MLA_Q_RANK ** -0.5),
        "mla_w_ukv": nrm(ks[9], (L, MLA_KV_RANK, MLA_HEADS * (MLA_NOPE + MLA_V)), jnp.float32) * (MLA_KV_RANK ** -0.5) * jnp.asarray(ukv_scale),
        "swa_sink": nrm(ks[10], (L, SWA_HEADS), jnp.float32) * 0.5,
        "out_norm_g": 1.0 + 0.02 * nrm(ks[11], (L, D_MIX), jnp.float32),
        "w_o": nrm(ks[12], (L, D_MIX, D), jnp.float32) * (D_MIX ** -0.5) * beta,
        "ln1_g": 1.0 + 0.02 * nrm(ks[13], (L, D), jnp.float32),
        "ln1_b": 0.02 * nrm(ks[14], (L, D), jnp.float32),
        "w_gu": nrm(ks[15], (L, D, 2 * D_FF), jnp.float32) * (D ** -0.5),
        "w_down": nrm(ks[16], (L, D_FF, D), jnp.float32) * (D_FF ** -0.5) * beta,
        "ln2_g": 1.0 + 0.02 * nrm(ks[17], (L, D), jnp.float32),
        "ln2_b": 0.02 * nrm(ks[18], (L, D), jnp.float32),
    }


def reference(x, c, w_ada, b_ada, w_in, na_rpb, mla_q_norm, mla_kv_norm, mla_w_uq, mla_w_ukv,
              swa_sink, out_norm_g, w_o, ln1_g, ln1_b, w_gu, w_down, ln2_g, ln2_b):
    B, S, D = x.shape
    pos = jnp.arange(S, dtype=jnp.int32)
    cond = jax.nn.silu(c)
    for l in range(DEPTH):
        mod = cond @ w_ada[l] + b_ada[l]
        sh1, sc1, g1, sh2, sc2, g2 = [m[:, None, :] for m in jnp.split(mod, 6, axis=-1)]

        u = x * (1.0 + sc1) + sh1
        proj = u @ w_in[l]
        pa = proj[..., :IN_A]
        pb = proj[..., IN_A:IN_A + IN_B]
        pc = proj[..., IN_A + IN_B:]

        qa = pa[..., :D_A].reshape(B, S, NA_HEADS, HEAD_DIM)
        ka = pa[..., D_A:2 * D_A].reshape(B, S, NA_HEADS, HEAD_DIM)
        va = pa[..., 2 * D_A:].reshape(B, S, NA_HEADS, HEAD_DIM)
        ya = neighbourhood_attention(qa, ka, va, na_rpb[l])

        cq = pb[..., :MLA_Q_RANK]
        ckv = pb[..., MLA_Q_RANK:MLA_Q_RANK + MLA_KV_RANK]
        kr = pb[..., MLA_Q_RANK + MLA_KV_RANK:]
        yb = latent_attention(cq, ckv, kr, mla_q_norm[l], mla_kv_norm[l], mla_w_uq[l], mla_w_ukv[l], pos)

        dq = SWA_HEADS * HEAD_DIM
        dk = SWA_KV_HEADS * HEAD_DIM
        qc = pc[..., :dq].reshape(B, S, SWA_HEADS, HEAD_DIM)
        kc = pc[..., dq:dq + dk].reshape(B, S, SWA_KV_HEADS, HEAD_DIM)
        vc = pc[..., dq + dk:].reshape(B, S, SWA_KV_HEADS, HEAD_DIM)
        yc = windowed_gqa(qc, kc, vc, swa_sink[l])

        gn = out_norm_g[l]
        y = jnp.concatenate([rms_norm(ya, gn[:D_A]),
                             rms_norm(yb, gn[D_A:D_A + D_B]),
                             rms_norm(yc, gn[D_A + D_B:])], axis=-1)
        x = layer_norm(DEEPNORM_ALPHA * x + (1.0 + g1) * (y @ w_o[l]), ln1_g[l], ln1_b[l])

        u = x * (1.0 + sc2) + sh2
        gu = u @ w_gu[l]
        h = jax.nn.silu(gu[..., :D_FF]) * gu[..., D_FF:]
        x = layer_norm(DEEPNORM_ALPHA * x + (1.0 + g2) * (h @ w_down[l]), ln2_g[l], ln2_b[l])
    return x
```

```python
import functools

import numpy as np
import jax
import jax.numpy as jnp
from jax import lax
from jax.experimental import pallas as pl
from jax.experimental.pallas import tpu as pltpu

F32 = jnp.float32
BF16 = jnp.bfloat16

D_MODEL = 2048
SEQ = 8192
DEPTH = 2
GRID_W = 64
GRID_ROWS = SEQ // GRID_W
HEAD_DIM = 128
NA_HEADS = 4
NA_WIN_ROWS = 8
NA_WIN_COLS = 16
MLA_HEADS = 6
MLA_Q_RANK = 512
MLA_KV_RANK = 256
MLA_NOPE = 128
MLA_ROPE = 64
MLA_V = 128
ROPE_THETA = 10000.0
SWA_HEADS = 6
SWA_KV_HEADS = 2
SWA_GROUP = SWA_HEADS // SWA_KV_HEADS
SWA_WINDOW = 128
SWA_BLOCK = 128
D_A = NA_HEADS * HEAD_DIM
D_B = MLA_HEADS * MLA_V
D_C = SWA_HEADS * HEAD_DIM
IN_A = 3 * D_A
IN_B = MLA_Q_RANK + MLA_KV_RANK + MLA_ROPE
IN_C = (SWA_HEADS + 2 * SWA_KV_HEADS) * HEAD_DIM
D_FF = 5632
DEEPNORM_ALPHA = (2 * DEPTH) ** 0.25
LN_EPS = 1e-5
RMS_EPS = 1e-6
NEG_INF = -1e30

PB_W = MLA_Q_RANK + MLA_KV_RANK + 2 * MLA_ROPE
P_W = IN_A + PB_W + IN_C
MLA_QK = 2 * HEAD_DIM

NA_QROWS = 8
NA_KROWS = 16
NA_TQ = NA_QROWS * GRID_W
NA_TK = NA_KROWS * GRID_W
NA_NBLK = GRID_ROWS // NA_QROWS

VMEM_LIMIT = 56 * 1024 * 1024


def _cparams(sem):
    return pltpu.CompilerParams(dimension_semantics=sem, vmem_limit_bytes=VMEM_LIMIT)


def _layer_norm(z, g, b):
    mu = jnp.mean(z, axis=-1, keepdims=True)
    zc = z - mu
    var = jnp.mean(zc * zc, axis=-1, keepdims=True)
    return zc * lax.rsqrt(var + LN_EPS) * g + b


def _rms_norm(x, g):
    ms = jnp.mean(x * x, axis=-1, keepdims=True)
    return x * lax.rsqrt(ms + RMS_EPS) * g


def _dot_nt(a, b):
    return lax.dot_general(a, b, (((1,), (1,)), ((), ())), preferred_element_type=F32)


ADA_TN = 1024
ADA_RC = 256


def _ada_kernel(c_ref, w_ref, b_ref, o_ref):
    tn = o_ref.shape[-1]
    acc = jnp.zeros((8, tn), F32)
    for r in range(0, D_MODEL, ADA_RC):
        c = c_ref[r:r + ADA_RC, :]
        cond = c * (1.0 / (1.0 + jnp.exp(-c)))
        prod = w_ref[0, r:r + ADA_RC, :] * cond
        acc = acc + jnp.sum(prod.reshape(ADA_RC // 8, 8, tn), axis=0)
    o_ref[0] = jnp.sum(acc, axis=0, keepdims=True) + b_ref[0]


def _ada_call(c_col, w_ada, b_ada3):
    n = w_ada.shape[-1]
    return pl.pallas_call(
        _ada_kernel,
        out_shape=jax.ShapeDtypeStruct((DEPTH, 1, n), F32),
        grid=(DEPTH, n // ADA_TN),
        in_specs=[
            pl.BlockSpec((D_MODEL, 1), lambda l, j: (0, 0)),
            pl.BlockSpec((1, D_MODEL, ADA_TN), lambda l, j: (l, 0, j)),
            pl.BlockSpec((1, 1, ADA_TN), lambda l, j: (l, 0, j)),
        ],
        out_specs=pl.BlockSpec((1, 1, ADA_TN), lambda l, j: (l, 0, j)),
        compiler_params=_cparams(("parallel", "parallel")),
        name="ada_mod",
    )(c_col, w_ada, b_ada3)


INPROJ_TM = 512
_INPROJ_CHUNKS = (
    (0, 512, 0, 0), (512, 1024, 0, 512), (1024, 1536, 0, 1024),
    (1536, 2048, 1, 0), (2048, 2432, 1, 512),
    (2432, 2944, 2, 0), (2944, 3456, 2, 512), (3456, 3712, 2, 1024),
)


def _inproj_kernel(x_ref, mod_ref, w_ref, cs_ref, oa_ref, ob_ref, oc_ref):
    outs = (oa_ref, ob_ref, oc_ref)
    sh = mod_ref[0:1, :]
    sc = mod_ref[1:2, :]
    u = (x_ref[...] * (1.0 + sc) + sh).astype(BF16)
    for c0, c1, oi, off in _INPROJ_CHUNKS:
        acc = jnp.dot(u, w_ref[:, c0:c1], preferred_element_type=F32) * cs_ref[:, c0:c1]
        outs[oi][:, off:off + (c1 - c0)] = acc.astype(outs[oi].dtype)


def _inproj_call(x2, mod_l, w_p, colscale):
    tm = INPROJ_TM
    return pl.pallas_call(
        _inproj_kernel,
        out_shape=(jax.ShapeDtypeStruct((SEQ, IN_A), BF16),
                   jax.ShapeDtypeStruct((SEQ, PB_W), F32),
                   jax.ShapeDtypeStruct((SEQ, IN_C), BF16)),
        grid=(SEQ // tm,),
        in_specs=[
            pl.BlockSpec((tm, D_MODEL), lambda i: (i, 0)),
            pl.BlockSpec((6, D_MODEL), lambda i: (0, 0)),
            pl.BlockSpec((D_MODEL, P_W), lambda i: (0, 0), pipeline_mode=pl.Buffered(1)),
            pl.BlockSpec((1, P_W), lambda i: (0, 0)),
        ],
        out_specs=(pl.BlockSpec((tm, IN_A), lambda i: (i, 0)),
                   pl.BlockSpec((tm, PB_W), lambda i: (i, 0)),
                   pl.BlockSpec((tm, IN_C), lambda i: (i, 0))),
        compiler_params=_cparams(("parallel",)),
        name="in_proj",
    )(x2, mod_l, w_p, colscale)


def _na_block_rule(btype, i, j):
    if btype == 0:
        r0 = max(i - NA_WIN_ROWS // 2, 0)
        valid = r0 <= j < r0 + NA_WIN_ROWS
        ro = j - i + (NA_WIN_ROWS - 1)
    elif btype == 1:
        valid = i <= j < i + NA_WIN_ROWS
        ro = j - i + (NA_WIN_ROWS - 1) - NA_WIN_ROWS // 2
    else:
        r = GRID_ROWS - NA_QROWS + i
        ks = GRID_ROWS - NA_KROWS
        r0 = min(r - NA_WIN_ROWS // 2, GRID_ROWS - NA_WIN_ROWS)
        valid = r0 <= ks + j < r0 + NA_WIN_ROWS
        ro = ks + j - r + (NA_WIN_ROWS - 1)
    return ro if valid else None


def _na_bias_kernel(rpb_ref, o_ref):
    h = pl.program_id(0)
    t = pl.program_id(1)
    n_ro = 2 * NA_WIN_ROWS - 1
    n_co = 2 * NA_WIN_COLS - 1
    cq = lax.broadcasted_iota(jnp.int32, (GRID_W, GRID_W), 0)
    ck = lax.broadcasted_iota(jnp.int32, (GRID_W, GRID_W), 1)
    c0 = jnp.clip(cq - NA_WIN_COLS // 2, 0, GRID_W - NA_WIN_COLS)
    coff = jnp.clip(ck - cq, -(NA_WIN_COLS - 1), NA_WIN_COLS - 1) + (NA_WIN_COLS - 1)
    neg = jnp.full((GRID_W, GRID_W), NEG_INF, F32)
    tblocks = []
    for ro in range(n_ro):
        tb = neg
        for j in range(n_co):
            tb = jnp.where(coff == j, rpb_ref[h * (n_ro * n_co) + ro * n_co + j], tb)
        inside = jnp.where(ck >= c0, jnp.where(ck < c0 + NA_WIN_COLS, 1, 0), 0)
        tblocks.append(jnp.where(inside == 1, tb, neg))
    for btype in range(3):
        @pl.when(t == btype)
        def _(btype=btype):
            for i in range(NA_QROWS):
                for jp in range(NA_KROWS // 2):
                    pair = []
                    for j in (2 * jp, 2 * jp + 1):
                        ro = _na_block_rule(btype, i, j)
                        pair.append(neg if ro is None else tblocks[ro])
                    o_ref[0, 0, i * GRID_W:(i + 1) * GRID_W, jp * 128:(jp + 1) * 128] = (
                        jnp.concatenate(pair, axis=1))


def _na_bias_call(rpb_flat):
    return pl.pallas_call(
        _na_bias_kernel,
        out_shape=jax.ShapeDtypeStruct((NA_HEADS, 3, NA_TQ, NA_TK), F32),
        grid=(NA_HEADS, 3),
        in_specs=[pl.BlockSpec(memory_space=pltpu.SMEM)],
        out_specs=pl.BlockSpec((1, 1, NA_TQ, NA_TK), lambda h, t: (h, t, 0, 0)),
        compiler_params=_cparams(("parallel", "parallel")),
        name="na_bias",
    )(rpb_flat)


def _na_kernel(q_ref, k_ref, v_ref, bias_ref, o_ref):
    b = pl.program_id(1)
    ks = jnp.clip(NA_QROWS * b - NA_WIN_ROWS // 2, 0, GRID_ROWS - NA_KROWS) * GRID_W
    ks = pl.multiple_of(ks, GRID_W)
    k = k_ref[pl.ds(ks, NA_TK), :]
    v = v_ref[pl.ds(ks, NA_TK), :]
    s = _dot_nt(q_ref[...], k) + bias_ref[0, 0]
    m = jnp.max(s, axis=1, keepdims=True)
    p = jnp.exp(s - m)
    l = jnp.sum(p, axis=1, keepdims=True)
    o = jnp.dot(p.astype(BF16), v, preferred_element_type=F32)
    o_ref[...] = o * (1.0 / l)


def _na_call(pa, bias):
    def btype(b):
        return jnp.where(b == 0, 0, jnp.where(b == NA_NBLK - 1, 2, 1))
    return pl.pallas_call(
        _na_kernel,
        out_shape=jax.ShapeDtypeStruct((SEQ, D_A), F32),
        grid=(NA_HEADS, NA_NBLK),
        in_specs=[
            pl.BlockSpec((NA_TQ, HEAD_DIM), lambda h, b: (b, h)),
            pl.BlockSpec((SEQ, HEAD_DIM), lambda h, b: (0, NA_HEADS + h)),
            pl.BlockSpec((SEQ, HEAD_DIM), lambda h, b: (0, 2 * NA_HEADS + h)),
            pl.BlockSpec((1, 1, NA_TQ, NA_TK), lambda h, b: (h, btype(b), 0, 0)),
        ],
        out_specs=pl.BlockSpec((NA_TQ, HEAD_DIM), lambda h, b: (b, h)),
        compiler_params=_cparams(("parallel", "arbitrary")),
        name="na_attn",
    )(pa, pa, pa, bias)


MLAP_TM = 512


def _mla_prep_kernel(pb_ref, gq_ref, gkv_ref, wq_ref, wkv_ref, cos_ref, sin_ref,
                     q_ref, k_ref, v_ref):
    tm = pb_ref.shape[0]
    cqn = _rms_norm(pb_ref[:, 0:MLA_Q_RANK], gq_ref[...]).astype(BF16)
    ckvn = _rms_norm(pb_ref[:, MLA_Q_RANK:MLA_Q_RANK + MLA_KV_RANK], gkv_ref[...]).astype(BF16)
    cos = cos_ref[...]
    sin = sin_ref[...]

    def rotary(t):
        return t * cos + pltpu.roll(t, MLA_ROPE, 1) * sin

    kpe = rotary(pb_ref[:, MLA_Q_RANK + MLA_KV_RANK:PB_W]).astype(BF16)
    ones = jnp.ones((tm, HEAD_DIM), BF16)
    scale = (MLA_NOPE + MLA_ROPE) ** -0.5
    for h in range(MLA_HEADS):
        qh = jnp.dot(cqn, wq_ref[:, h * MLA_QK:(h + 1) * MLA_QK], preferred_element_type=F32)
        q_ref[h, :, 0:HEAD_DIM] = (qh[:, 0:HEAD_DIM] * scale).astype(BF16)
        q_ref[h, :, HEAD_DIM:MLA_QK] = (rotary(qh[:, HEAD_DIM:MLA_QK]) * scale).astype(BF16)
        kvh = jnp.dot(ckvn, wkv_ref[:, h * MLA_QK:(h + 1) * MLA_QK], preferred_element_type=F32)
        k_ref[h, :, 0:HEAD_DIM] = kvh[:, 0:HEAD_DIM].astype(BF16)
        k_ref[h, :, HEAD_DIM:MLA_QK] = kpe
        v_ref[h, :, 0:HEAD_DIM] = kvh[:, HEAD_DIM:MLA_QK].astype(BF16)
        v_ref[h, :, HEAD_DIM:MLA_QK] = ones


def _mla_prep_call(pb, gq, gkv, wq, wkv, cos_t, sin_t):
    tm = MLAP_TM
    hsd = jax.ShapeDtypeStruct((MLA_HEADS, SEQ, MLA_QK), BF16)
    hspec = pl.BlockSpec((MLA_HEADS, tm, MLA_QK), lambda i: (0, i, 0))
    return pl.pallas_call(
        _mla_prep_kernel,
        out_shape=(hsd, hsd, hsd),
        grid=(SEQ // tm,),
        in_specs=[
            pl.BlockSpec((tm, PB_W), lambda i: (i, 0)),
            pl.BlockSpec((1, MLA_Q_RANK), lambda i: (0, 0)),
            pl.BlockSpec((1, MLA_KV_RANK), lambda i: (0, 0)),
            pl.BlockSpec((MLA_Q_RANK, MLA_HEADS * MLA_QK), lambda i: (0, 0)),
            pl.BlockSpec((MLA_KV_RANK, MLA_HEADS * MLA_QK), lambda i: (0, 0)),
            pl.BlockSpec((tm, HEAD_DIM), lambda i: (i, 0)),
            pl.BlockSpec((tm, HEAD_DIM), lambda i: (i, 0)),
        ],
        out_specs=(hspec, hspec, hspec),
        compiler_params=_cparams(("parallel",)),
        name="mla_prep",
    )(pb, gq, gkv, wq, wkv, cos_t, sin_t)


MLA_TQ = 256
MLA_TK = 512


def _mla_attn_kernel(q_ref, k_ref, v_ref, o_ref):
    q = q_ref[0]
    tq = q.shape[0]

    def body(i, carry):
        m, acc = carry
        kk = pl.multiple_of(i * MLA_TK, MLA_TK)
        s = _dot_nt(q, k_ref[0, pl.ds(kk, MLA_TK), :])
        m_new = jnp.maximum(m, jnp.max(s, axis=1, keepdims=True))
        alpha = jnp.exp(m - m_new)
        p = jnp.exp(s - m_new).astype(BF16)
        acc = alpha * acc + jnp.dot(p, v_ref[0, pl.ds(kk, MLA_TK), :], preferred_element_type=F32)
        return m_new, acc

    m0 = jnp.full((tq, 1), NEG_INF, F32)
    acc0 = jnp.zeros((tq, MLA_QK), F32)
    _, acc = lax.fori_loop(0, SEQ // MLA_TK, body, (m0, acc0))
    o_ref[...] = acc[:, 0:MLA_V] * (1.0 / acc[:, MLA_V:MLA_V + 1])


def _mla_attn_call(q, k, v):
    tq = MLA_TQ
    return pl.pallas_call(
        _mla_attn_kernel,
        out_shape=jax.ShapeDtypeStruct((SEQ, D_B), F32),
        grid=(MLA_HEADS, SEQ // tq),
        in_specs=[
            pl.BlockSpec((1, tq, MLA_QK), lambda h, i: (h, i, 0)),
            pl.BlockSpec((1, SEQ, MLA_QK), lambda h, i: (h, 0, 0)),
            pl.BlockSpec((1, SEQ, MLA_QK), lambda h, i: (h, 0, 0)),
        ],
        out_specs=pl.BlockSpec((tq, MLA_V), lambda h, i: (i, h)),
        compiler_params=_cparams(("parallel", "arbitrary")),
        name="mla_attn",
    )(q, k, v)


_SWA_SLOPES = tuple(2.0 ** (-8.0 * (i + 1) / SWA_HEADS) for i in range(SWA_HEADS))


def _swa_kernel(sink_ref, q_ref, kp_ref, kc_ref, kn_ref, vp_ref, vc_ref, vn_ref, o_ref):
    g_kv = pl.program_id(0)
    n = pl.program_id(1)
    nb = pl.num_programs(1)
    t = SWA_BLOCK
    rows = SWA_GROUP * t
    q = jnp.concatenate([q_ref[:, g * t:(g + 1) * t] for g in range(SWA_GROUP)], axis=0)
    k = jnp.concatenate([kp_ref[...], kc_ref[...], kn_ref[...]], axis=0)
    v = jnp.concatenate([vp_ref[...], vc_ref[...], vn_ref[...]], axis=0)
    s = _dot_nt(q, k)
    ri = lax.broadcasted_iota(jnp.int32, (rows, 3 * t), 0)
    ci = lax.broadcasted_iota(jnp.int32, (rows, 3 * t), 1)
    grp = jnp.right_shift(ri, 7)
    dist = jnp.abs(jnp.bitwise_and(ri, t - 1) - (ci - t))
    slope_even = jnp.where(grp == 0, _SWA_SLOPES[0], jnp.where(grp == 1, _SWA_SLOPES[1], _SWA_SLOPES[2]))
    slope_odd = jnp.where(grp == 0, _SWA_SLOPES[3], jnp.where(grp == 1, _SWA_SLOPES[4], _SWA_SLOPES[5]))
    slope = jnp.where(g_kv == 0, slope_even, slope_odd)
    kpos = (n - 1) * t + ci
    ok = jnp.where(dist <= SWA_WINDOW, jnp.where(kpos >= 0, jnp.where(kpos < nb * t, 1, 0), 0), 0)
    s = jnp.where(ok == 1, s - slope * dist.astype(F32), NEG_INF)
    rcol = jnp.right_shift(lax.broadcasted_iota(jnp.int32, (rows, 1), 0), 7)
    base = g_kv * SWA_GROUP
    sink = jnp.where(rcol == 0, sink_ref[base], jnp.where(rcol == 1, sink_ref[base + 1], sink_ref[base + 2]))
    m = jnp.maximum(jnp.max(s, axis=1, keepdims=True), sink)
    p = jnp.exp(s - m)
    l = jnp.sum(p, axis=1, keepdims=True) + jnp.exp(sink - m)
    o = jnp.dot(p.astype(BF16), v, preferred_element_type=F32) * (1.0 / l)
    for g in range(SWA_GROUP):
        o_ref[:, g * t:(g + 1) * t] = o[g * t:(g + 1) * t, :]


def _swa_call(sink, pc):
    t = SWA_BLOCK
    nb = SEQ // t
    kcol = SWA_HEADS
    vcol = SWA_HEADS + SWA_KV_HEADS
    prev = lambda n: jnp.maximum(n - 1, 0)
    nxt = lambda n: jnp.minimum(n + 1, nb - 1)
    return pl.pallas_call(
        _swa_kernel,
        out_shape=jax.ShapeDtypeStruct((SEQ, D_C), F32),
        grid=(SWA_KV_HEADS, nb),
        in_specs=[
            pl.BlockSpec(memory_space=pltpu.SMEM),
            pl.BlockSpec((t, SWA_GROUP * t), lambda g, n: (n, g)),
            pl.BlockSpec((t, t), lambda g, n: (prev(n), kcol + g)),
            pl.BlockSpec((t, t), lambda g, n: (n, kcol + g)),
            pl.BlockSpec((t, t), lambda g, n: (nxt(n), kcol + g)),
            pl.BlockSpec((t, t), lambda g, n: (prev(n), vcol + g)),
            pl.BlockSpec((t, t), lambda g, n: (n, vcol + g)),
            pl.BlockSpec((t, t), lambda g, n: (nxt(n), vcol + g)),
        ],
        out_specs=pl.BlockSpec((t, SWA_GROUP * t), lambda g, n: (n, g)),
        compiler_params=_cparams(("parallel", "arbitrary")),
        name="swa_attn",
    )(sink, pc, pc, pc, pc, pc, pc, pc)


OUTPROJ_TM = 256


def _outproj_kernel(ya_ref, yb_ref, yc_ref, x_ref, mod_ref, gn_ref, w_ref, lg_ref, lb_ref, o_ref):
    acc = None
    for y_ref, c0 in ((ya_ref, 0), (yb_ref, D_A), (yc_ref, D_A + D_B)):
        c1 = c0 + y_ref.shape[1]
        yn = _rms_norm(y_ref[...], gn_ref[:, c0:c1]).astype(BF16)
        part = jnp.dot(yn, w_ref[c0:c1, :], preferred_element_type=F32)
        acc = part if acc is None else acc + part
    gate = 1.0 + mod_ref[2:3, :]
    z = DEEPNORM_ALPHA * x_ref[...] + gate * acc
    o_ref[...] = _layer_norm(z, lg_ref[...], lb_ref[...])


def _outproj_call(ya, yb, yc, x2, mod_l, gn, w_o, lg, lb):
    tm = OUTPROJ_TM
    row = lambda w: pl.BlockSpec((tm, w), lambda i: (i, 0))
    full = lambda r, w: pl.BlockSpec((r, w), lambda i: (0, 0))
    return pl.pallas_call(
        _outproj_kernel,
        out_shape=jax.ShapeDtypeStruct((SEQ, D_MODEL), F32),
        grid=(SEQ // tm,),
        in_specs=[row(D_A), row(D_B), row(D_C), row(D_MODEL), full(6, D_MODEL), full(1, D_MODEL),
                  pl.BlockSpec((D_MODEL, D_MODEL), lambda i: (0, 0), pipeline_mode=pl.Buffered(1)),
                  full(1, D_MODEL), full(1, D_MODEL)],
        out_specs=row(D_MODEL),
        compiler_params=_cparams(("parallel",)),
        name="out_proj_ln",
    )(ya, yb, yc, x2, mod_l, gn, w_o, lg, lb)


FFN_TM = 512
FFN_TF = 512


def _ffn_kernel(x_ref, mod_ref, wg_ref, wu_ref, wd_ref, lg_ref, lb_ref, o_ref, u_sc, acc_sc):
    f = pl.program_id(1)

    @pl.when(f == 0)
    def _():
        u_sc[...] = (x_ref[...] * (1.0 + mod_ref[4:5, :]) + mod_ref[3:4, :]).astype(BF16)
        acc_sc[...] = jnp.zeros_like(acc_sc)

    u = u_sc[...]
    g = jnp.dot(u, wg_ref[...], preferred_element_type=F32)
    up = jnp.dot(u, wu_ref[...], preferred_element_type=F32)
    hdn = (g * (1.0 / (1.0 + jnp.exp(-g))) * up).astype(BF16)
    acc_sc[...] += jnp.dot(hdn, wd_ref[...], preferred_element_type=F32)

    @pl.when(f == pl.num_programs(1) - 1)
    def _():
        z = DEEPNORM_ALPHA * x_ref[...] + (1.0 + mod_ref[5:6, :]) * acc_sc[...]
        o_ref[...] = _layer_norm(z, lg_ref[...], lb_ref[...])


def _ffn_call(x2, mod_l, w_gu, w_down, lg, lb):
    tm, tf = FFN_TM, FFN_TF
    nf = D_FF // tf
    return pl.pallas_call(
        _ffn_kernel,
        out_shape=jax.ShapeDtypeStruct((SEQ, D_MODEL), F32),
        grid=(SEQ // tm, nf),
        in_specs=[
            pl.BlockSpec((tm, D_MODEL), lambda i, f: (i, 0)),
            pl.BlockSpec((6, D_MODEL), lambda i, f: (0, 0)),
            pl.BlockSpec((D_MODEL, tf), lambda i, f: (0, f)),
            pl.BlockSpec((D_MODEL, tf), lambda i, f: (0, nf + f)),
            pl.BlockSpec((tf, D_MODEL), lambda i, f: (f, 0)),
            pl.BlockSpec((1, D_MODEL), lambda i, f: (0, 0)),
            pl.BlockSpec((1, D_MODEL), lambda i, f: (0, 0)),
        ],
        out_specs=pl.BlockSpec((tm, D_MODEL), lambda i, f: (i, 0)),
        scratch_shapes=[pltpu.VMEM((tm, D_MODEL), BF16), pltpu.VMEM((tm, D_MODEL), F32)],
        compiler_params=_cparams(("parallel", "arbitrary")),
        name="ffn_ln",
    )(x2, mod_l, w_gu, w_gu, w_down, lg, lb)


def _rot_half_cols(w):
    half = w.shape[-1] // 2
    return jnp.concatenate([-w[..., half:], w[..., :half]], axis=-1)


def _rope_tables():
    half = MLA_ROPE // 2
    inv = ROPE_THETA ** (-jnp.arange(half, dtype=F32) / half)
    ang = jnp.arange(SEQ, dtype=F32)[:, None] * inv[None, :]
    zeros = jnp.zeros((SEQ, MLA_ROPE), F32)
    cos = jnp.cos(ang)
    sin = jnp.sin(ang)
    return (jnp.concatenate([cos, cos, zeros], axis=1), jnp.concatenate([sin, sin, zeros], axis=1))


def _colscale():
    cs = np.ones((1, P_W), np.float32)
    cs[0, 0:D_A] = HEAD_DIM ** -0.5
    c0 = IN_A + PB_W
    cs[0, c0:c0 + D_C] = HEAD_DIM ** -0.5
    return jnp.asarray(cs)


def kernel(x, c, w_ada, b_ada, w_in, na_rpb, mla_q_norm, mla_kv_norm, mla_w_uq, mla_w_ukv,
           swa_sink, out_norm_g, w_o, ln1_g, ln1_b, w_gu, w_down, ln2_g, ln2_b):
    assert x.shape == (1, SEQ, D_MODEL)
    x2 = x.reshape(SEQ, D_MODEL)
    mod = _ada_call(c.reshape(D_MODEL, 1), w_ada, b_ada.reshape(DEPTH, 1, -1))
    mod = mod.reshape(DEPTH, 6, D_MODEL)
    cos_t, sin_t = _rope_tables()
    colscale = _colscale()
    kr0 = IN_A + MLA_Q_RANK + MLA_KV_RANK
    for l in range(DEPTH):
        w_l = w_in[l]
        w_p = jnp.concatenate(
            [w_l[:, :kr0 + MLA_ROPE], _rot_half_cols(w_l[:, kr0:kr0 + MLA_ROPE]), w_l[:, kr0 + MLA_ROPE:]],
            axis=1).astype(BF16)
        wq = mla_w_uq[l].reshape(MLA_Q_RANK, MLA_HEADS, MLA_NOPE + MLA_ROPE)
        wq = jnp.concatenate([wq, _rot_half_cols(wq[..., MLA_NOPE:])], axis=-1)
        wq = wq.reshape(MLA_Q_RANK, MLA_HEADS * MLA_QK).astype(BF16)
        wkv = mla_w_ukv[l].astype(BF16)

        pa, pb, pc = _inproj_call(x2, mod[l], w_p, colscale)
        bias = _na_bias_call(na_rpb[l].reshape(-1))
        ya = _na_call(pa, bias)
        q, k, v = _mla_prep_call(pb, mla_q_norm[l].reshape(1, -1), mla_kv_norm[l].reshape(1, -1),
                                 wq, wkv, cos_t, sin_t)
        yb = _mla_attn_call(q, k, v)
        yc = _swa_call(swa_sink[l], pc)
        x2 = _outproj_call(ya, yb, yc, x2, mod[l], out_norm_g[l].reshape(1, -1), w_o[l].astype(BF16),
                           ln1_g[l].reshape(1, -1), ln1_b[l].reshape(1, -1))
        x2 = _ffn_call(x2, mod[l], w_gu[l].astype(BF16), w_down[l].astype(BF16),
                       ln2_g[l].reshape(1, -1), ln2_b[l].reshape(1, -1))
    return x2.reshape(1, SEQ, D_MODEL)
```

```python
import functools

import numpy as np
import jax
import jax.numpy as jnp
from jax import lax
from jax.experimental import pallas as pl
from jax.experimental.pallas import tpu as pltpu

F32 = jnp.float32
BF16 = jnp.bfloat16

D_MODEL = 2048
SEQ = 8192
DEPTH = 2
GRID_W = 64
GRID_ROWS = SEQ // GRID_W
HEAD_DIM = 128
NA_HEADS = 4
NA_WIN_ROWS = 8
NA_WIN_COLS = 16
MLA_HEADS = 6
MLA_Q_RANK = 512
MLA_KV_RANK = 256
MLA_NOPE = 128
MLA_ROPE = 64
MLA_V = 128
ROPE_THETA = 10000.0
SWA_HEADS = 6
SWA_KV_HEADS = 2
SWA_GROUP = SWA_HEADS // SWA_KV_HEADS
SWA_WINDOW = 128
SWA_BLOCK = 128
D_A = NA_HEADS * HEAD_DIM
D_B = MLA_HEADS * MLA_V
D_C = SWA_HEADS * HEAD_DIM
IN_A = 3 * D_A
IN_B = MLA_Q_RANK + MLA_KV_RANK + MLA_ROPE
IN_C = (SWA_HEADS + 2 * SWA_KV_HEADS) * HEAD_DIM
D_FF = 5632
DEEPNORM_ALPHA = (2 * DEPTH) ** 0.25
LN_EPS = 1e-5
RMS_EPS = 1e-6
NEG_INF = -1e30
LOG2_E = 1.4426950408889634

PB_W = MLA_Q_RANK + MLA_KV_RANK + 2 * MLA_ROPE
P_W = IN_A + PB_W + IN_C
MLA_QK = 2 * HEAD_DIM

NA_QROWS = 8
NA_KROWS = 16
NA_TQ = NA_QROWS * GRID_W
NA_TK = NA_KROWS * GRID_W
NA_NBLK = GRID_ROWS // NA_QROWS

VMEM_LIMIT = 56 * 1024 * 1024


def _cparams(sem):
    return pltpu.CompilerParams(dimension_semantics=sem, vmem_limit_bytes=VMEM_LIMIT)


def _layer_norm(z, g, b):
    mu = jnp.mean(z, axis=-1, keepdims=True)
    zc = z - mu
    var = jnp.mean(zc * zc, axis=-1, keepdims=True)
    return zc * lax.rsqrt(var + LN_EPS) * g + b


def _rms_norm(x, g):
    ms = jnp.mean(x * x, axis=-1, keepdims=True)
    return x * lax.rsqrt(ms + RMS_EPS) * g


def _dot_nt(a, b):
    return lax.dot_general(a, b, (((1,), (1,)), ((), ())), preferred_element_type=F32)


ADA_TN = 1024
ADA_RC = 256


def _ada_kernel(c_ref, w_ref, b_ref, o_ref):
    tn = o_ref.shape[-1]
    acc = jnp.zeros((8, tn), F32)
    for r in range(0, D_MODEL, ADA_RC):
        c = c_ref[r:r + ADA_RC, :]
        cond = c * (1.0 / (1.0 + jnp.exp(-c)))
        prod = w_ref[0, r:r + ADA_RC, :] * cond
        acc = acc + jnp.sum(prod.reshape(ADA_RC // 8, 8, tn), axis=0)
    o_ref[0] = jnp.sum(acc, axis=0, keepdims=True) + b_ref[0]


def _ada_call(c_col, w_ada, b_ada3):
    n = w_ada.shape[-1]
    return pl.pallas_call(
        _ada_kernel,
        out_shape=jax.ShapeDtypeStruct((DEPTH, 1, n), F32),
        grid=(DEPTH, n // ADA_TN),
        in_specs=[
            pl.BlockSpec((D_MODEL, 1), lambda l, j: (0, 0)),
            pl.BlockSpec((1, D_MODEL, ADA_TN), lambda l, j: (l, 0, j)),
            pl.BlockSpec((1, 1, ADA_TN), lambda l, j: (l, 0, j)),
        ],
        out_specs=pl.BlockSpec((1, 1, ADA_TN), lambda l, j: (l, 0, j)),
        compiler_params=_cparams(("parallel", "parallel")),
        name="ada_mod",
    )(c_col, w_ada, b_ada3)


INPROJ_TM = 512
_INPROJ_CHUNKS = (
    (0, 512, 0, 0), (512, 1024, 0, 512), (1024, 1536, 0, 1024),
    (1536, 2048, 1, 0), (2048, 2432, 1, 512),
    (2432, 2944, 2, 0), (2944, 3456, 2, 512), (3456, 3712, 2, 1024),
)


def _inproj_kernel(x_ref, mod_ref, w_ref, cs_ref, oa_ref, ob_ref, oc_ref):
    outs = (oa_ref, ob_ref, oc_ref)
    sh = mod_ref[0:1, :]
    sc = mod_ref[1:2, :]
    u = (x_ref[...] * (1.0 + sc) + sh).astype(BF16)
    for c0, c1, oi, off in _INPROJ_CHUNKS:
        acc = jnp.dot(u, w_ref[:, c0:c1], preferred_element_type=F32) * cs_ref[:, c0:c1]
        outs[oi][:, off:off + (c1 - c0)] = acc.astype(outs[oi].dtype)


def _inproj_call(l, x2, mod, w_p, colscale):
    tm = INPROJ_TM
    return pl.pallas_call(
        _inproj_kernel,
        out_shape=(jax.ShapeDtypeStruct((SEQ, IN_A), BF16),
                   jax.ShapeDtypeStruct((SEQ, PB_W), F32),
                   jax.ShapeDtypeStruct((SEQ, IN_C), BF16)),
        grid=(SEQ // tm,),
        in_specs=[
            pl.BlockSpec((tm, D_MODEL), lambda i: (i, 0)),
            pl.BlockSpec((None, 6, D_MODEL), lambda i: (l, 0, 0)),
            pl.BlockSpec((None, D_MODEL, P_W), lambda i: (l, 0, 0), pipeline_mode=pl.Buffered(1)),
            pl.BlockSpec((1, P_W), lambda i: (0, 0)),
        ],
        out_specs=(pl.BlockSpec((tm, IN_A), lambda i: (i, 0)),
                   pl.BlockSpec((tm, PB_W), lambda i: (i, 0)),
                   pl.BlockSpec((tm, IN_C), lambda i: (i, 0))),
        compiler_params=_cparams(("parallel",)),
        name="in_proj",
    )(x2, mod, w_p, colscale)


def _na_block_rule(btype, i, j):
    if btype == 0:
        r0 = max(i - NA_WIN_ROWS // 2, 0)
        valid = r0 <= j < r0 + NA_WIN_ROWS
        ro = j - i + (NA_WIN_ROWS - 1)
    elif btype == 1:
        valid = i <= j < i + NA_WIN_ROWS
        ro = j - i + (NA_WIN_ROWS - 1) - NA_WIN_ROWS // 2
    else:
        r = GRID_ROWS - NA_QROWS + i
        ks = GRID_ROWS - NA_KROWS
        r0 = min(r - NA_WIN_ROWS // 2, GRID_ROWS - NA_WIN_ROWS)
        valid = r0 <= ks + j < r0 + NA_WIN_ROWS
        ro = ks + j - r + (NA_WIN_ROWS - 1)
    return ro if valid else None


def _na_bias_kernel(rpb_ref, o_ref):
    h = pl.program_id(0)
    t = pl.program_id(1)
    n_ro = 2 * NA_WIN_ROWS - 1
    n_co = 2 * NA_WIN_COLS - 1
    cq = lax.broadcasted_iota(jnp.int32, (GRID_W, GRID_W), 0)
    ck = lax.broadcasted_iota(jnp.int32, (GRID_W, GRID_W), 1)
    c0 = jnp.clip(cq - NA_WIN_COLS // 2, 0, GRID_W - NA_WIN_COLS)
    coff = jnp.clip(ck - cq, -(NA_WIN_COLS - 1), NA_WIN_COLS - 1) + (NA_WIN_COLS - 1)
    neg = jnp.full((GRID_W, GRID_W), NEG_INF, F32)
    tblocks = []
    for ro in range(n_ro):
        tb = neg
        for j in range(n_co):
            tb = jnp.where(coff == j, rpb_ref[h * (n_ro * n_co) + ro * n_co + j], tb)
        inside = jnp.where(ck >= c0, jnp.where(ck < c0 + NA_WIN_COLS, 1, 0), 0)
        tblocks.append(jnp.where(inside == 1, tb, neg))
    for btype in range(3):
        @pl.when(t == btype)
        def _(btype=btype):
            for i in range(NA_QROWS):
                for jp in range(NA_KROWS // 2):
                    pair = []
                    for j in (2 * jp, 2 * jp + 1):
                        ro = _na_block_rule(btype, i, j)
                        pair.append(neg if ro is None else tblocks[ro])
                    o_ref[0, 0, i * GRID_W:(i + 1) * GRID_W, jp * 128:(jp + 1) * 128] = (
                        jnp.concatenate(pair, axis=1))


def _na_bias_call(rpb_flat):
    return pl.pallas_call(
        _na_bias_kernel,
        out_shape=jax.ShapeDtypeStruct((NA_HEADS, 3, NA_TQ, NA_TK), F32),
        grid=(NA_HEADS, 3),
        in_specs=[pl.BlockSpec(memory_space=pltpu.SMEM)],
        out_specs=pl.BlockSpec((1, 1, NA_TQ, NA_TK), lambda h, t: (h, t, 0, 0)),
        compiler_params=_cparams(("parallel", "parallel")),
        name="na_bias",
    )(rpb_flat)


def _na_kernel(q_ref, k_ref, v_ref, bias_ref, o_ref):
    b = pl.program_id(1)
    ks = jnp.clip(NA_QROWS * b - NA_WIN_ROWS // 2, 0, GRID_ROWS - NA_KROWS) * GRID_W
    ks = pl.multiple_of(ks, GRID_W)
    k = k_ref[pl.ds(ks, NA_TK), :]
    v = v_ref[pl.ds(ks, NA_TK), :]
    s = _dot_nt(q_ref[...], k) + bias_ref[0, 0]
    m = jnp.max(s, axis=1, keepdims=True)
    p = jnp.exp(s - m)
    l = jnp.sum(p, axis=1, keepdims=True)
    o = jnp.dot(p.astype(BF16), v, preferred_element_type=F32)
    o_ref[...] = o * (1.0 / l)


def _na_call(pa, bias):
    def btype(b):
        return jnp.where(b == 0, 0, jnp.where(b == NA_NBLK - 1, 2, 1))
    return pl.pallas_call(
        _na_kernel,
        out_shape=jax.ShapeDtypeStruct((SEQ, D_A), F32),
        grid=(NA_HEADS, NA_NBLK),
        in_specs=[
            pl.BlockSpec((NA_TQ, HEAD_DIM), lambda h, b: (b, h)),
            pl.BlockSpec((SEQ, HEAD_DIM), lambda h, b: (0, NA_HEADS + h)),
            pl.BlockSpec((SEQ, HEAD_DIM), lambda h, b: (0, 2 * NA_HEADS + h)),
            pl.BlockSpec((1, 1, NA_TQ, NA_TK), lambda h, b: (h, btype(b), 0, 0)),
        ],
        out_specs=pl.BlockSpec((NA_TQ, HEAD_DIM), lambda h, b: (b, h)),
        compiler_params=_cparams(("parallel", "arbitrary")),
        name="na_attn",
    )(pa, pa, pa, bias)


MLAP_TM = 512


def _mla_prep_kernel(pb_ref, gq_ref, gkv_ref, wq_ref, wkv_ref, cos_ref, sin_ref,
                     q_ref, k_ref, v_ref):
    tm = pb_ref.shape[0]
    cqn = _rms_norm(pb_ref[:, 0:MLA_Q_RANK], gq_ref[...]).astype(BF16)
    ckvn = _rms_norm(pb_ref[:, MLA_Q_RANK:MLA_Q_RANK + MLA_KV_RANK], gkv_ref[...]).astype(BF16)
    cos = cos_ref[...]
    sin = sin_ref[...]

    def rotary(t):
        return t * cos + pltpu.roll(t, MLA_ROPE, 1) * sin

    kpe = rotary(pb_ref[:, MLA_Q_RANK + MLA_KV_RANK:PB_W]).astype(BF16)
    ones = jnp.ones((tm, HEAD_DIM), BF16)
    scale = (MLA_NOPE + MLA_ROPE) ** -0.5 * LOG2_E
    for h in range(MLA_HEADS):
        qh = jnp.dot(cqn, wq_ref[:, h * MLA_QK:(h + 1) * MLA_QK], preferred_element_type=F32)
        q_ref[h, :, 0:HEAD_DIM] = (qh[:, 0:HEAD_DIM] * scale).astype(BF16)
        q_ref[h, :, HEAD_DIM:MLA_QK] = (rotary(qh[:, HEAD_DIM:MLA_QK]) * scale).astype(BF16)
        kvh = jnp.dot(ckvn, wkv_ref[:, h * MLA_QK:(h + 1) * MLA_QK], preferred_element_type=F32)
        k_ref[h, :, 0:HEAD_DIM] = kvh[:, 0:HEAD_DIM].astype(BF16)
        k_ref[h, :, HEAD_DIM:MLA_QK] = kpe
        v_ref[h, :, 0:HEAD_DIM] = kvh[:, HEAD_DIM:MLA_QK].astype(BF16)
        v_ref[h, :, HEAD_DIM:MLA_QK] = ones


def _mla_prep_call(l, pb, gq, gkv, wq, wkv, cos_t, sin_t):
    tm = MLAP_TM
    hsd = jax.ShapeDtypeStruct((MLA_HEADS, SEQ, MLA_QK), BF16)
    hspec = pl.BlockSpec((MLA_HEADS, tm, MLA_QK), lambda i: (0, i, 0))
    return pl.pallas_call(
        _mla_prep_kernel,
        out_shape=(hsd, hsd, hsd),
        grid=(SEQ // tm,),
        in_specs=[
            pl.BlockSpec((tm, PB_W), lambda i: (i, 0)),
            pl.BlockSpec((None, 1, MLA_Q_RANK), lambda i: (l, 0, 0)),
            pl.BlockSpec((None, 1, MLA_KV_RANK), lambda i: (l, 0, 0)),
            pl.BlockSpec((None, MLA_Q_RANK, MLA_HEADS * MLA_QK), lambda i: (l, 0, 0)),
            pl.BlockSpec((None, MLA_KV_RANK, MLA_HEADS * MLA_QK), lambda i: (l, 0, 0)),
            pl.BlockSpec((tm, HEAD_DIM), lambda i: (i, 0)),
            pl.BlockSpec((tm, HEAD_DIM), lambda i: (i, 0)),
        ],
        out_specs=(hspec, hspec, hspec),
        compiler_params=_cparams(("parallel",)),
        name="mla_prep",
    )(pb, gq, gkv, wq, wkv, cos_t, sin_t)


MLA_TQ = 256
MLA_TK = 512


def _mla_attn_kernel(q_ref, k_ref, v_ref, o_ref, s0_sc, s1_sc, m0_sc, m1_sc):
    i = pl.program_id(0)

    @pl.when(i == 0)
    def _():
        s1_sc[...] = jnp.zeros_like(s1_sc)
        m1_sc[...] = jnp.zeros_like(m1_sc)

    def step(sa_sc, ma_sc, sb_sc, mb_sc):
        q = q_ref[0]
        tq = q.shape[0]
        m_prev = mb_sc[...]
        mx = jnp.full((tq, 128), NEG_INF, F32)
        acc = jnp.zeros((tq, MLA_QK), F32)
        for c in range(SEQ // MLA_TK):
            cols = slice(c * MLA_TK, (c + 1) * MLA_TK)
            p = jnp.exp2(sb_sc[:, cols] - m_prev).astype(BF16)
            acc = acc + jnp.dot(p, v_ref[0, cols, :], preferred_element_type=F32)
            s = _dot_nt(q, k_ref[0, cols, :])
            sa_sc[:, cols] = s
            for j in range(MLA_TK // 128):
                mx = jnp.maximum(mx, s[:, j * 128:(j + 1) * 128])
        ma_sc[...] = jnp.max(mx, axis=1, keepdims=True)
        o_ref[...] = acc[:, 0:MLA_V] * (1.0 / acc[:, MLA_V:MLA_V + 1])

    parity = lax.rem(i, 2)

    @pl.when(parity == 0)
    def _():
        step(s0_sc, m0_sc, s1_sc, m1_sc)

    @pl.when(parity == 1)
    def _():
        step(s1_sc, m1_sc, s0_sc, m0_sc)


def _mla_attn_call(q, k, v):
    tq = MLA_TQ
    nq = SEQ // tq
    ntile = MLA_HEADS * nq
    tile_a = lambda i: jnp.minimum(i, ntile - 1)
    tile_b = lambda i: jnp.maximum(i - 1, 0)
    return pl.pallas_call(
        _mla_attn_kernel,
        out_shape=jax.ShapeDtypeStruct((SEQ, D_B), F32),
        grid=(ntile + 1,),
        in_specs=[
            pl.BlockSpec((1, tq, MLA_QK), lambda i: (tile_a(i) // nq, tile_a(i) % nq, 0)),
            pl.BlockSpec((1, SEQ, MLA_QK), lambda i: (tile_a(i) // nq, 0, 0), pipeline_mode=pl.Buffered(1)),
            pl.BlockSpec((1, SEQ, MLA_QK), lambda i: (tile_b(i) // nq, 0, 0), pipeline_mode=pl.Buffered(1)),
        ],
        out_specs=pl.BlockSpec((tq, MLA_V), lambda i: (tile_b(i) % nq, tile_b(i) // nq)),
        scratch_shapes=[pltpu.VMEM((tq, SEQ), F32), pltpu.VMEM((tq, SEQ), F32),
                        pltpu.VMEM((tq, 1), F32), pltpu.VMEM((tq, 1), F32)],
        compiler_params=_cparams(("arbitrary",)),
        name="mla_attn",
    )(q, k, v)


_SWA_SLOPES = tuple(2.0 ** (-8.0 * (i + 1) / SWA_HEADS) for i in range(SWA_HEADS))


def _swa_kernel(sink_ref, q_ref, kp_ref, kc_ref, kn_ref, vp_ref, vc_ref, vn_ref, o_ref):
    g_kv = pl.program_id(0)
    n = pl.program_id(1)
    nb = pl.num_programs(1)
    t = SWA_BLOCK
    rows = SWA_GROUP * t
    q = jnp.concatenate([q_ref[:, g * t:(g + 1) * t] for g in range(SWA_GROUP)], axis=0)
    k = jnp.concatenate([kp_ref[...], kc_ref[...], kn_ref[...]], axis=0)
    v = jnp.concatenate([vp_ref[...], vc_ref[...], vn_ref[...]], axis=0)
    s = _dot_nt(q, k)
    ri = lax.broadcasted_iota(jnp.int32, (rows, 3 * t), 0)
    ci = lax.broadcasted_iota(jnp.int32, (rows, 3 * t), 1)
    grp = jnp.right_shift(ri, 7)
    dist = jnp.abs(jnp.bitwise_and(ri, t - 1) - (ci - t))
    slope_even = jnp.where(grp == 0, _SWA_SLOPES[0], jnp.where(grp == 1, _SWA_SLOPES[1], _SWA_SLOPES[2]))
    slope_odd = jnp.where(grp == 0, _SWA_SLOPES[3], jnp.where(grp == 1, _SWA_SLOPES[4], _SWA_SLOPES[5]))
    slope = jnp.where(g_kv == 0, slope_even, slope_odd)
    kpos = (n - 1) * t + ci
    ok = jnp.where(dist <= SWA_WINDOW, jnp.where(kpos >= 0, jnp.where(kpos < nb * t, 1, 0), 0), 0)
    s = jnp.where(ok == 1, s - slope * dist.astype(F32), NEG_INF)
    rcol = jnp.right_shift(lax.broadcasted_iota(jnp.int32, (rows, 1), 0), 7)
    base = g_kv * SWA_GROUP
    sink = jnp.where(rcol == 0, sink_ref[base], jnp.where(rcol == 1, sink_ref[base + 1], sink_ref[base + 2]))
    m = jnp.maximum(jnp.max(s, axis=1, keepdims=True), sink)
    p = jnp.exp(s - m)
    l = jnp.sum(p, axis=1, keepdims=True) + jnp.exp(sink - m)
    o = jnp.dot(p.astype(BF16), v, preferred_element_type=F32) * (1.0 / l)
    for g in range(SWA_GROUP):
        o_ref[:, g * t:(g + 1) * t] = o[g * t:(g + 1) * t, :]


def _swa_call(sink, pc):
    t = SWA_BLOCK
    nb = SEQ // t
    kcol = SWA_HEADS
    vcol = SWA_HEADS + SWA_KV_HEADS
    prev = lambda n: jnp.maximum(n - 1, 0)
    nxt = lambda n: jnp.minimum(n + 1, nb - 1)
    return pl.pallas_call(
        _swa_kernel,
        out_shape=jax.ShapeDtypeStruct((SEQ, D_C), F32),
        grid=(SWA_KV_HEADS, nb),
        in_specs=[
            pl.BlockSpec(memory_space=pltpu.SMEM),
            pl.BlockSpec((t, SWA_GROUP * t), lambda g, n: (n, g)),
            pl.BlockSpec((t, t), lambda g, n: (prev(n), kcol + g)),
            pl.BlockSpec((t, t), lambda g, n: (n, kcol + g)),
            pl.BlockSpec((t, t), lambda g, n: (nxt(n), kcol + g)),
            pl.BlockSpec((t, t), lambda g, n: (prev(n), vcol + g)),
            pl.BlockSpec((t, t), lambda g, n: (n, vcol + g)),
            pl.BlockSpec((t, t), lambda g, n: (nxt(n), vcol + g)),
        ],
        out_specs=pl.BlockSpec((t, SWA_GROUP * t), lambda g, n: (n, g)),
        compiler_params=_cparams(("parallel", "arbitrary")),
        name="swa_attn",
    )(sink, pc, pc, pc, pc, pc, pc, pc)


OUTPROJ_TM = 256


def _outproj_kernel(ya_ref, yb_ref, yc_ref, x_ref, mod_ref, gn_ref, w_ref, lg_ref, lb_ref, o_ref):
    acc = None
    for y_ref, c0 in ((ya_ref, 0), (yb_ref, D_A), (yc_ref, D_A + D_B)):
        c1 = c0 + y_ref.shape[1]
        yn = _rms_norm(y_ref[...], gn_ref[:, c0:c1]).astype(BF16)
        part = jnp.dot(yn, w_ref[c0:c1, :], preferred_element_type=F32)
        acc = part if acc is None else acc + part
    gate = 1.0 + mod_ref[2:3, :]
    z = DEEPNORM_ALPHA * x_ref[...] + gate * acc
    o_ref[...] = _layer_norm(z, lg_ref[...], lb_ref[...])


def _outproj_call(l, ya, yb, yc, x2, mod, gn, w_o, lg, lb):
    tm = OUTPROJ_TM
    row = lambda w: pl.BlockSpec((tm, w), lambda i: (i, 0))
    full = lambda r, w: pl.BlockSpec((None, r, w), lambda i: (l, 0, 0))
    return pl.pallas_call(
        _outproj_kernel,
        out_shape=jax.ShapeDtypeStruct((SEQ, D_MODEL), F32),
        grid=(SEQ // tm,),
        in_specs=[row(D_A), row(D_B), row(D_C), row(D_MODEL), full(6, D_MODEL), full(1, D_MODEL),
                  pl.BlockSpec((None, D_MODEL, D_MODEL), lambda i: (l, 0, 0), pipeline_mode=pl.Buffered(1)),
                  full(1, D_MODEL), full(1, D_MODEL)],
        out_specs=row(D_MODEL),
        compiler_params=_cparams(("parallel",)),
        name="out_proj_ln",
    )(ya, yb, yc, x2, mod, gn, w_o, lg, lb)


FFN_TM = 512
FFN_TF = 512


def _ffn_kernel(x_ref, mod_ref, wg_ref, wu_ref, wd_ref, lg_ref, lb_ref, o_ref, u_sc, acc_sc):
    f = pl.program_id(1)

    @pl.when(f == 0)
    def _():
        u_sc[...] = (x_ref[...] * (1.0 + mod_ref[4:5, :]) + mod_ref[3:4, :]).astype(BF16)
        acc_sc[...] = jnp.zeros_like(acc_sc)

    u = u_sc[...]
    g = jnp.dot(u, wg_ref[...], preferred_element_type=F32)
    up = jnp.dot(u, wu_ref[...], preferred_element_type=F32)
    hdn = (g * (1.0 / (1.0 + jnp.exp(-g))) * up).astype(BF16)
    acc_sc[...] += jnp.dot(hdn, wd_ref[...], preferred_element_type=F32)

    @pl.when(f == pl.num_programs(1) - 1)
    def _():
        z = DEEPNORM_ALPHA * x_ref[...] + (1.0 + mod_ref[5:6, :]) * acc_sc[...]
        o_ref[...] = _layer_norm(z, lg_ref[...], lb_ref[...])


def _ffn_call(l, x2, mod, w_gu, w_down, lg, lb):
    tm, tf = FFN_TM, FFN_TF
    nf = D_FF // tf
    return pl.pallas_call(
        _ffn_kernel,
        out_shape=jax.ShapeDtypeStruct((SEQ, D_MODEL), F32),
        grid=(SEQ // tm, nf),
        in_specs=[
            pl.BlockSpec((tm, D_MODEL), lambda i, f: (i, 0)),
            pl.BlockSpec((None, 6, D_MODEL), lambda i, f: (l, 0, 0)),
            pl.BlockSpec((None, D_MODEL, tf), lambda i, f: (l, 0, f)),
            pl.BlockSpec((None, D_MODEL, tf), lambda i, f: (l, 0, nf + f)),
            pl.BlockSpec((None, tf, D_MODEL), lambda i, f: (l, f, 0)),
            pl.BlockSpec((None, 1, D_MODEL), lambda i, f: (l, 0, 0)),
            pl.BlockSpec((None, 1, D_MODEL), lambda i, f: (l, 0, 0)),
        ],
        out_specs=pl.BlockSpec((tm, D_MODEL), lambda i, f: (i, 0)),
        scratch_shapes=[pltpu.VMEM((tm, D_MODEL), BF16), pltpu.VMEM((tm, D_MODEL), F32)],
        compiler_params=_cparams(("parallel", "arbitrary")),
        name="ffn_ln",
    )(x2, mod, w_gu, w_gu, w_down, lg, lb)


def _rot_half_cols(w):
    half = w.shape[-1] // 2
    return jnp.concatenate([-w[..., half:], w[..., :half]], axis=-1)


def _rope_tables():
    half = MLA_ROPE // 2
    inv = ROPE_THETA ** (-jnp.arange(half, dtype=F32) / half)
    ang = jnp.arange(SEQ, dtype=F32)[:, None] * inv[None, :]
    zeros = jnp.zeros((SEQ, MLA_ROPE), F32)
    cos = jnp.cos(ang)
    sin = jnp.sin(ang)
    return (jnp.concatenate([cos, cos, zeros], axis=1), jnp.concatenate([sin, sin, zeros], axis=1))


def _colscale():
    cs = np.ones((1, P_W), np.float32)
    cs[0, 0:D_A] = HEAD_DIM ** -0.5
    c0 = IN_A + PB_W
    cs[0, c0:c0 + D_C] = HEAD_DIM ** -0.5
    return jnp.asarray(cs)


def kernel(x, c, w_ada, b_ada, w_in, na_rpb, mla_q_norm, mla_kv_norm, mla_w_uq, mla_w_ukv,
           swa_sink, out_norm_g, w_o, ln1_g, ln1_b, w_gu, w_down, ln2_g, ln2_b):
    assert x.shape == (1, SEQ, D_MODEL)
    x2 = x.reshape(SEQ, D_MODEL)
    mod = _ada_call(c.reshape(D_MODEL, 1), w_ada, b_ada.reshape(DEPTH, 1, -1))
    mod = mod.reshape(DEPTH, 6, D_MODEL)
    cos_t, sin_t = _rope_tables()
    colscale = _colscale()
    kr0 = IN_A + MLA_Q_RANK + MLA_KV_RANK
    w_p = jnp.concatenate(
        [w_in[..., :kr0 + MLA_ROPE], _rot_half_cols(w_in[..., kr0:kr0 + MLA_ROPE]), w_in[..., kr0 + MLA_ROPE:]],
        axis=-1).astype(BF16)
    wq = mla_w_uq.reshape(DEPTH, MLA_Q_RANK, MLA_HEADS, MLA_NOPE + MLA_ROPE)
    wq = jnp.concatenate([wq, _rot_half_cols(wq[..., MLA_NOPE:])], axis=-1)
    wq = wq.reshape(DEPTH, MLA_Q_RANK, MLA_HEADS * MLA_QK).astype(BF16)
    wkv = mla_w_ukv.astype(BF16)
    w_o_b = w_o.astype(BF16)
    w_gu_b = w_gu.astype(BF16)
    w_down_b = w_down.astype(BF16)
    row3 = lambda a: a.reshape(DEPTH, 1, -1)
    gq, gkv, gn = row3(mla_q_norm), row3(mla_kv_norm), row3(out_norm_g)
    l1g, l1b, l2g, l2b = row3(ln1_g), row3(ln1_b), row3(ln2_g), row3(ln2_b)
    for l in range(DEPTH):
        pa, pb, pc = _inproj_call(l, x2, mod, w_p, colscale)
        bias = _na_bias_call(na_rpb[l].reshape(-1))
        ya = _na_call(pa, bias)
        q, k, v = _mla_prep_call(l, pb, gq, gkv, wq, wkv, cos_t, sin_t)
        yb = _mla_attn_call(q, k, v)
        yc = _swa_call(swa_sink[l], pc)
        x2 = _outproj_call(l, ya, yb, yc, x2, mod, gn, w_o_b, l1g, l1b)
        x2 = _ffn_call(l, x2, mod, w_gu_b, w_down_b, l2g, l2b)
    return x2.reshape(1, SEQ, D_MODEL)
```

```python
import functools

import numpy as np
import jax
import jax.numpy as jnp
from jax import lax
from jax.experimental import pallas as pl
from jax.experimental.pallas import tpu as pltpu

F32 = jnp.float32
BF16 = jnp.bfloat16

D_MODEL = 2048
SEQ = 8192
DEPTH = 2
GRID_W = 64
GRID_ROWS = SEQ // GRID_W
HEAD_DIM = 128
NA_HEADS = 4
NA_WIN_ROWS = 8
NA_WIN_COLS = 16
MLA_HEADS = 6
MLA_Q_RANK = 512
MLA_KV_RANK = 256
MLA_NOPE = 128
MLA_ROPE = 64
MLA_V = 128
ROPE_THETA = 10000.0
SWA_HEADS = 6
SWA_KV_HEADS = 2
SWA_GROUP = SWA_HEADS // SWA_KV_HEADS
SWA_WINDOW = 128
SWA_BLOCK = 128
D_A = NA_HEADS * HEAD_DIM
D_B = MLA_HEADS * MLA_V
D_C = SWA_HEADS * HEAD_DIM
IN_A = 3 * D_A
IN_B = MLA_Q_RANK + MLA_KV_RANK + MLA_ROPE
IN_C = (SWA_HEADS + 2 * SWA_KV_HEADS) * HEAD_DIM
D_FF = 5632
DEEPNORM_ALPHA = (2 * DEPTH) ** 0.25
LN_EPS = 1e-5
RMS_EPS = 1e-6
NEG_INF = -1e30
LOG2_E = 1.4426950408889634

PB_W = MLA_Q_RANK + MLA_KV_RANK + 2 * MLA_ROPE
P_W = IN_A + PB_W + IN_C
MLA_QK = 2 * HEAD_DIM

NA_QROWS = 4
NA_KROWS = 12
NA_TQ = NA_QROWS * GRID_W
NA_TK = NA_KROWS * GRID_W
NA_NBLK = GRID_ROWS // NA_QROWS

VMEM_LIMIT = 56 * 1024 * 1024


def _cparams(sem):
    return pltpu.CompilerParams(dimension_semantics=sem, vmem_limit_bytes=VMEM_LIMIT)


def _layer_norm(z, g, b):
    mu = jnp.mean(z, axis=-1, keepdims=True)
    zc = z - mu
    var = jnp.mean(zc * zc, axis=-1, keepdims=True)
    return zc * lax.rsqrt(var + LN_EPS) * g + b


def _rms_norm(x, g):
    ms = jnp.mean(x * x, axis=-1, keepdims=True)
    return x * lax.rsqrt(ms + RMS_EPS) * g


def _dot_nt(a, b):
    return lax.dot_general(a, b, (((1,), (1,)), ((), ())), preferred_element_type=F32)


ADA_TN = 1024
ADA_RC = 256


def _ada_kernel(c_ref, w_ref, b_ref, o_ref):
    tn = o_ref.shape[-1]
    acc = jnp.zeros((8, tn), F32)
    for r in range(0, D_MODEL, ADA_RC):
        c = c_ref[r:r + ADA_RC, :]
        cond = c * (1.0 / (1.0 + jnp.exp(-c)))
        prod = w_ref[0, r:r + ADA_RC, :] * cond
        acc = acc + jnp.sum(prod.reshape(ADA_RC // 8, 8, tn), axis=0)
    o_ref[0] = jnp.sum(acc, axis=0, keepdims=True) + b_ref[0]


def _ada_call(c_col, w_ada, b_ada3):
    n = w_ada.shape[-1]
    return pl.pallas_call(
        _ada_kernel,
        out_shape=jax.ShapeDtypeStruct((DEPTH, 1, n), F32),
        grid=(DEPTH, n // ADA_TN),
        in_specs=[
            pl.BlockSpec((D_MODEL, 1), lambda l, j: (0, 0)),
            pl.BlockSpec((1, D_MODEL, ADA_TN), lambda l, j: (l, 0, j)),
            pl.BlockSpec((1, 1, ADA_TN), lambda l, j: (l, 0, j)),
        ],
        out_specs=pl.BlockSpec((1, 1, ADA_TN), lambda l, j: (l, 0, j)),
        compiler_params=_cparams(("parallel", "parallel")),
        name="ada_mod",
    )(c_col, w_ada, b_ada3)


WPREP_TK = 256
KR0 = IN_A + MLA_Q_RANK + MLA_KV_RANK


def _winprep_kernel(w_ref, o_ref):
    half = MLA_ROPE // 2
    o_ref[:, 0:KR0] = w_ref[:, 0:KR0].astype(BF16)
    kr = w_ref[:, KR0:KR0 + MLA_ROPE]
    rot = jnp.concatenate([-kr[:, half:], kr[:, :half]], axis=1)
    o_ref[:, KR0:KR0 + 2 * MLA_ROPE] = jnp.concatenate([kr, rot], axis=1).astype(BF16)
    o_ref[:, KR0 + 2 * MLA_ROPE:] = w_ref[:, KR0 + MLA_ROPE:].astype(BF16)


def _winprep_call(w_in):
    tk = WPREP_TK
    return pl.pallas_call(
        _winprep_kernel,
        out_shape=jax.ShapeDtypeStruct((DEPTH, D_MODEL, P_W), BF16),
        grid=(DEPTH, D_MODEL // tk),
        in_specs=[pl.BlockSpec((None, tk, w_in.shape[-1]), lambda l, i: (l, i, 0))],
        out_specs=pl.BlockSpec((None, tk, P_W), lambda l, i: (l, i, 0)),
        compiler_params=_cparams(("parallel", "parallel")),
        name="w_in_prep",
    )(w_in)


INPROJ_TM = 512
_INPROJ_CHUNKS = (
    (0, 512, 0, 0), (512, 1024, 0, 512), (1024, 1536, 0, 1024),
    (1536, 2048, 1, 0), (2048, 2432, 1, 512),
    (2432, 2944, 2, 0), (2944, 3456, 2, 512), (3456, 3712, 2, 1024),
)


def _inproj_kernel(x_ref, mod_ref, w_ref, cs_ref, oa_ref, ob_ref, oc_ref):
    outs = (oa_ref, ob_ref, oc_ref)
    sh = mod_ref[0:1, :]
    sc = mod_ref[1:2, :]
    u = (x_ref[...] * (1.0 + sc) + sh).astype(BF16)
    for c0, c1, oi, off in _INPROJ_CHUNKS:
        acc = jnp.dot(u, w_ref[:, c0:c1], preferred_element_type=F32) * cs_ref[:, c0:c1]
        outs[oi][:, off:off + (c1 - c0)] = acc.astype(outs[oi].dtype)


def _inproj_call(l, x2, mod, w_p, colscale):
    tm = INPROJ_TM
    return pl.pallas_call(
        _inproj_kernel,
        out_shape=(jax.ShapeDtypeStruct((SEQ, IN_A), BF16),
                   jax.ShapeDtypeStruct((SEQ, PB_W), F32),
                   jax.ShapeDtypeStruct((SEQ, IN_C), BF16)),
        grid=(SEQ // tm,),
        in_specs=[
            pl.BlockSpec((tm, D_MODEL), lambda i: (i, 0)),
            pl.BlockSpec((None, 6, D_MODEL), lambda i: (l, 0, 0)),
            pl.BlockSpec((None, D_MODEL, P_W), lambda i: (l, 0, 0), pipeline_mode=pl.Buffered(1)),
            pl.BlockSpec((1, P_W), lambda i: (0, 0)),
        ],
        out_specs=(pl.BlockSpec((tm, IN_A), lambda i: (i, 0)),
                   pl.BlockSpec((tm, PB_W), lambda i: (i, 0)),
                   pl.BlockSpec((tm, IN_C), lambda i: (i, 0))),
        compiler_params=_cparams(("parallel",)),
        name="in_proj",
    )(x2, mod, w_p, colscale)


def _na_block_rule(btype, i, j):
    if btype == 0:
        r0 = max(i - NA_WIN_ROWS // 2, 0)
        valid = r0 <= j < r0 + NA_WIN_ROWS
        ro = j - i + (NA_WIN_ROWS - 1)
    elif btype == 1:
        valid = i <= j < i + NA_WIN_ROWS
        ro = j - i + (NA_WIN_ROWS - 1) - NA_WIN_ROWS // 2
    else:
        r = GRID_ROWS - NA_QROWS + i
        ks = GRID_ROWS - NA_KROWS
        r0 = min(r - NA_WIN_ROWS // 2, GRID_ROWS - NA_WIN_ROWS)
        valid = r0 <= ks + j < r0 + NA_WIN_ROWS
        ro = ks + j - r + (NA_WIN_ROWS - 1)
    return ro if valid else None


def _na_bias_kernel(rpb_ref, o_ref):
    h = pl.program_id(0)
    t = pl.program_id(1)
    n_ro = 2 * NA_WIN_ROWS - 1
    n_co = 2 * NA_WIN_COLS - 1
    cq = lax.broadcasted_iota(jnp.int32, (GRID_W, GRID_W), 0)
    ck = lax.broadcasted_iota(jnp.int32, (GRID_W, GRID_W), 1)
    c0 = jnp.clip(cq - NA_WIN_COLS // 2, 0, GRID_W - NA_WIN_COLS)
    coff = jnp.clip(ck - cq, -(NA_WIN_COLS - 1), NA_WIN_COLS - 1) + (NA_WIN_COLS - 1)
    neg = jnp.full((GRID_W, GRID_W), NEG_INF, F32)
    tblocks = []
    for ro in range(n_ro):
        tb = neg
        for j in range(n_co):
            tb = jnp.where(coff == j, rpb_ref[h * (n_ro * n_co) + ro * n_co + j], tb)
        inside = jnp.where(ck >= c0, jnp.where(ck < c0 + NA_WIN_COLS, 1, 0), 0)
        tblocks.append(jnp.where(inside == 1, tb, neg))
    for btype in range(3):
        @pl.when(t == btype)
        def _(btype=btype):
            for i in range(NA_QROWS):
                for jp in range(NA_KROWS // 2):
                    pair = []
                    for j in (2 * jp, 2 * jp + 1):
                        ro = _na_block_rule(btype, i, j)
                        pair.append(neg if ro is None else tblocks[ro])
                    o_ref[0, 0, i * GRID_W:(i + 1) * GRID_W, jp * 128:(jp + 1) * 128] = (
                        jnp.concatenate(pair, axis=1))


def _na_bias_call(rpb_flat):
    return pl.pallas_call(
        _na_bias_kernel,
        out_shape=jax.ShapeDtypeStruct((NA_HEADS, 3, NA_TQ, NA_TK), F32),
        grid=(NA_HEADS, 3),
        in_specs=[pl.BlockSpec(memory_space=pltpu.SMEM)],
        out_specs=pl.BlockSpec((1, 1, NA_TQ, NA_TK), lambda h, t: (h, t, 0, 0)),
        compiler_params=_cparams(("parallel", "parallel")),
        name="na_bias",
    )(rpb_flat)


def _na_kernel(q_ref, k_ref, v_ref, bias_ref, o_ref):
    b = pl.program_id(0)
    ks = jnp.clip(NA_QROWS * b - NA_WIN_ROWS // 2, 0, GRID_ROWS - NA_KROWS) * GRID_W
    ks = pl.multiple_of(ks, GRID_W)
    ones = jnp.ones((NA_TK, HEAD_DIM), BF16)
    for h in range(NA_HEADS):
        cols = slice(h * HEAD_DIM, (h + 1) * HEAD_DIM)
        s = _dot_nt(q_ref[:, cols], k_ref[pl.ds(ks, NA_TK), cols]) + bias_ref[h, 0]
        m = jnp.max(s, axis=1, keepdims=True)
        p = jnp.exp(s - m).astype(BF16)
        vext = jnp.concatenate([v_ref[pl.ds(ks, NA_TK), cols], ones], axis=1)
        acc = jnp.dot(p, vext, preferred_element_type=F32)
        o_ref[:, cols] = acc[:, 0:HEAD_DIM] * (1.0 / acc[:, HEAD_DIM:HEAD_DIM + 1])


def _na_call(pa, bias):
    def btype(b):
        return jnp.where(b == 0, 0, jnp.where(b == NA_NBLK - 1, 2, 1))
    return pl.pallas_call(
        _na_kernel,
        out_shape=jax.ShapeDtypeStruct((SEQ, D_A), F32),
        grid=(NA_NBLK,),
        in_specs=[
            pl.BlockSpec((NA_TQ, D_A), lambda b: (b, 0)),
            pl.BlockSpec((SEQ, D_A), lambda b: (0, 1), pipeline_mode=pl.Buffered(1)),
            pl.BlockSpec((SEQ, D_A), lambda b: (0, 2), pipeline_mode=pl.Buffered(1)),
            pl.BlockSpec((NA_HEADS, 1, NA_TQ, NA_TK), lambda b: (0, btype(b), 0, 0)),
        ],
        out_specs=pl.BlockSpec((NA_TQ, D_A), lambda b: (b, 0)),
        compiler_params=_cparams(("arbitrary",)),
        name="na_attn",
    )(pa, pa, pa, bias)


MLAP_TM = 512


def _mla_prep_kernel(pb_ref, gq_ref, gkv_ref, wq_ref, wkv_ref, cos_ref, sin_ref,
                     q_ref, k_ref, v_ref):
    tm = pb_ref.shape[0]
    cqn = _rms_norm(pb_ref[:, 0:MLA_Q_RANK], gq_ref[...]).astype(BF16)
    ckvn = _rms_norm(pb_ref[:, MLA_Q_RANK:MLA_Q_RANK + MLA_KV_RANK], gkv_ref[...]).astype(BF16)
    cos = cos_ref[...]
    sin = sin_ref[...]

    def rotary(t):
        return t * cos + pltpu.roll(t, MLA_ROPE, 1) * sin

    kpe = rotary(pb_ref[:, MLA_Q_RANK + MLA_KV_RANK:PB_W]).astype(BF16)
    ones = jnp.ones((tm, HEAD_DIM), BF16)
    scale = (MLA_NOPE + MLA_ROPE) ** -0.5 * LOG2_E
    for h in range(MLA_HEADS):
        qh = jnp.dot(cqn, wq_ref[:, h * MLA_QK:(h + 1) * MLA_QK], preferred_element_type=F32)
        q_ref[h, :, 0:HEAD_DIM] = (qh[:, 0:HEAD_DIM] * scale).astype(BF16)
        q_ref[h, :, HEAD_DIM:MLA_QK] = (rotary(qh[:, HEAD_DIM:MLA_QK]) * scale).astype(BF16)
        kvh = jnp.dot(ckvn, wkv_ref[:, h * MLA_QK:(h + 1) * MLA_QK], preferred_element_type=F32)
        k_ref[h, :, 0:HEAD_DIM] = kvh[:, 0:HEAD_DIM].astype(BF16)
        k_ref[h, :, HEAD_DIM:MLA_QK] = kpe
        v_ref[h, :, 0:HEAD_DIM] = kvh[:, HEAD_DIM:MLA_QK].astype(BF16)
        v_ref[h, :, HEAD_DIM:MLA_QK] = ones


def _mla_prep_call(l, pb, gq, gkv, wq, wkv, cos_t, sin_t):
    tm = MLAP_TM
    hsd = jax.ShapeDtypeStruct((MLA_HEADS, SEQ, MLA_QK), BF16)
    hspec = pl.BlockSpec((MLA_HEADS, tm, MLA_QK), lambda i: (0, i, 0))
    return pl.pallas_call(
        _mla_prep_kernel,
        out_shape=(hsd, hsd, hsd),
        grid=(SEQ // tm,),
        in_specs=[
            pl.BlockSpec((tm, PB_W), lambda i: (i, 0)),
            pl.BlockSpec((None, 1, MLA_Q_RANK), lambda i: (l, 0, 0)),
            pl.BlockSpec((None, 1, MLA_KV_RANK), lambda i: (l, 0, 0)),
            pl.BlockSpec((None, MLA_Q_RANK, MLA_HEADS * MLA_QK), lambda i: (l, 0, 0)),
            pl.BlockSpec((None, MLA_KV_RANK, MLA_HEADS * MLA_QK), lambda i: (l, 0, 0)),
            pl.BlockSpec((tm, HEAD_DIM), lambda i: (i, 0)),
            pl.BlockSpec((tm, HEAD_DIM), lambda i: (i, 0)),
        ],
        out_specs=(hspec, hspec, hspec),
        compiler_params=_cparams(("parallel",)),
        name="mla_prep",
    )(pb, gq, gkv, wq, wkv, cos_t, sin_t)


MLA_TQ = 256
MLA_TK = 512


def _mla_attn_kernel(q_ref, k_ref, v_ref, o_ref, s0_sc, s1_sc, m0_sc, m1_sc):
    i = pl.program_id(0)

    @pl.when(i == 0)
    def _():
        s1_sc[...] = jnp.zeros_like(s1_sc)
        m1_sc[...] = jnp.zeros_like(m1_sc)

    def step(sa_sc, ma_sc, sb_sc, mb_sc):
        q = q_ref[0]
        tq = q.shape[0]
        m_prev = mb_sc[...]
        mx = jnp.full((tq, 128), NEG_INF, F32)
        acc = jnp.zeros((tq, MLA_QK), F32)
        for c in range(SEQ // MLA_TK):
            cols = slice(c * MLA_TK, (c + 1) * MLA_TK)
            p = jnp.exp2(sb_sc[:, cols] - m_prev).astype(BF16)
            acc = acc + jnp.dot(p, v_ref[0, cols, :], preferred_element_type=F32)
            s = _dot_nt(q, k_ref[0, cols, :])
            sa_sc[:, cols] = s
            for j in range(MLA_TK // 128):
                mx = jnp.maximum(mx, s[:, j * 128:(j + 1) * 128])
        ma_sc[...] = jnp.max(mx, axis=1, keepdims=True)
        o_ref[...] = acc[:, 0:MLA_V] * (1.0 / acc[:, MLA_V:MLA_V + 1])

    parity = lax.rem(i, 2)

    @pl.when(parity == 0)
    def _():
        step(s0_sc, m0_sc, s1_sc, m1_sc)

    @pl.when(parity == 1)
    def _():
        step(s1_sc, m1_sc, s0_sc, m0_sc)


def _mla_attn_call(q, k, v):
    tq = MLA_TQ
    nq = SEQ // tq
    ntile = MLA_HEADS * nq
    tile_a = lambda i: jnp.minimum(i, ntile - 1)
    tile_b = lambda i: jnp.maximum(i - 1, 0)
    return pl.pallas_call(
        _mla_attn_kernel,
        out_shape=jax.ShapeDtypeStruct((SEQ, D_B), F32),
        grid=(ntile + 1,),
        in_specs=[
            pl.BlockSpec((1, tq, MLA_QK), lambda i: (tile_a(i) // nq, tile_a(i) % nq, 0)),
            pl.BlockSpec((1, SEQ, MLA_QK), lambda i: (tile_a(i) // nq, 0, 0), pipeline_mode=pl.Buffered(1)),
            pl.BlockSpec((1, SEQ, MLA_QK), lambda i: (tile_b(i) // nq, 0, 0), pipeline_mode=pl.Buffered(1)),
        ],
        out_specs=pl.BlockSpec((tq, MLA_V), lambda i: (tile_b(i) % nq, tile_b(i) // nq)),
        scratch_shapes=[pltpu.VMEM((tq, SEQ), F32), pltpu.VMEM((tq, SEQ), F32),
                        pltpu.VMEM((tq, 1), F32), pltpu.VMEM((tq, 1), F32)],
        compiler_params=_cparams(("arbitrary",)),
        name="mla_attn",
    )(q, k, v)


_SWA_SLOPES = tuple(2.0 ** (-8.0 * (i + 1) / SWA_HEADS) for i in range(SWA_HEADS))


SWA_NB = 4


def _swa_kernel(sink_ref, q_ref, kp_ref, kc_ref, kn_ref, vp_ref, vc_ref, vn_ref, o_ref):
    g_kv = pl.program_id(0)
    j = pl.program_id(1)
    t = SWA_BLOCK
    rows = SWA_GROUP * t
    kwin = jnp.concatenate([kp_ref[...], kc_ref[...], kn_ref[...]], axis=0)
    vwin = jnp.concatenate([vp_ref[...], vc_ref[...], vn_ref[...]], axis=0)
    ones = jnp.ones((3 * t, HEAD_DIM), BF16)
    ri = lax.broadcasted_iota(jnp.int32, (rows, 3 * t), 0)
    ci = lax.broadcasted_iota(jnp.int32, (rows, 3 * t), 1)
    grp = jnp.right_shift(ri, 7)
    dist = jnp.abs(jnp.bitwise_and(ri, t - 1) - (ci - t))
    slope_lo = jnp.where(grp == 0, _SWA_SLOPES[0], jnp.where(grp == 1, _SWA_SLOPES[1], _SWA_SLOPES[2]))
    slope_hi = jnp.where(grp == 0, _SWA_SLOPES[3], jnp.where(grp == 1, _SWA_SLOPES[4], _SWA_SLOPES[5]))
    slope = jnp.where(g_kv == 0, slope_lo, slope_hi)
    band = jnp.where(dist <= SWA_WINDOW, -slope * dist.astype(F32), NEG_INF)
    rcol = jnp.right_shift(lax.broadcasted_iota(jnp.int32, (rows, 1), 0), 7)
    base = g_kv * SWA_GROUP
    sink = jnp.where(rcol == 0, sink_ref[base], jnp.where(rcol == 1, sink_ref[base + 1], sink_ref[base + 2]))
    first_cols = jnp.where(j == 0, t, 0)
    last_cols = jnp.where(j == pl.num_programs(1) - 1, 2 * t, 3 * t)
    for b in range(SWA_NB):
        q = jnp.concatenate([q_ref[b * t:(b + 1) * t, g * t:(g + 1) * t] for g in range(SWA_GROUP)], axis=0)
        s = _dot_nt(q, kwin[b * t:(b + 3) * t, :]) + band
        if b == 0:
            s = jnp.where(ci < first_cols, NEG_INF, s)
        if b == SWA_NB - 1:
            s = jnp.where(ci >= last_cols, NEG_INF, s)
        m = jnp.maximum(jnp.max(s, axis=1, keepdims=True), sink)
        p = jnp.exp(s - m).astype(BF16)
        vext = jnp.concatenate([vwin[b * t:(b + 3) * t, :], ones], axis=1)
        acc = jnp.dot(p, vext, preferred_element_type=F32)
        l = acc[:, HEAD_DIM:HEAD_DIM + 1] + jnp.exp(sink - m)
        o = acc[:, 0:HEAD_DIM] * (1.0 / l)
        for g in range(SWA_GROUP):
            o_ref[b * t:(b + 1) * t, g * t:(g + 1) * t] = o[g * t:(g + 1) * t, :]


def _swa_call(sink, pc):
    t = SWA_BLOCK
    nb = SEQ // t
    tq = SWA_NB * t
    kcol = SWA_HEADS
    vcol = SWA_HEADS + SWA_KV_HEADS
    prev = lambda j: jnp.maximum(SWA_NB * j - 1, 0)
    nxt = lambda j: jnp.minimum(SWA_NB * (j + 1), nb - 1)
    return pl.pallas_call(
        _swa_kernel,
        out_shape=jax.ShapeDtypeStruct((SEQ, D_C), F32),
        grid=(SWA_KV_HEADS, SEQ // tq),
        in_specs=[
            pl.BlockSpec(memory_space=pltpu.SMEM),
            pl.BlockSpec((tq, SWA_GROUP * t), lambda g, j: (j, g)),
            pl.BlockSpec((t, t), lambda g, j: (prev(j), kcol + g)),
            pl.BlockSpec((tq, t), lambda g, j: (j, kcol + g)),
            pl.BlockSpec((t, t), lambda g, j: (nxt(j), kcol + g)),
            pl.BlockSpec((t, t), lambda g, j: (prev(j), vcol + g)),
            pl.BlockSpec((tq, t), lambda g, j: (j, vcol + g)),
            pl.BlockSpec((t, t), lambda g, j: (nxt(j), vcol + g)),
        ],
        out_specs=pl.BlockSpec((tq, SWA_GROUP * t), lambda g, j: (j, g)),
        compiler_params=_cparams(("parallel", "arbitrary")),
        name="swa_attn",
    )(sink, pc, pc, pc, pc, pc, pc, pc)


OUTPROJ_TM = 256


def _outproj_kernel(ya_ref, yb_ref, yc_ref, x_ref, mod_ref, gn_ref, w_ref, lg_ref, lb_ref, o_ref):
    acc = None
    for y_ref, c0 in ((ya_ref, 0), (yb_ref, D_A), (yc_ref, D_A + D_B)):
        c1 = c0 + y_ref.shape[1]
        yn = _rms_norm(y_ref[...], gn_ref[:, c0:c1]).astype(BF16)
        part = jnp.dot(yn, w_ref[c0:c1, :], preferred_element_type=F32)
        acc = part if acc is None else acc + part
    gate = 1.0 + mod_ref[2:3, :]
    z = DEEPNORM_ALPHA * x_ref[...] + gate * acc
    o_ref[...] = _layer_norm(z, lg_ref[...], lb_ref[...])


def _outproj_call(l, ya, yb, yc, x2, mod, gn, w_o, lg, lb):
    tm = OUTPROJ_TM
    row = lambda w: pl.BlockSpec((tm, w), lambda i: (i, 0))
    full = lambda r, w: pl.BlockSpec((None, r, w), lambda i: (l, 0, 0))
    return pl.pallas_call(
        _outproj_kernel,
        out_shape=jax.ShapeDtypeStruct((SEQ, D_MODEL), F32),
        grid=(SEQ // tm,),
        in_specs=[row(D_A), row(D_B), row(D_C), row(D_MODEL), full(6, D_MODEL), full(1, D_MODEL),
                  pl.BlockSpec((None, D_MODEL, D_MODEL), lambda i: (l, 0, 0), pipeline_mode=pl.Buffered(1)),
                  full(1, D_MODEL), full(1, D_MODEL)],
        out_specs=row(D_MODEL),
        compiler_params=_cparams(("parallel",)),
        name="out_proj_ln",
    )(ya, yb, yc, x2, mod, gn, w_o, lg, lb)


FFN_TM = 1024
FFN_TF = 512


def _ffn_kernel(x_ref, mod_ref, wg_ref, wu_ref, wd_ref, lg_ref, lb_ref, o_ref, u_sc):
    f = pl.program_id(1)

    @pl.when(f == 0)
    def _():
        u_sc[...] = (x_ref[...] * (1.0 + mod_ref[4:5, :]) + mod_ref[3:4, :]).astype(BF16)
        o_ref[...] = jnp.zeros_like(o_ref)

    u = u_sc[...]
    g = jnp.dot(u, wg_ref[...], preferred_element_type=F32)
    up = jnp.dot(u, wu_ref[...], preferred_element_type=F32)
    hdn = (g * (1.0 / (1.0 + jnp.exp(-g))) * up).astype(BF16)
    o_ref[...] += jnp.dot(hdn, wd_ref[...], preferred_element_type=F32)

    @pl.when(f == pl.num_programs(1) - 1)
    def _():
        z = DEEPNORM_ALPHA * x_ref[...] + (1.0 + mod_ref[5:6, :]) * o_ref[...]
        o_ref[...] = _layer_norm(z, lg_ref[...], lb_ref[...])


def _ffn_call(l, x2, mod, w_gu, w_down, lg, lb):
    tm, tf = FFN_TM, FFN_TF
    nf = D_FF // tf
    return pl.pallas_call(
        _ffn_kernel,
        out_shape=jax.ShapeDtypeStruct((SEQ, D_MODEL), F32),
        grid=(SEQ // tm, nf),
        in_specs=[
            pl.BlockSpec((tm, D_MODEL), lambda i, f: (i, 0), pipeline_mode=pl.Buffered(1)),
            pl.BlockSpec((None, 6, D_MODEL), lambda i, f: (l, 0, 0)),
            pl.BlockSpec((None, D_MODEL, tf), lambda i, f: (l, 0, f)),
            pl.BlockSpec((None, D_MODEL, tf), lambda i, f: (l, 0, nf + f)),
            pl.BlockSpec((None, tf, D_MODEL), lambda i, f: (l, f, 0)),
            pl.BlockSpec((None, 1, D_MODEL), lambda i, f: (l, 0, 0)),
            pl.BlockSpec((None, 1, D_MODEL), lambda i, f: (l, 0, 0)),
        ],
        out_specs=pl.BlockSpec((tm, D_MODEL), lambda i, f: (i, 0)),
        scratch_shapes=[pltpu.VMEM((tm, D_MODEL), BF16)],
        compiler_params=_cparams(("parallel", "arbitrary")),
        name="ffn_ln",
    )(x2, mod, w_gu, w_gu, w_down, lg, lb)


def _rot_half_cols(w):
    half = w.shape[-1] // 2
    return jnp.concatenate([-w[..., half:], w[..., :half]], axis=-1)


def _rope_tables():
    half = MLA_ROPE // 2
    inv = ROPE_THETA ** (-jnp.arange(half, dtype=F32) / half)
    ang = jnp.arange(SEQ, dtype=F32)[:, None] * inv[None, :]
    zeros = jnp.zeros((SEQ, MLA_ROPE), F32)
    cos = jnp.cos(ang)
    sin = jnp.sin(ang)
    return (jnp.concatenate([cos, cos, zeros], axis=1), jnp.concatenate([sin, sin, zeros], axis=1))


def _colscale():
    cs = np.ones((1, P_W), np.float32)
    cs[0, 0:D_A] = HEAD_DIM ** -0.5
    c0 = IN_A + PB_W
    cs[0, c0:c0 + D_C] = HEAD_DIM ** -0.5
    return jnp.asarray(cs)


def kernel(x, c, w_ada, b_ada, w_in, na_rpb, mla_q_norm, mla_kv_norm, mla_w_uq, mla_w_ukv,
           swa_sink, out_norm_g, w_o, ln1_g, ln1_b, w_gu, w_down, ln2_g, ln2_b):
    assert x.shape == (1, SEQ, D_MODEL)
    x2 = x.reshape(SEQ, D_MODEL)
    mod = _ada_call(c.reshape(D_MODEL, 1), w_ada, b_ada.reshape(DEPTH, 1, -1))
    mod = mod.reshape(DEPTH, 6, D_MODEL)
    cos_t, sin_t = _rope_tables()
    colscale = _colscale()
    w_p = _winprep_call(w_in)
    wq = mla_w_uq.reshape(DEPTH, MLA_Q_RANK, MLA_HEADS, MLA_NOPE + MLA_ROPE)
    wq = jnp.concatenate([wq, _rot_half_cols(wq[..., MLA_NOPE:])], axis=-1)
    wq = wq.reshape(DEPTH, MLA_Q_RANK, MLA_HEADS * MLA_QK).astype(BF16)
    wkv = mla_w_ukv.astype(BF16)
    w_o_b = w_o.astype(BF16)
    w_gu_b = w_gu.astype(BF16)
    w_down_b = w_down.astype(BF16)
    row3 = lambda a: a.reshape(DEPTH, 1, -1)
    gq, gkv, gn = row3(mla_q_norm), row3(mla_kv_norm), row3(out_norm_g)
    l1g, l1b, l2g, l2b = row3(ln1_g), row3(ln1_b), row3(ln2_g), row3(ln2_b)
    for l in range(DEPTH):
        pa, pb, pc = _inproj_call(l, x2, mod, w_p, colscale)
        bias = _na_bias_call(na_rpb[l].reshape(-1))
        ya = _na_call(pa, bias)
        q, k, v = _mla_prep_call(l, pb, gq, gkv, wq, wkv, cos_t, sin_t)
        yb = _mla_attn_call(q, k, v)
        yc = _swa_call(swa_sink[l], pc)
        x2 = _outproj_call(l, ya, yb, yc, x2, mod, gn, w_o_b, l1g, l1b)
        x2 = _ffn_call(l, x2, mod, w_gu_b, w_down_b, l2g, l2b)
    return x2.reshape(1, SEQ, D_MODEL)
```

```python
import functools

import numpy as np
import jax
import jax.numpy as jnp
from jax import lax
from jax.experimental import pallas as pl
from jax.experimental.pallas import tpu as pltpu

F32 = jnp.float32
BF16 = jnp.bfloat16

D_MODEL = 2048
SEQ = 8192
DEPTH = 2
GRID_W = 64
GRID_ROWS = SEQ // GRID_W
HEAD_DIM = 128
NA_HEADS = 4
NA_WIN_ROWS = 8
NA_WIN_COLS = 16
MLA_HEADS = 6
MLA_Q_RANK = 512
MLA_KV_RANK = 256
MLA_NOPE = 128
MLA_ROPE = 64
MLA_V = 128
ROPE_THETA = 10000.0
SWA_HEADS = 6
SWA_KV_HEADS = 2
SWA_GROUP = SWA_HEADS // SWA_KV_HEADS
SWA_WINDOW = 128
SWA_BLOCK = 128
D_A = NA_HEADS * HEAD_DIM
D_B = MLA_HEADS * MLA_V
D_C = SWA_HEADS * HEAD_DIM
IN_A = 3 * D_A
IN_B = MLA_Q_RANK + MLA_KV_RANK + MLA_ROPE
IN_C = (SWA_HEADS + 2 * SWA_KV_HEADS) * HEAD_DIM
D_FF = 5632
DEEPNORM_ALPHA = (2 * DEPTH) ** 0.25
LN_EPS = 1e-5
RMS_EPS = 1e-6
NEG_INF = -1e30
LOG2_E = 1.4426950408889634

PB_W = MLA_Q_RANK + MLA_KV_RANK + 2 * MLA_ROPE
P_W = IN_A + PB_W + IN_C
MLA_QK = 2 * HEAD_DIM

NA_QROWS = 4
NA_KROWS = 12
NA_TQ = NA_QROWS * GRID_W
NA_TK = NA_KROWS * GRID_W
NA_NBLK = GRID_ROWS // NA_QROWS

VMEM_LIMIT = 56 * 1024 * 1024


def _cparams(sem):
    return pltpu.CompilerParams(dimension_semantics=sem, vmem_limit_bytes=VMEM_LIMIT)


def _layer_norm(z, g, b):
    mu = jnp.mean(z, axis=-1, keepdims=True)
    zc = z - mu
    var = jnp.mean(zc * zc, axis=-1, keepdims=True)
    return zc * lax.rsqrt(var + LN_EPS) * g + b


def _rms_norm(x, g):
    ms = jnp.mean(x * x, axis=-1, keepdims=True)
    return x * lax.rsqrt(ms + RMS_EPS) * g


def _dot_nt(a, b):
    return lax.dot_general(a, b, (((1,), (1,)), ((), ())), preferred_element_type=F32)


ADA_TN = 1024
ADA_RC = 256


def _ada_kernel(c_ref, w_ref, b_ref, o_ref):
    tn = o_ref.shape[-1]
    acc = jnp.zeros((8, tn), F32)
    for r in range(0, D_MODEL, ADA_RC):
        c = c_ref[r:r + ADA_RC, :]
        cond = c * (1.0 / (1.0 + jnp.exp(-c)))
        prod = w_ref[0, r:r + ADA_RC, :] * cond
        acc = acc + jnp.sum(prod.reshape(ADA_RC // 8, 8, tn), axis=0)
    o_ref[0] = jnp.sum(acc, axis=0, keepdims=True) + b_ref[0]


def _ada_call(c_col, w_ada, b_ada3):
    n = w_ada.shape[-1]
    return pl.pallas_call(
        _ada_kernel,
        out_shape=jax.ShapeDtypeStruct((DEPTH, 1, n), F32),
        grid=(DEPTH, n // ADA_TN),
        in_specs=[
            pl.BlockSpec((D_MODEL, 1), lambda l, j: (0, 0)),
            pl.BlockSpec((1, D_MODEL, ADA_TN), lambda l, j: (l, 0, j)),
            pl.BlockSpec((1, 1, ADA_TN), lambda l, j: (l, 0, j)),
        ],
        out_specs=pl.BlockSpec((1, 1, ADA_TN), lambda l, j: (l, 0, j)),
        compiler_params=_cparams(("parallel", "parallel")),
        name="ada_mod",
    )(c_col, w_ada, b_ada3)


WPREP_TN = 512
KR0 = IN_A + MLA_Q_RANK + MLA_KV_RANK


def _winprep_kernel(w_ref, o_ref):
    half = MLA_ROPE // 2
    kr1 = KR0 + MLA_ROPE
    o_ref[0:kr1, :] = w_ref[0:kr1, :].astype(BF16)
    o_ref[kr1:kr1 + half, :] = (-w_ref[KR0 + half:kr1, :]).astype(BF16)
    o_ref[kr1 + half:kr1 + MLA_ROPE, :] = w_ref[KR0:KR0 + half, :].astype(BF16)
    o_ref[kr1 + MLA_ROPE:, :] = w_ref[kr1:, :].astype(BF16)


def _winprep_call(w_in_t):
    tn = WPREP_TN
    return pl.pallas_call(
        _winprep_kernel,
        out_shape=jax.ShapeDtypeStruct((DEPTH, P_W, D_MODEL), BF16),
        grid=(DEPTH, D_MODEL // tn),
        in_specs=[pl.BlockSpec((None, w_in_t.shape[1], tn), lambda l, i: (l, 0, i))],
        out_specs=pl.BlockSpec((None, P_W, tn), lambda l, i: (l, 0, i)),
        compiler_params=_cparams(("parallel", "parallel")),
        name="w_in_prep",
    )(w_in_t)


INPROJ_TM = 512
_INPROJ_CHUNKS = (
    (0, 512, 0, 0), (512, 1024, 0, 512), (1024, 1536, 0, 1024),
    (1536, 2048, 1, 0), (2048, 2432, 1, 512),
    (2432, 2944, 2, 0), (2944, 3456, 2, 512), (3456, 3712, 2, 1024),
)


def _inproj_kernel(x_ref, mod_ref, w_ref, cs_ref, oa_ref, ob_ref, oc_ref):
    outs = (oa_ref, ob_ref, oc_ref)
    sh = mod_ref[0:1, :]
    sc = mod_ref[1:2, :]
    u = (x_ref[...] * (1.0 + sc) + sh).astype(BF16)
    for c0, c1, oi, off in _INPROJ_CHUNKS:
        acc = _dot_nt(u, w_ref[c0:c1, :]) * cs_ref[:, c0:c1]
        outs[oi][:, off:off + (c1 - c0)] = acc.astype(outs[oi].dtype)


def _inproj_call(l, x2, mod, w_p, colscale):
    tm = INPROJ_TM
    return pl.pallas_call(
        _inproj_kernel,
        out_shape=(jax.ShapeDtypeStruct((SEQ, IN_A), BF16),
                   jax.ShapeDtypeStruct((SEQ, PB_W), F32),
                   jax.ShapeDtypeStruct((SEQ, IN_C), BF16)),
        grid=(SEQ // tm,),
        in_specs=[
            pl.BlockSpec((tm, D_MODEL), lambda i: (i, 0)),
            pl.BlockSpec((None, 6, D_MODEL), lambda i: (l, 0, 0)),
            pl.BlockSpec((None, P_W, D_MODEL), lambda i: (l, 0, 0), pipeline_mode=pl.Buffered(1)),
            pl.BlockSpec((1, P_W), lambda i: (0, 0)),
        ],
        out_specs=(pl.BlockSpec((tm, IN_A), lambda i: (i, 0)),
                   pl.BlockSpec((tm, PB_W), lambda i: (i, 0)),
                   pl.BlockSpec((tm, IN_C), lambda i: (i, 0))),
        compiler_params=_cparams(("parallel",)),
        name="in_proj",
    )(x2, mod, w_p, colscale)


def _na_block_rule(btype, i, j):
    if btype == 0:
        r0 = max(i - NA_WIN_ROWS // 2, 0)
        valid = r0 <= j < r0 + NA_WIN_ROWS
        ro = j - i + (NA_WIN_ROWS - 1)
    elif btype == 1:
        valid = i <= j < i + NA_WIN_ROWS
        ro = j - i + (NA_WIN_ROWS - 1) - NA_WIN_ROWS // 2
    else:
        r = GRID_ROWS - NA_QROWS + i
        ks = GRID_ROWS - NA_KROWS
        r0 = min(r - NA_WIN_ROWS // 2, GRID_ROWS - NA_WIN_ROWS)
        valid = r0 <= ks + j < r0 + NA_WIN_ROWS
        ro = ks + j - r + (NA_WIN_ROWS - 1)
    return ro if valid else None


def _na_bias_kernel(rpb_ref, o_ref):
    h = pl.program_id(0)
    t = pl.program_id(1)
    n_ro = 2 * NA_WIN_ROWS - 1
    n_co = 2 * NA_WIN_COLS - 1
    cq = lax.broadcasted_iota(jnp.int32, (GRID_W, GRID_W), 0)
    ck = lax.broadcasted_iota(jnp.int32, (GRID_W, GRID_W), 1)
    c0 = jnp.clip(cq - NA_WIN_COLS // 2, 0, GRID_W - NA_WIN_COLS)
    coff = jnp.clip(ck - cq, -(NA_WIN_COLS - 1), NA_WIN_COLS - 1) + (NA_WIN_COLS - 1)
    neg = jnp.full((GRID_W, GRID_W), NEG_INF, F32)
    tblocks = []
    for ro in range(n_ro):
        tb = neg
        for j in range(n_co):
            tb = jnp.where(coff == j, rpb_ref[h * (n_ro * n_co) + ro * n_co + j], tb)
        inside = jnp.where(ck >= c0, jnp.where(ck < c0 + NA_WIN_COLS, 1, 0), 0)
        tblocks.append(jnp.where(inside == 1, tb, neg))
    for btype in range(3):
        @pl.when(t == btype)
        def _(btype=btype):
            for i in range(NA_QROWS):
                for jp in range(NA_KROWS // 2):
                    pair = []
                    for j in (2 * jp, 2 * jp + 1):
                        ro = _na_block_rule(btype, i, j)
                        pair.append(neg if ro is None else tblocks[ro])
                    o_ref[0, 0, i * GRID_W:(i + 1) * GRID_W, jp * 128:(jp + 1) * 128] = (
                        jnp.concatenate(pair, axis=1))


def _na_bias_call(rpb_flat):
    return pl.pallas_call(
        _na_bias_kernel,
        out_shape=jax.ShapeDtypeStruct((NA_HEADS, 3, NA_TQ, NA_TK), F32),
        grid=(NA_HEADS, 3),
        in_specs=[pl.BlockSpec(memory_space=pltpu.SMEM)],
        out_specs=pl.BlockSpec((1, 1, NA_TQ, NA_TK), lambda h, t: (h, t, 0, 0)),
        compiler_params=_cparams(("parallel", "parallel")),
        name="na_bias",
    )(rpb_flat)


def _na_kernel(q_ref, k_ref, v_ref, bias_ref, o_ref):
    b = pl.program_id(0)
    ks = jnp.clip(NA_QROWS * b - NA_WIN_ROWS // 2, 0, GRID_ROWS - NA_KROWS) * GRID_W
    ks = pl.multiple_of(ks, GRID_W)
    ones = jnp.ones((NA_TK, HEAD_DIM), BF16)
    for h in range(NA_HEADS):
        cols = slice(h * HEAD_DIM, (h + 1) * HEAD_DIM)
        s = _dot_nt(q_ref[:, cols], k_ref[pl.ds(ks, NA_TK), cols]) + bias_ref[h, 0]
        m = jnp.max(s, axis=1, keepdims=True)
        p = jnp.exp(s - m).astype(BF16)
        vext = jnp.concatenate([v_ref[pl.ds(ks, NA_TK), cols], ones], axis=1)
        acc = jnp.dot(p, vext, preferred_element_type=F32)
        o_ref[:, cols] = acc[:, 0:HEAD_DIM] * (1.0 / acc[:, HEAD_DIM:HEAD_DIM + 1])


def _na_call(pa, bias):
    def btype(b):
        return jnp.where(b == 0, 0, jnp.where(b == NA_NBLK - 1, 2, 1))
    return pl.pallas_call(
        _na_kernel,
        out_shape=jax.ShapeDtypeStruct((SEQ, D_A), F32),
        grid=(NA_NBLK,),
        in_specs=[
            pl.BlockSpec((NA_TQ, D_A), lambda b: (b, 0)),
            pl.BlockSpec((SEQ, D_A), lambda b: (0, 1), pipeline_mode=pl.Buffered(1)),
            pl.BlockSpec((SEQ, D_A), lambda b: (0, 2), pipeline_mode=pl.Buffered(1)),
            pl.BlockSpec((NA_HEADS, 1, NA_TQ, NA_TK), lambda b: (0, btype(b), 0, 0)),
        ],
        out_specs=pl.BlockSpec((NA_TQ, D_A), lambda b: (b, 0)),
        compiler_params=_cparams(("arbitrary",)),
        name="na_attn",
    )(pa, pa, pa, bias)


MLAP_TM = 512


def _mla_prep_kernel(pb_ref, gq_ref, gkv_ref, wq_ref, wkv_ref, cos_ref, sin_ref,
                     q_ref, k_ref, v_ref):
    tm = pb_ref.shape[0]
    cqn = _rms_norm(pb_ref[:, 0:MLA_Q_RANK], gq_ref[...]).astype(BF16)
    ckvn = _rms_norm(pb_ref[:, MLA_Q_RANK:MLA_Q_RANK + MLA_KV_RANK], gkv_ref[...]).astype(BF16)
    cos = cos_ref[...]
    sin = sin_ref[...]

    def rotary(t):
        return t * cos + pltpu.roll(t, MLA_ROPE, 1) * sin

    kpe = rotary(pb_ref[:, MLA_Q_RANK + MLA_KV_RANK:PB_W]).astype(BF16)
    ones = jnp.ones((tm, HEAD_DIM), BF16)
    scale = (MLA_NOPE + MLA_ROPE) ** -0.5 * LOG2_E
    for h in range(MLA_HEADS):
        qh = jnp.dot(cqn, wq_ref[:, h * MLA_QK:(h + 1) * MLA_QK], preferred_element_type=F32)
        q_ref[h, :, 0:HEAD_DIM] = (qh[:, 0:HEAD_DIM] * scale).astype(BF16)
        q_ref[h, :, HEAD_DIM:MLA_QK] = (rotary(qh[:, HEAD_DIM:MLA_QK]) * scale).astype(BF16)
        kvh = jnp.dot(ckvn, wkv_ref[:, h * MLA_QK:(h + 1) * MLA_QK], preferred_element_type=F32)
        k_ref[h, :, 0:HEAD_DIM] = kvh[:, 0:HEAD_DIM].astype(BF16)
        k_ref[h, :, HEAD_DIM:MLA_QK] = kpe
        v_ref[h, :, 0:HEAD_DIM] = kvh[:, HEAD_DIM:MLA_QK].astype(BF16)
        v_ref[h, :, HEAD_DIM:MLA_QK] = ones


def _mla_prep_call(l, pb, gq, gkv, wq, wkv, cos_t, sin_t):
    tm = MLAP_TM
    hsd = jax.ShapeDtypeStruct((MLA_HEADS, SEQ, MLA_QK), BF16)
    hspec = pl.BlockSpec((MLA_HEADS, tm, MLA_QK), lambda i: (0, i, 0))
    return pl.pallas_call(
        _mla_prep_kernel,
        out_shape=(hsd, hsd, hsd),
        grid=(SEQ // tm,),
        in_specs=[
            pl.BlockSpec((tm, PB_W), lambda i: (i, 0)),
            pl.BlockSpec((None, 1, MLA_Q_RANK), lambda i: (l, 0, 0)),
            pl.BlockSpec((None, 1, MLA_KV_RANK), lambda i: (l, 0, 0)),
            pl.BlockSpec((None, MLA_Q_RANK, MLA_HEADS * MLA_QK), lambda i: (l, 0, 0)),
            pl.BlockSpec((None, MLA_KV_RANK, MLA_HEADS * MLA_QK), lambda i: (l, 0, 0)),
            pl.BlockSpec((tm, HEAD_DIM), lambda i: (i, 0)),
            pl.BlockSpec((tm, HEAD_DIM), lambda i: (i, 0)),
        ],
        out_specs=(hspec, hspec, hspec),
        compiler_params=_cparams(("parallel",)),
        name="mla_prep",
    )(pb, gq, gkv, wq, wkv, cos_t, sin_t)


MLA_TQ = 256
MLA_TK = 512


def _mla_attn_kernel(qe_ref, qo_ref, k_ref, vp_ref, vc_ref, oo_ref, oe_ref, s0_sc, s1_sc, m0_sc, m1_sc):
    j = pl.program_id(0)

    @pl.when(j == 0)
    def _():
        s1_sc[...] = jnp.zeros_like(s1_sc)
        m1_sc[...] = jnp.zeros_like(m1_sc)

    def stage(q_ref, sa_sc, ma_sc, sb_sc, mb_sc, v_ref, o_ref):
        q = q_ref[0]
        tq = q.shape[0]
        m_prev = mb_sc[...]
        mx = jnp.full((tq, 128), NEG_INF, F32)
        acc = jnp.zeros((tq, MLA_QK), F32)
        for c in range(SEQ // MLA_TK):
            cols = slice(c * MLA_TK, (c + 1) * MLA_TK)
            p = jnp.exp2(sb_sc[:, cols] - m_prev).astype(BF16)
            acc = acc + jnp.dot(p, v_ref[0, cols, :], preferred_element_type=F32)
            s = _dot_nt(q, k_ref[0, cols, :])
            sa_sc[:, cols] = s
            for t in range(MLA_TK // 128):
                mx = jnp.maximum(mx, s[:, t * 128:(t + 1) * 128])
        ma_sc[...] = jnp.max(mx, axis=1, keepdims=True)
        o_ref[...] = acc[:, 0:MLA_V] * (1.0 / acc[:, MLA_V:MLA_V + 1])

    stage(qe_ref, s0_sc, m0_sc, s1_sc, m1_sc, vp_ref, oo_ref)
    stage(qo_ref, s1_sc, m1_sc, s0_sc, m0_sc, vc_ref, oe_ref)


def _mla_attn_call(q, k, v):
    tq = MLA_TQ
    npair = SEQ // (2 * tq)
    last = MLA_HEADS * npair - 1
    pair_a = lambda j: jnp.minimum(j, last)
    pair_b = lambda j: jnp.maximum(j - 1, 0)
    half = jax.ShapeDtypeStruct((SEQ // 2, D_B), F32)
    return pl.pallas_call(
        _mla_attn_kernel,
        out_shape=(half, half),
        grid=(last + 2,),
        in_specs=[
            pl.BlockSpec((1, tq, MLA_QK), lambda j: (pair_a(j) // npair, 2 * (pair_a(j) % npair), 0)),
            pl.BlockSpec((1, tq, MLA_QK), lambda j: (pair_a(j) // npair, 2 * (pair_a(j) % npair) + 1, 0)),
            pl.BlockSpec((1, SEQ, MLA_QK), lambda j: (pair_a(j) // npair, 0, 0), pipeline_mode=pl.Buffered(1)),
            pl.BlockSpec((1, SEQ, MLA_QK), lambda j: (pair_b(j) // npair, 0, 0), pipeline_mode=pl.Buffered(1)),
            pl.BlockSpec((1, SEQ, MLA_QK), lambda j: (pair_a(j) // npair, 0, 0), pipeline_mode=pl.Buffered(1)),
        ],
        out_specs=(pl.BlockSpec((tq, MLA_V), lambda j: (pair_b(j) % npair, pair_b(j) // npair)),
                   pl.BlockSpec((tq, MLA_V), lambda j: (pair_a(j) % npair, pair_a(j) // npair))),
        scratch_shapes=[pltpu.VMEM((tq, SEQ), F32), pltpu.VMEM((tq, SEQ), F32),
                        pltpu.VMEM((tq, 1), F32), pltpu.VMEM((tq, 1), F32)],
        compiler_params=_cparams(("arbitrary",)),
        name="mla_attn",
    )(q, q, k, v, v)


_SWA_SLOPES = tuple(2.0 ** (-8.0 * (i + 1) / SWA_HEADS) for i in range(SWA_HEADS))


SWA_NB = 4


def _swa_kernel(sink_ref, q_ref, kp_ref, kc_ref, kn_ref, vp_ref, vc_ref, vn_ref, o_ref):
    g_kv = pl.program_id(0)
    j = pl.program_id(1)
    t = SWA_BLOCK
    rows = SWA_GROUP * t
    kwin = jnp.concatenate([kp_ref[...], kc_ref[...], kn_ref[...]], axis=0)
    vwin = jnp.concatenate([vp_ref[...], vc_ref[...], vn_ref[...]], axis=0)
    ones = jnp.ones((3 * t, HEAD_DIM), BF16)
    ri = lax.broadcasted_iota(jnp.int32, (rows, 3 * t), 0)
    ci = lax.broadcasted_iota(jnp.int32, (rows, 3 * t), 1)
    grp = jnp.right_shift(ri, 7)
    dist = jnp.abs(jnp.bitwise_and(ri, t - 1) - (ci - t))
    slope_lo = jnp.where(grp == 0, _SWA_SLOPES[0], jnp.where(grp == 1, _SWA_SLOPES[1], _SWA_SLOPES[2]))
    slope_hi = jnp.where(grp == 0, _SWA_SLOPES[3], jnp.where(grp == 1, _SWA_SLOPES[4], _SWA_SLOPES[5]))
    slope = jnp.where(g_kv == 0, slope_lo, slope_hi)
    band = jnp.where(dist <= SWA_WINDOW, -slope * dist.astype(F32), NEG_INF)
    rcol = jnp.right_shift(lax.broadcasted_iota(jnp.int32, (rows, 1), 0), 7)
    base = g_kv * SWA_GROUP
    sink = jnp.where(rcol == 0, sink_ref[base], jnp.where(rcol == 1, sink_ref[base + 1], sink_ref[base + 2]))
    first_cols = jnp.where(j == 0, t, 0)
    last_cols = jnp.where(j == pl.num_programs(1) - 1, 2 * t, 3 * t)
    for b in range(SWA_NB):
        q = jnp.concatenate([q_ref[b * t:(b + 1) * t, g * t:(g + 1) * t] for g in range(SWA_GROUP)], axis=0)
        s = _dot_nt(q, kwin[b * t:(b + 3) * t, :]) + band
        if b == 0:
            s = jnp.where(ci < first_cols, NEG_INF, s)
        if b == SWA_NB - 1:
            s = jnp.where(ci >= last_cols, NEG_INF, s)
        m = jnp.maximum(jnp.max(s, axis=1, keepdims=True), sink)
        p = jnp.exp(s - m).astype(BF16)
        vext = jnp.concatenate([vwin[b * t:(b + 3) * t, :], ones], axis=1)
        acc = jnp.dot(p, vext, preferred_element_type=F32)
        l = acc[:, HEAD_DIM:HEAD_DIM + 1] + jnp.exp(sink - m)
        o = acc[:, 0:HEAD_DIM] * (1.0 / l)
        for g in range(SWA_GROUP):
            o_ref[b * t:(b + 1) * t, g * t:(g + 1) * t] = o[g * t:(g + 1) * t, :]


def _swa_call(sink, pc):
    t = SWA_BLOCK
    nb = SEQ // t
    tq = SWA_NB * t
    kcol = SWA_HEADS
    vcol = SWA_HEADS + SWA_KV_HEADS
    prev = lambda j: jnp.maximum(SWA_NB * j - 1, 0)
    nxt = lambda j: jnp.minimum(SWA_NB * (j + 1), nb - 1)
    return pl.pallas_call(
        _swa_kernel,
        out_shape=jax.ShapeDtypeStruct((SEQ, D_C), F32),
        grid=(SWA_KV_HEADS, SEQ // tq),
        in_specs=[
            pl.BlockSpec(memory_space=pltpu.SMEM),
            pl.BlockSpec((tq, SWA_GROUP * t), lambda g, j: (j, g)),
            pl.BlockSpec((t, t), lambda g, j: (prev(j), kcol + g)),
            pl.BlockSpec((tq, t), lambda g, j: (j, kcol + g)),
            pl.BlockSpec((t, t), lambda g, j: (nxt(j), kcol + g)),
            pl.BlockSpec((t, t), lambda g, j: (prev(j), vcol + g)),
            pl.BlockSpec((tq, t), lambda g, j: (j, vcol + g)),
            pl.BlockSpec((t, t), lambda g, j: (nxt(j), vcol + g)),
        ],
        out_specs=pl.BlockSpec((tq, SWA_GROUP * t), lambda g, j: (j, g)),
        compiler_params=_cparams(("parallel", "arbitrary")),
        name="swa_attn",
    )(sink, pc, pc, pc, pc, pc, pc, pc)


OUTPROJ_TM = 2 * MLA_TQ


def _outproj_kernel(ya_ref, ybe_ref, ybo_ref, yc_ref, x_ref, mod_ref, gn_ref, w_ref, lg_ref, lb_ref,
                    o_ref, u_ref):
    hm = OUTPROJ_TM // 2
    gate = 1.0 + mod_ref[2:3, :]
    for half, yb_ref in enumerate((ybe_ref, ybo_ref)):
        rows = slice(half * hm, (half + 1) * hm)
        acc = None
        for y, c0 in ((ya_ref[rows, :], 0), (yb_ref[...], D_A), (yc_ref[rows, :], D_A + D_B)):
            c1 = c0 + y.shape[1]
            yn = _rms_norm(y, gn_ref[:, c0:c1]).astype(BF16)
            part = jnp.dot(yn, w_ref[c0:c1, :], preferred_element_type=F32)
            acc = part if acc is None else acc + part
        x1 = _layer_norm(DEEPNORM_ALPHA * x_ref[rows, :] + gate * acc, lg_ref[...], lb_ref[...])
        o_ref[rows, :] = x1
        u_ref[rows, :] = (x1 * (1.0 + mod_ref[4:5, :]) + mod_ref[3:4, :]).astype(BF16)


def _outproj_call(l, ya, yb_even, yb_odd, yc, x2, mod, gn, w_o, lg, lb):
    tm = OUTPROJ_TM
    row = lambda r, w: pl.BlockSpec((r, w), lambda i: (i, 0))
    full = lambda r, w: pl.BlockSpec((None, r, w), lambda i: (l, 0, 0))
    return pl.pallas_call(
        _outproj_kernel,
        out_shape=(jax.ShapeDtypeStruct((SEQ, D_MODEL), F32), jax.ShapeDtypeStruct((SEQ, D_MODEL), BF16)),
        grid=(SEQ // tm,),
        in_specs=[row(tm, D_A), row(tm // 2, D_B), row(tm // 2, D_B), row(tm, D_C), row(tm, D_MODEL),
                  full(6, D_MODEL), full(1, D_MODEL),
                  pl.BlockSpec((None, D_MODEL, D_MODEL), lambda i: (l, 0, 0), pipeline_mode=pl.Buffered(1)),
                  full(1, D_MODEL), full(1, D_MODEL)],
        out_specs=(row(tm, D_MODEL), row(tm, D_MODEL)),
        compiler_params=_cparams(("parallel",)),
        name="out_proj_ln",
    )(ya, yb_even, yb_odd, yc, x2, mod, gn, w_o, lg, lb)


FFN_TM = 1024
FFN_SUB = 512
FFN_TF = 512


def _ffn_kernel(u_ref, x_hbm, mod_ref, wg_ref, wu_ref, wd_ref, lg_ref, lb_ref, o_ref, x_sc, x_sem):
    i = pl.program_id(0)
    f = pl.program_id(1)
    tm = o_ref.shape[0]
    x_copy = pltpu.make_async_copy(x_hbm.at[pl.ds(pl.multiple_of(i * tm, tm), tm), :], x_sc, x_sem)

    @pl.when(f == 0)
    def _():
        x_copy.start()
        o_ref[...] = jnp.zeros_like(o_ref)

    for r in range(0, tm, FFN_SUB):
        rows = slice(r, r + FFN_SUB)
        u = u_ref[rows, :]
        g = jnp.dot(u, wg_ref[...], preferred_element_type=F32)
        up = jnp.dot(u, wu_ref[...], preferred_element_type=F32)
        hdn = (g * (1.0 / (1.0 + jnp.exp(-g))) * up).astype(BF16)
        o_ref[rows, :] += jnp.dot(hdn, wd_ref[...], preferred_element_type=F32)

    @pl.when(f == pl.num_programs(1) - 1)
    def _():
        x_copy.wait()
        z = DEEPNORM_ALPHA * x_sc[...] + (1.0 + mod_ref[5:6, :]) * o_ref[...]
        o_ref[...] = _layer_norm(z, lg_ref[...], lb_ref[...])


def _ffn_call(l, u, x2, mod, w_gu, w_down, lg, lb):
    tm, tf = FFN_TM, FFN_TF
    nf = D_FF // tf
    return pl.pallas_call(
        _ffn_kernel,
        out_shape=jax.ShapeDtypeStruct((SEQ, D_MODEL), F32),
        grid=(SEQ // tm, nf),
        in_specs=[
            pl.BlockSpec((tm, D_MODEL), lambda i, f: (i, 0)),
            pl.BlockSpec(memory_space=pl.ANY),
            pl.BlockSpec((None, 6, D_MODEL), lambda i, f: (l, 0, 0)),
            pl.BlockSpec((None, D_MODEL, tf), lambda i, f: (l, 0, f)),
            pl.BlockSpec((None, D_MODEL, tf), lambda i, f: (l, 0, nf + f)),
            pl.BlockSpec((None, tf, D_MODEL), lambda i, f: (l, f, 0)),
            pl.BlockSpec((None, 1, D_MODEL), lambda i, f: (l, 0, 0)),
            pl.BlockSpec((None, 1, D_MODEL), lambda i, f: (l, 0, 0)),
        ],
        out_specs=pl.BlockSpec((tm, D_MODEL), lambda i, f: (i, 0)),
        scratch_shapes=[pltpu.VMEM((tm, D_MODEL), F32), pltpu.SemaphoreType.DMA(())],
        compiler_params=_cparams(("arbitrary", "arbitrary")),
        name="ffn_ln",
    )(u, x2, mod, w_gu, w_gu, w_down, lg, lb)


def _rot_half_cols(w):
    half = w.shape[-1] // 2
    return jnp.concatenate([-w[..., half:], w[..., :half]], axis=-1)


def _rope_tables():
    half = MLA_ROPE // 2
    inv = ROPE_THETA ** (-jnp.arange(half, dtype=F32) / half)
    ang = jnp.arange(SEQ, dtype=F32)[:, None] * inv[None, :]
    zeros = jnp.zeros((SEQ, MLA_ROPE), F32)
    cos = jnp.cos(ang)
    sin = jnp.sin(ang)
    return (jnp.concatenate([cos, cos, zeros], axis=1), jnp.concatenate([sin, sin, zeros], axis=1))


def _colscale():
    cs = np.ones((1, P_W), np.float32)
    cs[0, 0:D_A] = HEAD_DIM ** -0.5
    c0 = IN_A + PB_W
    cs[0, c0:c0 + D_C] = HEAD_DIM ** -0.5
    return jnp.asarray(cs)


def kernel(x, c, w_ada, b_ada, w_in, na_rpb, mla_q_norm, mla_kv_norm, mla_w_uq, mla_w_ukv,
           swa_sink, out_norm_g, w_o, ln1_g, ln1_b, w_gu, w_down, ln2_g, ln2_b):
    assert x.shape == (1, SEQ, D_MODEL)
    x2 = x.reshape(SEQ, D_MODEL)
    mod = _ada_call(c.reshape(D_MODEL, 1), w_ada, b_ada.reshape(DEPTH, 1, -1))
    mod = mod.reshape(DEPTH, 6, D_MODEL)
    cos_t, sin_t = _rope_tables()
    colscale = _colscale()
    w_p = _winprep_call(jnp.swapaxes(w_in, 1, 2))
    wq = mla_w_uq.reshape(DEPTH, MLA_Q_RANK, MLA_HEADS, MLA_NOPE + MLA_ROPE)
    wq = jnp.concatenate([wq, _rot_half_cols(wq[..., MLA_NOPE:])], axis=-1)
    wq = wq.reshape(DEPTH, MLA_Q_RANK, MLA_HEADS * MLA_QK).astype(BF16)
    wkv = mla_w_ukv.astype(BF16)
    w_o_b = w_o.astype(BF16)
    w_gu_b = w_gu.astype(BF16)
    w_down_b = w_down.astype(BF16)
    row3 = lambda a: a.reshape(DEPTH, 1, -1)
    gq, gkv, gn = row3(mla_q_norm), row3(mla_kv_norm), row3(out_norm_g)
    l1g, l1b, l2g, l2b = row3(ln1_g), row3(ln1_b), row3(ln2_g), row3(ln2_b)
    for l in range(DEPTH):
        pa, pb, pc = _inproj_call(l, x2, mod, w_p, colscale)
        bias = _na_bias_call(na_rpb[l].reshape(-1))
        ya = _na_call(pa, bias)
        q, k, v = _mla_prep_call(l, pb, gq, gkv, wq, wkv, cos_t, sin_t)
        yb_odd, yb_even = _mla_attn_call(q, k, v)
        yc = _swa_call(swa_sink[l], pc)
        x2, u = _outproj_call(l, ya, yb_even, yb_odd, yc, x2, mod, gn, w_o_b, l1g, l1b)
        x2 = _ffn_call(l, u, x2, mod, w_gu_b, w_down_b, l2g, l2b)
    return x2.reshape(1, SEQ, D_MODEL)
```

```python
import functools

import numpy as np
import jax
import jax.numpy as jnp
from jax import lax
from jax.experimental import pallas as pl
from jax.experimental.pallas import tpu as pltpu

F32 = jnp.float32
BF16 = jnp.bfloat16

D_MODEL = 2048
SEQ = 8192
DEPTH = 2
GRID_W = 64
GRID_ROWS = SEQ // GRID_W
HEAD_DIM = 128
NA_HEADS = 4
NA_WIN_ROWS = 8
NA_WIN_COLS = 16
MLA_HEADS = 6
MLA_Q_RANK = 512
MLA_KV_RANK = 256
MLA_NOPE = 128
MLA_ROPE = 64
MLA_V = 128
ROPE_THETA = 10000.0
SWA_HEADS = 6
SWA_KV_HEADS = 2
SWA_GROUP = SWA_HEADS // SWA_KV_HEADS
SWA_WINDOW = 128
SWA_BLOCK = 128
D_A = NA_HEADS * HEAD_DIM
D_B = MLA_HEADS * MLA_V
D_C = SWA_HEADS * HEAD_DIM
IN_A = 3 * D_A
IN_B = MLA_Q_RANK + MLA_KV_RANK + MLA_ROPE
IN_C = (SWA_HEADS + 2 * SWA_KV_HEADS) * HEAD_DIM
D_FF = 5632
DEEPNORM_ALPHA = (2 * DEPTH) ** 0.25
LN_EPS = 1e-5
RMS_EPS = 1e-6
NEG_INF = -1e30
LOG2_E = 1.4426950408889634

PB_W = MLA_Q_RANK + MLA_KV_RANK + 2 * MLA_ROPE
P_W = IN_A + PB_W + IN_C
MLA_QK = 2 * HEAD_DIM
MLA_VT = MLA_V + 16

NA_QROWS = 4
NA_KROWS = 12
NA_TQ = NA_QROWS * GRID_W
NA_TK = NA_KROWS * GRID_W
NA_NBLK = GRID_ROWS // NA_QROWS

VMEM_LIMIT = 56 * 1024 * 1024


def _cparams(sem):
    return pltpu.CompilerParams(dimension_semantics=sem, vmem_limit_bytes=VMEM_LIMIT)


def _layer_norm(z, g, b):
    mu = jnp.mean(z, axis=-1, keepdims=True)
    zc = z - mu
    var = jnp.mean(zc * zc, axis=-1, keepdims=True)
    return zc * lax.rsqrt(var + LN_EPS) * g + b


def _rms_norm(x, g):
    ms = jnp.mean(x * x, axis=-1, keepdims=True)
    return x * lax.rsqrt(ms + RMS_EPS) * g


def _dot_nt(a, b):
    return lax.dot_general(a, b, (((1,), (1,)), ((), ())), preferred_element_type=F32)


ADA_TN = 1024
ADA_RC = 256


def _ada_kernel(c_ref, w_ref, b_ref, o_ref):
    tn = o_ref.shape[-1]
    acc = jnp.zeros((8, tn), F32)
    for r in range(0, D_MODEL, ADA_RC):
        c = c_ref[r:r + ADA_RC, :]
        cond = c * (1.0 / (1.0 + jnp.exp(-c)))
        prod = w_ref[0, r:r + ADA_RC, :] * cond
        acc = acc + jnp.sum(prod.reshape(ADA_RC // 8, 8, tn), axis=0)
    o_ref[0] = jnp.sum(acc, axis=0, keepdims=True) + b_ref[0]


def _ada_call(c_col, w_ada, b_ada3):
    n = w_ada.shape[-1]
    return pl.pallas_call(
        _ada_kernel,
        out_shape=jax.ShapeDtypeStruct((DEPTH, 1, n), F32),
        grid=(DEPTH, n // ADA_TN),
        in_specs=[
            pl.BlockSpec((D_MODEL, 1), lambda l, j: (0, 0)),
            pl.BlockSpec((1, D_MODEL, ADA_TN), lambda l, j: (l, 0, j)),
            pl.BlockSpec((1, 1, ADA_TN), lambda l, j: (l, 0, j)),
        ],
        out_specs=pl.BlockSpec((1, 1, ADA_TN), lambda l, j: (l, 0, j)),
        compiler_params=_cparams(("parallel", "parallel")),
        name="ada_mod",
    )(c_col, w_ada, b_ada3)


WPREP_TN = 512
KR0 = IN_A + MLA_Q_RANK + MLA_KV_RANK


def _winprep_kernel(w_ref, o_ref):
    half = MLA_ROPE // 2
    kr1 = KR0 + MLA_ROPE
    o_ref[0:kr1, :] = w_ref[0:kr1, :].astype(BF16)
    o_ref[kr1:kr1 + half, :] = (-w_ref[KR0 + half:kr1, :]).astype(BF16)
    o_ref[kr1 + half:kr1 + MLA_ROPE, :] = w_ref[KR0:KR0 + half, :].astype(BF16)
    o_ref[kr1 + MLA_ROPE:, :] = w_ref[kr1:, :].astype(BF16)


def _winprep_call(w_in_t):
    tn = WPREP_TN
    return pl.pallas_call(
        _winprep_kernel,
        out_shape=jax.ShapeDtypeStruct((DEPTH, P_W, D_MODEL), BF16),
        grid=(DEPTH, D_MODEL // tn),
        in_specs=[pl.BlockSpec((None, w_in_t.shape[1], tn), lambda l, i: (l, 0, i))],
        out_specs=pl.BlockSpec((None, P_W, tn), lambda l, i: (l, 0, i)),
        compiler_params=_cparams(("parallel", "parallel")),
        name="w_in_prep",
    )(w_in_t)


INPROJ_TM = 512
_INPROJ_CHUNKS = (
    (0, 512, 0, 0), (512, 1024, 0, 512), (1024, 1536, 0, 1024),
    (1536, 2048, 1, 0), (2048, 2432, 1, 512),
    (2432, 2944, 2, 0), (2944, 3456, 2, 512), (3456, 3712, 2, 1024),
)


def _inproj_kernel(x_ref, mod_ref, w_ref, cs_ref, oa_ref, ob_ref, oc_ref):
    outs = (oa_ref, ob_ref, oc_ref)
    sh = mod_ref[0:1, :]
    sc = mod_ref[1:2, :]
    u = (x_ref[...] * (1.0 + sc) + sh).astype(BF16)
    for c0, c1, oi, off in _INPROJ_CHUNKS:
        acc = _dot_nt(u, w_ref[c0:c1, :]) * cs_ref[:, c0:c1]
        outs[oi][:, off:off + (c1 - c0)] = acc.astype(outs[oi].dtype)


def _inproj_call(l, x2, mod, w_p, colscale):
    tm = INPROJ_TM
    return pl.pallas_call(
        _inproj_kernel,
        out_shape=(jax.ShapeDtypeStruct((SEQ, IN_A), BF16),
                   jax.ShapeDtypeStruct((SEQ, PB_W), F32),
                   jax.ShapeDtypeStruct((SEQ, IN_C), BF16)),
        grid=(SEQ // tm,),
        in_specs=[
            pl.BlockSpec((tm, D_MODEL), lambda i: (i, 0)),
            pl.BlockSpec((None, 6, D_MODEL), lambda i: (l, 0, 0)),
            pl.BlockSpec((None, P_W, D_MODEL), lambda i: (l, 0, 0), pipeline_mode=pl.Buffered(1)),
            pl.BlockSpec((1, P_W), lambda i: (0, 0)),
        ],
        out_specs=(pl.BlockSpec((tm, IN_A), lambda i: (i, 0)),
                   pl.BlockSpec((tm, PB_W), lambda i: (i, 0)),
                   pl.BlockSpec((tm, IN_C), lambda i: (i, 0))),
        compiler_params=_cparams(("parallel",)),
        name="in_proj",
    )(x2, mod, w_p, colscale)


def _na_block_rule(btype, i, j):
    if btype == 0:
        r0 = max(i - NA_WIN_ROWS // 2, 0)
        valid = r0 <= j < r0 + NA_WIN_ROWS
        ro = j - i + (NA_WIN_ROWS - 1)
    elif btype == 1:
        valid = i <= j < i + NA_WIN_ROWS
        ro = j - i + (NA_WIN_ROWS - 1) - NA_WIN_ROWS // 2
    else:
        r = GRID_ROWS - NA_QROWS + i
        ks = GRID_ROWS - NA_KROWS
        r0 = min(r - NA_WIN_ROWS // 2, GRID_ROWS - NA_WIN_ROWS)
        valid = r0 <= ks + j < r0 + NA_WIN_ROWS
        ro = ks + j - r + (NA_WIN_ROWS - 1)
    return ro if valid else None


def _na_bias_kernel(rpb_ref, o_ref):
    h = pl.program_id(0)
    n_ro = 2 * NA_WIN_ROWS - 1
    n_co = 2 * NA_WIN_COLS - 1
    cq = lax.broadcasted_iota(jnp.int32, (GRID_W, GRID_W), 0)
    ck = lax.broadcasted_iota(jnp.int32, (GRID_W, GRID_W), 1)
    c0 = jnp.clip(cq - NA_WIN_COLS // 2, 0, GRID_W - NA_WIN_COLS)
    coff = jnp.clip(ck - cq, -(NA_WIN_COLS - 1), NA_WIN_COLS - 1) + (NA_WIN_COLS - 1)
    neg = jnp.full((GRID_W, GRID_W), NEG_INF, F32)
    tblocks = []
    for ro in range(n_ro):
        tb = neg
        for j in range(n_co):
            tb = jnp.where(coff == j, rpb_ref[h * (n_ro * n_co) + ro * n_co + j], tb)
        inside = jnp.where(ck >= c0, jnp.where(ck < c0 + NA_WIN_COLS, 1, 0), 0)
        tblocks.append(jnp.where(inside == 1, tb, neg))
    for btype in range(3):
        for i in range(NA_QROWS):
            for jp in range(NA_KROWS // 2):
                pair = []
                for j in (2 * jp, 2 * jp + 1):
                    ro = _na_block_rule(btype, i, j)
                    pair.append(neg if ro is None else tblocks[ro])
                o_ref[0, btype, i * GRID_W:(i + 1) * GRID_W, jp * 128:(jp + 1) * 128] = (
                    jnp.concatenate(pair, axis=1))


def _na_bias_call(rpb_flat):
    return pl.pallas_call(
        _na_bias_kernel,
        out_shape=jax.ShapeDtypeStruct((NA_HEADS, 3, NA_TQ, NA_TK), F32),
        grid=(NA_HEADS,),
        in_specs=[pl.BlockSpec(memory_space=pltpu.SMEM)],
        out_specs=pl.BlockSpec((1, 3, NA_TQ, NA_TK), lambda h: (h, 0, 0, 0)),
        compiler_params=_cparams(("parallel",)),
        name="na_bias",
    )(rpb_flat)


def _na_kernel(q_ref, k_ref, v_ref, bias_ref, o_ref):
    b = pl.program_id(0)
    ks = jnp.clip(NA_QROWS * b - NA_WIN_ROWS // 2, 0, GRID_ROWS - NA_KROWS) * GRID_W
    ks = pl.multiple_of(ks, GRID_W)
    ones = jnp.ones((NA_TK, HEAD_DIM), BF16)
    for h in range(NA_HEADS):
        cols = slice(h * HEAD_DIM, (h + 1) * HEAD_DIM)
        s = _dot_nt(q_ref[:, cols], k_ref[pl.ds(ks, NA_TK), cols]) + bias_ref[h, 0]
        m = jnp.max(s, axis=1, keepdims=True)
        p = jnp.exp(s - m).astype(BF16)
        vext = jnp.concatenate([v_ref[pl.ds(ks, NA_TK), cols], ones], axis=1)
        acc = jnp.dot(p, vext, preferred_element_type=F32)
        o_ref[:, cols] = acc[:, 0:HEAD_DIM] * (1.0 / acc[:, HEAD_DIM:HEAD_DIM + 1])


def _na_call(pa, bias):
    def btype(b):
        return jnp.where(b == 0, 0, jnp.where(b == NA_NBLK - 1, 2, 1))
    return pl.pallas_call(
        _na_kernel,
        out_shape=jax.ShapeDtypeStruct((SEQ, D_A), F32),
        grid=(NA_NBLK,),
        in_specs=[
            pl.BlockSpec((NA_TQ, D_A), lambda b: (b, 0)),
            pl.BlockSpec((SEQ, D_A), lambda b: (0, 1), pipeline_mode=pl.Buffered(1)),
            pl.BlockSpec((SEQ, D_A), lambda b: (0, 2), pipeline_mode=pl.Buffered(1)),
            pl.BlockSpec((NA_HEADS, 1, NA_TQ, NA_TK), lambda b: (0, btype(b), 0, 0)),
        ],
        out_specs=pl.BlockSpec((NA_TQ, D_A), lambda b: (b, 0)),
        compiler_params=_cparams(("arbitrary",)),
        name="na_attn",
    )(pa, pa, pa, bias)


MLAP_TM = 512


def _mla_prep_kernel(pb_ref, gq_ref, gkv_ref, wq_ref, wk_ref, wvt_ref, cos_ref, sin_ref,
                     q_ref, k_ref, vt_ref):
    tm = pb_ref.shape[0]
    cqn = _rms_norm(pb_ref[:, 0:MLA_Q_RANK], gq_ref[...]).astype(BF16)
    ckvn = _rms_norm(pb_ref[:, MLA_Q_RANK:MLA_Q_RANK + MLA_KV_RANK], gkv_ref[...]).astype(BF16)
    cos = cos_ref[...]
    sin = sin_ref[...]

    def rotary(t):
        return t * cos + pltpu.roll(t, MLA_ROPE, 1) * sin

    kpe = rotary(pb_ref[:, MLA_Q_RANK + MLA_KV_RANK:PB_W]).astype(BF16)
    ones = jnp.ones((MLA_VT - MLA_V, tm), BF16)
    scale = (MLA_NOPE + MLA_ROPE) ** -0.5 * LOG2_E
    for h in range(MLA_HEADS):
        qh = jnp.dot(cqn, wq_ref[:, h * MLA_QK:(h + 1) * MLA_QK], preferred_element_type=F32)
        q_ref[h, :, 0:HEAD_DIM] = (qh[:, 0:HEAD_DIM] * scale).astype(BF16)
        q_ref[h, :, HEAD_DIM:MLA_QK] = (rotary(qh[:, HEAD_DIM:MLA_QK]) * scale).astype(BF16)
        kh = jnp.dot(ckvn, wk_ref[:, h * MLA_NOPE:(h + 1) * MLA_NOPE], preferred_element_type=F32)
        k_ref[h, :, 0:HEAD_DIM] = kh.astype(BF16)
        k_ref[h, :, HEAD_DIM:MLA_QK] = kpe
        vt_ref[h, 0:MLA_V, :] = _dot_nt(wvt_ref[h * MLA_V:(h + 1) * MLA_V, :], ckvn).astype(BF16)
        vt_ref[h, MLA_V:MLA_VT, :] = ones


def _mla_prep_call(l, pb, gq, gkv, wq, wk, wvt, cos_t, sin_t):
    tm = MLAP_TM
    hsd = jax.ShapeDtypeStruct((MLA_HEADS, SEQ, MLA_QK), BF16)
    hspec = pl.BlockSpec((MLA_HEADS, tm, MLA_QK), lambda i: (0, i, 0))
    return pl.pallas_call(
        _mla_prep_kernel,
        out_shape=(hsd, hsd, jax.ShapeDtypeStruct((MLA_HEADS, MLA_VT, SEQ), BF16)),
        grid=(SEQ // tm,),
        in_specs=[
            pl.BlockSpec((tm, PB_W), lambda i: (i, 0)),
            pl.BlockSpec((None, 1, MLA_Q_RANK), lambda i: (l, 0, 0)),
            pl.BlockSpec((None, 1, MLA_KV_RANK), lambda i: (l, 0, 0)),
            pl.BlockSpec((None, MLA_Q_RANK, MLA_HEADS * MLA_QK), lambda i: (l, 0, 0)),
            pl.BlockSpec((None, MLA_KV_RANK, MLA_HEADS * MLA_NOPE), lambda i: (l, 0, 0)),
            pl.BlockSpec((None, MLA_HEADS * MLA_V, MLA_KV_RANK), lambda i: (l, 0, 0)),
            pl.BlockSpec((tm, HEAD_DIM), lambda i: (i, 0)),
            pl.BlockSpec((tm, HEAD_DIM), lambda i: (i, 0)),
        ],
        out_specs=(hspec, hspec, pl.BlockSpec((MLA_HEADS, MLA_VT, tm), lambda i: (0, 0, i))),
        compiler_params=_cparams(("parallel",)),
        name="mla_prep",
    )(pb, gq, gkv, wq, wk, wvt, cos_t, sin_t)


MLA_TQ = 256
MLA_TK = 512


def _mla_attn_kernel(qe_ref, qo_ref, k_ref, vt_ref, oe_ref, oo_ref,
                     s0_sc, s1_sc, s2_sc, s3_sc, m0_sc, m1_sc, m2_sc, m3_sc):
    j = pl.program_id(0)

    @pl.when(j == 0)
    def _():
        for ref in (s2_sc, s3_sc, m2_sc, m3_sc):
            ref[...] = jnp.zeros_like(ref)

    def stage(q_ref, sa_sc, ma_sc, sb_sc, mb_sc, vt_ref, o_ref):
        q = q_ref[0]
        tq = q.shape[0]
        m_prev = mb_sc[...]
        mx = jnp.full((8, tq), NEG_INF, F32)
        acc = jnp.zeros((MLA_VT, tq), F32)
        for c in range(SEQ // MLA_TK):
            keys = slice(c * MLA_TK, (c + 1) * MLA_TK)
            p = jnp.exp2(sb_sc[keys, :] - m_prev).astype(BF16)
            acc = acc + jnp.dot(vt_ref[0, :, keys], p, preferred_element_type=F32)
            s = _dot_nt(k_ref[0, keys, :], q)
            sa_sc[keys, :] = s
            mx = jnp.maximum(mx, jnp.max(s.reshape(MLA_TK // 8, 8, tq), axis=0))
        ma_sc[...] = jnp.max(mx, axis=0, keepdims=True)
        o = acc[0:MLA_V, :] * (1.0 / acc[MLA_V:MLA_V + 1, :])
        o_ref[...] = o.T

    def step(a_even, a_odd, b_even, b_odd):
        stage(qe_ref, *a_even, *b_even, vt_ref, oe_ref)
        stage(qo_ref, *a_odd, *b_odd, vt_ref, oo_ref)

    buf = ((s0_sc, m0_sc), (s1_sc, m1_sc), (s2_sc, m2_sc), (s3_sc, m3_sc))
    parity = lax.rem(j, 2)

    @pl.when(parity == 0)
    def _():
        step(buf[0], buf[1], buf[2], buf[3])

    @pl.when(parity == 1)
    def _():
        step(buf[2], buf[3], buf[0], buf[1])


def _mla_attn_call(q, k, vt):
    tq = MLA_TQ
    npair = SEQ // (2 * tq)
    last = MLA_HEADS * npair - 1
    pair_a = lambda j: jnp.minimum(j, last)
    pair_b = lambda j: jnp.maximum(j - 1, 0)
    half = jax.ShapeDtypeStruct((SEQ // 2, D_B), F32)
    out_spec = pl.BlockSpec((tq, MLA_V), lambda j: (pair_b(j) % npair, pair_b(j) // npair))
    return pl.pallas_call(
        _mla_attn_kernel,
        out_shape=(half, half),
        grid=(last + 2,),
        in_specs=[
            pl.BlockSpec((1, tq, MLA_QK), lambda j: (pair_a(j) // npair, 2 * (pair_a(j) % npair), 0)),
            pl.BlockSpec((1, tq, MLA_QK), lambda j: (pair_a(j) // npair, 2 * (pair_a(j) % npair) + 1, 0)),
            pl.BlockSpec((1, SEQ, MLA_QK), lambda j: (pair_a(j) // npair, 0, 0), pipeline_mode=pl.Buffered(1)),
            pl.BlockSpec((1, MLA_VT, SEQ), lambda j: (pair_b(j) // npair, 0, 0), pipeline_mode=pl.Buffered(1)),
        ],
        out_specs=(out_spec, out_spec),
        scratch_shapes=[pltpu.VMEM((SEQ, tq), F32)] * 4 + [pltpu.VMEM((1, tq), F32)] * 4,
        compiler_params=_cparams(("arbitrary",)),
        name="mla_attn",
    )(q, q, k, vt)


_SWA_SLOPES = tuple(2.0 ** (-8.0 * (i + 1) / SWA_HEADS) for i in range(SWA_HEADS))


SWA_NB = 4


def _swa_kernel(sink_ref, q_ref, kp_ref, kc_ref, kn_ref, vp_ref, vc_ref, vn_ref, o_ref):
    g_kv = pl.program_id(0)
    j = pl.program_id(1)
    t = SWA_BLOCK
    rows = SWA_GROUP * t
    kwin = jnp.concatenate([kp_ref[...], kc_ref[...], kn_ref[...]], axis=0)
    vwin = jnp.concatenate([vp_ref[...], vc_ref[...], vn_ref[...]], axis=0)
    ones = jnp.ones((3 * t, HEAD_DIM), BF16)
    ri = lax.broadcasted_iota(jnp.int32, (rows, 3 * t), 0)
    ci = lax.broadcasted_iota(jnp.int32, (rows, 3 * t), 1)
    grp = jnp.right_shift(ri, 7)
    dist = jnp.abs(jnp.bitwise_and(ri, t - 1) - (ci - t))
    slope_lo = jnp.where(grp == 0, _SWA_SLOPES[0], jnp.where(grp == 1, _SWA_SLOPES[1], _SWA_SLOPES[2]))
    slope_hi = jnp.where(grp == 0, _SWA_SLOPES[3], jnp.where(grp == 1, _SWA_SLOPES[4], _SWA_SLOPES[5]))
    slope = jnp.where(g_kv == 0, slope_lo, slope_hi)
    band = jnp.where(dist <= SWA_WINDOW, -slope * dist.astype(F32), NEG_INF)
    rcol = jnp.right_shift(lax.broadcasted_iota(jnp.int32, (rows, 1), 0), 7)
    base = g_kv * SWA_GROUP
    sink = jnp.where(rcol == 0, sink_ref[base], jnp.where(rcol == 1, sink_ref[base + 1], sink_ref[base + 2]))
    first_cols = jnp.where(j == 0, t, 0)
    last_cols = jnp.where(j == pl.num_programs(1) - 1, 2 * t, 3 * t)
    for b in range(SWA_NB):
        q = jnp.concatenate([q_ref[b * t:(b + 1) * t, g * t:(g + 1) * t] for g in range(SWA_GROUP)], axis=0)
        s = _dot_nt(q, kwin[b * t:(b + 3) * t, :]) + band
        if b == 0:
            s = jnp.where(ci < first_cols, NEG_INF, s)
        if b == SWA_NB - 1:
            s = jnp.where(ci >= last_cols, NEG_INF, s)
        m = jnp.maximum(jnp.max(s, axis=1, keepdims=True), sink)
        p = jnp.exp(s - m).astype(BF16)
        vext = jnp.concatenate([vwin[b * t:(b + 3) * t, :], ones], axis=1)
        acc = jnp.dot(p, vext, preferred_element_type=F32)
        l = acc[:, HEAD_DIM:HEAD_DIM + 1] + jnp.exp(sink - m)
        o = acc[:, 0:HEAD_DIM] * (1.0 / l)
        for g in range(SWA_GROUP):
            o_ref[b * t:(b + 1) * t, g * t:(g + 1) * t] = o[g * t:(g + 1) * t, :]


def _swa_call(sink, pc):
    t = SWA_BLOCK
    nb = SEQ // t
    tq = SWA_NB * t
    kcol = SWA_HEADS
    vcol = SWA_HEADS + SWA_KV_HEADS
    prev = lambda j: jnp.maximum(SWA_NB * j - 1, 0)
    nxt = lambda j: jnp.minimum(SWA_NB * (j + 1), nb - 1)
    return pl.pallas_call(
        _swa_kernel,
        out_shape=jax.ShapeDtypeStruct((SEQ, D_C), F32),
        grid=(SWA_KV_HEADS, SEQ // tq),
        in_specs=[
            pl.BlockSpec(memory_space=pltpu.SMEM),
            pl.BlockSpec((tq, SWA_GROUP * t), lambda g, j: (j, g)),
            pl.BlockSpec((t, t), lambda g, j: (prev(j), kcol + g)),
            pl.BlockSpec((tq, t), lambda g, j: (j, kcol + g)),
            pl.BlockSpec((t, t), lambda g, j: (nxt(j), kcol + g)),
            pl.BlockSpec((t, t), lambda g, j: (prev(j), vcol + g)),
            pl.BlockSpec((tq, t), lambda g, j: (j, vcol + g)),
            pl.BlockSpec((t, t), lambda g, j: (nxt(j), vcol + g)),
        ],
        out_specs=pl.BlockSpec((tq, SWA_GROUP * t), lambda g, j: (j, g)),
        compiler_params=_cparams(("parallel", "arbitrary")),
        name="swa_attn",
    )(sink, pc, pc, pc, pc, pc, pc, pc)


OUTPROJ_TM = 2 * MLA_TQ


def _outproj_kernel(ya_ref, ybe_ref, ybo_ref, yc_ref, x_ref, mod_ref, gn_ref, w_ref, lg_ref, lb_ref,
                    o_ref, u_ref):
    hm = OUTPROJ_TM // 2
    gate = 1.0 + mod_ref[2:3, :]
    for half, yb_ref in enumerate((ybe_ref, ybo_ref)):
        rows = slice(half * hm, (half + 1) * hm)
        acc = None
        for y, c0 in ((ya_ref[rows, :], 0), (yb_ref[...], D_A), (yc_ref[rows, :], D_A + D_B)):
            c1 = c0 + y.shape[1]
            yn = _rms_norm(y, gn_ref[:, c0:c1]).astype(BF16)
            part = jnp.dot(yn, w_ref[c0:c1, :], preferred_element_type=F32)
            acc = part if acc is None else acc + part
        x1 = _layer_norm(DEEPNORM_ALPHA * x_ref[rows, :] + gate * acc, lg_ref[...], lb_ref[...])
        o_ref[rows, :] = x1
        u_ref[rows, :] = (x1 * (1.0 + mod_ref[4:5, :]) + mod_ref[3:4, :]).astype(BF16)


def _outproj_call(l, ya, yb_even, yb_odd, yc, x2, mod, gn, w_o, lg, lb):
    tm = OUTPROJ_TM
    row = lambda r, w: pl.BlockSpec((r, w), lambda i: (i, 0))
    full = lambda r, w: pl.BlockSpec((None, r, w), lambda i: (l, 0, 0))
    return pl.pallas_call(
        _outproj_kernel,
        out_shape=(jax.ShapeDtypeStruct((SEQ, D_MODEL), F32), jax.ShapeDtypeStruct((SEQ, D_MODEL), BF16)),
        grid=(SEQ // tm,),
        in_specs=[row(tm, D_A), row(tm // 2, D_B), row(tm // 2, D_B), row(tm, D_C), row(tm, D_MODEL),
                  full(6, D_MODEL), full(1, D_MODEL),
                  pl.BlockSpec((None, D_MODEL, D_MODEL), lambda i: (l, 0, 0), pipeline_mode=pl.Buffered(1)),
                  full(1, D_MODEL), full(1, D_MODEL)],
        out_specs=(row(tm, D_MODEL), row(tm, D_MODEL)),
        compiler_params=_cparams(("parallel",)),
        name="out_proj_ln",
    )(ya, yb_even, yb_odd, yc, x2, mod, gn, w_o, lg, lb)


FFN_TM = 1024
FFN_SUB = 512
FFN_TF = 512


def _ffn_kernel(u_ref, x_hbm, mod_ref, wg_ref, wu_ref, wd_ref, lg_ref, lb_ref, o_ref, x_sc, x_sem):
    i = pl.program_id(0)
    f = pl.program_id(1)
    tm = o_ref.shape[0]
    x_copy = pltpu.make_async_copy(x_hbm.at[pl.ds(pl.multiple_of(i * tm, tm), tm), :], x_sc, x_sem)

    @pl.when(f == 0)
    def _():
        x_copy.start()
        o_ref[...] = jnp.zeros_like(o_ref)

    for r in range(0, tm, FFN_SUB):
        rows = slice(r, r + FFN_SUB)
        u = u_ref[rows, :]
        g = jnp.dot(u, wg_ref[...], preferred_element_type=F32)
        up = jnp.dot(u, wu_ref[...], preferred_element_type=F32)
        hdn = (g * (1.0 / (1.0 + jnp.exp(-g))) * up).astype(BF16)
        o_ref[rows, :] += jnp.dot(hdn, wd_ref[...], preferred_element_type=F32)

    @pl.when(f == pl.num_programs(1) - 1)
    def _():
        x_copy.wait()
        z = DEEPNORM_ALPHA * x_sc[...] + (1.0 + mod_ref[5:6, :]) * o_ref[...]
        o_ref[...] = _layer_norm(z, lg_ref[...], lb_ref[...])


def _ffn_call(l, u, x2, mod, w_gu, w_down, lg, lb):
    tm, tf = FFN_TM, FFN_TF
    nf = D_FF // tf
    return pl.pallas_call(
        _ffn_kernel,
        out_shape=jax.ShapeDtypeStruct((SEQ, D_MODEL), F32),
        grid=(SEQ // tm, nf),
        in_specs=[
            pl.BlockSpec((tm, D_MODEL), lambda i, f: (i, 0)),
            pl.BlockSpec(memory_space=pl.ANY),
            pl.BlockSpec((None, 6, D_MODEL), lambda i, f: (l, 0, 0)),
            pl.BlockSpec((None, D_MODEL, tf), lambda i, f: (l, 0, f)),
            pl.BlockSpec((None, D_MODEL, tf), lambda i, f: (l, 0, nf + f)),
            pl.BlockSpec((None, tf, D_MODEL), lambda i, f: (l, f, 0)),
            pl.BlockSpec((None, 1, D_MODEL), lambda i, f: (l, 0, 0)),
            pl.BlockSpec((None, 1, D_MODEL), lambda i, f: (l, 0, 0)),
        ],
        out_specs=pl.BlockSpec((tm, D_MODEL), lambda i, f: (i, 0)),
        scratch_shapes=[pltpu.VMEM((tm, D_MODEL), F32), pltpu.SemaphoreType.DMA(())],
        compiler_params=_cparams(("arbitrary", "arbitrary")),
        name="ffn_ln",
    )(u, x2, mod, w_gu, w_gu, w_down, lg, lb)


def _rot_half_cols(w):
    half = w.shape[-1] // 2
    return jnp.concatenate([-w[..., half:], w[..., :half]], axis=-1)


def _rope_tables():
    half = MLA_ROPE // 2
    inv = ROPE_THETA ** (-jnp.arange(half, dtype=F32) / half)
    ang = jnp.arange(SEQ, dtype=F32)[:, None] * inv[None, :]
    zeros = jnp.zeros((SEQ, MLA_ROPE), F32)
    cos = jnp.cos(ang)
    sin = jnp.sin(ang)
    return (jnp.concatenate([cos, cos, zeros], axis=1), jnp.concatenate([sin, sin, zeros], axis=1))


def _colscale():
    cs = np.ones((1, P_W), np.float32)
    cs[0, 0:D_A] = HEAD_DIM ** -0.5
    c0 = IN_A + PB_W
    cs[0, c0:c0 + D_C] = HEAD_DIM ** -0.5
    return jnp.asarray(cs)


def kernel(x, c, w_ada, b_ada, w_in, na_rpb, mla_q_norm, mla_kv_norm, mla_w_uq, mla_w_ukv,
           swa_sink, out_norm_g, w_o, ln1_g, ln1_b, w_gu, w_down, ln2_g, ln2_b):
    assert x.shape == (1, SEQ, D_MODEL)
    x2 = x.reshape(SEQ, D_MODEL)
    mod = _ada_call(c.reshape(D_MODEL, 1), w_ada, b_ada.reshape(DEPTH, 1, -1))
    mod = mod.reshape(DEPTH, 6, D_MODEL)
    cos_t, sin_t = _rope_tables()
    colscale = _colscale()
    w_p = _winprep_call(jnp.swapaxes(w_in, 1, 2))
    wq = mla_w_uq.reshape(DEPTH, MLA_Q_RANK, MLA_HEADS, MLA_NOPE + MLA_ROPE)
    wq = jnp.concatenate([wq, _rot_half_cols(wq[..., MLA_NOPE:])], axis=-1)
    wq = wq.reshape(DEPTH, MLA_Q_RANK, MLA_HEADS * MLA_QK).astype(BF16)
    wkv = mla_w_ukv.reshape(DEPTH, MLA_KV_RANK, MLA_HEADS, MLA_NOPE + MLA_V)
    wk = wkv[..., :MLA_NOPE].reshape(DEPTH, MLA_KV_RANK, MLA_HEADS * MLA_NOPE).astype(BF16)
    wvt = jnp.transpose(wkv[..., MLA_NOPE:], (0, 2, 3, 1)).reshape(DEPTH, MLA_HEADS * MLA_V, MLA_KV_RANK)
    wvt = wvt.astype(BF16)
    w_o_b = w_o.astype(BF16)
    w_gu_b = w_gu.astype(BF16)
    w_down_b = w_down.astype(BF16)
    row3 = lambda a: a.reshape(DEPTH, 1, -1)
    gq, gkv, gn = row3(mla_q_norm), row3(mla_kv_norm), row3(out_norm_g)
    l1g, l1b, l2g, l2b = row3(ln1_g), row3(ln1_b), row3(ln2_g), row3(ln2_b)
    for l in range(DEPTH):
        pa, pb, pc = _inproj_call(l, x2, mod, w_p, colscale)
        bias = _na_bias_call(na_rpb[l].reshape(-1))
        ya = _na_call(pa, bias)
        q, k, vt = _mla_prep_call(l, pb, gq, gkv, wq, wk, wvt, cos_t, sin_t)
        yb_even, yb_odd = _mla_attn_call(q, k, vt)
        yc = _swa_call(swa_sink[l], pc)
        x2, u = _outproj_call(l, ya, yb_even, yb_odd, yc, x2, mod, gn, w_o_b, l1g, l1b)
        x2 = _ffn_call(l, u, x2, mod, w_gu_b, w_down_b, l2g, l2b)
    return x2.reshape(1, SEQ, D_MODEL)
```

```python
import functools

import numpy as np
import jax
import jax.numpy as jnp
from jax import lax
from jax.experimental import pallas as pl
from jax.experimental.pallas import tpu as pltpu

F32 = jnp.float32
BF16 = jnp.bfloat16

D_MODEL = 2048
SEQ = 8192
DEPTH = 2
GRID_W = 64
GRID_ROWS = SEQ // GRID_W
HEAD_DIM = 128
NA_HEADS = 4
NA_WIN_ROWS = 8
NA_WIN_COLS = 16
MLA_HEADS = 6
MLA_Q_RANK = 512
MLA_KV_RANK = 256
MLA_NOPE = 128
MLA_ROPE = 64
MLA_V = 128
ROPE_THETA = 10000.0
SWA_HEADS = 6
SWA_KV_HEADS = 2
SWA_GROUP = SWA_HEADS // SWA_KV_HEADS
SWA_WINDOW = 128
SWA_BLOCK = 128
D_A = NA_HEADS * HEAD_DIM
D_B = MLA_HEADS * MLA_V
D_C = SWA_HEADS * HEAD_DIM
IN_A = 3 * D_A
IN_B = MLA_Q_RANK + MLA_KV_RANK + MLA_ROPE
IN_C = (SWA_HEADS + 2 * SWA_KV_HEADS) * HEAD_DIM
D_FF = 5632
DEEPNORM_ALPHA = (2 * DEPTH) ** 0.25
LN_EPS = 1e-5
RMS_EPS = 1e-6
NEG_INF = -1e30
LOG2_E = 1.4426950408889634

PB_W = MLA_Q_RANK + MLA_KV_RANK + 2 * MLA_ROPE
P_W = IN_A + PB_W + IN_C
MLA_QK = 2 * HEAD_DIM
MLA_VT = MLA_V + 16

NA_QROWS = 4
NA_KROWS = 12
NA_TQ = NA_QROWS * GRID_W
NA_TK = NA_KROWS * GRID_W
NA_NBLK = GRID_ROWS // NA_QROWS

VMEM_LIMIT = 56 * 1024 * 1024


def _cparams(sem):
    return pltpu.CompilerParams(dimension_semantics=sem, vmem_limit_bytes=VMEM_LIMIT)


def _layer_norm(z, g, b):
    mu = jnp.mean(z, axis=-1, keepdims=True)
    zc = z - mu
    var = jnp.mean(zc * zc, axis=-1, keepdims=True)
    return zc * lax.rsqrt(var + LN_EPS) * g + b


def _rms_norm(x, g):
    ms = jnp.mean(x * x, axis=-1, keepdims=True)
    return x * lax.rsqrt(ms + RMS_EPS) * g


def _dot_nt(a, b):
    return lax.dot_general(a, b, (((1,), (1,)), ((), ())), preferred_element_type=F32)


ADA_TN = 1024
ADA_RC = 256


def _ada_kernel(c_ref, w_ref, b_ref, o_ref):
    tn = o_ref.shape[-1]
    acc = jnp.zeros((8, tn), F32)
    for r in range(0, D_MODEL, ADA_RC):
        c = c_ref[r:r + ADA_RC, :]
        cond = c * (1.0 / (1.0 + jnp.exp(-c)))
        prod = w_ref[0, r:r + ADA_RC, :] * cond
        acc = acc + jnp.sum(prod.reshape(ADA_RC // 8, 8, tn), axis=0)
    o_ref[0] = jnp.sum(acc, axis=0, keepdims=True) + b_ref[0]


def _ada_call(c_col, w_ada, b_ada3):
    n = w_ada.shape[-1]
    return pl.pallas_call(
        _ada_kernel,
        out_shape=jax.ShapeDtypeStruct((DEPTH, 1, n), F32),
        grid=(DEPTH, n // ADA_TN),
        in_specs=[
            pl.BlockSpec((D_MODEL, 1), lambda l, j: (0, 0)),
            pl.BlockSpec((1, D_MODEL, ADA_TN), lambda l, j: (l, 0, j)),
            pl.BlockSpec((1, 1, ADA_TN), lambda l, j: (l, 0, j)),
        ],
        out_specs=pl.BlockSpec((1, 1, ADA_TN), lambda l, j: (l, 0, j)),
        compiler_params=_cparams(("parallel", "parallel")),
        name="ada_mod",
    )(c_col, w_ada, b_ada3)


WPREP_TN = 512
KR0 = IN_A + MLA_Q_RANK + MLA_KV_RANK


def _winprep_kernel(w_ref, o_ref):
    half = MLA_ROPE // 2
    kr1 = KR0 + MLA_ROPE
    o_ref[0:kr1, :] = w_ref[0:kr1, :].astype(BF16)
    o_ref[kr1:kr1 + half, :] = (-w_ref[KR0 + half:kr1, :]).astype(BF16)
    o_ref[kr1 + half:kr1 + MLA_ROPE, :] = w_ref[KR0:KR0 + half, :].astype(BF16)
    o_ref[kr1 + MLA_ROPE:, :] = w_ref[kr1:, :].astype(BF16)


def _winprep_call(w_in_t):
    tn = WPREP_TN
    return pl.pallas_call(
        _winprep_kernel,
        out_shape=jax.ShapeDtypeStruct((DEPTH, P_W, D_MODEL), BF16),
        grid=(DEPTH, D_MODEL // tn),
        in_specs=[pl.BlockSpec((None, w_in_t.shape[1], tn), lambda l, i: (l, 0, i))],
        out_specs=pl.BlockSpec((None, P_W, tn), lambda l, i: (l, 0, i)),
        compiler_params=_cparams(("parallel", "parallel")),
        name="w_in_prep",
    )(w_in_t)


INPROJ_TM = 512
_INPROJ_CHUNKS = (
    (0, 512, 0, 0), (512, 1024, 0, 512), (1024, 1536, 0, 1024),
    (1536, 2048, 1, 0), (2048, 2432, 1, 512),
    (2432, 2944, 2, 0), (2944, 3456, 2, 512), (3456, 3712, 2, 1024),
)


def _inproj_kernel(x_ref, mod_ref, w_ref, cs_ref, oa_ref, ob_ref, oc_ref):
    outs = (oa_ref, ob_ref, oc_ref)
    sh = mod_ref[0:1, :]
    sc = mod_ref[1:2, :]
    u = (x_ref[...] * (1.0 + sc) + sh).astype(BF16)
    for c0, c1, oi, off in _INPROJ_CHUNKS:
        acc = _dot_nt(u, w_ref[c0:c1, :]) * cs_ref[:, c0:c1]
        outs[oi][:, off:off + (c1 - c0)] = acc.astype(outs[oi].dtype)


def _inproj_call(l, x2, mod, w_p, colscale):
    tm = INPROJ_TM
    return pl.pallas_call(
        _inproj_kernel,
        out_shape=(jax.ShapeDtypeStruct((SEQ, IN_A), BF16),
                   jax.ShapeDtypeStruct((SEQ, PB_W), F32),
                   jax.ShapeDtypeStruct((SEQ, IN_C), BF16)),
        grid=(SEQ // tm,),
        in_specs=[
            pl.BlockSpec((tm, D_MODEL), lambda i: (i, 0)),
            pl.BlockSpec((None, 6, D_MODEL), lambda i: (l, 0, 0)),
            pl.BlockSpec((None, P_W, D_MODEL), lambda i: (l, 0, 0), pipeline_mode=pl.Buffered(1)),
            pl.BlockSpec((1, P_W), lambda i: (0, 0)),
        ],
        out_specs=(pl.BlockSpec((tm, IN_A), lambda i: (i, 0)),
                   pl.BlockSpec((tm, PB_W), lambda i: (i, 0)),
                   pl.BlockSpec((tm, IN_C), lambda i: (i, 0))),
        compiler_params=_cparams(("parallel",)),
        name="in_proj",
    )(x2, mod, w_p, colscale)


def _na_block_rule(btype, i, j):
    if btype == 0:
        r0 = max(i - NA_WIN_ROWS // 2, 0)
        valid = r0 <= j < r0 + NA_WIN_ROWS
        ro = j - i + (NA_WIN_ROWS - 1)
    elif btype == 1:
        valid = i <= j < i + NA_WIN_ROWS
        ro = j - i + (NA_WIN_ROWS - 1) - NA_WIN_ROWS // 2
    else:
        r = GRID_ROWS - NA_QROWS + i
        ks = GRID_ROWS - NA_KROWS
        r0 = min(r - NA_WIN_ROWS // 2, GRID_ROWS - NA_WIN_ROWS)
        valid = r0 <= ks + j < r0 + NA_WIN_ROWS
        ro = ks + j - r + (NA_WIN_ROWS - 1)
    return ro if valid else None


def _na_bias_kernel(rpb_ref, o_ref):
    h = pl.program_id(0)
    n_ro = 2 * NA_WIN_ROWS - 1
    n_co = 2 * NA_WIN_COLS - 1
    cq = lax.broadcasted_iota(jnp.int32, (GRID_W, GRID_W), 0)
    ck = lax.broadcasted_iota(jnp.int32, (GRID_W, GRID_W), 1)
    c0 = jnp.clip(cq - NA_WIN_COLS // 2, 0, GRID_W - NA_WIN_COLS)
    coff = jnp.clip(ck - cq, -(NA_WIN_COLS - 1), NA_WIN_COLS - 1) + (NA_WIN_COLS - 1)
    neg = jnp.full((GRID_W, GRID_W), NEG_INF, F32)
    tblocks = []
    for ro in range(n_ro):
        tb = neg
        for j in range(n_co):
            tb = jnp.where(coff == j, rpb_ref[h * (n_ro * n_co) + ro * n_co + j], tb)
        inside = jnp.where(ck >= c0, jnp.where(ck < c0 + NA_WIN_COLS, 1, 0), 0)
        tblocks.append(jnp.where(inside == 1, tb, neg))
    for btype in range(3):
        for i in range(NA_QROWS):
            for jp in range(NA_KROWS // 2):
                pair = []
                for j in (2 * jp, 2 * jp + 1):
                    ro = _na_block_rule(btype, i, j)
                    pair.append(neg if ro is None else tblocks[ro])
                o_ref[0, btype, i * GRID_W:(i + 1) * GRID_W, jp * 128:(jp + 1) * 128] = (
                    jnp.concatenate(pair, axis=1))


def _na_bias_call(rpb_flat):
    return pl.pallas_call(
        _na_bias_kernel,
        out_shape=jax.ShapeDtypeStruct((NA_HEADS, 3, NA_TQ, NA_TK), F32),
        grid=(NA_HEADS,),
        in_specs=[pl.BlockSpec(memory_space=pltpu.SMEM)],
        out_specs=pl.BlockSpec((1, 3, NA_TQ, NA_TK), lambda h: (h, 0, 0, 0)),
        compiler_params=_cparams(("parallel",)),
        name="na_bias",
    )(rpb_flat)


NA_BPS = 2


def _na_kernel(q_ref, k_ref, v_ref, *rest):
    bias_refs, o_ref = rest[:NA_BPS], rest[NA_BPS]
    j = pl.program_id(0)
    ones = jnp.ones((NA_TK, HEAD_DIM), BF16)
    for sub, bias_ref in enumerate(bias_refs):
        b = NA_BPS * j + sub
        ks = jnp.clip(NA_QROWS * b - NA_WIN_ROWS // 2, 0, GRID_ROWS - NA_KROWS) * GRID_W
        ks = pl.multiple_of(ks, GRID_W)
        rows = slice(sub * NA_TQ, (sub + 1) * NA_TQ)
        for h in range(NA_HEADS):
            cols = slice(h * HEAD_DIM, (h + 1) * HEAD_DIM)
            s = _dot_nt(q_ref[rows, cols], k_ref[pl.ds(ks, NA_TK), cols]) + bias_ref[h, 0]
            m = jnp.max(s, axis=1, keepdims=True)
            p = jnp.exp(s - m).astype(BF16)
            vext = jnp.concatenate([v_ref[pl.ds(ks, NA_TK), cols], ones], axis=1)
            acc = jnp.dot(p, vext, preferred_element_type=F32)
            o_ref[rows, cols] = acc[:, 0:HEAD_DIM] * (1.0 / acc[:, HEAD_DIM:HEAD_DIM + 1])


def _na_call(pa, bias):
    def btype(b):
        return jnp.where(b == 0, 0, jnp.where(b == NA_NBLK - 1, 2, 1))
    bias_specs = [pl.BlockSpec((NA_HEADS, 1, NA_TQ, NA_TK),
                               functools.partial(lambda j, sub: (0, btype(NA_BPS * j + sub), 0, 0), sub=sub))
                  for sub in range(NA_BPS)]
    return pl.pallas_call(
        _na_kernel,
        out_shape=jax.ShapeDtypeStruct((SEQ, D_A), F32),
        grid=(NA_NBLK // NA_BPS,),
        in_specs=[
            pl.BlockSpec((NA_BPS * NA_TQ, D_A), lambda j: (j, 0)),
            pl.BlockSpec((SEQ, D_A), lambda j: (0, 1), pipeline_mode=pl.Buffered(1)),
            pl.BlockSpec((SEQ, D_A), lambda j: (0, 2), pipeline_mode=pl.Buffered(1)),
        ] + bias_specs,
        out_specs=pl.BlockSpec((NA_BPS * NA_TQ, D_A), lambda j: (j, 0)),
        compiler_params=_cparams(("arbitrary",)),
        name="na_attn",
    )(pa, pa, pa, *([bias] * NA_BPS))


MLAP_TM = 512


def _mla_prep_kernel(pb_ref, gq_ref, gkv_ref, wq_ref, wk_ref, wvt_ref, cos_ref, sin_ref,
                     q_ref, k_ref, vt_ref):
    tm = pb_ref.shape[0]
    cqn = _rms_norm(pb_ref[:, 0:MLA_Q_RANK], gq_ref[...]).astype(BF16)
    ckvn = _rms_norm(pb_ref[:, MLA_Q_RANK:MLA_Q_RANK + MLA_KV_RANK], gkv_ref[...]).astype(BF16)
    cos = cos_ref[...]
    sin = sin_ref[...]

    def rotary(t):
        return t * cos + pltpu.roll(t, MLA_ROPE, 1) * sin

    kpe = rotary(pb_ref[:, MLA_Q_RANK + MLA_KV_RANK:PB_W]).astype(BF16)
    ones = jnp.ones((MLA_VT - MLA_V, tm), BF16)
    scale = (MLA_NOPE + MLA_ROPE) ** -0.5 * LOG2_E
    for h in range(MLA_HEADS):
        qh = jnp.dot(cqn, wq_ref[:, h * MLA_QK:(h + 1) * MLA_QK], preferred_element_type=F32)
        q_ref[h, :, 0:HEAD_DIM] = (qh[:, 0:HEAD_DIM] * scale).astype(BF16)
        q_ref[h, :, HEAD_DIM:MLA_QK] = (rotary(qh[:, HEAD_DIM:MLA_QK]) * scale).astype(BF16)
        kh = jnp.dot(ckvn, wk_ref[:, h * MLA_NOPE:(h + 1) * MLA_NOPE], preferred_element_type=F32)
        k_ref[h, :, 0:HEAD_DIM] = kh.astype(BF16)
        k_ref[h, :, HEAD_DIM:MLA_QK] = kpe
        vt_ref[h, 0:MLA_V, :] = _dot_nt(wvt_ref[h * MLA_V:(h + 1) * MLA_V, :], ckvn).astype(BF16)
        vt_ref[h, MLA_V:MLA_VT, :] = ones


def _mla_prep_call(l, pb, gq, gkv, wq, wk, wvt, cos_t, sin_t):
    tm = MLAP_TM
    hsd = jax.ShapeDtypeStruct((MLA_HEADS, SEQ, MLA_QK), BF16)
    hspec = pl.BlockSpec((MLA_HEADS, tm, MLA_QK), lambda i: (0, i, 0))
    return pl.pallas_call(
        _mla_prep_kernel,
        out_shape=(hsd, hsd, jax.ShapeDtypeStruct((MLA_HEADS, MLA_VT, SEQ), BF16)),
        grid=(SEQ // tm,),
        in_specs=[
            pl.BlockSpec((tm, PB_W), lambda i: (i, 0)),
            pl.BlockSpec((None, 1, MLA_Q_RANK), lambda i: (l, 0, 0)),
            pl.BlockSpec((None, 1, MLA_KV_RANK), lambda i: (l, 0, 0)),
            pl.BlockSpec((None, MLA_Q_RANK, MLA_HEADS * MLA_QK), lambda i: (l, 0, 0)),
            pl.BlockSpec((None, MLA_KV_RANK, MLA_HEADS * MLA_NOPE), lambda i: (l, 0, 0)),
            pl.BlockSpec((None, MLA_HEADS * MLA_V, MLA_KV_RANK), lambda i: (l, 0, 0)),
            pl.BlockSpec((tm, HEAD_DIM), lambda i: (i, 0)),
            pl.BlockSpec((tm, HEAD_DIM), lambda i: (i, 0)),
        ],
        out_specs=(hspec, hspec, pl.BlockSpec((MLA_HEADS, MLA_VT, tm), lambda i: (0, 0, i))),
        compiler_params=_cparams(("parallel",)),
        name="mla_prep",
    )(pb, gq, gkv, wq, wk, wvt, cos_t, sin_t)


MLA_TQ = 256
MLA_TK = 512


def _mla_attn_kernel(qe_ref, qo_ref, k_ref, vt_ref, wo_ref, wgu_ref, wdn_ref,
                     oe_ref, oo_ref, wo_b_ref, wgu_b_ref, wdn_b_ref,
                     s0_sc, s1_sc, s2_sc, s3_sc, m0_sc, m1_sc, m2_sc, m3_sc):
    cast_refs = ((wo_ref, wo_b_ref), (wgu_ref, wgu_b_ref), (wdn_ref, wdn_b_ref))
    j = pl.program_id(0)

    @pl.when(j == 0)
    def _():
        for ref in (s2_sc, s3_sc, m2_sc, m3_sc):
            ref[...] = jnp.zeros_like(ref)

    def stage(q_ref, sa_sc, ma_sc, sb_sc, mb_sc, vt_ref, o_ref):
        q = q_ref[0]
        tq = q.shape[0]
        m_prev = mb_sc[...]
        mx = jnp.full((8, tq), NEG_INF, F32)
        acc = jnp.zeros((MLA_VT, tq), F32)
        for c in range(SEQ // MLA_TK):
            keys = slice(c * MLA_TK, (c + 1) * MLA_TK)
            p = jnp.exp2(sb_sc[keys, :] - m_prev).astype(BF16)
            acc = acc + jnp.dot(vt_ref[0, :, keys], p, preferred_element_type=F32)
            s = _dot_nt(k_ref[0, keys, :], q)
            sa_sc[keys, :] = s
            mx = jnp.maximum(mx, jnp.max(s.reshape(MLA_TK // 8, 8, tq), axis=0))
        ma_sc[...] = jnp.max(mx, axis=0, keepdims=True)
        o = acc[0:MLA_V, :] * (1.0 / acc[MLA_V:MLA_V + 1, :])
        o_ref[...] = o.T

    def step(a_even, a_odd, b_even, b_odd):
        for w_ref, wb_ref in cast_refs:
            wb_ref[...] = w_ref[...].astype(BF16)
        stage(qe_ref, *a_even, *b_even, vt_ref, oe_ref)
        stage(qo_ref, *a_odd, *b_odd, vt_ref, oo_ref)

    buf = ((s0_sc, m0_sc), (s1_sc, m1_sc), (s2_sc, m2_sc), (s3_sc, m3_sc))
    parity = lax.rem(j, 2)

    @pl.when(parity == 0)
    def _():
        step(buf[0], buf[1], buf[2], buf[3])

    @pl.when(parity == 1)
    def _():
        step(buf[2], buf[3], buf[0], buf[1])


WO_SLAB = (32, D_MODEL)
WGU_SLAB = (256, 1024)
WDN_SLAB = (64, D_MODEL)


def _mla_attn_call(l, q, k, vt, w_o, w_gu, w_down):
    tq = MLA_TQ
    npair = SEQ // (2 * tq)
    last = MLA_HEADS * npair - 1
    nstep = last + 2
    pair_a = lambda j: jnp.minimum(j, last)
    pair_b = lambda j: jnp.maximum(j - 1, 0)
    half = jax.ShapeDtypeStruct((SEQ // 2, D_B), F32)
    out_spec = pl.BlockSpec((tq, MLA_V), lambda j: (pair_b(j) % npair, pair_b(j) // npair))

    def slab_specs(shape, slab):
        nr, nc = shape[0] // slab[0], shape[1] // slab[1]
        assert nr * slab[0] == shape[0] and nc * slab[1] == shape[1] and nr * nc <= nstep
        idx = lambda j: jnp.minimum(j, nr * nc - 1)
        return (pl.BlockSpec((None,) + slab, lambda j: (l, idx(j) // nc, idx(j) % nc)),
                pl.BlockSpec(slab, lambda j: (idx(j) // nc, idx(j) % nc)))

    wo_in, wo_out = slab_specs(w_o.shape[1:], WO_SLAB)
    wgu_in, wgu_out = slab_specs(w_gu.shape[1:], WGU_SLAB)
    wdn_in, wdn_out = slab_specs(w_down.shape[1:], WDN_SLAB)
    return pl.pallas_call(
        _mla_attn_kernel,
        out_shape=(half, half,
                   jax.ShapeDtypeStruct(w_o.shape[1:], BF16),
                   jax.ShapeDtypeStruct(w_gu.shape[1:], BF16),
                   jax.ShapeDtypeStruct(w_down.shape[1:], BF16)),
        grid=(nstep,),
        in_specs=[
            pl.BlockSpec((1, tq, MLA_QK), lambda j: (pair_a(j) // npair, 2 * (pair_a(j) % npair), 0)),
            pl.BlockSpec((1, tq, MLA_QK), lambda j: (pair_a(j) // npair, 2 * (pair_a(j) % npair) + 1, 0)),
            pl.BlockSpec((1, SEQ, MLA_QK), lambda j: (pair_a(j) // npair, 0, 0), pipeline_mode=pl.Buffered(1)),
            pl.BlockSpec((1, MLA_VT, SEQ), lambda j: (pair_b(j) // npair, 0, 0), pipeline_mode=pl.Buffered(1)),
            wo_in, wgu_in, wdn_in,
        ],
        out_specs=(out_spec, out_spec, wo_out, wgu_out, wdn_out),
        scratch_shapes=[pltpu.VMEM((SEQ, tq), F32)] * 4 + [pltpu.VMEM((1, tq), F32)] * 4,
        compiler_params=_cparams(("arbitrary",)),
        name="mla_attn",
    )(q, q, k, vt, w_o, w_gu, w_down)


_SWA_SLOPES = tuple(2.0 ** (-8.0 * (i + 1) / SWA_HEADS) for i in range(SWA_HEADS))


SWA_NB = 8


def _swa_kernel(sink_ref, q_ref, kp_ref, kc_ref, kn_ref, vp_ref, vc_ref, vn_ref, o_ref):
    g_kv = pl.program_id(0)
    j = pl.program_id(1)
    t = SWA_BLOCK
    rows = SWA_GROUP * t
    kwin = jnp.concatenate([kp_ref[...], kc_ref[...], kn_ref[...]], axis=0)
    vwin = jnp.concatenate([vp_ref[...], vc_ref[...], vn_ref[...]], axis=0)
    ones = jnp.ones((3 * t, HEAD_DIM), BF16)
    ri = lax.broadcasted_iota(jnp.int32, (rows, 3 * t), 0)
    ci = lax.broadcasted_iota(jnp.int32, (rows, 3 * t), 1)
    grp = jnp.right_shift(ri, 7)
    dist = jnp.abs(jnp.bitwise_and(ri, t - 1) - (ci - t))
    slope_lo = jnp.where(grp == 0, _SWA_SLOPES[0], jnp.where(grp == 1, _SWA_SLOPES[1], _SWA_SLOPES[2]))
    slope_hi = jnp.where(grp == 0, _SWA_SLOPES[3], jnp.where(grp == 1, _SWA_SLOPES[4], _SWA_SLOPES[5]))
    slope = jnp.where(g_kv == 0, slope_lo, slope_hi)
    band = jnp.where(dist <= SWA_WINDOW, -slope * dist.astype(F32), NEG_INF)
    rcol = jnp.right_shift(lax.broadcasted_iota(jnp.int32, (rows, 1), 0), 7)
    base = g_kv * SWA_GROUP
    sink = jnp.where(rcol == 0, sink_ref[base], jnp.where(rcol == 1, sink_ref[base + 1], sink_ref[base + 2]))
    first_cols = jnp.where(j == 0, t, 0)
    last_cols = jnp.where(j == pl.num_programs(1) - 1, 2 * t, 3 * t)
    for b in range(SWA_NB):
        q = jnp.concatenate([q_ref[b * t:(b + 1) * t, g * t:(g + 1) * t] for g in range(SWA_GROUP)], axis=0)
        s = _dot_nt(q, kwin[b * t:(b + 3) * t, :]) + band
        if b == 0:
            s = jnp.where(ci < first_cols, NEG_INF, s)
        if b == SWA_NB - 1:
            s = jnp.where(ci >= last_cols, NEG_INF, s)
        m = jnp.maximum(jnp.max(s, axis=1, keepdims=True), sink)
        p = jnp.exp(s - m).astype(BF16)
        vext = jnp.concatenate([vwin[b * t:(b + 3) * t, :], ones], axis=1)
        acc = jnp.dot(p, vext, preferred_element_type=F32)
        l = acc[:, HEAD_DIM:HEAD_DIM + 1] + jnp.exp(sink - m)
        o = acc[:, 0:HEAD_DIM] * (1.0 / l)
        for g in range(SWA_GROUP):
            o_ref[b * t:(b + 1) * t, g * t:(g + 1) * t] = o[g * t:(g + 1) * t, :]


def _swa_call(sink, pc):
    t = SWA_BLOCK
    nb = SEQ // t
    tq = SWA_NB * t
    kcol = SWA_HEADS
    vcol = SWA_HEADS + SWA_KV_HEADS
    prev = lambda j: jnp.maximum(SWA_NB * j - 1, 0)
    nxt = lambda j: jnp.minimum(SWA_NB * (j + 1), nb - 1)
    return pl.pallas_call(
        _swa_kernel,
        out_shape=jax.ShapeDtypeStruct((SEQ, D_C), F32),
        grid=(SWA_KV_HEADS, SEQ // tq),
        in_specs=[
            pl.BlockSpec(memory_space=pltpu.SMEM),
            pl.BlockSpec((tq, SWA_GROUP * t), lambda g, j: (j, g)),
            pl.BlockSpec((t, t), lambda g, j: (prev(j), kcol + g)),
            pl.BlockSpec((tq, t), lambda g, j: (j, kcol + g)),
            pl.BlockSpec((t, t), lambda g, j: (nxt(j), kcol + g)),
            pl.BlockSpec((t, t), lambda g, j: (prev(j), vcol + g)),
            pl.BlockSpec((tq, t), lambda g, j: (j, vcol + g)),
            pl.BlockSpec((t, t), lambda g, j: (nxt(j), vcol + g)),
        ],
        out_specs=pl.BlockSpec((tq, SWA_GROUP * t), lambda g, j: (j, g)),
        compiler_params=_cparams(("parallel", "arbitrary")),
        name="swa_attn",
    )(sink, pc, pc, pc, pc, pc, pc, pc)


OUTPROJ_TM = 2 * MLA_TQ


def _outproj_kernel(ya_ref, ybe_ref, ybo_ref, yc_ref, x_ref, mod_ref, gn_ref, w_ref, lg_ref, lb_ref,
                    o_ref, u_ref):
    hm = OUTPROJ_TM // 2
    gate = 1.0 + mod_ref[2:3, :]
    for half, yb_ref in enumerate((ybe_ref, ybo_ref)):
        rows = slice(half * hm, (half + 1) * hm)
        acc = None
        for y, c0 in ((ya_ref[rows, :], 0), (yb_ref[...], D_A), (yc_ref[rows, :], D_A + D_B)):
            c1 = c0 + y.shape[1]
            yn = _rms_norm(y, gn_ref[:, c0:c1]).astype(BF16)
            part = jnp.dot(yn, w_ref[c0:c1, :], preferred_element_type=F32)
            acc = part if acc is None else acc + part
        x1 = _layer_norm(DEEPNORM_ALPHA * x_ref[rows, :] + gate * acc, lg_ref[...], lb_ref[...])
        o_ref[rows, :] = x1
        u_ref[rows, :] = (x1 * (1.0 + mod_ref[4:5, :]) + mod_ref[3:4, :]).astype(BF16)


def _outproj_call(l, ya, yb_even, yb_odd, yc, x2, mod, gn, w_o, lg, lb):
    tm = OUTPROJ_TM
    row = lambda r, w: pl.BlockSpec((r, w), lambda i: (i, 0))
    full = lambda r, w: pl.BlockSpec((None, r, w), lambda i: (l, 0, 0))
    return pl.pallas_call(
        _outproj_kernel,
        out_shape=(jax.ShapeDtypeStruct((SEQ, D_MODEL), F32), jax.ShapeDtypeStruct((SEQ, D_MODEL), BF16)),
        grid=(SEQ // tm,),
        in_specs=[row(tm, D_A), row(tm // 2, D_B), row(tm // 2, D_B), row(tm, D_C), row(tm, D_MODEL),
                  full(6, D_MODEL), full(1, D_MODEL),
                  pl.BlockSpec((D_MODEL, D_MODEL), lambda i: (0, 0), pipeline_mode=pl.Buffered(1)),
                  full(1, D_MODEL), full(1, D_MODEL)],
        out_specs=(row(tm, D_MODEL), row(tm, D_MODEL)),
        compiler_params=_cparams(("parallel",)),
        name="out_proj_ln",
    )(ya, yb_even, yb_odd, yc, x2, mod, gn, w_o, lg, lb)


FFN_TM = 1024
FFN_SUB = 512
FFN_TF = 512


def _ffn_kernel(u_ref, x_hbm, mod_ref, wg_ref, wu_ref, wd_ref, lg_ref, lb_ref, o_ref, x_sc, x_sem):
    i = pl.program_id(0)
    f = pl.program_id(1)
    tm = o_ref.shape[0]
    x_copy = pltpu.make_async_copy(x_hbm.at[pl.ds(pl.multiple_of(i * tm, tm), tm), :], x_sc, x_sem)

    @pl.when(f == 0)
    def _():
        x_copy.start()
        o_ref[...] = jnp.zeros_like(o_ref)

    for r in range(0, tm, FFN_SUB):
        rows = slice(r, r + FFN_SUB)
        u = u_ref[rows, :]
        g = jnp.dot(u, wg_ref[...], preferred_element_type=F32)
        up = jnp.dot(u, wu_ref[...], preferred_element_type=F32)
        hdn = (g * (1.0 / (1.0 + jnp.exp(-g))) * up).astype(BF16)
        o_ref[rows, :] += jnp.dot(hdn, wd_ref[...], preferred_element_type=F32)

    @pl.when(f == pl.num_programs(1) - 1)
    def _():
        x_copy.wait()
        z = DEEPNORM_ALPHA * x_sc[...] + (1.0 + mod_ref[5:6, :]) * o_ref[...]
        o_ref[...] = _layer_norm(z, lg_ref[...], lb_ref[...])


def _ffn_call(l, u, x2, mod, w_gu, w_down, lg, lb):
    tm, tf = FFN_TM, FFN_TF
    nf = D_FF // tf
    return pl.pallas_call(
        _ffn_kernel,
        out_shape=jax.ShapeDtypeStruct((SEQ, D_MODEL), F32),
        grid=(SEQ // tm, nf),
        in_specs=[
            pl.BlockSpec((tm, D_MODEL), lambda i, f: (i, 0)),
            pl.BlockSpec(memory_space=pl.ANY),
            pl.BlockSpec((None, 6, D_MODEL), lambda i, f: (l, 0, 0)),
            pl.BlockSpec((D_MODEL, tf), lambda i, f: (0, f)),
            pl.BlockSpec((D_MODEL, tf), lambda i, f: (0, nf + f)),
            pl.BlockSpec((tf, D_MODEL), lambda i, f: (f, 0)),
            pl.BlockSpec((None, 1, D_MODEL), lambda i, f: (l, 0, 0)),
            pl.BlockSpec((None, 1, D_MODEL), lambda i, f: (l, 0, 0)),
        ],
        out_specs=pl.BlockSpec((tm, D_MODEL), lambda i, f: (i, 0)),
        scratch_shapes=[pltpu.VMEM((tm, D_MODEL), F32), pltpu.SemaphoreType.DMA(())],
        compiler_params=_cparams(("arbitrary", "arbitrary")),
        name="ffn_ln",
    )(u, x2, mod, w_gu, w_gu, w_down, lg, lb)


def _rot_half_cols(w):
    half = w.shape[-1] // 2
    return jnp.concatenate([-w[..., half:], w[..., :half]], axis=-1)


def _rope_tables():
    half = MLA_ROPE // 2
    inv = ROPE_THETA ** (-jnp.arange(half, dtype=F32) / half)
    ang = jnp.arange(SEQ, dtype=F32)[:, None] * inv[None, :]
    zeros = jnp.zeros((SEQ, MLA_ROPE), F32)
    cos = jnp.cos(ang)
    sin = jnp.sin(ang)
    return (jnp.concatenate([cos, cos, zeros], axis=1), jnp.concatenate([sin, sin, zeros], axis=1))


def _colscale():
    cs = np.ones((1, P_W), np.float32)
    cs[0, 0:D_A] = HEAD_DIM ** -0.5
    c0 = IN_A + PB_W
    cs[0, c0:c0 + D_C] = HEAD_DIM ** -0.5
    return jnp.asarray(cs)


def kernel(x, c, w_ada, b_ada, w_in, na_rpb, mla_q_norm, mla_kv_norm, mla_w_uq, mla_w_ukv,
           swa_sink, out_norm_g, w_o, ln1_g, ln1_b, w_gu, w_down, ln2_g, ln2_b):
    assert x.shape == (1, SEQ, D_MODEL)
    x2 = x.reshape(SEQ, D_MODEL)
    mod = _ada_call(c.reshape(D_MODEL, 1), w_ada, b_ada.reshape(DEPTH, 1, -1))
    mod = mod.reshape(DEPTH, 6, D_MODEL)
    cos_t, sin_t = _rope_tables()
    colscale = _colscale()
    w_p = _winprep_call(jnp.swapaxes(w_in, 1, 2))
    wq = mla_w_uq.reshape(DEPTH, MLA_Q_RANK, MLA_HEADS, MLA_NOPE + MLA_ROPE)
    wq = jnp.concatenate([wq, _rot_half_cols(wq[..., MLA_NOPE:])], axis=-1)
    wq = wq.reshape(DEPTH, MLA_Q_RANK, MLA_HEADS * MLA_QK).astype(BF16)
    wkv = mla_w_ukv.reshape(DEPTH, MLA_KV_RANK, MLA_HEADS, MLA_NOPE + MLA_V)
    wk = wkv[..., :MLA_NOPE].reshape(DEPTH, MLA_KV_RANK, MLA_HEADS * MLA_NOPE).astype(BF16)
    wvt = jnp.transpose(wkv[..., MLA_NOPE:], (0, 2, 3, 1)).reshape(DEPTH, MLA_HEADS * MLA_V, MLA_KV_RANK)
    wvt = wvt.astype(BF16)
    row3 = lambda a: a.reshape(DEPTH, 1, -1)
    gq, gkv, gn = row3(mla_q_norm), row3(mla_kv_norm), row3(out_norm_g)
    l1g, l1b, l2g, l2b = row3(ln1_g), row3(ln1_b), row3(ln2_g), row3(ln2_b)
    for l in range(DEPTH):
        pa, pb, pc = _inproj_call(l, x2, mod, w_p, colscale)
        bias = _na_bias_call(na_rpb[l].reshape(-1))
        ya = _na_call(pa, bias)
        q, k, vt = _mla_prep_call(l, pb, gq, gkv, wq, wk, wvt, cos_t, sin_t)
        yb_even, yb_odd, w_o_b, w_gu_b, w_down_b = _mla_attn_call(l, q, k, vt, w_o, w_gu, w_down)
        yc = _swa_call(swa_sink[l], pc)
        x2, u = _outproj_call(l, ya, yb_even, yb_odd, yc, x2, mod, gn, w_o_b, l1g, l1b)
        x2 = _ffn_call(l, u, x2, mod, w_gu_b, w_down_b, l2g, l2b)
    return x2.reshape(1, SEQ, D_MODEL)
```

```python
import functools

import numpy as np
import jax
import jax.numpy as jnp
from jax import lax
from jax.experimental import pallas as pl
from jax.experimental.pallas import tpu as pltpu

F32 = jnp.float32
BF16 = jnp.bfloat16

D_MODEL = 2048
SEQ = 8192
DEPTH = 2
GRID_W = 64
GRID_ROWS = SEQ // GRID_W
HEAD_DIM = 128
NA_HEADS = 4
NA_WIN_ROWS = 8
NA_WIN_COLS = 16
MLA_HEADS = 6
MLA_Q_RANK = 512
MLA_KV_RANK = 256
MLA_NOPE = 128
MLA_ROPE = 64
MLA_V = 128
ROPE_THETA = 10000.0
SWA_HEADS = 6
SWA_KV_HEADS = 2
SWA_GROUP = SWA_HEADS // SWA_KV_HEADS
SWA_WINDOW = 128
SWA_BLOCK = 128
D_A = NA_HEADS * HEAD_DIM
D_B = MLA_HEADS * MLA_V
D_C = SWA_HEADS * HEAD_DIM
IN_A = 3 * D_A
IN_B = MLA_Q_RANK + MLA_KV_RANK + MLA_ROPE
IN_C = (SWA_HEADS + 2 * SWA_KV_HEADS) * HEAD_DIM
D_FF = 5632
DEEPNORM_ALPHA = (2 * DEPTH) ** 0.25
LN_EPS = 1e-5
RMS_EPS = 1e-6
NEG_INF = -1e30
LOG2_E = 1.4426950408889634

PB_W = MLA_Q_RANK + MLA_KV_RANK + 2 * MLA_ROPE
P_W = IN_A + PB_W + IN_C
MLA_QK = 2 * HEAD_DIM
MLA_VT = MLA_V + 16

NA_QROWS = 4
NA_KROWS = 12
NA_TQ = NA_QROWS * GRID_W
NA_TK = NA_KROWS * GRID_W
NA_NBLK = GRID_ROWS // NA_QROWS

VMEM_LIMIT = 56 * 1024 * 1024


def _cparams(sem):
    return pltpu.CompilerParams(dimension_semantics=sem, vmem_limit_bytes=VMEM_LIMIT)


def _layer_norm(z, g, b):
    mu = jnp.mean(z, axis=-1, keepdims=True)
    zc = z - mu
    var = jnp.mean(zc * zc, axis=-1, keepdims=True)
    return zc * lax.rsqrt(var + LN_EPS) * g + b


def _rms_norm(x, g):
    ms = jnp.mean(x * x, axis=-1, keepdims=True)
    return x * lax.rsqrt(ms + RMS_EPS) * g


def _dot_nt(a, b):
    return lax.dot_general(a, b, (((1,), (1,)), ((), ())), preferred_element_type=F32)


ADA_TN = 1024
ADA_RC = 256
ADA_HEAD = 2 * D_MODEL
ADA_SLAB = 256
ADA_NSLAB = (DEPTH * 6 * D_MODEL - ADA_HEAD) // ADA_SLAB


def _ada_slab(cb_ref, w_ref, b_ref, o_ref):
    tn = o_ref.shape[-1]
    acc = jnp.zeros((8, tn), F32)
    for r in range(0, D_MODEL, ADA_RC):
        cb = jnp.concatenate([cb_ref[r:r + ADA_RC, :]] * (tn // 128), axis=1)
        prod = w_ref[r:r + ADA_RC, :] * cb
        acc = acc + jnp.sum(prod.reshape(ADA_RC // 8, 8, tn), axis=0)
    o_ref[...] = jnp.sum(acc, axis=0, keepdims=True) + b_ref[...]


def _ada_head_kernel(c_ref, w_ref, b_ref, o_ref, cb_ref):
    for r in range(0, D_MODEL, ADA_RC):
        c = c_ref[r:r + ADA_RC, :]
        cond = c * (1.0 / (1.0 + jnp.exp(-c)))
        cb_ref[r:r + ADA_RC, :] = jnp.broadcast_to(cond, (ADA_RC, 128))
    _ada_slab(cb_ref, w_ref, b_ref, o_ref)


def _ada_head_call(c_col, w_ada, b_ada3):
    return pl.pallas_call(
        _ada_head_kernel,
        out_shape=(jax.ShapeDtypeStruct((1, ADA_HEAD), F32), jax.ShapeDtypeStruct((D_MODEL, 128), F32)),
        grid=(ADA_HEAD // ADA_TN,),
        in_specs=[
            pl.BlockSpec((D_MODEL, 1), lambda j: (0, 0)),
            pl.BlockSpec((None, D_MODEL, ADA_TN), lambda j: (0, 0, j)),
            pl.BlockSpec((None, 1, ADA_TN), lambda j: (0, 0, j)),
        ],
        out_specs=(pl.BlockSpec((1, ADA_TN), lambda j: (0, j)),
                   pl.BlockSpec((D_MODEL, 128), lambda j: (0, 0))),
        compiler_params=_cparams(("arbitrary",)),
        name="ada_head",
    )(c_col, w_ada, b_ada3)


def _ada_slab_index(j):
    s = jnp.minimum(j, ADA_NSLAB - 1)
    n0 = (6 * D_MODEL - ADA_HEAD) // ADA_SLAB
    return jnp.where(s < n0, 0, 1), jnp.where(s < n0, s + ADA_HEAD // ADA_SLAB, s - n0), s


WPREP_TN = 512
KR0 = IN_A + MLA_Q_RANK + MLA_KV_RANK


def _winprep_kernel(w_ref, o_ref):
    half = MLA_ROPE // 2
    kr1 = KR0 + MLA_ROPE
    o_ref[0:kr1, :] = w_ref[0:kr1, :].astype(BF16)
    o_ref[kr1:kr1 + half, :] = (-w_ref[KR0 + half:kr1, :]).astype(BF16)
    o_ref[kr1 + half:kr1 + MLA_ROPE, :] = w_ref[KR0:KR0 + half, :].astype(BF16)
    o_ref[kr1 + MLA_ROPE:, :] = w_ref[kr1:, :].astype(BF16)


def _winprep_call(w_in_t):
    tn = WPREP_TN
    return pl.pallas_call(
        _winprep_kernel,
        out_shape=jax.ShapeDtypeStruct((DEPTH, P_W, D_MODEL), BF16),
        grid=(DEPTH, D_MODEL // tn),
        in_specs=[pl.BlockSpec((None, w_in_t.shape[1], tn), lambda l, i: (l, 0, i))],
        out_specs=pl.BlockSpec((None, P_W, tn), lambda l, i: (l, 0, i)),
        compiler_params=_cparams(("parallel", "parallel")),
        name="w_in_prep",
    )(w_in_t)


INPROJ_TM = 512
_INPROJ_CHUNKS = (
    (0, 512, 0, 0), (512, 1024, 0, 512), (1024, 1536, 0, 1024),
    (1536, 2048, 1, 0), (2048, 2432, 1, 512),
    (2432, 2944, 2, 0), (2944, 3456, 2, 512), (3456, 3712, 2, 1024),
)


def _inproj_kernel(x_ref, mod_ref, w_ref, cs_ref, oa_ref, ob_ref, oc_ref):
    outs = (oa_ref, ob_ref, oc_ref)
    sh = mod_ref[0:1, :]
    sc = mod_ref[1:2, :]
    u = (x_ref[...] * (1.0 + sc) + sh).astype(BF16)
    for c0, c1, oi, off in _INPROJ_CHUNKS:
        acc = _dot_nt(u, w_ref[c0:c1, :]) * cs_ref[:, c0:c1]
        outs[oi][:, off:off + (c1 - c0)] = acc.astype(outs[oi].dtype)


def _inproj_call(l, x2, mod, lmod, w_p, colscale):
    tm = INPROJ_TM
    return pl.pallas_call(
        _inproj_kernel,
        out_shape=(jax.ShapeDtypeStruct((SEQ, IN_A), BF16),
                   jax.ShapeDtypeStruct((SEQ, PB_W), F32),
                   jax.ShapeDtypeStruct((SEQ, IN_C), BF16)),
        grid=(SEQ // tm,),
        in_specs=[
            pl.BlockSpec((tm, D_MODEL), lambda i: (i, 0)),
            pl.BlockSpec((None, 6, D_MODEL), lambda i: (lmod, 0, 0)),
            pl.BlockSpec((None, P_W, D_MODEL), lambda i: (l, 0, 0), pipeline_mode=pl.Buffered(1)),
            pl.BlockSpec((1, P_W), lambda i: (0, 0)),
        ],
        out_specs=(pl.BlockSpec((tm, IN_A), lambda i: (i, 0)),
                   pl.BlockSpec((tm, PB_W), lambda i: (i, 0)),
                   pl.BlockSpec((tm, IN_C), lambda i: (i, 0))),
        compiler_params=_cparams(("parallel",)),
        name="in_proj",
    )(x2, mod, w_p, colscale)


def _na_block_rule(btype, i, j):
    if btype == 0:
        r0 = max(i - NA_WIN_ROWS // 2, 0)
        valid = r0 <= j < r0 + NA_WIN_ROWS
        ro = j - i + (NA_WIN_ROWS - 1)
    elif btype == 1:
        valid = i <= j < i + NA_WIN_ROWS
        ro = j - i + (NA_WIN_ROWS - 1) - NA_WIN_ROWS // 2
    else:
        r = GRID_ROWS - NA_QROWS + i
        ks = GRID_ROWS - NA_KROWS
        r0 = min(r - NA_WIN_ROWS // 2, GRID_ROWS - NA_WIN_ROWS)
        valid = r0 <= ks + j < r0 + NA_WIN_ROWS
        ro = ks + j - r + (NA_WIN_ROWS - 1)
    return ro if valid else None


def _na_bias_kernel(rpb_ref, o_ref):
    h = pl.program_id(0)
    n_ro = 2 * NA_WIN_ROWS - 1
    n_co = 2 * NA_WIN_COLS - 1
    cq = lax.broadcasted_iota(jnp.int32, (GRID_W, GRID_W), 0)
    ck = lax.broadcasted_iota(jnp.int32, (GRID_W, GRID_W), 1)
    c0 = jnp.clip(cq - NA_WIN_COLS // 2, 0, GRID_W - NA_WIN_COLS)
    coff = jnp.clip(ck - cq, -(NA_WIN_COLS - 1), NA_WIN_COLS - 1) + (NA_WIN_COLS - 1)
    neg = jnp.full((GRID_W, GRID_W), NEG_INF, F32)
    tblocks = []
    for ro in range(n_ro):
        tb = neg
        for j in range(n_co):
            tb = jnp.where(coff == j, rpb_ref[h * (n_ro * n_co) + ro * n_co + j], tb)
        inside = jnp.where(ck >= c0, jnp.where(ck < c0 + NA_WIN_COLS, 1, 0), 0)
        tblocks.append(jnp.where(inside == 1, tb, neg))
    for btype in range(3):
        for i in range(NA_QROWS):
            for jp in range(NA_KROWS // 2):
                pair = []
                for j in (2 * jp, 2 * jp + 1):
                    ro = _na_block_rule(btype, i, j)
                    pair.append(neg if ro is None else tblocks[ro])
                o_ref[0, btype, i * GRID_W:(i + 1) * GRID_W, jp * 128:(jp + 1) * 128] = (
                    jnp.concatenate(pair, axis=1))


def _na_bias_call(rpb_flat):
    return pl.pallas_call(
        _na_bias_kernel,
        out_shape=jax.ShapeDtypeStruct((NA_HEADS, 3, NA_TQ, NA_TK), F32),
        grid=(NA_HEADS,),
        in_specs=[pl.BlockSpec(memory_space=pltpu.SMEM)],
        out_specs=pl.BlockSpec((1, 3, NA_TQ, NA_TK), lambda h: (h, 0, 0, 0)),
        compiler_params=_cparams(("parallel",)),
        name="na_bias",
    )(rpb_flat)


NA_BPS = 2


def _na_kernel(q_ref, k_ref, v_ref, *rest):
    bias_refs, o_ref = rest[:NA_BPS], rest[NA_BPS]
    j = pl.program_id(0)
    ones = jnp.ones((NA_TK, HEAD_DIM), BF16)
    for sub, bias_ref in enumerate(bias_refs):
        b = NA_BPS * j + sub
        ks = jnp.clip(NA_QROWS * b - NA_WIN_ROWS // 2, 0, GRID_ROWS - NA_KROWS) * GRID_W
        ks = pl.multiple_of(ks, GRID_W)
        rows = slice(sub * NA_TQ, (sub + 1) * NA_TQ)
        for h in range(NA_HEADS):
            cols = slice(h * HEAD_DIM, (h + 1) * HEAD_DIM)
            s = _dot_nt(q_ref[rows, cols], k_ref[pl.ds(ks, NA_TK), cols]) + bias_ref[h, 0]
            m = jnp.max(s, axis=1, keepdims=True)
            p = jnp.exp(s - m).astype(BF16)
            vext = jnp.concatenate([v_ref[pl.ds(ks, NA_TK), cols], ones], axis=1)
            acc = jnp.dot(p, vext, preferred_element_type=F32)
            o_ref[rows, cols] = acc[:, 0:HEAD_DIM] * (1.0 / acc[:, HEAD_DIM:HEAD_DIM + 1])


def _na_call(pa, bias):
    def btype(b):
        return jnp.where(b == 0, 0, jnp.where(b == NA_NBLK - 1, 2, 1))
    bias_specs = [pl.BlockSpec((NA_HEADS, 1, NA_TQ, NA_TK),
                               functools.partial(lambda j, sub: (0, btype(NA_BPS * j + sub), 0, 0), sub=sub))
                  for sub in range(NA_BPS)]
    return pl.pallas_call(
        _na_kernel,
        out_shape=jax.ShapeDtypeStruct((SEQ, D_A), F32),
        grid=(NA_NBLK // NA_BPS,),
        in_specs=[
            pl.BlockSpec((NA_BPS * NA_TQ, D_A), lambda j: (j, 0)),
            pl.BlockSpec((SEQ, D_A), lambda j: (0, 1), pipeline_mode=pl.Buffered(1)),
            pl.BlockSpec((SEQ, D_A), lambda j: (0, 2), pipeline_mode=pl.Buffered(1)),
        ] + bias_specs,
        out_specs=pl.BlockSpec((NA_BPS * NA_TQ, D_A), lambda j: (j, 0)),
        compiler_params=_cparams(("arbitrary",)),
        name="na_attn",
    )(pa, pa, pa, *([bias] * NA_BPS))


MLAP_TM = 512


def _mla_prep_kernel(pb_ref, gq_ref, gkv_ref, wq_ref, wk_ref, wvt_ref, cos_ref, sin_ref,
                     q_ref, k_ref, vt_ref):
    tm = pb_ref.shape[0]
    cqn = _rms_norm(pb_ref[:, 0:MLA_Q_RANK], gq_ref[...]).astype(BF16)
    ckvn = _rms_norm(pb_ref[:, MLA_Q_RANK:MLA_Q_RANK + MLA_KV_RANK], gkv_ref[...]).astype(BF16)
    cos = cos_ref[...]
    sin = sin_ref[...]

    def rotary(t):
        return t * cos + pltpu.roll(t, MLA_ROPE, 1) * sin

    kpe = rotary(pb_ref[:, MLA_Q_RANK + MLA_KV_RANK:PB_W]).astype(BF16)
    ones = jnp.ones((MLA_VT - MLA_V, tm), BF16)
    scale = (MLA_NOPE + MLA_ROPE) ** -0.5 * LOG2_E
    k_all = jnp.dot(ckvn, wk_ref[...], preferred_element_type=F32).astype(BF16)
    for h in range(MLA_HEADS):
        qh = jnp.dot(cqn, wq_ref[:, h * MLA_QK:(h + 1) * MLA_QK], preferred_element_type=F32)
        q_ref[h, :, 0:HEAD_DIM] = (qh[:, 0:HEAD_DIM] * scale).astype(BF16)
        q_ref[h, :, HEAD_DIM:MLA_QK] = (rotary(qh[:, HEAD_DIM:MLA_QK]) * scale).astype(BF16)
        k_ref[h, :, 0:HEAD_DIM] = k_all[:, h * MLA_NOPE:(h + 1) * MLA_NOPE]
        k_ref[h, :, HEAD_DIM:MLA_QK] = kpe
        vt_ref[h, 0:MLA_V, :] = _dot_nt(wvt_ref[h * MLA_V:(h + 1) * MLA_V, :], ckvn).astype(BF16)
        vt_ref[h, MLA_V:MLA_VT, :] = ones


def _mla_prep_call(l, pb, gq, gkv, wq, wk, wvt, cos_t, sin_t):
    tm = MLAP_TM
    hsd = jax.ShapeDtypeStruct((MLA_HEADS, SEQ, MLA_QK), BF16)
    hspec = pl.BlockSpec((MLA_HEADS, tm, MLA_QK), lambda i: (0, i, 0))
    return pl.pallas_call(
        _mla_prep_kernel,
        out_shape=(hsd, hsd, jax.ShapeDtypeStruct((MLA_HEADS, MLA_VT, SEQ), BF16)),
        grid=(SEQ // tm,),
        in_specs=[
            pl.BlockSpec((tm, PB_W), lambda i: (i, 0)),
            pl.BlockSpec((None, 1, MLA_Q_RANK), lambda i: (l, 0, 0)),
            pl.BlockSpec((None, 1, MLA_KV_RANK), lambda i: (l, 0, 0)),
            pl.BlockSpec((None, MLA_Q_RANK, MLA_HEADS * MLA_QK), lambda i: (l, 0, 0)),
            pl.BlockSpec((None, MLA_KV_RANK, MLA_HEADS * MLA_NOPE), lambda i: (l, 0, 0)),
            pl.BlockSpec((None, MLA_HEADS * MLA_V, MLA_KV_RANK), lambda i: (l, 0, 0)),
            pl.BlockSpec((tm, HEAD_DIM), lambda i: (i, 0)),
            pl.BlockSpec((tm, HEAD_DIM), lambda i: (i, 0)),
        ],
        out_specs=(hspec, hspec, pl.BlockSpec((MLA_HEADS, MLA_VT, tm), lambda i: (0, 0, i))),
        compiler_params=_cparams(("parallel",)),
        name="mla_prep",
    )(pb, gq, gkv, wq, wk, wvt, cos_t, sin_t)


MLA_TQ = 256
MLA_TK = 1024


def _mla_attn_kernel(*refs, with_ada):
    qe_ref, qo_ref, k_ref, vt_ref, wo_ref, wgu_ref, wdn_ref = refs[:7]
    ada_in = refs[7:10] if with_ada else ()
    outs = refs[7 + len(ada_in):]
    oe_ref, oo_ref, wo_b_ref, wgu_b_ref, wdn_b_ref = outs[:5]
    ada_out = outs[5:6] if with_ada else ()
    s0_sc, s1_sc, s2_sc, s3_sc, m0_sc, m1_sc, m2_sc, m3_sc = outs[5 + len(ada_out):]
    cast_refs = ((wo_ref, wo_b_ref), (wgu_ref, wgu_b_ref), (wdn_ref, wdn_b_ref))
    j = pl.program_id(0)

    @pl.when(j == 0)
    def _():
        for ref in (s2_sc, s3_sc, m2_sc, m3_sc):
            ref[...] = jnp.zeros_like(ref)

    def stage(q_ref, sa_sc, ma_sc, sb_sc, mb_sc, vt_ref, o_ref):
        q = q_ref[0]
        tq = q.shape[0]
        m_prev = mb_sc[...]
        mx = jnp.full((8, tq), NEG_INF, F32)
        acc = jnp.zeros((MLA_VT, tq), F32)
        for c in range(SEQ // MLA_TK):
            keys = slice(c * MLA_TK, (c + 1) * MLA_TK)
            p = jnp.exp2(sb_sc[keys, :] - m_prev).astype(BF16)
            acc = acc + jnp.dot(vt_ref[0, :, keys], p, preferred_element_type=F32)
            s = _dot_nt(k_ref[0, keys, :], q)
            sa_sc[keys, :] = s
            mx = jnp.maximum(mx, jnp.max(s.reshape(MLA_TK // 8, 8, tq), axis=0))
        ma_sc[...] = jnp.max(mx, axis=0, keepdims=True)
        o = acc[0:MLA_V, :] * (1.0 / acc[MLA_V:MLA_V + 1, :])
        o_ref[...] = o.T

    def step(a_even, a_odd, b_even, b_odd):
        for w_ref, wb_ref in cast_refs:
            wb_ref[...] = w_ref[...].astype(BF16)
        if with_ada:
            _ada_slab(*ada_in, *ada_out)
        stage(qe_ref, *a_even, *b_even, vt_ref, oe_ref)
        stage(qo_ref, *a_odd, *b_odd, vt_ref, oo_ref)

    buf = ((s0_sc, m0_sc), (s1_sc, m1_sc), (s2_sc, m2_sc), (s3_sc, m3_sc))
    parity = lax.rem(j, 2)

    @pl.when(parity == 0)
    def _():
        step(buf[0], buf[1], buf[2], buf[3])

    @pl.when(parity == 1)
    def _():
        step(buf[2], buf[3], buf[0], buf[1])


WO_SLAB = (32, D_MODEL)
WGU_SLAB = (256, 1024)
WDN_SLAB = (64, D_MODEL)


def _mla_attn_call(l, q, k, vt, w_o, w_gu, w_down, ada=None):
    tq = MLA_TQ
    npair = SEQ // (2 * tq)
    last = MLA_HEADS * npair - 1
    nstep = last + 2
    pair_a = lambda j: jnp.minimum(j, last)
    pair_b = lambda j: jnp.maximum(j - 1, 0)
    half = jax.ShapeDtypeStruct((SEQ // 2, D_B), F32)
    out_spec = pl.BlockSpec((tq, MLA_V), lambda j: (pair_b(j) % npair, pair_b(j) // npair))

    def slab_specs(shape, slab):
        nr, nc = shape[0] // slab[0], shape[1] // slab[1]
        assert nr * slab[0] == shape[0] and nc * slab[1] == shape[1] and nr * nc <= nstep
        idx = lambda j: jnp.minimum(j, nr * nc - 1)
        return (pl.BlockSpec((None,) + slab, lambda j: (l, idx(j) // nc, idx(j) % nc)),
                pl.BlockSpec(slab, lambda j: (idx(j) // nc, idx(j) % nc)))

    wo_in, wo_out = slab_specs(w_o.shape[1:], WO_SLAB)
    wgu_in, wgu_out = slab_specs(w_gu.shape[1:], WGU_SLAB)
    wdn_in, wdn_out = slab_specs(w_down.shape[1:], WDN_SLAB)
    ada_args, ada_in, ada_out, ada_shape = (), [], (), ()
    if ada is not None:
        assert ADA_NSLAB <= nstep
        ada_args = ada
        ada_in = [
            pl.BlockSpec((D_MODEL, 128), lambda j: (0, 0)),
            pl.BlockSpec((None, D_MODEL, ADA_SLAB), lambda j: (_ada_slab_index(j)[0], 0, _ada_slab_index(j)[1])),
            pl.BlockSpec((None, 1, ADA_SLAB), lambda j: (_ada_slab_index(j)[0], 0, _ada_slab_index(j)[1])),
        ]
        ada_out = (pl.BlockSpec((1, ADA_SLAB), lambda j: (0, _ada_slab_index(j)[2])),)
        ada_shape = (jax.ShapeDtypeStruct((1, ADA_NSLAB * ADA_SLAB), F32),)
    return pl.pallas_call(
        functools.partial(_mla_attn_kernel, with_ada=ada is not None),
        out_shape=(half, half,
                   jax.ShapeDtypeStruct(w_o.shape[1:], BF16),
                   jax.ShapeDtypeStruct(w_gu.shape[1:], BF16),
                   jax.ShapeDtypeStruct(w_down.shape[1:], BF16)) + ada_shape,
        grid=(nstep,),
        in_specs=[
            pl.BlockSpec((1, tq, MLA_QK), lambda j: (pair_a(j) // npair, 2 * (pair_a(j) % npair), 0)),
            pl.BlockSpec((1, tq, MLA_QK), lambda j: (pair_a(j) // npair, 2 * (pair_a(j) % npair) + 1, 0)),
            pl.BlockSpec((1, SEQ, MLA_QK), lambda j: (pair_a(j) // npair, 0, 0), pipeline_mode=pl.Buffered(1)),
            pl.BlockSpec((1, MLA_VT, SEQ), lambda j: (pair_b(j) // npair, 0, 0), pipeline_mode=pl.Buffered(1)),
            wo_in, wgu_in, wdn_in,
        ] + ada_in,
        out_specs=(out_spec, out_spec, wo_out, wgu_out, wdn_out) + ada_out,
        scratch_shapes=[pltpu.VMEM((SEQ, tq), F32)] * 4 + [pltpu.VMEM((1, tq), F32)] * 4,
        compiler_params=_cparams(("arbitrary",)),
        name="mla_attn",
    )(q, q, k, vt, w_o, w_gu, w_down, *ada_args)


_SWA_SLOPES = tuple(2.0 ** (-8.0 * (i + 1) / SWA_HEADS) for i in range(SWA_HEADS))


SWA_NB = 8


def _swa_kernel(sink_ref, q_ref, kp_ref, kc_ref, kn_ref, vp_ref, vc_ref, vn_ref, o_ref):
    g_kv = pl.program_id(0)
    j = pl.program_id(1)
    t = SWA_BLOCK
    rows = SWA_GROUP * t
    kwin = jnp.concatenate([kp_ref[...], kc_ref[...], kn_ref[...]], axis=0)
    vwin = jnp.concatenate([vp_ref[...], vc_ref[...], vn_ref[...]], axis=0)
    ones = jnp.ones((3 * t, HEAD_DIM), BF16)
    ri = lax.broadcasted_iota(jnp.int32, (rows, 3 * t), 0)
    ci = lax.broadcasted_iota(jnp.int32, (rows, 3 * t), 1)
    grp = jnp.right_shift(ri, 7)
    dist = jnp.abs(jnp.bitwise_and(ri, t - 1) - (ci - t))
    slope_lo = jnp.where(grp == 0, _SWA_SLOPES[0], jnp.where(grp == 1, _SWA_SLOPES[1], _SWA_SLOPES[2]))
    slope_hi = jnp.where(grp == 0, _SWA_SLOPES[3], jnp.where(grp == 1, _SWA_SLOPES[4], _SWA_SLOPES[5]))
    slope = jnp.where(g_kv == 0, slope_lo, slope_hi)
    band = jnp.where(dist <= SWA_WINDOW, -slope * dist.astype(F32), NEG_INF)
    rcol = jnp.right_shift(lax.broadcasted_iota(jnp.int32, (rows, 1), 0), 7)
    base = g_kv * SWA_GROUP
    sink = jnp.where(rcol == 0, sink_ref[base], jnp.where(rcol == 1, sink_ref[base + 1], sink_ref[base + 2]))
    first_cols = jnp.where(j == 0, t, 0)
    last_cols = jnp.where(j == pl.num_programs(1) - 1, 2 * t, 3 * t)
    for b in range(SWA_NB):
        q = jnp.concatenate([q_ref[b * t:(b + 1) * t, g * t:(g + 1) * t] for g in range(SWA_GROUP)], axis=0)
        s = _dot_nt(q, kwin[b * t:(b + 3) * t, :]) + band
        if b == 0:
            s = jnp.where(ci < first_cols, NEG_INF, s)
        if b == SWA_NB - 1:
            s = jnp.where(ci >= last_cols, NEG_INF, s)
        m = jnp.maximum(jnp.max(s, axis=1, keepdims=True), sink)
        p = jnp.exp(s - m).astype(BF16)
        vext = jnp.concatenate([vwin[b * t:(b + 3) * t, :], ones], axis=1)
        acc = jnp.dot(p, vext, preferred_element_type=F32)
        l = acc[:, HEAD_DIM:HEAD_DIM + 1] + jnp.exp(sink - m)
        o = acc[:, 0:HEAD_DIM] * (1.0 / l)
        for g in range(SWA_GROUP):
            o_ref[b * t:(b + 1) * t, g * t:(g + 1) * t] = o[g * t:(g + 1) * t, :]


def _swa_call(sink, pc):
    t = SWA_BLOCK
    nb = SEQ // t
    tq = SWA_NB * t
    kcol = SWA_HEADS
    vcol = SWA_HEADS + SWA_KV_HEADS
    prev = lambda j: jnp.maximum(SWA_NB * j - 1, 0)
    nxt = lambda j: jnp.minimum(SWA_NB * (j + 1), nb - 1)
    return pl.pallas_call(
        _swa_kernel,
        out_shape=jax.ShapeDtypeStruct((SEQ, D_C), F32),
        grid=(SWA_KV_HEADS, SEQ // tq),
        in_specs=[
            pl.BlockSpec(memory_space=pltpu.SMEM),
            pl.BlockSpec((tq, SWA_GROUP * t), lambda g, j: (j, g)),
            pl.BlockSpec((t, t), lambda g, j: (prev(j), kcol + g)),
            pl.BlockSpec((tq, t), lambda g, j: (j, kcol + g)),
            pl.BlockSpec((t, t), lambda g, j: (nxt(j), kcol + g)),
            pl.BlockSpec((t, t), lambda g, j: (prev(j), vcol + g)),
            pl.BlockSpec((tq, t), lambda g, j: (j, vcol + g)),
            pl.BlockSpec((t, t), lambda g, j: (nxt(j), vcol + g)),
        ],
        out_specs=pl.BlockSpec((tq, SWA_GROUP * t), lambda g, j: (j, g)),
        compiler_params=_cparams(("parallel", "arbitrary")),
        name="swa_attn",
    )(sink, pc, pc, pc, pc, pc, pc, pc)


OUTPROJ_TM = 2 * MLA_TQ


def _outproj_kernel(ya_ref, ybe_ref, ybo_ref, yc_ref, x_ref, mod_ref, gn_ref, w_ref, lg_ref, lb_ref,
                    o_ref, u_ref):
    hm = OUTPROJ_TM // 2
    gate = 1.0 + mod_ref[2:3, :]
    for half, yb_ref in enumerate((ybe_ref, ybo_ref)):
        rows = slice(half * hm, (half + 1) * hm)
        acc = None
        for y, c0 in ((ya_ref[rows, :], 0), (yb_ref[...], D_A), (yc_ref[rows, :], D_A + D_B)):
            c1 = c0 + y.shape[1]
            yn = _rms_norm(y, gn_ref[:, c0:c1]).astype(BF16)
            part = jnp.dot(yn, w_ref[c0:c1, :], preferred_element_type=F32)
            acc = part if acc is None else acc + part
        x1 = _layer_norm(DEEPNORM_ALPHA * x_ref[rows, :] + gate * acc, lg_ref[...], lb_ref[...])
        o_ref[rows, :] = x1
        u_ref[rows, :] = (x1 * (1.0 + mod_ref[4:5, :]) + mod_ref[3:4, :]).astype(BF16)


def _outproj_call(l, ya, yb_even, yb_odd, yc, x2, mod, gn, w_o, lg, lb):
    tm = OUTPROJ_TM
    row = lambda r, w: pl.BlockSpec((r, w), lambda i: (i, 0))
    full = lambda r, w: pl.BlockSpec((None, r, w), lambda i: (l, 0, 0))
    return pl.pallas_call(
        _outproj_kernel,
        out_shape=(jax.ShapeDtypeStruct((SEQ, D_MODEL), F32), jax.ShapeDtypeStruct((SEQ, D_MODEL), BF16)),
        grid=(SEQ // tm,),
        in_specs=[row(tm, D_A), row(tm // 2, D_B), row(tm // 2, D_B), row(tm, D_C), row(tm, D_MODEL),
                  full(6, D_MODEL), full(1, D_MODEL),
                  pl.BlockSpec((D_MODEL, D_MODEL), lambda i: (0, 0), pipeline_mode=pl.Buffered(1)),
                  full(1, D_MODEL), full(1, D_MODEL)],
        out_specs=(row(tm, D_MODEL), row(tm, D_MODEL)),
        compiler_params=_cparams(("parallel",)),
        name="out_proj_ln",
    )(ya, yb_even, yb_odd, yc, x2, mod, gn, w_o, lg, lb)


FFN_TM = 1024
FFN_SUB = 512
FFN_SUB_LAST = 256
FFN_TF = 512


def _ffn_kernel(u_ref, x_hbm, mod_ref, wg_ref, wu_ref, wd_ref, lg_ref, lb_ref, o_ref, x_sc, x_sem):
    i = pl.program_id(0)
    f = pl.program_id(1)
    tm = o_ref.shape[0]
    x_copy = pltpu.make_async_copy(x_hbm.at[pl.ds(pl.multiple_of(i * tm, tm), tm), :], x_sc, x_sem)

    last = pl.num_programs(1) - 1

    def down_partial(rows):
        u = u_ref[rows, :]
        g = jnp.dot(u, wg_ref[...], preferred_element_type=F32)
        up = jnp.dot(u, wu_ref[...], preferred_element_type=F32)
        hdn = (g * (1.0 / (1.0 + jnp.exp(-g))) * up).astype(BF16)
        return jnp.dot(hdn, wd_ref[...], preferred_element_type=F32)

    @pl.when(f == 0)
    def _():
        x_copy.start()
        o_ref[...] = jnp.zeros_like(o_ref)

    @pl.when(f < last)
    def _():
        for r in range(0, tm, FFN_SUB):
            rows = slice(r, r + FFN_SUB)
            o_ref[rows, :] += down_partial(rows)

    @pl.when(f == last)
    def _():
        x_copy.wait()
        gate = 1.0 + mod_ref[5:6, :]
        for r in range(0, tm, FFN_SUB_LAST):
            rows = slice(r, r + FFN_SUB_LAST)
            acc = o_ref[rows, :] + down_partial(rows)
            o_ref[rows, :] = _layer_norm(DEEPNORM_ALPHA * x_sc[rows, :] + gate * acc, lg_ref[...], lb_ref[...])


def _ffn_call(l, u, x2, mod, w_gu, w_down, lg, lb):
    tm, tf = FFN_TM, FFN_TF
    nf = D_FF // tf
    return pl.pallas_call(
        _ffn_kernel,
        out_shape=jax.ShapeDtypeStruct((SEQ, D_MODEL), F32),
        grid=(SEQ // tm, nf),
        in_specs=[
            pl.BlockSpec((tm, D_MODEL), lambda i, f: (i, 0)),
            pl.BlockSpec(memory_space=pl.ANY),
            pl.BlockSpec((None, 6, D_MODEL), lambda i, f: (l, 0, 0)),
            pl.BlockSpec((D_MODEL, tf), lambda i, f: (0, f)),
            pl.BlockSpec((D_MODEL, tf), lambda i, f: (0, nf + f)),
            pl.BlockSpec((tf, D_MODEL), lambda i, f: (f, 0)),
            pl.BlockSpec((None, 1, D_MODEL), lambda i, f: (l, 0, 0)),
            pl.BlockSpec((None, 1, D_MODEL), lambda i, f: (l, 0, 0)),
        ],
        out_specs=pl.BlockSpec((tm, D_MODEL), lambda i, f: (i, 0)),
        scratch_shapes=[pltpu.VMEM((tm, D_MODEL), F32), pltpu.SemaphoreType.DMA(())],
        compiler_params=_cparams(("arbitrary", "arbitrary")),
        name="ffn_ln",
    )(u, x2, mod, w_gu, w_gu, w_down, lg, lb)


def _rot_half_cols(w):
    half = w.shape[-1] // 2
    return jnp.concatenate([-w[..., half:], w[..., :half]], axis=-1)


def _rope_tables():
    half = MLA_ROPE // 2
    inv = ROPE_THETA ** (-jnp.arange(half, dtype=F32) / half)
    ang = jnp.arange(SEQ, dtype=F32)[:, None] * inv[None, :]
    zeros = jnp.zeros((SEQ, MLA_ROPE), F32)
    cos = jnp.cos(ang)
    sin = jnp.sin(ang)
    return (jnp.concatenate([cos, cos, zeros], axis=1), jnp.concatenate([sin, sin, zeros], axis=1))


def _colscale():
    cs = np.ones((1, P_W), np.float32)
    cs[0, 0:D_A] = HEAD_DIM ** -0.5
    c0 = IN_A + PB_W
    cs[0, c0:c0 + D_C] = HEAD_DIM ** -0.5
    return jnp.asarray(cs)


def kernel(x, c, w_ada, b_ada, w_in, na_rpb, mla_q_norm, mla_kv_norm, mla_w_uq, mla_w_ukv,
           swa_sink, out_norm_g, w_o, ln1_g, ln1_b, w_gu, w_down, ln2_g, ln2_b):
    assert x.shape == (1, SEQ, D_MODEL)
    x2 = x.reshape(SEQ, D_MODEL)
    b_ada3 = b_ada.reshape(DEPTH, 1, -1)
    mod_head, cond_lanes = _ada_head_call(c.reshape(D_MODEL, 1), w_ada, b_ada3)
    mod = jnp.concatenate([mod_head, jnp.zeros((1, 6 * D_MODEL - ADA_HEAD), F32)], axis=1).reshape(1, 6, D_MODEL)
    cos_t, sin_t = _rope_tables()
    colscale = _colscale()
    w_p = _winprep_call(jnp.swapaxes(w_in, 1, 2))
    wq = mla_w_uq.reshape(DEPTH, MLA_Q_RANK, MLA_HEADS, MLA_NOPE + MLA_ROPE)
    wq = jnp.concatenate([wq, _rot_half_cols(wq[..., MLA_NOPE:])], axis=-1)
    wq = wq.reshape(DEPTH, MLA_Q_RANK, MLA_HEADS * MLA_QK).astype(BF16)
    wkv = mla_w_ukv.reshape(DEPTH, MLA_KV_RANK, MLA_HEADS, MLA_NOPE + MLA_V)
    wk = wkv[..., :MLA_NOPE].reshape(DEPTH, MLA_KV_RANK, MLA_HEADS * MLA_NOPE).astype(BF16)
    wvt = jnp.transpose(wkv[..., MLA_NOPE:], (0, 2, 3, 1)).reshape(DEPTH, MLA_HEADS * MLA_V, MLA_KV_RANK)
    wvt = wvt.astype(BF16)
    row3 = lambda a: a.reshape(DEPTH, 1, -1)
    gq, gkv, gn = row3(mla_q_norm), row3(mla_kv_norm), row3(out_norm_g)
    l1g, l1b, l2g, l2b = row3(ln1_g), row3(ln1_b), row3(ln2_g), row3(ln2_b)
    for l in range(DEPTH):
        pa, pb, pc = _inproj_call(l, x2, mod, l, w_p, colscale)
        bias = _na_bias_call(na_rpb[l].reshape(-1))
        ya = _na_call(pa, bias)
        q, k, vt = _mla_prep_call(l, pb, gq, gkv, wq, wk, wvt, cos_t, sin_t)
        if l == 0:
            yb_even, yb_odd, w_o_b, w_gu_b, w_down_b, mod_rest = _mla_attn_call(
                l, q, k, vt, w_o, w_gu, w_down, ada=(cond_lanes, w_ada, b_ada3))
            mod = jnp.concatenate([mod_head, mod_rest], axis=1).reshape(DEPTH, 6, D_MODEL)
        else:
            yb_even, yb_odd, w_o_b, w_gu_b, w_down_b = _mla_attn_call(l, q, k, vt, w_o, w_gu, w_down)
        yc = _swa_call(swa_sink[l], pc)
        x2, u = _outproj_call(l, ya, yb_even, yb_odd, yc, x2, mod, gn, w_o_b, l1g, l1b)
        x2 = _ffn_call(l, u, x2, mod, w_gu_b, w_down_b, l2g, l2b)
    return x2.reshape(1, SEQ, D_MODEL)
```

```python
import functools

import numpy as np
import jax
import jax.numpy as jnp
from jax import lax
from jax.experimental import pallas as pl
from jax.experimental.pallas import tpu as pltpu

F32 = jnp.float32
BF16 = jnp.bfloat16

D_MODEL = 2048
SEQ = 8192
DEPTH = 2
GRID_W = 64
GRID_ROWS = SEQ // GRID_W
HEAD_DIM = 128
NA_HEADS = 4
NA_WIN_ROWS = 8
NA_WIN_COLS = 16
MLA_HEADS = 6
MLA_Q_RANK = 512
MLA_KV_RANK = 256
MLA_NOPE = 128
MLA_ROPE = 64
MLA_V = 128
ROPE_THETA = 10000.0
SWA_HEADS = 6
SWA_KV_HEADS = 2
SWA_GROUP = SWA_HEADS // SWA_KV_HEADS
SWA_WINDOW = 128
SWA_BLOCK = 128
D_A = NA_HEADS * HEAD_DIM
D_B = MLA_HEADS * MLA_V
D_C = SWA_HEADS * HEAD_DIM
IN_A = 3 * D_A
IN_B = MLA_Q_RANK + MLA_KV_RANK + MLA_ROPE
IN_C = (SWA_HEADS + 2 * SWA_KV_HEADS) * HEAD_DIM
D_FF = 5632
DEEPNORM_ALPHA = (2 * DEPTH) ** 0.25
LN_EPS = 1e-5
RMS_EPS = 1e-6
NEG_INF = -1e30
LOG2_E = 1.4426950408889634

PB_W = MLA_Q_RANK + MLA_KV_RANK + 2 * MLA_ROPE
P_W = IN_A + PB_W + IN_C
MLA_QK = 2 * HEAD_DIM
MLA_VT = MLA_V + 16

NA_QROWS = 4
NA_KROWS = 12
NA_TQ = NA_QROWS * GRID_W
NA_TK = NA_KROWS * GRID_W
NA_NBLK = GRID_ROWS // NA_QROWS

VMEM_LIMIT = 56 * 1024 * 1024


def _cparams(sem):
    return pltpu.CompilerParams(dimension_semantics=sem, vmem_limit_bytes=VMEM_LIMIT)


def _layer_norm(z, g, b):
    mu = jnp.mean(z, axis=-1, keepdims=True)
    zc = z - mu
    var = jnp.mean(zc * zc, axis=-1, keepdims=True)
    return zc * lax.rsqrt(var + LN_EPS) * g + b


def _rms_norm(x, g):
    ms = jnp.mean(x * x, axis=-1, keepdims=True)
    return x * lax.rsqrt(ms + RMS_EPS) * g


def _dot_nt(a, b):
    return lax.dot_general(a, b, (((1,), (1,)), ((), ())), preferred_element_type=F32)


ADA_TN = 1024
ADA_RC = 256
ADA_HEAD = 2 * D_MODEL
ADA_SLAB = 256
ADA_NSLAB = (DEPTH * 6 * D_MODEL - ADA_HEAD) // ADA_SLAB


def _ada_slab(cb_ref, w_ref, b_ref, o_ref):
    tn = o_ref.shape[-1]
    acc = jnp.zeros((8, tn), F32)
    for r in range(0, D_MODEL, ADA_RC):
        cb = jnp.concatenate([cb_ref[r:r + ADA_RC, :]] * (tn // 128), axis=1)
        prod = w_ref[r:r + ADA_RC, :] * cb
        acc = acc + jnp.sum(prod.reshape(ADA_RC // 8, 8, tn), axis=0)
    o_ref[...] = jnp.sum(acc, axis=0, keepdims=True) + b_ref[...]


def _ada_head_kernel(c_ref, w_ref, b_ref, o_ref, cb_ref):
    for r in range(0, D_MODEL, ADA_RC):
        c = c_ref[r:r + ADA_RC, :]
        cond = c * (1.0 / (1.0 + jnp.exp(-c)))
        cb_ref[r:r + ADA_RC, :] = jnp.broadcast_to(cond, (ADA_RC, 128))
    _ada_slab(cb_ref, w_ref, b_ref, o_ref)


def _ada_head_call(c_col, w_ada, b_ada3):
    return pl.pallas_call(
        _ada_head_kernel,
        out_shape=(jax.ShapeDtypeStruct((1, ADA_HEAD), F32), jax.ShapeDtypeStruct((D_MODEL, 128), F32)),
        grid=(ADA_HEAD // ADA_TN,),
        in_specs=[
            pl.BlockSpec((D_MODEL, 1), lambda j: (0, 0)),
            pl.BlockSpec((None, D_MODEL, ADA_TN), lambda j: (0, 0, j)),
            pl.BlockSpec((None, 1, ADA_TN), lambda j: (0, 0, j)),
        ],
        out_specs=(pl.BlockSpec((1, ADA_TN), lambda j: (0, j)),
                   pl.BlockSpec((D_MODEL, 128), lambda j: (0, 0))),
        compiler_params=_cparams(("arbitrary",)),
        name="ada_head",
    )(c_col, w_ada, b_ada3)


def _ada_slab_index(j):
    s = jnp.minimum(j, ADA_NSLAB - 1)
    n0 = (6 * D_MODEL - ADA_HEAD) // ADA_SLAB
    return jnp.where(s < n0, 0, 1), jnp.where(s < n0, s + ADA_HEAD // ADA_SLAB, s - n0), s


WPREP_TN = 512
KR0 = IN_A + MLA_Q_RANK + MLA_KV_RANK


def _winprep_kernel(w_ref, o_ref):
    half = MLA_ROPE // 2
    kr1 = KR0 + MLA_ROPE
    o_ref[0:kr1, :] = w_ref[0:kr1, :].astype(BF16)
    o_ref[kr1:kr1 + half, :] = (-w_ref[KR0 + half:kr1, :]).astype(BF16)
    o_ref[kr1 + half:kr1 + MLA_ROPE, :] = w_ref[KR0:KR0 + half, :].astype(BF16)
    o_ref[kr1 + MLA_ROPE:, :] = w_ref[kr1:, :].astype(BF16)


def _winprep_call(w_in_t):
    tn = WPREP_TN
    return pl.pallas_call(
        _winprep_kernel,
        out_shape=jax.ShapeDtypeStruct((DEPTH, P_W, D_MODEL), BF16),
        grid=(DEPTH, D_MODEL // tn),
        in_specs=[pl.BlockSpec((None, w_in_t.shape[1], tn), lambda l, i: (l, 0, i))],
        out_specs=pl.BlockSpec((None, P_W, tn), lambda l, i: (l, 0, i)),
        compiler_params=_cparams(("parallel", "parallel")),
        name="w_in_prep",
    )(w_in_t)


INPROJ_TM = 512
_INPROJ_CHUNKS = (
    (0, 512, 0, 0), (512, 1024, 0, 512), (1024, 1536, 0, 1024),
    (2432, 2944, 1, 0), (2944, 3456, 1, 512), (3456, 3712, 1, 1024),
)
_PB0 = IN_A


def _inproj_kernel(x_ref, mod_ref, w_ref, cs_ref, gq_ref, gkv_ref, wq_ref, wk_ref, wvt_ref, cos_ref, sin_ref,
                   oa_ref, oc_ref, q_ref, k_ref, vt_ref):
    outs = (oa_ref, oc_ref)
    sh = mod_ref[0:1, :]
    sc = mod_ref[1:2, :]
    u = (x_ref[...] * (1.0 + sc) + sh).astype(BF16)
    for c0, c1, oi, off in _INPROJ_CHUNKS:
        acc = _dot_nt(u, w_ref[c0:c1, :]) * cs_ref[:, c0:c1]
        outs[oi][:, off:off + (c1 - c0)] = acc.astype(outs[oi].dtype)
    cq = _dot_nt(u, w_ref[_PB0:_PB0 + MLA_Q_RANK, :])
    ckv_kr = _dot_nt(u, w_ref[_PB0 + MLA_Q_RANK:_PB0 + PB_W, :])
    _mla_up(cq, ckv_kr[:, 0:MLA_KV_RANK], ckv_kr[:, MLA_KV_RANK:], gq_ref, gkv_ref, wq_ref, wk_ref, wvt_ref,
            cos_ref, sin_ref, q_ref, k_ref, vt_ref)


def _inproj_call(l, x2, mod, lmod, w_p, colscale, gq, gkv, wq, wk, wvt, cos_t, sin_t):
    tm = INPROJ_TM
    hsd = jax.ShapeDtypeStruct((MLA_HEADS, SEQ, MLA_QK), BF16)
    hspec = pl.BlockSpec((MLA_HEADS, tm, MLA_QK), lambda i: (0, i, 0))
    layer = lambda r, c: pl.BlockSpec((None, r, c), lambda i: (l, 0, 0))
    return pl.pallas_call(
        _inproj_kernel,
        out_shape=(jax.ShapeDtypeStruct((SEQ, IN_A), BF16),
                   jax.ShapeDtypeStruct((SEQ, IN_C), BF16),
                   hsd, hsd, jax.ShapeDtypeStruct((MLA_HEADS, MLA_VT, SEQ), BF16)),
        grid=(SEQ // tm,),
        in_specs=[
            pl.BlockSpec((tm, D_MODEL), lambda i: (i, 0)),
            pl.BlockSpec((None, 6, D_MODEL), lambda i: (lmod, 0, 0)),
            pl.BlockSpec((None, P_W, D_MODEL), lambda i: (l, 0, 0), pipeline_mode=pl.Buffered(1)),
            pl.BlockSpec((1, P_W), lambda i: (0, 0)),
            layer(1, MLA_Q_RANK), layer(1, MLA_KV_RANK),
            layer(MLA_Q_RANK, MLA_HEADS * MLA_QK), layer(MLA_KV_RANK, MLA_HEADS * MLA_NOPE),
            layer(MLA_HEADS * MLA_V, MLA_KV_RANK),
            pl.BlockSpec((tm, HEAD_DIM), lambda i: (i, 0)),
            pl.BlockSpec((tm, HEAD_DIM), lambda i: (i, 0)),
        ],
        out_specs=(pl.BlockSpec((tm, IN_A), lambda i: (i, 0)),
                   pl.BlockSpec((tm, IN_C), lambda i: (i, 0)),
                   hspec, hspec, pl.BlockSpec((MLA_HEADS, MLA_VT, tm), lambda i: (0, 0, i))),
        compiler_params=_cparams(("parallel",)),
        name="in_proj",
    )(x2, mod, w_p, colscale, gq, gkv, wq, wk, wvt, cos_t, sin_t)


def _na_block_rule(btype, i, j):
    if btype == 0:
        r0 = max(i - NA_WIN_ROWS // 2, 0)
        valid = r0 <= j < r0 + NA_WIN_ROWS
        ro = j - i + (NA_WIN_ROWS - 1)
    elif btype == 1:
        valid = i <= j < i + NA_WIN_ROWS
        ro = j - i + (NA_WIN_ROWS - 1) - NA_WIN_ROWS // 2
    else:
        r = GRID_ROWS - NA_QROWS + i
        ks = GRID_ROWS - NA_KROWS
        r0 = min(r - NA_WIN_ROWS // 2, GRID_ROWS - NA_WIN_ROWS)
        valid = r0 <= ks + j < r0 + NA_WIN_ROWS
        ro = ks + j - r + (NA_WIN_ROWS - 1)
    return ro if valid else None


def _na_bias_kernel(rpb_ref, o_ref):
    h = pl.program_id(0)
    n_ro = 2 * NA_WIN_ROWS - 1
    n_co = 2 * NA_WIN_COLS - 1
    cq = lax.broadcasted_iota(jnp.int32, (GRID_W, GRID_W), 0)
    ck = lax.broadcasted_iota(jnp.int32, (GRID_W, GRID_W), 1)
    c0 = jnp.clip(cq - NA_WIN_COLS // 2, 0, GRID_W - NA_WIN_COLS)
    coff = jnp.clip(ck - cq, -(NA_WIN_COLS - 1), NA_WIN_COLS - 1) + (NA_WIN_COLS - 1)
    neg = jnp.full((GRID_W, GRID_W), NEG_INF, F32)
    tblocks = []
    for ro in range(n_ro):
        tb = neg
        for j in range(n_co):
            tb = jnp.where(coff == j, rpb_ref[h * (n_ro * n_co) + ro * n_co + j], tb)
        inside = jnp.where(ck >= c0, jnp.where(ck < c0 + NA_WIN_COLS, 1, 0), 0)
        tblocks.append(jnp.where(inside == 1, tb, neg))
    for btype in range(3):
        for i in range(NA_QROWS):
            for jp in range(NA_KROWS // 2):
                pair = []
                for j in (2 * jp, 2 * jp + 1):
                    ro = _na_block_rule(btype, i, j)
                    pair.append(neg if ro is None else tblocks[ro])
                o_ref[0, btype, i * GRID_W:(i + 1) * GRID_W, jp * 128:(jp + 1) * 128] = (
                    jnp.concatenate(pair, axis=1))


def _na_bias_call(rpb_flat):
    return pl.pallas_call(
        _na_bias_kernel,
        out_shape=jax.ShapeDtypeStruct((NA_HEADS, 3, NA_TQ, NA_TK), F32),
        grid=(NA_HEADS,),
        in_specs=[pl.BlockSpec(memory_space=pltpu.SMEM)],
        out_specs=pl.BlockSpec((1, 3, NA_TQ, NA_TK), lambda h: (h, 0, 0, 0)),
        compiler_params=_cparams(("parallel",)),
        name="na_bias",
    )(rpb_flat)


NA_BPS = 2


def _na_kernel(q_ref, k_ref, v_ref, *rest):
    bias_refs, o_ref = rest[:NA_BPS], rest[NA_BPS]
    j = pl.program_id(0)
    ones = jnp.ones((NA_TK, HEAD_DIM), BF16)
    for sub, bias_ref in enumerate(bias_refs):
        b = NA_BPS * j + sub
        ks = jnp.clip(NA_QROWS * b - NA_WIN_ROWS // 2, 0, GRID_ROWS - NA_KROWS) * GRID_W
        ks = pl.multiple_of(ks, GRID_W)
        rows = slice(sub * NA_TQ, (sub + 1) * NA_TQ)
        for h in range(NA_HEADS):
            cols = slice(h * HEAD_DIM, (h + 1) * HEAD_DIM)
            s = _dot_nt(q_ref[rows, cols], k_ref[pl.ds(ks, NA_TK), cols]) + bias_ref[h, 0]
            m = jnp.max(s, axis=1, keepdims=True)
            p = jnp.exp(s - m).astype(BF16)
            vext = jnp.concatenate([v_ref[pl.ds(ks, NA_TK), cols], ones], axis=1)
            acc = jnp.dot(p, vext, preferred_element_type=F32)
            o_ref[rows, cols] = acc[:, 0:HEAD_DIM] * (1.0 / acc[:, HEAD_DIM:HEAD_DIM + 1])


def _na_call(pa, bias):
    def btype(b):
        return jnp.where(b == 0, 0, jnp.where(b == NA_NBLK - 1, 2, 1))
    bias_specs = [pl.BlockSpec((NA_HEADS, 1, NA_TQ, NA_TK),
                               functools.partial(lambda j, sub: (0, btype(NA_BPS * j + sub), 0, 0), sub=sub))
                  for sub in range(NA_BPS)]
    return pl.pallas_call(
        _na_kernel,
        out_shape=jax.ShapeDtypeStruct((SEQ, D_A), F32),
        grid=(NA_NBLK // NA_BPS,),
        in_specs=[
            pl.BlockSpec((NA_BPS * NA_TQ, D_A), lambda j: (j, 0)),
            pl.BlockSpec((SEQ, D_A), lambda j: (0, 1), pipeline_mode=pl.Buffered(1)),
            pl.BlockSpec((SEQ, D_A), lambda j: (0, 2), pipeline_mode=pl.Buffered(1)),
        ] + bias_specs,
        out_specs=pl.BlockSpec((NA_BPS * NA_TQ, D_A), lambda j: (j, 0)),
        compiler_params=_cparams(("arbitrary",)),
        name="na_attn",
    )(pa, pa, pa, *([bias] * NA_BPS))


def _mla_up(cq, ckv, krb, gq_ref, gkv_ref, wq_ref, wk_ref, wvt_ref, cos_ref, sin_ref, q_ref, k_ref, vt_ref):
    tm = cq.shape[0]
    cqn = _rms_norm(cq, gq_ref[...]).astype(BF16)
    ckvn = _rms_norm(ckv, gkv_ref[...]).astype(BF16)
    cos = cos_ref[...]
    sin = sin_ref[...]

    def rotary(t):
        return t * cos + pltpu.roll(t, MLA_ROPE, 1) * sin

    kpe = rotary(krb).astype(BF16)
    ones = jnp.ones((MLA_VT - MLA_V, tm), BF16)
    scale = (MLA_NOPE + MLA_ROPE) ** -0.5 * LOG2_E
    k_all = jnp.dot(ckvn, wk_ref[...], preferred_element_type=F32).astype(BF16)
    for h in range(MLA_HEADS):
        qh = jnp.dot(cqn, wq_ref[:, h * MLA_QK:(h + 1) * MLA_QK], preferred_element_type=F32)
        q_ref[h, :, 0:HEAD_DIM] = (qh[:, 0:HEAD_DIM] * scale).astype(BF16)
        q_ref[h, :, HEAD_DIM:MLA_QK] = (rotary(qh[:, HEAD_DIM:MLA_QK]) * scale).astype(BF16)
        k_ref[h, :, 0:HEAD_DIM] = k_all[:, h * MLA_NOPE:(h + 1) * MLA_NOPE]
        k_ref[h, :, HEAD_DIM:MLA_QK] = kpe
        vt_ref[h, 0:MLA_V, :] = _dot_nt(wvt_ref[h * MLA_V:(h + 1) * MLA_V, :], ckvn).astype(BF16)
        vt_ref[h, MLA_V:MLA_VT, :] = ones


MLA_TQ = 256
MLA_TK = 1024


def _mla_attn_kernel(*refs, with_ada):
    qe_ref, qo_ref, k_ref, vt_ref, wo_ref, wgu_ref, wdn_ref = refs[:7]
    ada_in = refs[7:10] if with_ada else ()
    outs = refs[7 + len(ada_in):]
    oe_ref, oo_ref, wo_b_ref, wgu_b_ref, wdn_b_ref = outs[:5]
    ada_out = outs[5:6] if with_ada else ()
    s0_sc, s1_sc, s2_sc, s3_sc, m0_sc, m1_sc, m2_sc, m3_sc = outs[5 + len(ada_out):]
    cast_refs = ((wo_ref, wo_b_ref), (wgu_ref, wgu_b_ref), (wdn_ref, wdn_b_ref))
    j = pl.program_id(0)

    @pl.when(j == 0)
    def _():
        for ref in (s2_sc, s3_sc, m2_sc, m3_sc):
            ref[...] = jnp.zeros_like(ref)

    def stage(q_ref, sa_sc, ma_sc, sb_sc, mb_sc, vt_ref, o_ref):
        q = q_ref[0]
        tq = q.shape[0]
        m_prev = mb_sc[...]
        mx = jnp.full((8, tq), NEG_INF, F32)
        acc = jnp.zeros((MLA_VT, tq), F32)
        for c in range(SEQ // MLA_TK):
            keys = slice(c * MLA_TK, (c + 1) * MLA_TK)
            p = jnp.exp2(sb_sc[keys, :] - m_prev).astype(BF16)
            acc = acc + jnp.dot(vt_ref[0, :, keys], p, preferred_element_type=F32)
            s = _dot_nt(k_ref[0, keys, :], q)
            sa_sc[keys, :] = s
            mx = jnp.maximum(mx, jnp.max(s.reshape(MLA_TK // 8, 8, tq), axis=0))
        ma_sc[...] = jnp.max(mx, axis=0, keepdims=True)
        o = acc[0:MLA_V, :] * (1.0 / acc[MLA_V:MLA_V + 1, :])
        o_ref[...] = o.T

    def step(a_even, a_odd, b_even, b_odd):
        for w_ref, wb_ref in cast_refs:
            wb_ref[...] = w_ref[...].astype(BF16)
        if with_ada:
            _ada_slab(*ada_in, *ada_out)
        stage(qe_ref, *a_even, *b_even, vt_ref, oe_ref)
        stage(qo_ref, *a_odd, *b_odd, vt_ref, oo_ref)

    buf = ((s0_sc, m0_sc), (s1_sc, m1_sc), (s2_sc, m2_sc), (s3_sc, m3_sc))
    parity = lax.rem(j, 2)

    @pl.when(parity == 0)
    def _():
        step(buf[0], buf[1], buf[2], buf[3])

    @pl.when(parity == 1)
    def _():
        step(buf[2], buf[3], buf[0], buf[1])


WO_SLAB = (32, D_MODEL)
WGU_SLAB = (256, 1024)
WDN_SLAB = (64, D_MODEL)


def _mla_attn_call(l, q, k, vt, w_o, w_gu, w_down, ada=None):
    tq = MLA_TQ
    npair = SEQ // (2 * tq)
    last = MLA_HEADS * npair - 1
    nstep = last + 2
    pair_a = lambda j: jnp.minimum(j, last)
    pair_b = lambda j: jnp.maximum(j - 1, 0)
    half = jax.ShapeDtypeStruct((SEQ // 2, D_B), F32)
    out_spec = pl.BlockSpec((tq, MLA_V), lambda j: (pair_b(j) % npair, pair_b(j) // npair))

    def slab_specs(shape, slab):
        nr, nc = shape[0] // slab[0], shape[1] // slab[1]
        assert nr * slab[0] == shape[0] and nc * slab[1] == shape[1] and nr * nc <= nstep
        idx = lambda j: jnp.minimum(j, nr * nc - 1)
        return (pl.BlockSpec((None,) + slab, lambda j: (l, idx(j) // nc, idx(j) % nc)),
                pl.BlockSpec(slab, lambda j: (idx(j) // nc, idx(j) % nc)))

    wo_in, wo_out = slab_specs(w_o.shape[1:], WO_SLAB)
    wgu_in, wgu_out = slab_specs(w_gu.shape[1:], WGU_SLAB)
    wdn_in, wdn_out = slab_specs(w_down.shape[1:], WDN_SLAB)
    ada_args, ada_in, ada_out, ada_shape = (), [], (), ()
    if ada is not None:
        assert ADA_NSLAB <= nstep
        ada_args = ada
        ada_in = [
            pl.BlockSpec((D_MODEL, 128), lambda j: (0, 0)),
            pl.BlockSpec((None, D_MODEL, ADA_SLAB), lambda j: (_ada_slab_index(j)[0], 0, _ada_slab_index(j)[1])),
            pl.BlockSpec((None, 1, ADA_SLAB), lambda j: (_ada_slab_index(j)[0], 0, _ada_slab_index(j)[1])),
        ]
        ada_out = (pl.BlockSpec((1, ADA_SLAB), lambda j: (0, _ada_slab_index(j)[2])),)
        ada_shape = (jax.ShapeDtypeStruct((1, ADA_NSLAB * ADA_SLAB), F32),)
    return pl.pallas_call(
        functools.partial(_mla_attn_kernel, with_ada=ada is not None),
        out_shape=(half, half,
                   jax.ShapeDtypeStruct(w_o.shape[1:], BF16),
                   jax.ShapeDtypeStruct(w_gu.shape[1:], BF16),
                   jax.ShapeDtypeStruct(w_down.shape[1:], BF16)) + ada_shape,
        grid=(nstep,),
        in_specs=[
            pl.BlockSpec((1, tq, MLA_QK), lambda j: (pair_a(j) // npair, 2 * (pair_a(j) % npair), 0)),
            pl.BlockSpec((1, tq, MLA_QK), lambda j: (pair_a(j) // npair, 2 * (pair_a(j) % npair) + 1, 0)),
            pl.BlockSpec((1, SEQ, MLA_QK), lambda j: (pair_a(j) // npair, 0, 0), pipeline_mode=pl.Buffered(1)),
            pl.BlockSpec((1, MLA_VT, SEQ), lambda j: (pair_b(j) // npair, 0, 0), pipeline_mode=pl.Buffered(1)),
            wo_in, wgu_in, wdn_in,
        ] + ada_in,
        out_specs=(out_spec, out_spec, wo_out, wgu_out, wdn_out) + ada_out,
        scratch_shapes=[pltpu.VMEM((SEQ, tq), F32)] * 4 + [pltpu.VMEM((1, tq), F32)] * 4,
        compiler_params=_cparams(("arbitrary",)),
        name="mla_attn",
    )(q, q, k, vt, w_o, w_gu, w_down, *ada_args)


_SWA_SLOPES = tuple(2.0 ** (-8.0 * (i + 1) / SWA_HEADS) for i in range(SWA_HEADS))


SWA_NB = 8


def _swa_kernel(sink_ref, q_ref, kp_ref, kc_ref, kn_ref, vp_ref, vc_ref, vn_ref, o_ref):
    g_kv = pl.program_id(0)
    j = pl.program_id(1)
    t = SWA_BLOCK
    rows = SWA_GROUP * t
    kwin = jnp.concatenate([kp_ref[...], kc_ref[...], kn_ref[...]], axis=0)
    vwin = jnp.concatenate([vp_ref[...], vc_ref[...], vn_ref[...]], axis=0)
    ones = jnp.ones((3 * t, HEAD_DIM), BF16)
    ri = lax.broadcasted_iota(jnp.int32, (rows, 3 * t), 0)
    ci = lax.broadcasted_iota(jnp.int32, (rows, 3 * t), 1)
    grp = jnp.right_shift(ri, 7)
    dist = jnp.abs(jnp.bitwise_and(ri, t - 1) - (ci - t))
    slope_lo = jnp.where(grp == 0, _SWA_SLOPES[0], jnp.where(grp == 1, _SWA_SLOPES[1], _SWA_SLOPES[2]))
    slope_hi = jnp.where(grp == 0, _SWA_SLOPES[3], jnp.where(grp == 1, _SWA_SLOPES[4], _SWA_SLOPES[5]))
    slope = jnp.where(g_kv == 0, slope_lo, slope_hi)
    band = jnp.where(dist <= SWA_WINDOW, -slope * dist.astype(F32), NEG_INF)
    rcol = jnp.right_shift(lax.broadcasted_iota(jnp.int32, (rows, 1), 0), 7)
    base = g_kv * SWA_GROUP
    sink = jnp.where(rcol == 0, sink_ref[base], jnp.where(rcol == 1, sink_ref[base + 1], sink_ref[base + 2]))
    first_cols = jnp.where(j == 0, t, 0)
    last_cols = jnp.where(j == pl.num_programs(1) - 1, 2 * t, 3 * t)
    for b in range(SWA_NB):
        q = jnp.concatenate([q_ref[b * t:(b + 1) * t, g * t:(g + 1) * t] for g in range(SWA_GROUP)], axis=0)
        s = _dot_nt(q, kwin[b * t:(b + 3) * t, :]) + band
        if b == 0:
            s = jnp.where(ci < first_cols, NEG_INF, s)
        if b == SWA_NB - 1:
            s = jnp.where(ci >= last_cols, NEG_INF, s)
        m = jnp.maximum(jnp.max(s, axis=1, keepdims=True), sink)
        p = jnp.exp(s - m).astype(BF16)
        vext = jnp.concatenate([vwin[b * t:(b + 3) * t, :], ones], axis=1)
        acc = jnp.dot(p, vext, preferred_element_type=F32)
        l = acc[:, HEAD_DIM:HEAD_DIM + 1] + jnp.exp(sink - m)
        o = acc[:, 0:HEAD_DIM] * (1.0 / l)
        for g in range(SWA_GROUP):
            o_ref[b * t:(b + 1) * t, g * t:(g + 1) * t] = o[g * t:(g + 1) * t, :]


def _swa_call(sink, pc):
    t = SWA_BLOCK
    nb = SEQ // t
    tq = SWA_NB * t
    kcol = SWA_HEADS
    vcol = SWA_HEADS + SWA_KV_HEADS
    prev = lambda j: jnp.maximum(SWA_NB * j - 1, 0)
    nxt = lambda j: jnp.minimum(SWA_NB * (j + 1), nb - 1)
    return pl.pallas_call(
        _swa_kernel,
        out_shape=jax.ShapeDtypeStruct((SEQ, D_C), F32),
        grid=(SWA_KV_HEADS, SEQ // tq),
        in_specs=[
            pl.BlockSpec(memory_space=pltpu.SMEM),
            pl.BlockSpec((tq, SWA_GROUP * t), lambda g, j: (j, g)),
            pl.BlockSpec((t, t), lambda g, j: (prev(j), kcol + g)),
            pl.BlockSpec((tq, t), lambda g, j: (j, kcol + g)),
            pl.BlockSpec((t, t), lambda g, j: (nxt(j), kcol + g)),
            pl.BlockSpec((t, t), lambda g, j: (prev(j), vcol + g)),
            pl.BlockSpec((tq, t), lambda g, j: (j, vcol + g)),
            pl.BlockSpec((t, t), lambda g, j: (nxt(j), vcol + g)),
        ],
        out_specs=pl.BlockSpec((tq, SWA_GROUP * t), lambda g, j: (j, g)),
        compiler_params=_cparams(("parallel", "arbitrary")),
        name="swa_attn",
    )(sink, pc, pc, pc, pc, pc, pc, pc)


OUTPROJ_TM = 2 * MLA_TQ


def _outproj_kernel(ya_ref, ybe_ref, ybo_ref, yc_ref, x_ref, mod_ref, gn_ref, w_ref, lg_ref, lb_ref,
                    o_ref, u_ref):
    hm = OUTPROJ_TM // 2
    gate = 1.0 + mod_ref[2:3, :]
    for half, yb_ref in enumerate((ybe_ref, ybo_ref)):
        rows = slice(half * hm, (half + 1) * hm)
        acc = None
        for y, c0 in ((ya_ref[rows, :], 0), (yb_ref[...], D_A), (yc_ref[rows, :], D_A + D_B)):
            c1 = c0 + y.shape[1]
            yn = _rms_norm(y, gn_ref[:, c0:c1]).astype(BF16)
            part = jnp.dot(yn, w_ref[c0:c1, :], preferred_element_type=F32)
            acc = part if acc is None else acc + part
        x1 = _layer_norm(DEEPNORM_ALPHA * x_ref[rows, :] + gate * acc, lg_ref[...], lb_ref[...])
        o_ref[rows, :] = x1
        u_ref[rows, :] = (x1 * (1.0 + mod_ref[4:5, :]) + mod_ref[3:4, :]).astype(BF16)


def _outproj_call(l, ya, yb_even, yb_odd, yc, x2, mod, gn, w_o, lg, lb):
    tm = OUTPROJ_TM
    row = lambda r, w: pl.BlockSpec((r, w), lambda i: (i, 0))
    full = lambda r, w: pl.BlockSpec((None, r, w), lambda i: (l, 0, 0))
    return pl.pallas_call(
        _outproj_kernel,
        out_shape=(jax.ShapeDtypeStruct((SEQ, D_MODEL), F32), jax.ShapeDtypeStruct((SEQ, D_MODEL), BF16)),
        grid=(SEQ // tm,),
        in_specs=[row(tm, D_A), row(tm // 2, D_B), row(tm // 2, D_B), row(tm, D_C), row(tm, D_MODEL),
                  full(6, D_MODEL), full(1, D_MODEL),
                  pl.BlockSpec((D_MODEL, D_MODEL), lambda i: (0, 0), pipeline_mode=pl.Buffered(1)),
                  full(1, D_MODEL), full(1, D_MODEL)],
        out_specs=(row(tm, D_MODEL), row(tm, D_MODEL)),
        compiler_params=_cparams(("parallel",)),
        name="out_proj_ln",
    )(ya, yb_even, yb_odd, yc, x2, mod, gn, w_o, lg, lb)


FFN_TM = 1024
FFN_SUB = 512
FFN_SUB_LAST = 256
FFN_TF = 512


def _ffn_kernel(u_ref, x_hbm, mod_ref, wg_ref, wu_ref, wd_ref, lg_ref, lb_ref, o_ref, x_sc, x_sem):
    i = pl.program_id(0)
    f = pl.program_id(1)
    tm = o_ref.shape[0]
    x_copy = pltpu.make_async_copy(x_hbm.at[pl.ds(pl.multiple_of(i * tm, tm), tm), :], x_sc, x_sem)

    last = pl.num_programs(1) - 1

    def down_partial(rows):
        u = u_ref[rows, :]
        g = jnp.dot(u, wg_ref[...], preferred_element_type=F32)
        up = jnp.dot(u, wu_ref[...], preferred_element_type=F32)
        hdn = (g * (1.0 / (1.0 + jnp.exp(-g))) * up).astype(BF16)
        return jnp.dot(hdn, wd_ref[...], preferred_element_type=F32)

    @pl.when(f == 0)
    def _():
        x_copy.start()
        o_ref[...] = jnp.zeros_like(o_ref)

    @pl.when(f < last)
    def _():
        for r in range(0, tm, FFN_SUB):
            rows = slice(r, r + FFN_SUB)
            o_ref[rows, :] += down_partial(rows)

    @pl.when(f == last)
    def _():
        x_copy.wait()
        gate = 1.0 + mod_ref[5:6, :]
        for r in range(0, tm, FFN_SUB_LAST):
            rows = slice(r, r + FFN_SUB_LAST)
            acc = o_ref[rows, :] + down_partial(rows)
            o_ref[rows, :] = _layer_norm(DEEPNORM_ALPHA * x_sc[rows, :] + gate * acc, lg_ref[...], lb_ref[...])


def _ffn_call(l, u, x2, mod, w_gu, w_down, lg, lb):
    tm, tf = FFN_TM, FFN_TF
    nf = D_FF // tf
    return pl.pallas_call(
        _ffn_kernel,
        out_shape=jax.ShapeDtypeStruct((SEQ, D_MODEL), F32),
        grid=(SEQ // tm, nf),
        in_specs=[
            pl.BlockSpec((tm, D_MODEL), lambda i, f: (i, 0)),
            pl.BlockSpec(memory_space=pl.ANY),
            pl.BlockSpec((None, 6, D_MODEL), lambda i, f: (l, 0, 0)),
            pl.BlockSpec((D_MODEL, tf), lambda i, f: (0, f)),
            pl.BlockSpec((D_MODEL, tf), lambda i, f: (0, nf + f)),
            pl.BlockSpec((tf, D_MODEL), lambda i, f: (f, 0)),
            pl.BlockSpec((None, 1, D_MODEL), lambda i, f: (l, 0, 0)),
            pl.BlockSpec((None, 1, D_MODEL), lambda i, f: (l, 0, 0)),
        ],
        out_specs=pl.BlockSpec((tm, D_MODEL), lambda i, f: (i, 0)),
        scratch_shapes=[pltpu.VMEM((tm, D_MODEL), F32), pltpu.SemaphoreType.DMA(())],
        compiler_params=_cparams(("arbitrary", "arbitrary")),
        name="ffn_ln",
    )(u, x2, mod, w_gu, w_gu, w_down, lg, lb)


def _rot_half_cols(w):
    half = w.shape[-1] // 2
    return jnp.concatenate([-w[..., half:], w[..., :half]], axis=-1)


def _rope_tables():
    half = MLA_ROPE // 2
    inv = ROPE_THETA ** (-jnp.arange(half, dtype=F32) / half)
    ang = jnp.arange(SEQ, dtype=F32)[:, None] * inv[None, :]
    zeros = jnp.zeros((SEQ, MLA_ROPE), F32)
    cos = jnp.cos(ang)
    sin = jnp.sin(ang)
    return (jnp.concatenate([cos, cos, zeros], axis=1), jnp.concatenate([sin, sin, zeros], axis=1))


def _colscale():
    cs = np.ones((1, P_W), np.float32)
    cs[0, 0:D_A] = HEAD_DIM ** -0.5
    c0 = IN_A + PB_W
    cs[0, c0:c0 + D_C] = HEAD_DIM ** -0.5
    return jnp.asarray(cs)


def kernel(x, c, w_ada, b_ada, w_in, na_rpb, mla_q_norm, mla_kv_norm, mla_w_uq, mla_w_ukv,
           swa_sink, out_norm_g, w_o, ln1_g, ln1_b, w_gu, w_down, ln2_g, ln2_b):
    assert x.shape == (1, SEQ, D_MODEL)
    x2 = x.reshape(SEQ, D_MODEL)
    b_ada3 = b_ada.reshape(DEPTH, 1, -1)
    mod_head, cond_lanes = _ada_head_call(c.reshape(D_MODEL, 1), w_ada, b_ada3)
    mod = jnp.concatenate([mod_head, jnp.zeros((1, 6 * D_MODEL - ADA_HEAD), F32)], axis=1).reshape(1, 6, D_MODEL)
    cos_t, sin_t = _rope_tables()
    colscale = _colscale()
    w_p = _winprep_call(jnp.swapaxes(w_in, 1, 2))
    wq = mla_w_uq.reshape(DEPTH, MLA_Q_RANK, MLA_HEADS, MLA_NOPE + MLA_ROPE)
    wq = jnp.concatenate([wq, _rot_half_cols(wq[..., MLA_NOPE:])], axis=-1)
    wq = wq.reshape(DEPTH, MLA_Q_RANK, MLA_HEADS * MLA_QK).astype(BF16)
    wkv = mla_w_ukv.reshape(DEPTH, MLA_KV_RANK, MLA_HEADS, MLA_NOPE + MLA_V)
    wk = wkv[..., :MLA_NOPE].reshape(DEPTH, MLA_KV_RANK, MLA_HEADS * MLA_NOPE).astype(BF16)
    wvt = jnp.transpose(wkv[..., MLA_NOPE:], (0, 2, 3, 1)).reshape(DEPTH, MLA_HEADS * MLA_V, MLA_KV_RANK)
    wvt = wvt.astype(BF16)
    row3 = lambda a: a.reshape(DEPTH, 1, -1)
    gq, gkv, gn = row3(mla_q_norm), row3(mla_kv_norm), row3(out_norm_g)
    l1g, l1b, l2g, l2b = row3(ln1_g), row3(ln1_b), row3(ln2_g), row3(ln2_b)
    for l in range(DEPTH):
        pa, pc, q, k, vt = _inproj_call(l, x2, mod, l, w_p, colscale, gq, gkv, wq, wk, wvt, cos_t, sin_t)
        bias = _na_bias_call(na_rpb[l].reshape(-1))
        ya = _na_call(pa, bias)
        if l == 0:
            yb_even, yb_odd, w_o_b, w_gu_b, w_down_b, mod_rest = _mla_attn_call(
                l, q, k, vt, w_o, w_gu, w_down, ada=(cond_lanes, w_ada, b_ada3))
            mod = jnp.concatenate([mod_head, mod_rest], axis=1).reshape(DEPTH, 6, D_MODEL)
        else:
            yb_even, yb_odd, w_o_b, w_gu_b, w_down_b = _mla_attn_call(l, q, k, vt, w_o, w_gu, w_down)
        yc = _swa_call(swa_sink[l], pc)
        x2, u = _outproj_call(l, ya, yb_even, yb_odd, yc, x2, mod, gn, w_o_b, l1g, l1b)
        x2 = _ffn_call(l, u, x2, mod, w_gu_b, w_down_b, l2g, l2b)
    return x2.reshape(1, SEQ, D_MODEL)
```

```python
import functools

import numpy as np
import jax
import jax.numpy as jnp
from jax import lax
from jax.experimental import pallas as pl
from jax.experimental.pallas import tpu as pltpu

F32 = jnp.float32
BF16 = jnp.bfloat16

D_MODEL = 2048
SEQ = 8192
DEPTH = 2
GRID_W = 64
GRID_ROWS = SEQ // GRID_W
HEAD_DIM = 128
NA_HEADS = 4
NA_WIN_ROWS = 8
NA_WIN_COLS = 16
MLA_HEADS = 6
MLA_Q_RANK = 512
MLA_KV_RANK = 256
MLA_NOPE = 128
MLA_ROPE = 64
MLA_V = 128
ROPE_THETA = 10000.0
SWA_HEADS = 6
SWA_KV_HEADS = 2
SWA_GROUP = SWA_HEADS // SWA_KV_HEADS
SWA_WINDOW = 128
SWA_BLOCK = 128
D_A = NA_HEADS * HEAD_DIM
D_B = MLA_HEADS * MLA_V
D_C = SWA_HEADS * HEAD_DIM
IN_A = 3 * D_A
IN_B = MLA_Q_RANK + MLA_KV_RANK + MLA_ROPE
IN_C = (SWA_HEADS + 2 * SWA_KV_HEADS) * HEAD_DIM
D_FF = 5632
DEEPNORM_ALPHA = (2 * DEPTH) ** 0.25
LN_EPS = 1e-5
RMS_EPS = 1e-6
NEG_INF = -1e30
LOG2_E = 1.4426950408889634

PB_W = MLA_Q_RANK + MLA_KV_RANK + 2 * MLA_ROPE
P_W = IN_A + PB_W + IN_C
MLA_QK = 2 * HEAD_DIM
MLA_VT = MLA_V + 16

NA_QROWS = 4
NA_KROWS = 12
NA_TQ = NA_QROWS * GRID_W
NA_TK = NA_KROWS * GRID_W
NA_NBLK = GRID_ROWS // NA_QROWS

VMEM_LIMIT = 56 * 1024 * 1024


def _cparams(sem):
    return pltpu.CompilerParams(dimension_semantics=sem, vmem_limit_bytes=VMEM_LIMIT)


def _layer_norm(z, g, b):
    mu = jnp.mean(z, axis=-1, keepdims=True)
    zc = z - mu
    var = jnp.mean(zc * zc, axis=-1, keepdims=True)
    return zc * lax.rsqrt(var + LN_EPS) * g + b


def _rms_norm(x, g):
    ms = jnp.mean(x * x, axis=-1, keepdims=True)
    return x * lax.rsqrt(ms + RMS_EPS) * g


def _dot_nt(a, b):
    return lax.dot_general(a, b, (((1,), (1,)), ((), ())), preferred_element_type=F32)


ADA_TN = 1024
ADA_RC = 256
ADA_HEAD = 2 * D_MODEL
ADA_SLAB = 256
ADA_NSLAB = (DEPTH * 6 * D_MODEL - ADA_HEAD) // ADA_SLAB


def _ada_slab(cb_ref, w_ref, b_ref, o_ref):
    tn = o_ref.shape[-1]
    acc = jnp.zeros((8, tn), F32)
    for r in range(0, D_MODEL, ADA_RC):
        cb = jnp.concatenate([cb_ref[r:r + ADA_RC, :]] * (tn // 128), axis=1)
        prod = w_ref[r:r + ADA_RC, :] * cb
        acc = acc + jnp.sum(prod.reshape(ADA_RC // 8, 8, tn), axis=0)
    o_ref[...] = jnp.sum(acc, axis=0, keepdims=True) + b_ref[...]


def _ada_head_kernel(c_ref, w_ref, b_ref, o_ref, cb_ref):
    for r in range(0, D_MODEL, ADA_RC):
        c = c_ref[r:r + ADA_RC, :]
        cond = c * (1.0 / (1.0 + jnp.exp(-c)))
        cb_ref[r:r + ADA_RC, :] = jnp.broadcast_to(cond, (ADA_RC, 128))
    _ada_slab(cb_ref, w_ref, b_ref, o_ref)


def _ada_head_call(c_col, w_ada, b_ada3):
    return pl.pallas_call(
        _ada_head_kernel,
        out_shape=(jax.ShapeDtypeStruct((1, ADA_HEAD), F32), jax.ShapeDtypeStruct((D_MODEL, 128), F32)),
        grid=(ADA_HEAD // ADA_TN,),
        in_specs=[
            pl.BlockSpec((D_MODEL, 1), lambda j: (0, 0)),
            pl.BlockSpec((None, D_MODEL, ADA_TN), lambda j: (0, 0, j)),
            pl.BlockSpec((None, 1, ADA_TN), lambda j: (0, 0, j)),
        ],
        out_specs=(pl.BlockSpec((1, ADA_TN), lambda j: (0, j)),
                   pl.BlockSpec((D_MODEL, 128), lambda j: (0, 0))),
        compiler_params=_cparams(("arbitrary",)),
        name="ada_head",
    )(c_col, w_ada, b_ada3)


def _ada_slab_index(j):
    s = jnp.minimum(j, ADA_NSLAB - 1)
    n0 = (6 * D_MODEL - ADA_HEAD) // ADA_SLAB
    return jnp.where(s < n0, 0, 1), jnp.where(s < n0, s + ADA_HEAD // ADA_SLAB, s - n0), s


WPREP_TN = 512
KR0 = IN_A + MLA_Q_RANK + MLA_KV_RANK


def _winprep_kernel(w_ref, o_ref):
    half = MLA_ROPE // 2
    kr1 = KR0 + MLA_ROPE
    o_ref[0:kr1, :] = w_ref[0:kr1, :].astype(BF16)
    o_ref[kr1:kr1 + half, :] = (-w_ref[KR0 + half:kr1, :]).astype(BF16)
    o_ref[kr1 + half:kr1 + MLA_ROPE, :] = w_ref[KR0:KR0 + half, :].astype(BF16)
    o_ref[kr1 + MLA_ROPE:, :] = w_ref[kr1:, :].astype(BF16)


def _winprep_call(w_in_t):
    tn = WPREP_TN
    return pl.pallas_call(
        _winprep_kernel,
        out_shape=jax.ShapeDtypeStruct((DEPTH, P_W, D_MODEL), BF16),
        grid=(DEPTH, D_MODEL // tn),
        in_specs=[pl.BlockSpec((None, w_in_t.shape[1], tn), lambda l, i: (l, 0, i))],
        out_specs=pl.BlockSpec((None, P_W, tn), lambda l, i: (l, 0, i)),
        compiler_params=_cparams(("parallel", "parallel")),
        name="w_in_prep",
    )(w_in_t)


INPROJ_TM = 512
_INPROJ_CHUNKS = (
    (0, 512, 0, 0), (512, 1024, 0, 512), (1024, 1536, 0, 1024),
    (2432, 2944, 1, 0), (2944, 3456, 1, 512), (3456, 3712, 1, 1024),
)
_PB0 = IN_A


def _inproj_kernel(x_ref, mod_ref, w_ref, cs_ref, gq_ref, gkv_ref, wq_ref, wk_ref, wvt_ref, cos_ref, sin_ref,
                   oa_ref, oc_ref, q_ref, k_ref, vt_ref):
    outs = (oa_ref, oc_ref)
    sh = mod_ref[0:1, :]
    sc = mod_ref[1:2, :]
    u = (x_ref[...] * (1.0 + sc) + sh).astype(BF16)
    for c0, c1, oi, off in _INPROJ_CHUNKS:
        acc = _dot_nt(u, w_ref[c0:c1, :]) * cs_ref[:, c0:c1]
        outs[oi][:, off:off + (c1 - c0)] = acc.astype(outs[oi].dtype)
    cq = _dot_nt(u, w_ref[_PB0:_PB0 + MLA_Q_RANK, :])
    ckv_kr = _dot_nt(u, w_ref[_PB0 + MLA_Q_RANK:_PB0 + PB_W, :])
    _mla_up(cq, ckv_kr[:, 0:MLA_KV_RANK], ckv_kr[:, MLA_KV_RANK:], gq_ref, gkv_ref, wq_ref, wk_ref, wvt_ref,
            cos_ref, sin_ref, q_ref, k_ref, vt_ref)


def _inproj_call(l, x2, mod, lmod, w_p, colscale, gq, gkv, wq, wk, wvt, cos_t, sin_t):
    tm = INPROJ_TM
    hsd = jax.ShapeDtypeStruct((MLA_HEADS, SEQ, MLA_QK), BF16)
    hspec = pl.BlockSpec((MLA_HEADS, tm, MLA_QK), lambda i: (0, i, 0))
    layer = lambda r, c: pl.BlockSpec((None, r, c), lambda i: (l, 0, 0))
    return pl.pallas_call(
        _inproj_kernel,
        out_shape=(jax.ShapeDtypeStruct((SEQ, IN_A), BF16),
                   jax.ShapeDtypeStruct((SEQ, IN_C), BF16),
                   hsd, hsd, jax.ShapeDtypeStruct((MLA_HEADS, MLA_VT, SEQ), BF16)),
        grid=(SEQ // tm,),
        in_specs=[
            pl.BlockSpec((tm, D_MODEL), lambda i: (i, 0)),
            pl.BlockSpec((None, 6, D_MODEL), lambda i: (lmod, 0, 0)),
            pl.BlockSpec((None, P_W, D_MODEL), lambda i: (l, 0, 0), pipeline_mode=pl.Buffered(1)),
            pl.BlockSpec((1, P_W), lambda i: (0, 0)),
            layer(1, MLA_Q_RANK), layer(1, MLA_KV_RANK),
            layer(MLA_Q_RANK, MLA_HEADS * MLA_QK), layer(MLA_KV_RANK, MLA_HEADS * MLA_NOPE),
            layer(MLA_HEADS * MLA_V, MLA_KV_RANK),
            pl.BlockSpec((tm, HEAD_DIM), lambda i: (i, 0)),
            pl.BlockSpec((tm, HEAD_DIM), lambda i: (i, 0)),
        ],
        out_specs=(pl.BlockSpec((tm, IN_A), lambda i: (i, 0)),
                   pl.BlockSpec((tm, IN_C), lambda i: (i, 0)),
                   hspec, hspec, pl.BlockSpec((MLA_HEADS, MLA_VT, tm), lambda i: (0, 0, i))),
        compiler_params=_cparams(("parallel",)),
        name="in_proj",
    )(x2, mod, w_p, colscale, gq, gkv, wq, wk, wvt, cos_t, sin_t)


def _na_block_rule(btype, i, j):
    if btype == 0:
        r0 = max(i - NA_WIN_ROWS // 2, 0)
        valid = r0 <= j < r0 + NA_WIN_ROWS
        ro = j - i + (NA_WIN_ROWS - 1)
    elif btype == 1:
        valid = i <= j < i + NA_WIN_ROWS
        ro = j - i + (NA_WIN_ROWS - 1) - NA_WIN_ROWS // 2
    else:
        r = GRID_ROWS - NA_QROWS + i
        ks = GRID_ROWS - NA_KROWS
        r0 = min(r - NA_WIN_ROWS // 2, GRID_ROWS - NA_WIN_ROWS)
        valid = r0 <= ks + j < r0 + NA_WIN_ROWS
        ro = ks + j - r + (NA_WIN_ROWS - 1)
    return ro if valid else None


def _na_bias_kernel(rpb_ref, o_ref):
    h = pl.program_id(0)
    n_ro = 2 * NA_WIN_ROWS - 1
    n_co = 2 * NA_WIN_COLS - 1
    cq = lax.broadcasted_iota(jnp.int32, (GRID_W, GRID_W), 0)
    ck = lax.broadcasted_iota(jnp.int32, (GRID_W, GRID_W), 1)
    c0 = jnp.clip(cq - NA_WIN_COLS // 2, 0, GRID_W - NA_WIN_COLS)
    coff = jnp.clip(ck - cq, -(NA_WIN_COLS - 1), NA_WIN_COLS - 1) + (NA_WIN_COLS - 1)
    neg = jnp.full((GRID_W, GRID_W), NEG_INF, F32)
    tblocks = []
    for ro in range(n_ro):
        tb = neg
        for j in range(n_co):
            tb = jnp.where(coff == j, rpb_ref[h * (n_ro * n_co) + ro * n_co + j], tb)
        inside = jnp.where(ck >= c0, jnp.where(ck < c0 + NA_WIN_COLS, 1, 0), 0)
        tblocks.append(jnp.where(inside == 1, tb, neg))
    for btype in range(3):
        for i in range(NA_QROWS):
            for jp in range(NA_KROWS // 2):
                pair = []
                for j in (2 * jp, 2 * jp + 1):
                    ro = _na_block_rule(btype, i, j)
                    pair.append(neg if ro is None else tblocks[ro])
                o_ref[0, btype, i * GRID_W:(i + 1) * GRID_W, jp * 128:(jp + 1) * 128] = (
                    jnp.concatenate(pair, axis=1))


def _na_bias_call(rpb_flat):
    return pl.pallas_call(
        _na_bias_kernel,
        out_shape=jax.ShapeDtypeStruct((NA_HEADS, 3, NA_TQ, NA_TK), F32),
        grid=(NA_HEADS,),
        in_specs=[pl.BlockSpec(memory_space=pltpu.SMEM)],
        out_specs=pl.BlockSpec((1, 3, NA_TQ, NA_TK), lambda h: (h, 0, 0, 0)),
        compiler_params=_cparams(("parallel",)),
        name="na_bias",
    )(rpb_flat)


NA_BPS = 2


def _na_kernel(q_ref, k_ref, v_ref, *rest):
    bias_refs, o_ref = rest[:NA_BPS], rest[NA_BPS]
    j = pl.program_id(0)
    ones = jnp.ones((NA_TK, HEAD_DIM), BF16)
    for sub, bias_ref in enumerate(bias_refs):
        b = NA_BPS * j + sub
        ks = jnp.clip(NA_QROWS * b - NA_WIN_ROWS // 2, 0, GRID_ROWS - NA_KROWS) * GRID_W
        ks = pl.multiple_of(ks, GRID_W)
        rows = slice(sub * NA_TQ, (sub + 1) * NA_TQ)
        for h in range(NA_HEADS):
            cols = slice(h * HEAD_DIM, (h + 1) * HEAD_DIM)
            s = _dot_nt(q_ref[rows, cols], k_ref[pl.ds(ks, NA_TK), cols]) + bias_ref[h, 0]
            m = jnp.max(s, axis=1, keepdims=True)
            p = jnp.exp(s - m).astype(BF16)
            vext = jnp.concatenate([v_ref[pl.ds(ks, NA_TK), cols], ones], axis=1)
            acc = jnp.dot(p, vext, preferred_element_type=F32)
            o_ref[rows, cols] = acc[:, 0:HEAD_DIM] * (1.0 / acc[:, HEAD_DIM:HEAD_DIM + 1])


def _na_call(pa, bias):
    def btype(b):
        return jnp.where(b == 0, 0, jnp.where(b == NA_NBLK - 1, 2, 1))
    bias_specs = [pl.BlockSpec((NA_HEADS, 1, NA_TQ, NA_TK),
                               functools.partial(lambda j, sub: (0, btype(NA_BPS * j + sub), 0, 0), sub=sub))
                  for sub in range(NA_BPS)]
    return pl.pallas_call(
        _na_kernel,
        out_shape=jax.ShapeDtypeStruct((SEQ, D_A), F32),
        grid=(NA_NBLK // NA_BPS,),
        in_specs=[
            pl.BlockSpec((NA_BPS * NA_TQ, D_A), lambda j: (j, 0)),
            pl.BlockSpec((SEQ, D_A), lambda j: (0, 1), pipeline_mode=pl.Buffered(1)),
            pl.BlockSpec((SEQ, D_A), lambda j: (0, 2), pipeline_mode=pl.Buffered(1)),
        ] + bias_specs,
        out_specs=pl.BlockSpec((NA_BPS * NA_TQ, D_A), lambda j: (j, 0)),
        compiler_params=_cparams(("arbitrary",)),
        name="na_attn",
    )(pa, pa, pa, *([bias] * NA_BPS))


def _mla_up(cq, ckv, krb, gq_ref, gkv_ref, wq_ref, wk_ref, wvt_ref, cos_ref, sin_ref, q_ref, k_ref, vt_ref):
    tm = cq.shape[0]
    cqn = _rms_norm(cq, gq_ref[...]).astype(BF16)
    ckvn = _rms_norm(ckv, gkv_ref[...]).astype(BF16)
    cos = cos_ref[...]
    sin = sin_ref[...]

    def rotary(t):
        return t * cos + pltpu.roll(t, MLA_ROPE, 1) * sin

    kpe = rotary(krb).astype(BF16)
    ones = jnp.ones((MLA_VT - MLA_V, tm), BF16)
    scale = (MLA_NOPE + MLA_ROPE) ** -0.5 * LOG2_E
    k_all = jnp.dot(ckvn, wk_ref[...], preferred_element_type=F32).astype(BF16)
    for h in range(MLA_HEADS):
        qh = jnp.dot(cqn, wq_ref[:, h * MLA_QK:(h + 1) * MLA_QK], preferred_element_type=F32)
        q_ref[h, :, 0:HEAD_DIM] = (qh[:, 0:HEAD_DIM] * scale).astype(BF16)
        q_ref[h, :, HEAD_DIM:MLA_QK] = (rotary(qh[:, HEAD_DIM:MLA_QK]) * scale).astype(BF16)
        k_ref[h, :, 0:HEAD_DIM] = k_all[:, h * MLA_NOPE:(h + 1) * MLA_NOPE]
        k_ref[h, :, HEAD_DIM:MLA_QK] = kpe
        vt_ref[h, 0:MLA_V, :] = _dot_nt(wvt_ref[h * MLA_V:(h + 1) * MLA_V, :], ckvn).astype(BF16)
        vt_ref[h, MLA_V:MLA_VT, :] = ones


MLA_TQ = 256
MLA_TK = 1024


def _mla_attn_kernel(*refs, with_ada):
    qe_ref, qo_ref, k_ref, vt_ref, wo_ref, wgu_ref, wdn_ref = refs[:7]
    ada_in = refs[7:10] if with_ada else ()
    outs = refs[7 + len(ada_in):]
    oe_ref, oo_ref, wo_b_ref, wgu_b_ref, wdn_b_ref = outs[:5]
    ada_out = outs[5:6] if with_ada else ()
    s0_sc, s1_sc, s2_sc, s3_sc, m0_sc, m1_sc, m2_sc, m3_sc = outs[5 + len(ada_out):]
    cast_refs = ((wo_ref, wo_b_ref), (wgu_ref, wgu_b_ref), (wdn_ref, wdn_b_ref))
    j = pl.program_id(0)

    @pl.when(j == 0)
    def _():
        for ref in (s2_sc, s3_sc, m2_sc, m3_sc):
            ref[...] = jnp.zeros_like(ref)

    def stage(q_ref, sa_sc, ma_sc, sb_sc, mb_sc, vt_ref, o_ref):
        q = q_ref[0]
        tq = q.shape[0]
        m_prev = mb_sc[...]
        mx = jnp.full((8, tq), NEG_INF, F32)
        acc = jnp.zeros((MLA_VT, tq), F32)
        for c in range(SEQ // MLA_TK):
            keys = slice(c * MLA_TK, (c + 1) * MLA_TK)
            p = jnp.exp2(sb_sc[keys, :] - m_prev).astype(BF16)
            acc = acc + jnp.dot(vt_ref[0, :, keys], p, preferred_element_type=F32)
            s = _dot_nt(k_ref[0, keys, :], q)
            sa_sc[keys, :] = s
            mx = jnp.maximum(mx, jnp.max(s.reshape(MLA_TK // 8, 8, tq), axis=0))
        ma_sc[...] = jnp.max(mx, axis=0, keepdims=True)
        o = acc[0:MLA_V, :] * (1.0 / acc[MLA_V:MLA_V + 1, :])
        o_ref[...] = o.T

    def step(a_even, a_odd, b_even, b_odd):
        for w_ref, wb_ref in cast_refs:
            wb_ref[...] = w_ref[...].astype(BF16)
        if with_ada:
            _ada_slab(*ada_in, *ada_out)
        stage(qe_ref, *a_even, *b_even, vt_ref, oe_ref)
        stage(qo_ref, *a_odd, *b_odd, vt_ref, oo_ref)

    buf = ((s0_sc, m0_sc), (s1_sc, m1_sc), (s2_sc, m2_sc), (s3_sc, m3_sc))
    parity = lax.rem(j, 2)

    @pl.when(parity == 0)
    def _():
        step(buf[0], buf[1], buf[2], buf[3])

    @pl.when(parity == 1)
    def _():
        step(buf[2], buf[3], buf[0], buf[1])


WO_SLAB = (32, D_MODEL)
WGU_SLAB = (256, 1024)
WDN_SLAB = (64, D_MODEL)


def _mla_attn_call(l, q, k, vt, w_o, w_gu, w_down, ada=None):
    tq = MLA_TQ
    npair = SEQ // (2 * tq)
    last = MLA_HEADS * npair - 1
    nstep = last + 2
    pair_a = lambda j: jnp.minimum(j, last)
    pair_b = lambda j: jnp.maximum(j - 1, 0)
    half = jax.ShapeDtypeStruct((SEQ // 2, D_B), F32)
    out_spec = pl.BlockSpec((tq, MLA_V), lambda j: (pair_b(j) % npair, pair_b(j) // npair))

    def slab_specs(shape, slab):
        nr, nc = shape[0] // slab[0], shape[1] // slab[1]
        assert nr * slab[0] == shape[0] and nc * slab[1] == shape[1] and nr * nc <= nstep
        idx = lambda j: jnp.minimum(j, nr * nc - 1)
        return (pl.BlockSpec((None,) + slab, lambda j: (l, idx(j) // nc, idx(j) % nc)),
                pl.BlockSpec(slab, lambda j: (idx(j) // nc, idx(j) % nc)))

    wo_in, wo_out = slab_specs(w_o.shape[1:], WO_SLAB)
    wgu_in, wgu_out = slab_specs(w_gu.shape[1:], WGU_SLAB)
    wdn_in, wdn_out = slab_specs(w_down.shape[1:], WDN_SLAB)
    ada_args, ada_in, ada_out, ada_shape = (), [], (), ()
    if ada is not None:
        assert ADA_NSLAB <= nstep
        ada_args = ada
        ada_in = [
            pl.BlockSpec((D_MODEL, 128), lambda j: (0, 0)),
            pl.BlockSpec((None, D_MODEL, ADA_SLAB), lambda j: (_ada_slab_index(j)[0], 0, _ada_slab_index(j)[1])),
            pl.BlockSpec((None, 1, ADA_SLAB), lambda j: (_ada_slab_index(j)[0], 0, _ada_slab_index(j)[1])),
        ]
        ada_out = (pl.BlockSpec((1, ADA_SLAB), lambda j: (0, _ada_slab_index(j)[2])),)
        ada_shape = (jax.ShapeDtypeStruct((1, ADA_NSLAB * ADA_SLAB), F32),)
    return pl.pallas_call(
        functools.partial(_mla_attn_kernel, with_ada=ada is not None),
        out_shape=(half, half,
                   jax.ShapeDtypeStruct(w_o.shape[1:], BF16),
                   jax.ShapeDtypeStruct(w_gu.shape[1:], BF16),
                   jax.ShapeDtypeStruct(w_down.shape[1:], BF16)) + ada_shape,
        grid=(nstep,),
        in_specs=[
            pl.BlockSpec((1, tq, MLA_QK), lambda j: (pair_a(j) // npair, 2 * (pair_a(j) % npair), 0)),
            pl.BlockSpec((1, tq, MLA_QK), lambda j: (pair_a(j) // npair, 2 * (pair_a(j) % npair) + 1, 0)),
            pl.BlockSpec((1, SEQ, MLA_QK), lambda j: (pair_a(j) // npair, 0, 0)),
            pl.BlockSpec((1, MLA_VT, SEQ), lambda j: (pair_b(j) // npair, 0, 0), pipeline_mode=pl.Buffered(1)),
            wo_in, wgu_in, wdn_in,
        ] + ada_in,
        out_specs=(out_spec, out_spec, wo_out, wgu_out, wdn_out) + ada_out,
        scratch_shapes=[pltpu.VMEM((SEQ, tq), F32)] * 4 + [pltpu.VMEM((1, tq), F32)] * 4,
        compiler_params=_cparams(("arbitrary",)),
        name="mla_attn",
    )(q, q, k, vt, w_o, w_gu, w_down, *ada_args)


_SWA_SLOPES = tuple(2.0 ** (-8.0 * (i + 1) / SWA_HEADS) for i in range(SWA_HEADS))


SWA_NB = 8


def _swa_kernel(sink_ref, q_ref, kp_ref, kc_ref, kn_ref, vp_ref, vc_ref, vn_ref, o_ref):
    g_kv = pl.program_id(0)
    j = pl.program_id(1)
    t = SWA_BLOCK
    rows = SWA_GROUP * t
    kwin = jnp.concatenate([kp_ref[...], kc_ref[...], kn_ref[...]], axis=0)
    vwin = jnp.concatenate([vp_ref[...], vc_ref[...], vn_ref[...]], axis=0)
    ones = jnp.ones((3 * t, HEAD_DIM), BF16)
    ri = lax.broadcasted_iota(jnp.int32, (rows, 3 * t), 0)
    ci = lax.broadcasted_iota(jnp.int32, (rows, 3 * t), 1)
    grp = jnp.right_shift(ri, 7)
    dist = jnp.abs(jnp.bitwise_and(ri, t - 1) - (ci - t))
    slope_lo = jnp.where(grp == 0, _SWA_SLOPES[0], jnp.where(grp == 1, _SWA_SLOPES[1], _SWA_SLOPES[2]))
    slope_hi = jnp.where(grp == 0, _SWA_SLOPES[3], jnp.where(grp == 1, _SWA_SLOPES[4], _SWA_SLOPES[5]))
    slope = jnp.where(g_kv == 0, slope_lo, slope_hi)
    band = jnp.where(dist <= SWA_WINDOW, -slope * dist.astype(F32), NEG_INF)
    rcol = jnp.right_shift(lax.broadcasted_iota(jnp.int32, (rows, 1), 0), 7)
    base = g_kv * SWA_GROUP
    sink = jnp.where(rcol == 0, sink_ref[base], jnp.where(rcol == 1, sink_ref[base + 1], sink_ref[base + 2]))
    first_cols = jnp.where(j == 0, t, 0)
    last_cols = jnp.where(j == pl.num_programs(1) - 1, 2 * t, 3 * t)
    for b in range(SWA_NB):
        q = jnp.concatenate([q_ref[b * t:(b + 1) * t, g * t:(g + 1) * t] for g in range(SWA_GROUP)], axis=0)
        s = _dot_nt(q, kwin[b * t:(b + 3) * t, :]) + band
        if b == 0:
            s = jnp.where(ci < first_cols, NEG_INF, s)
        if b == SWA_NB - 1:
            s = jnp.where(ci >= last_cols, NEG_INF, s)
        m = jnp.maximum(jnp.max(s, axis=1, keepdims=True), sink)
        p = jnp.exp(s - m).astype(BF16)
        vext = jnp.concatenate([vwin[b * t:(b + 3) * t, :], ones], axis=1)
        acc = jnp.dot(p, vext, preferred_element_type=F32)
        l = acc[:, HEAD_DIM:HEAD_DIM + 1] + jnp.exp(sink - m)
        o = acc[:, 0:HEAD_DIM] * (1.0 / l)
        for g in range(SWA_GROUP):
            o_ref[b * t:(b + 1) * t, g * t:(g + 1) * t] = o[g * t:(g + 1) * t, :]


def _swa_call(sink, pc):
    t = SWA_BLOCK
    nb = SEQ // t
    tq = SWA_NB * t
    kcol = SWA_HEADS
    vcol = SWA_HEADS + SWA_KV_HEADS
    prev = lambda j: jnp.maximum(SWA_NB * j - 1, 0)
    nxt = lambda j: jnp.minimum(SWA_NB * (j + 1), nb - 1)
    return pl.pallas_call(
        _swa_kernel,
        out_shape=jax.ShapeDtypeStruct((SEQ, D_C), F32),
        grid=(SWA_KV_HEADS, SEQ // tq),
        in_specs=[
            pl.BlockSpec(memory_space=pltpu.SMEM),
            pl.BlockSpec((tq, SWA_GROUP * t), lambda g, j: (j, g)),
            pl.BlockSpec((t, t), lambda g, j: (prev(j), kcol + g)),
            pl.BlockSpec((tq, t), lambda g, j: (j, kcol + g)),
            pl.BlockSpec((t, t), lambda g, j: (nxt(j), kcol + g)),
            pl.BlockSpec((t, t), lambda g, j: (prev(j), vcol + g)),
            pl.BlockSpec((tq, t), lambda g, j: (j, vcol + g)),
            pl.BlockSpec((t, t), lambda g, j: (nxt(j), vcol + g)),
        ],
        out_specs=pl.BlockSpec((tq, SWA_GROUP * t), lambda g, j: (j, g)),
        compiler_params=_cparams(("parallel", "arbitrary")),
        name="swa_attn",
    )(sink, pc, pc, pc, pc, pc, pc, pc)


OUTPROJ_TM = 2 * MLA_TQ


def _outproj_kernel(ya_ref, ybe_ref, ybo_ref, yc_ref, x_ref, mod_ref, gn_ref, w_ref, lg_ref, lb_ref,
                    o_ref, u_ref):
    hm = OUTPROJ_TM // 2
    gate = 1.0 + mod_ref[2:3, :]
    for half, yb_ref in enumerate((ybe_ref, ybo_ref)):
        rows = slice(half * hm, (half + 1) * hm)
        acc = None
        for y, c0 in ((ya_ref[rows, :], 0), (yb_ref[...], D_A), (yc_ref[rows, :], D_A + D_B)):
            c1 = c0 + y.shape[1]
            yn = _rms_norm(y, gn_ref[:, c0:c1]).astype(BF16)
            part = jnp.dot(yn, w_ref[c0:c1, :], preferred_element_type=F32)
            acc = part if acc is None else acc + part
        x1 = _layer_norm(DEEPNORM_ALPHA * x_ref[rows, :] + gate * acc, lg_ref[...], lb_ref[...])
        o_ref[rows, :] = x1
        u_ref[rows, :] = (x1 * (1.0 + mod_ref[4:5, :]) + mod_ref[3:4, :]).astype(BF16)


def _outproj_call(l, ya, yb_even, yb_odd, yc, x2, mod, gn, w_o, lg, lb):
    tm = OUTPROJ_TM
    row = lambda r, w: pl.BlockSpec((r, w), lambda i: (i, 0))
    full = lambda r, w: pl.BlockSpec((None, r, w), lambda i: (l, 0, 0))
    return pl.pallas_call(
        _outproj_kernel,
        out_shape=(jax.ShapeDtypeStruct((SEQ, D_MODEL), F32), jax.ShapeDtypeStruct((SEQ, D_MODEL), BF16)),
        grid=(SEQ // tm,),
        in_specs=[row(tm, D_A), row(tm // 2, D_B), row(tm // 2, D_B), row(tm, D_C), row(tm, D_MODEL),
                  full(6, D_MODEL), full(1, D_MODEL),
                  pl.BlockSpec((D_MODEL, D_MODEL), lambda i: (0, 0), pipeline_mode=pl.Buffered(1)),
                  full(1, D_MODEL), full(1, D_MODEL)],
        out_specs=(row(tm, D_MODEL), row(tm, D_MODEL)),
        compiler_params=_cparams(("parallel",)),
        name="out_proj_ln",
    )(ya, yb_even, yb_odd, yc, x2, mod, gn, w_o, lg, lb)


FFN_TM = 1024
FFN_SUB = 512
FFN_SUB_LAST = 256
FFN_TF = 512


def _ffn_kernel(u_ref, x_hbm, mod_ref, wg_ref, wu_ref, wd_ref, lg_ref, lb_ref, o_ref, x_sc, x_sem):
    i = pl.program_id(0)
    f = pl.program_id(1)
    tm = o_ref.shape[0]
    x_copy = pltpu.make_async_copy(x_hbm.at[pl.ds(pl.multiple_of(i * tm, tm), tm), :], x_sc, x_sem)

    last = pl.num_programs(1) - 1

    def down_partial(rows):
        u = u_ref[rows, :]
        g = jnp.dot(u, wg_ref[...], preferred_element_type=F32)
        up = jnp.dot(u, wu_ref[...], preferred_element_type=F32)
        hdn = (g * (1.0 / (1.0 + jnp.exp(-g))) * up).astype(BF16)
        return jnp.dot(hdn, wd_ref[...], preferred_element_type=F32)

    @pl.when(f == 0)
    def _():
        x_copy.start()
        o_ref[...] = jnp.zeros_like(o_ref)

    @pl.when(f < last)
    def _():
        for r in range(0, tm, FFN_SUB):
            rows = slice(r, r + FFN_SUB)
            o_ref[rows, :] += down_partial(rows)

    @pl.when(f == last)
    def _():
        x_copy.wait()
        gate = 1.0 + mod_ref[5:6, :]
        for r in range(0, tm, FFN_SUB_LAST):
            rows = slice(r, r + FFN_SUB_LAST)
            acc = o_ref[rows, :] + down_partial(rows)
            o_ref[rows, :] = _layer_norm(DEEPNORM_ALPHA * x_sc[rows, :] + gate * acc, lg_ref[...], lb_ref[...])


def _ffn_call(l, u, x2, mod, w_gu, w_down, lg, lb):
    tm, tf = FFN_TM, FFN_TF
    nf = D_FF // tf
    return pl.pallas_call(
        _ffn_kernel,
        out_shape=jax.ShapeDtypeStruct((SEQ, D_MODEL), F32),
        grid=(SEQ // tm, nf),
        in_specs=[
            pl.BlockSpec((tm, D_MODEL), lambda i, f: (i, 0)),
            pl.BlockSpec(memory_space=pl.ANY),
            pl.BlockSpec((None, 6, D_MODEL), lambda i, f: (l, 0, 0)),
            pl.BlockSpec((D_MODEL, tf), lambda i, f: (0, f)),
            pl.BlockSpec((D_MODEL, tf), lambda i, f: (0, nf + f)),
            pl.BlockSpec((tf, D_MODEL), lambda i, f: (f, 0)),
            pl.BlockSpec((None, 1, D_MODEL), lambda i, f: (l, 0, 0)),
            pl.BlockSpec((None, 1, D_MODEL), lambda i, f: (l, 0, 0)),
        ],
        out_specs=pl.BlockSpec((tm, D_MODEL), lambda i, f: (i, 0)),
        scratch_shapes=[pltpu.VMEM((tm, D_MODEL), F32), pltpu.SemaphoreType.DMA(())],
        compiler_params=_cparams(("arbitrary", "arbitrary")),
        name="ffn_ln",
    )(u, x2, mod, w_gu, w_gu, w_down, lg, lb)


def _rot_half_cols(w):
    half = w.shape[-1] // 2
    return jnp.concatenate([-w[..., half:], w[..., :half]], axis=-1)


def _rope_tables():
    half = MLA_ROPE // 2
    inv = ROPE_THETA ** (-jnp.arange(half, dtype=F32) / half)
    ang = jnp.arange(SEQ, dtype=F32)[:, None] * inv[None, :]
    zeros = jnp.zeros((SEQ, MLA_ROPE), F32)
    cos = jnp.cos(ang)
    sin = jnp.sin(ang)
    return (jnp.concatenate([cos, cos, zeros], axis=1), jnp.concatenate([sin, sin, zeros], axis=1))


def _colscale():
    cs = np.ones((1, P_W), np.float32)
    cs[0, 0:D_A] = HEAD_DIM ** -0.5
    c0 = IN_A + PB_W
    cs[0, c0:c0 + D_C] = HEAD_DIM ** -0.5
    return jnp.asarray(cs)


def kernel(x, c, w_ada, b_ada, w_in, na_rpb, mla_q_norm, mla_kv_norm, mla_w_uq, mla_w_ukv,
           swa_sink, out_norm_g, w_o, ln1_g, ln1_b, w_gu, w_down, ln2_g, ln2_b):
    assert x.shape == (1, SEQ, D_MODEL)
    x2 = x.reshape(SEQ, D_MODEL)
    b_ada3 = b_ada.reshape(DEPTH, 1, -1)
    mod_head, cond_lanes = _ada_head_call(c.reshape(D_MODEL, 1), w_ada, b_ada3)
    mod = jnp.concatenate([mod_head, jnp.zeros((1, 6 * D_MODEL - ADA_HEAD), F32)], axis=1).reshape(1, 6, D_MODEL)
    cos_t, sin_t = _rope_tables()
    colscale = _colscale()
    w_p = _winprep_call(jnp.swapaxes(w_in, 1, 2))
    wq = mla_w_uq.reshape(DEPTH, MLA_Q_RANK, MLA_HEADS, MLA_NOPE + MLA_ROPE)
    wq = jnp.concatenate([wq, _rot_half_cols(wq[..., MLA_NOPE:])], axis=-1)
    wq = wq.reshape(DEPTH, MLA_Q_RANK, MLA_HEADS * MLA_QK).astype(BF16)
    wkv = mla_w_ukv.reshape(DEPTH, MLA_KV_RANK, MLA_HEADS, MLA_NOPE + MLA_V)
    wk = wkv[..., :MLA_NOPE].reshape(DEPTH, MLA_KV_RANK, MLA_HEADS * MLA_NOPE).astype(BF16)
    wvt = jnp.transpose(wkv[..., MLA_NOPE:], (0, 2, 3, 1)).reshape(DEPTH, MLA_HEADS * MLA_V, MLA_KV_RANK)
    wvt = wvt.astype(BF16)
    row3 = lambda a: a.reshape(DEPTH, 1, -1)
    gq, gkv, gn = row3(mla_q_norm), row3(mla_kv_norm), row3(out_norm_g)
    l1g, l1b, l2g, l2b = row3(ln1_g), row3(ln1_b), row3(ln2_g), row3(ln2_b)
    for l in range(DEPTH):
        pa, pc, q, k, vt = _inproj_call(l, x2, mod, l, w_p, colscale, gq, gkv, wq, wk, wvt, cos_t, sin_t)
        bias = _na_bias_call(na_rpb[l].reshape(-1))
        ya = _na_call(pa, bias)
        if l == 0:
            yb_even, yb_odd, w_o_b, w_gu_b, w_down_b, mod_rest = _mla_attn_call(
                l, q, k, vt, w_o, w_gu, w_down, ada=(cond_lanes, w_ada, b_ada3))
            mod = jnp.concatenate([mod_head, mod_rest], axis=1).reshape(DEPTH, 6, D_MODEL)
        else:
            yb_even, yb_odd, w_o_b, w_gu_b, w_down_b = _mla_attn_call(l, q, k, vt, w_o, w_gu, w_down)
        yc = _swa_call(swa_sink[l], pc)
        x2, u = _outproj_call(l, ya, yb_even, yb_odd, yc, x2, mod, gn, w_o_b, l1g, l1b)
        x2 = _ffn_call(l, u, x2, mod, w_gu_b, w_down_b, l2g, l2b)
    return x2.reshape(1, SEQ, D_MODEL)
```

```python
import functools

import numpy as np
import jax
import jax.numpy as jnp
from jax import lax
from jax.experimental import pallas as pl
from jax.experimental.pallas import tpu as pltpu

F32 = jnp.float32
BF16 = jnp.bfloat16

D_MODEL = 2048
SEQ = 8192
DEPTH = 2
GRID_W = 64
GRID_ROWS = SEQ // GRID_W
HEAD_DIM = 128
NA_HEADS = 4
NA_WIN_ROWS = 8
NA_WIN_COLS = 16
MLA_HEADS = 6
MLA_Q_RANK = 512
MLA_KV_RANK = 256
MLA_NOPE = 128
MLA_ROPE = 64
MLA_V = 128
ROPE_THETA = 10000.0
SWA_HEADS = 6
SWA_KV_HEADS = 2
SWA_GROUP = SWA_HEADS // SWA_KV_HEADS
SWA_WINDOW = 128
SWA_BLOCK = 128
D_A = NA_HEADS * HEAD_DIM
D_B = MLA_HEADS * MLA_V
D_C = SWA_HEADS * HEAD_DIM
IN_A = 3 * D_A
IN_B = MLA_Q_RANK + MLA_KV_RANK + MLA_ROPE
IN_C = (SWA_HEADS + 2 * SWA_KV_HEADS) * HEAD_DIM
D_FF = 5632
DEEPNORM_ALPHA = (2 * DEPTH) ** 0.25
LN_EPS = 1e-5
RMS_EPS = 1e-6
NEG_INF = -1e30
LOG2_E = 1.4426950408889634

PB_W = MLA_Q_RANK + MLA_KV_RANK + 2 * MLA_ROPE
P_W = IN_A + PB_W + IN_C
MLA_QK = 2 * HEAD_DIM
MLA_VT = MLA_V + 16

NA_QROWS = 4
NA_KROWS = 12
NA_TQ = NA_QROWS * GRID_W
NA_TK = NA_KROWS * GRID_W
NA_NBLK = GRID_ROWS // NA_QROWS

VMEM_LIMIT = 56 * 1024 * 1024


def _cparams(sem):
    return pltpu.CompilerParams(dimension_semantics=sem, vmem_limit_bytes=VMEM_LIMIT)


def _layer_norm(z, g, b):
    mu = jnp.mean(z, axis=-1, keepdims=True)
    zc = z - mu
    var = jnp.mean(zc * zc, axis=-1, keepdims=True)
    return zc * lax.rsqrt(var + LN_EPS) * g + b


def _rms_norm(x, g):
    ms = jnp.mean(x * x, axis=-1, keepdims=True)
    return x * lax.rsqrt(ms + RMS_EPS) * g


def _dot_nt(a, b):
    return lax.dot_general(a, b, (((1,), (1,)), ((), ())), preferred_element_type=F32)


ADA_TN = 1024
ADA_RC = 256
ADA_HEAD = 2 * D_MODEL
ADA_SLAB = 256
ADA_NSLAB = (DEPTH * 6 * D_MODEL - ADA_HEAD) // ADA_SLAB


def _ada_slab(cb_ref, w_ref, b_ref, o_ref):
    tn = o_ref.shape[-1]
    acc = jnp.zeros((8, tn), F32)
    for r in range(0, D_MODEL, ADA_RC):
        cb = jnp.concatenate([cb_ref[r:r + ADA_RC, :]] * (tn // 128), axis=1)
        prod = w_ref[r:r + ADA_RC, :] * cb
        acc = acc + jnp.sum(prod.reshape(ADA_RC // 8, 8, tn), axis=0)
    o_ref[...] = jnp.sum(acc, axis=0, keepdims=True) + b_ref[...]


def _ada_head_kernel(c_ref, w_ref, b_ref, o_ref, cb_ref):
    for r in range(0, D_MODEL, ADA_RC):
        c = c_ref[r:r + ADA_RC, :]
        cond = c * (1.0 / (1.0 + jnp.exp(-c)))
        cb_ref[r:r + ADA_RC, :] = jnp.broadcast_to(cond, (ADA_RC, 128))
    _ada_slab(cb_ref, w_ref, b_ref, o_ref)


def _ada_head_call(c_col, w_ada, b_ada3):
    return pl.pallas_call(
        _ada_head_kernel,
        out_shape=(jax.ShapeDtypeStruct((1, ADA_HEAD), F32), jax.ShapeDtypeStruct((D_MODEL, 128), F32)),
        grid=(ADA_HEAD // ADA_TN,),
        in_specs=[
            pl.BlockSpec((D_MODEL, 1), lambda j: (0, 0)),
            pl.BlockSpec((None, D_MODEL, ADA_TN), lambda j: (0, 0, j)),
            pl.BlockSpec((None, 1, ADA_TN), lambda j: (0, 0, j)),
        ],
        out_specs=(pl.BlockSpec((1, ADA_TN), lambda j: (0, j)),
                   pl.BlockSpec((D_MODEL, 128), lambda j: (0, 0))),
        compiler_params=_cparams(("arbitrary",)),
        name="ada_head",
    )(c_col, w_ada, b_ada3)


def _ada_slab_index(j):
    s = jnp.minimum(j, ADA_NSLAB - 1)
    n0 = (6 * D_MODEL - ADA_HEAD) // ADA_SLAB
    return jnp.where(s < n0, 0, 1), jnp.where(s < n0, s + ADA_HEAD // ADA_SLAB, s - n0), s


WPREP_TN = 512
KR0 = IN_A + MLA_Q_RANK + MLA_KV_RANK


def _winprep_kernel(w_ref, o_ref):
    half = MLA_ROPE // 2
    kr1 = KR0 + MLA_ROPE
    o_ref[0:kr1, :] = w_ref[0:kr1, :].astype(BF16)
    o_ref[kr1:kr1 + half, :] = (-w_ref[KR0 + half:kr1, :]).astype(BF16)
    o_ref[kr1 + half:kr1 + MLA_ROPE, :] = w_ref[KR0:KR0 + half, :].astype(BF16)
    o_ref[kr1 + MLA_ROPE:, :] = w_ref[kr1:, :].astype(BF16)


def _winprep_call(w_in_t):
    tn = WPREP_TN
    return pl.pallas_call(
        _winprep_kernel,
        out_shape=jax.ShapeDtypeStruct((DEPTH, P_W, D_MODEL), BF16),
        grid=(DEPTH, D_MODEL // tn),
        in_specs=[pl.BlockSpec((None, w_in_t.shape[1], tn), lambda l, i: (l, 0, i))],
        out_specs=pl.BlockSpec((None, P_W, tn), lambda l, i: (l, 0, i)),
        compiler_params=_cparams(("parallel", "parallel")),
        name="w_in_prep",
    )(w_in_t)


INPROJ_TM = 512
_INPROJ_CHUNKS = (
    (0, 512, 0, 0), (512, 1024, 0, 512), (1024, 1536, 0, 1024),
    (2432, 2944, 1, 0), (2944, 3456, 1, 512), (3456, 3712, 1, 1024),
)
_PB0 = IN_A


def _inproj_kernel(x_ref, mod_ref, w_ref, cs_ref, gq_ref, gkv_ref, wq_ref, wk_ref, wvt_ref, cos_ref, sin_ref,
                   oa_ref, oc_ref, q_ref, k_ref, vt_ref):
    outs = (oa_ref, oc_ref)
    sh = mod_ref[0:1, :]
    sc = mod_ref[1:2, :]
    u = (x_ref[...] * (1.0 + sc) + sh).astype(BF16)
    for c0, c1, oi, off in _INPROJ_CHUNKS:
        acc = _dot_nt(u, w_ref[c0:c1, :]) * cs_ref[:, c0:c1]
        outs[oi][:, off:off + (c1 - c0)] = acc.astype(outs[oi].dtype)
    cq = _dot_nt(u, w_ref[_PB0:_PB0 + MLA_Q_RANK, :])
    ckv_kr = _dot_nt(u, w_ref[_PB0 + MLA_Q_RANK:_PB0 + PB_W, :])
    _mla_up(cq, ckv_kr[:, 0:MLA_KV_RANK], ckv_kr[:, MLA_KV_RANK:], gq_ref, gkv_ref, wq_ref, wk_ref, wvt_ref,
            cos_ref, sin_ref, q_ref, k_ref, vt_ref)


def _inproj_call(l, x2, mod, lmod, w_p, colscale, gq, gkv, wq, wk, wvt, cos_t, sin_t):
    tm = INPROJ_TM
    hsd = jax.ShapeDtypeStruct((MLA_HEADS, SEQ, MLA_QK), BF16)
    hspec = pl.BlockSpec((MLA_HEADS, tm, MLA_QK), lambda i: (0, i, 0))
    layer = lambda r, c: pl.BlockSpec((None, r, c), lambda i: (l, 0, 0))
    return pl.pallas_call(
        _inproj_kernel,
        out_shape=(jax.ShapeDtypeStruct((SEQ, IN_A), BF16),
                   jax.ShapeDtypeStruct((SEQ, IN_C), BF16),
                   hsd, hsd, jax.ShapeDtypeStruct((MLA_HEADS, MLA_VT, SEQ), BF16)),
        grid=(SEQ // tm,),
        in_specs=[
            pl.BlockSpec((tm, D_MODEL), lambda i: (i, 0)),
            pl.BlockSpec((None, 6, D_MODEL), lambda i: (lmod, 0, 0)),
            pl.BlockSpec((None, P_W, D_MODEL), lambda i: (l, 0, 0), pipeline_mode=pl.Buffered(1)),
            pl.BlockSpec((1, P_W), lambda i: (0, 0)),
            layer(1, MLA_Q_RANK), layer(1, MLA_KV_RANK),
            layer(MLA_Q_RANK, MLA_HEADS * MLA_QK), layer(MLA_KV_RANK, MLA_HEADS * MLA_NOPE),
            layer(MLA_HEADS * MLA_V, MLA_KV_RANK),
            pl.BlockSpec((tm, HEAD_DIM), lambda i: (i, 0)),
            pl.BlockSpec((tm, HEAD_DIM), lambda i: (i, 0)),
        ],
        out_specs=(pl.BlockSpec((tm, IN_A), lambda i: (i, 0)),
                   pl.BlockSpec((tm, IN_C), lambda i: (i, 0)),
                   hspec, hspec, pl.BlockSpec((MLA_HEADS, MLA_VT, tm), lambda i: (0, 0, i))),
        compiler_params=_cparams(("parallel",)),
        name="in_proj",
    )(x2, mod, w_p, colscale, gq, gkv, wq, wk, wvt, cos_t, sin_t)


def _na_block_rule(btype, i, j):
    if btype == 0:
        r0 = max(i - NA_WIN_ROWS // 2, 0)
        valid = r0 <= j < r0 + NA_WIN_ROWS
        ro = j - i + (NA_WIN_ROWS - 1)
    elif btype == 1:
        valid = i <= j < i + NA_WIN_ROWS
        ro = j - i + (NA_WIN_ROWS - 1) - NA_WIN_ROWS // 2
    else:
        r = GRID_ROWS - NA_QROWS + i
        ks = GRID_ROWS - NA_KROWS
        r0 = min(r - NA_WIN_ROWS // 2, GRID_ROWS - NA_WIN_ROWS)
        valid = r0 <= ks + j < r0 + NA_WIN_ROWS
        ro = ks + j - r + (NA_WIN_ROWS - 1)
    return ro if valid else None


def _na_bias_kernel(rpb_ref, o_ref):
    h = pl.program_id(0)
    n_ro = 2 * NA_WIN_ROWS - 1
    n_co = 2 * NA_WIN_COLS - 1
    cq = lax.broadcasted_iota(jnp.int32, (GRID_W, GRID_W), 0)
    ck = lax.broadcasted_iota(jnp.int32, (GRID_W, GRID_W), 1)
    c0 = jnp.clip(cq - NA_WIN_COLS // 2, 0, GRID_W - NA_WIN_COLS)
    coff = jnp.clip(ck - cq, -(NA_WIN_COLS - 1), NA_WIN_COLS - 1) + (NA_WIN_COLS - 1)
    neg = jnp.full((GRID_W, GRID_W), NEG_INF, F32)
    tblocks = []
    for ro in range(n_ro):
        tb = neg
        for j in range(n_co):
            tb = jnp.where(coff == j, LOG2_E * rpb_ref[h * (n_ro * n_co) + ro * n_co + j], tb)
        inside = jnp.where(ck >= c0, jnp.where(ck < c0 + NA_WIN_COLS, 1, 0), 0)
        tblocks.append(jnp.where(inside == 1, tb, neg))
    for btype in range(3):
        for i in range(NA_QROWS):
            for jp in range(NA_KROWS // 2):
                pair = []
                for j in (2 * jp, 2 * jp + 1):
                    ro = _na_block_rule(btype, i, j)
                    pair.append(neg if ro is None else tblocks[ro])
                o_ref[0, btype, i * GRID_W:(i + 1) * GRID_W, jp * 128:(jp + 1) * 128] = (
                    jnp.concatenate(pair, axis=1))


def _na_bias_call(rpb_flat):
    return pl.pallas_call(
        _na_bias_kernel,
        out_shape=jax.ShapeDtypeStruct((NA_HEADS, 3, NA_TQ, NA_TK), F32),
        grid=(NA_HEADS,),
        in_specs=[pl.BlockSpec(memory_space=pltpu.SMEM)],
        out_specs=pl.BlockSpec((1, 3, NA_TQ, NA_TK), lambda h: (h, 0, 0, 0)),
        compiler_params=_cparams(("parallel",)),
        name="na_bias",
    )(rpb_flat)


NA_BPS = 4


def _na_kernel(q_ref, k_ref, v_ref, *rest):
    bias_refs, o_ref = rest[:NA_BPS], rest[NA_BPS]
    j = pl.program_id(0)
    ones = jnp.ones((NA_TK, HEAD_DIM), BF16)
    for sub, bias_ref in enumerate(bias_refs):
        b = NA_BPS * j + sub
        ks = jnp.clip(NA_QROWS * b - NA_WIN_ROWS // 2, 0, GRID_ROWS - NA_KROWS) * GRID_W
        ks = pl.multiple_of(ks, GRID_W)
        rows = slice(sub * NA_TQ, (sub + 1) * NA_TQ)
        for h in range(NA_HEADS):
            cols = slice(h * HEAD_DIM, (h + 1) * HEAD_DIM)
            s = _dot_nt(q_ref[rows, cols], k_ref[pl.ds(ks, NA_TK), cols]) + bias_ref[h, 0]
            m = jnp.max(s, axis=1, keepdims=True)
            p = jnp.exp2(s - m).astype(BF16)
            vext = jnp.concatenate([v_ref[pl.ds(ks, NA_TK), cols], ones], axis=1)
            acc = jnp.dot(p, vext, preferred_element_type=F32)
            o_ref[rows, cols] = acc[:, 0:HEAD_DIM] * (1.0 / acc[:, HEAD_DIM:HEAD_DIM + 1])


def _na_call(pa, bias):
    def btype(b):
        return jnp.where(b == 0, 0, jnp.where(b == NA_NBLK - 1, 2, 1))
    bias_specs = [pl.BlockSpec((NA_HEADS, 1, NA_TQ, NA_TK),
                               functools.partial(lambda j, sub: (0, btype(NA_BPS * j + sub), 0, 0), sub=sub))
                  for sub in range(NA_BPS)]
    return pl.pallas_call(
        _na_kernel,
        out_shape=jax.ShapeDtypeStruct((SEQ, D_A), F32),
        grid=(NA_NBLK // NA_BPS,),
        in_specs=[
            pl.BlockSpec((NA_BPS * NA_TQ, D_A), lambda j: (j, 0)),
            pl.BlockSpec((SEQ, D_A), lambda j: (0, 1), pipeline_mode=pl.Buffered(1)),
            pl.BlockSpec((SEQ, D_A), lambda j: (0, 2), pipeline_mode=pl.Buffered(1)),
        ] + bias_specs,
        out_specs=pl.BlockSpec((NA_BPS * NA_TQ, D_A), lambda j: (j, 0)),
        compiler_params=_cparams(("arbitrary",)),
        name="na_attn",
    )(pa, pa, pa, *([bias] * NA_BPS))


def _mla_up(cq, ckv, krb, gq_ref, gkv_ref, wq_ref, wk_ref, wvt_ref, cos_ref, sin_ref, q_ref, k_ref, vt_ref):
    tm = cq.shape[0]
    cqn = _rms_norm(cq, gq_ref[...]).astype(BF16)
    ckvn = _rms_norm(ckv, gkv_ref[...]).astype(BF16)
    cos = cos_ref[...]
    sin = sin_ref[...]

    def rotary(t):
        return t * cos + pltpu.roll(t, MLA_ROPE, 1) * sin

    kpe = rotary(krb).astype(BF16)
    ones = jnp.ones((MLA_VT - MLA_V, tm), BF16)
    scale = (MLA_NOPE + MLA_ROPE) ** -0.5 * LOG2_E
    k_all = jnp.dot(ckvn, wk_ref[...], preferred_element_type=F32).astype(BF16)
    for h in range(MLA_HEADS):
        qh = jnp.dot(cqn, wq_ref[:, h * MLA_QK:(h + 1) * MLA_QK], preferred_element_type=F32)
        q_ref[h, :, 0:HEAD_DIM] = (qh[:, 0:HEAD_DIM] * scale).astype(BF16)
        q_ref[h, :, HEAD_DIM:MLA_QK] = (rotary(qh[:, HEAD_DIM:MLA_QK]) * scale).astype(BF16)
        k_ref[h, :, 0:HEAD_DIM] = k_all[:, h * MLA_NOPE:(h + 1) * MLA_NOPE]
        k_ref[h, :, HEAD_DIM:MLA_QK] = kpe
        vt_ref[h, 0:MLA_V, :] = _dot_nt(wvt_ref[h * MLA_V:(h + 1) * MLA_V, :], ckvn).astype(BF16)
        vt_ref[h, MLA_V:MLA_VT, :] = ones


MLA_TQ = 256
MLA_TK = 1024


def _mla_attn_kernel(*refs, with_ada):
    qe_ref, qo_ref, k_ref, vt_ref, wo_ref, wgu_ref, wdn_ref = refs[:7]
    ada_in = refs[7:10] if with_ada else ()
    outs = refs[7 + len(ada_in):]
    oe_ref, oo_ref, wo_b_ref, wgu_b_ref, wdn_b_ref = outs[:5]
    ada_out = outs[5:6] if with_ada else ()
    s0_sc, s1_sc, s2_sc, s3_sc, m0_sc, m1_sc, m2_sc, m3_sc = outs[5 + len(ada_out):]
    cast_refs = ((wo_ref, wo_b_ref), (wgu_ref, wgu_b_ref), (wdn_ref, wdn_b_ref))
    j = pl.program_id(0)

    @pl.when(j == 0)
    def _():
        for ref in (s2_sc, s3_sc, m2_sc, m3_sc):
            ref[...] = jnp.zeros_like(ref)

    def stage(q_ref, sa_sc, ma_sc, sb_sc, mb_sc, vt_ref, o_ref):
        q = q_ref[0]
        tq = q.shape[0]
        m_prev = mb_sc[...]
        mx = jnp.full((8, tq), NEG_INF, F32)
        acc = jnp.zeros((MLA_VT, tq), F32)
        for c in range(SEQ // MLA_TK):
            keys = slice(c * MLA_TK, (c + 1) * MLA_TK)
            p = jnp.exp2(sb_sc[keys, :] - m_prev).astype(BF16)
            acc = acc + jnp.dot(vt_ref[0, :, keys], p, preferred_element_type=F32)
            s = _dot_nt(k_ref[0, keys, :], q)
            sa_sc[keys, :] = s
            mx = jnp.maximum(mx, jnp.max(s.reshape(MLA_TK // 8, 8, tq), axis=0))
        ma_sc[...] = jnp.max(mx, axis=0, keepdims=True)
        o = acc[0:MLA_V, :] * (1.0 / acc[MLA_V:MLA_V + 1, :])
        o_ref[...] = o.T

    def step(a_even, a_odd, b_even, b_odd):
        for w_ref, wb_ref in cast_refs:
            wb_ref[...] = w_ref[...].astype(BF16)
        if with_ada:
            _ada_slab(*ada_in, *ada_out)
        stage(qe_ref, *a_even, *b_even, vt_ref, oe_ref)
        stage(qo_ref, *a_odd, *b_odd, vt_ref, oo_ref)

    buf = ((s0_sc, m0_sc), (s1_sc, m1_sc), (s2_sc, m2_sc), (s3_sc, m3_sc))
    parity = lax.rem(j, 2)

    @pl.when(parity == 0)
    def _():
        step(buf[0], buf[1], buf[2], buf[3])

    @pl.when(parity == 1)
    def _():
        step(buf[2], buf[3], buf[0], buf[1])


WO_SLAB = (32, D_MODEL)
WGU_SLAB = (256, 1024)
WDN_SLAB = (64, D_MODEL)


def _mla_attn_call(l, q, k, vt, w_o, w_gu, w_down, ada=None):
    tq = MLA_TQ
    npair = SEQ // (2 * tq)
    last = MLA_HEADS * npair - 1
    nstep = last + 2
    pair_a = lambda j: jnp.minimum(j, last)
    pair_b = lambda j: jnp.maximum(j - 1, 0)
    half = jax.ShapeDtypeStruct((SEQ // 2, D_B), F32)
    out_spec = pl.BlockSpec((tq, MLA_V), lambda j: (pair_b(j) % npair, pair_b(j) // npair))

    def slab_specs(shape, slab):
        nr, nc = shape[0] // slab[0], shape[1] // slab[1]
        assert nr * slab[0] == shape[0] and nc * slab[1] == shape[1] and nr * nc <= nstep
        idx = lambda j: jnp.minimum(j, nr * nc - 1)
        return (pl.BlockSpec((None,) + slab, lambda j: (l, idx(j) // nc, idx(j) % nc)),
                pl.BlockSpec(slab, lambda j: (idx(j) // nc, idx(j) % nc)))

    wo_in, wo_out = slab_specs(w_o.shape[1:], WO_SLAB)
    wgu_in, wgu_out = slab_specs(w_gu.shape[1:], WGU_SLAB)
    wdn_in, wdn_out = slab_specs(w_down.shape[1:], WDN_SLAB)
    ada_args, ada_in, ada_out, ada_shape = (), [], (), ()
    if ada is not None:
        assert ADA_NSLAB <= nstep
        ada_args = ada
        ada_in = [
            pl.BlockSpec((D_MODEL, 128), lambda j: (0, 0)),
            pl.BlockSpec((None, D_MODEL, ADA_SLAB), lambda j: (_ada_slab_index(j)[0], 0, _ada_slab_index(j)[1])),
            pl.BlockSpec((None, 1, ADA_SLAB), lambda j: (_ada_slab_index(j)[0], 0, _ada_slab_index(j)[1])),
        ]
        ada_out = (pl.BlockSpec((1, ADA_SLAB), lambda j: (0, _ada_slab_index(j)[2])),)
        ada_shape = (jax.ShapeDtypeStruct((1, ADA_NSLAB * ADA_SLAB), F32),)
    return pl.pallas_call(
        functools.partial(_mla_attn_kernel, with_ada=ada is not None),
        out_shape=(half, half,
                   jax.ShapeDtypeStruct(w_o.shape[1:], BF16),
                   jax.ShapeDtypeStruct(w_gu.shape[1:], BF16),
                   jax.ShapeDtypeStruct(w_down.shape[1:], BF16)) + ada_shape,
        grid=(nstep,),
        in_specs=[
            pl.BlockSpec((1, tq, MLA_QK), lambda j: (pair_a(j) // npair, 2 * (pair_a(j) % npair), 0)),
            pl.BlockSpec((1, tq, MLA_QK), lambda j: (pair_a(j) // npair, 2 * (pair_a(j) % npair) + 1, 0)),
            pl.BlockSpec((1, SEQ, MLA_QK), lambda j: (pair_a(j) // npair, 0, 0)),
            pl.BlockSpec((1, MLA_VT, SEQ), lambda j: (pair_b(j) // npair, 0, 0), pipeline_mode=pl.Buffered(1)),
            wo_in, wgu_in, wdn_in,
        ] + ada_in,
        out_specs=(out_spec, out_spec, wo_out, wgu_out, wdn_out) + ada_out,
        scratch_shapes=[pltpu.VMEM((SEQ, tq), F32)] * 4 + [pltpu.VMEM((1, tq), F32)] * 4,
        compiler_params=_cparams(("arbitrary",)),
        name="mla_attn",
    )(q, q, k, vt, w_o, w_gu, w_down, *ada_args)


_SWA_SLOPES = tuple(2.0 ** (-8.0 * (i + 1) / SWA_HEADS) for i in range(SWA_HEADS))


SWA_NB = 16


def _swa_kernel(sink_ref, q_ref, kp_ref, kc_ref, kn_ref, vp_ref, vc_ref, vn_ref, o_ref):
    g_kv = pl.program_id(0)
    j = pl.program_id(1)
    t = SWA_BLOCK
    rows = SWA_GROUP * t
    kwin = jnp.concatenate([kp_ref[...], kc_ref[...], kn_ref[...]], axis=0)
    vwin = jnp.concatenate([vp_ref[...], vc_ref[...], vn_ref[...]], axis=0)
    ones = jnp.ones((3 * t, HEAD_DIM), BF16)
    ri = lax.broadcasted_iota(jnp.int32, (rows, 3 * t), 0)
    ci = lax.broadcasted_iota(jnp.int32, (rows, 3 * t), 1)
    grp = jnp.right_shift(ri, 7)
    dist = jnp.abs(jnp.bitwise_and(ri, t - 1) - (ci - t))
    slope_lo = jnp.where(grp == 0, _SWA_SLOPES[0], jnp.where(grp == 1, _SWA_SLOPES[1], _SWA_SLOPES[2]))
    slope_hi = jnp.where(grp == 0, _SWA_SLOPES[3], jnp.where(grp == 1, _SWA_SLOPES[4], _SWA_SLOPES[5]))
    slope = jnp.where(g_kv == 0, slope_lo, slope_hi)
    band = jnp.where(dist <= SWA_WINDOW, (-LOG2_E) * slope * dist.astype(F32), NEG_INF)
    rcol = jnp.right_shift(lax.broadcasted_iota(jnp.int32, (rows, 1), 0), 7)
    base = g_kv * SWA_GROUP
    sink = LOG2_E * jnp.where(rcol == 0, sink_ref[base],
                              jnp.where(rcol == 1, sink_ref[base + 1], sink_ref[base + 2]))
    first_cols = jnp.where(j == 0, t, 0)
    last_cols = jnp.where(j == pl.num_programs(1) - 1, 2 * t, 3 * t)
    for b in range(SWA_NB):
        q = jnp.concatenate([q_ref[b * t:(b + 1) * t, g * t:(g + 1) * t] for g in range(SWA_GROUP)], axis=0)
        s = _dot_nt(q, kwin[b * t:(b + 3) * t, :]) + band
        if b == 0:
            s = jnp.where(ci < first_cols, NEG_INF, s)
        if b == SWA_NB - 1:
            s = jnp.where(ci >= last_cols, NEG_INF, s)
        m = jnp.maximum(jnp.max(s, axis=1, keepdims=True), sink)
        p = jnp.exp2(s - m).astype(BF16)
        vext = jnp.concatenate([vwin[b * t:(b + 3) * t, :], ones], axis=1)
        acc = jnp.dot(p, vext, preferred_element_type=F32)
        l = acc[:, HEAD_DIM:HEAD_DIM + 1] + jnp.exp2(sink - m)
        o = acc[:, 0:HEAD_DIM] * (1.0 / l)
        for g in range(SWA_GROUP):
            o_ref[b * t:(b + 1) * t, g * t:(g + 1) * t] = o[g * t:(g + 1) * t, :]


def _swa_call(sink, pc):
    t = SWA_BLOCK
    nb = SEQ // t
    tq = SWA_NB * t
    kcol = SWA_HEADS
    vcol = SWA_HEADS + SWA_KV_HEADS
    prev = lambda j: jnp.maximum(SWA_NB * j - 1, 0)
    nxt = lambda j: jnp.minimum(SWA_NB * (j + 1), nb - 1)
    return pl.pallas_call(
        _swa_kernel,
        out_shape=jax.ShapeDtypeStruct((SEQ, D_C), F32),
        grid=(SWA_KV_HEADS, SEQ // tq),
        in_specs=[
            pl.BlockSpec(memory_space=pltpu.SMEM),
            pl.BlockSpec((tq, SWA_GROUP * t), lambda g, j: (j, g)),
            pl.BlockSpec((t, t), lambda g, j: (prev(j), kcol + g)),
            pl.BlockSpec((tq, t), lambda g, j: (j, kcol + g)),
            pl.BlockSpec((t, t), lambda g, j: (nxt(j), kcol + g)),
            pl.BlockSpec((t, t), lambda g, j: (prev(j), vcol + g)),
            pl.BlockSpec((tq, t), lambda g, j: (j, vcol + g)),
            pl.BlockSpec((t, t), lambda g, j: (nxt(j), vcol + g)),
        ],
        out_specs=pl.BlockSpec((tq, SWA_GROUP * t), lambda g, j: (j, g)),
        compiler_params=_cparams(("parallel", "arbitrary")),
        name="swa_attn",
    )(sink, pc, pc, pc, pc, pc, pc, pc)


OUTPROJ_TM = 2 * MLA_TQ


def _outproj_kernel(ya_ref, ybe_ref, ybo_ref, yc_ref, x_ref, mod_ref, gn_ref, w_ref, lg_ref, lb_ref,
                    o_ref, u_ref):
    hm = OUTPROJ_TM // 2
    gate = 1.0 + mod_ref[2:3, :]
    for half, yb_ref in enumerate((ybe_ref, ybo_ref)):
        rows = slice(half * hm, (half + 1) * hm)
        acc = None
        for y, c0 in ((ya_ref[rows, :], 0), (yb_ref[...], D_A), (yc_ref[rows, :], D_A + D_B)):
            c1 = c0 + y.shape[1]
            yn = _rms_norm(y, gn_ref[:, c0:c1]).astype(BF16)
            part = jnp.dot(yn, w_ref[c0:c1, :], preferred_element_type=F32)
            acc = part if acc is None else acc + part
        x1 = _layer_norm(DEEPNORM_ALPHA * x_ref[rows, :] + gate * acc, lg_ref[...], lb_ref[...])
        o_ref[rows, :] = x1
        u_ref[rows, :] = (x1 * (1.0 + mod_ref[4:5, :]) + mod_ref[3:4, :]).astype(BF16)


def _outproj_call(l, ya, yb_even, yb_odd, yc, x2, mod, gn, w_o, lg, lb):
    tm = OUTPROJ_TM
    row = lambda r, w: pl.BlockSpec((r, w), lambda i: (i, 0))
    full = lambda r, w: pl.BlockSpec((None, r, w), lambda i: (l, 0, 0))
    return pl.pallas_call(
        _outproj_kernel,
        out_shape=(jax.ShapeDtypeStruct((SEQ, D_MODEL), F32), jax.ShapeDtypeStruct((SEQ, D_MODEL), BF16)),
        grid=(SEQ // tm,),
        in_specs=[row(tm, D_A), row(tm // 2, D_B), row(tm // 2, D_B), row(tm, D_C), row(tm, D_MODEL),
                  full(6, D_MODEL), full(1, D_MODEL),
                  pl.BlockSpec((D_MODEL, D_MODEL), lambda i: (0, 0), pipeline_mode=pl.Buffered(1)),
                  full(1, D_MODEL), full(1, D_MODEL)],
        out_specs=(row(tm, D_MODEL), row(tm, D_MODEL)),
        compiler_params=_cparams(("parallel",)),
        name="out_proj_ln",
    )(ya, yb_even, yb_odd, yc, x2, mod, gn, w_o, lg, lb)


FFN_TM = 1024
FFN_SUB = 512
FFN_SUB_LAST = 256
FFN_TF = 512


def _ffn_kernel(u_ref, x_hbm, mod_ref, wg_ref, wu_ref, wd_ref, lg_ref, lb_ref, o_ref, x_sc, x_sem):
    i = pl.program_id(0)
    f = pl.program_id(1)
    tm = o_ref.shape[0]
    x_copy = pltpu.make_async_copy(x_hbm.at[pl.ds(pl.multiple_of(i * tm, tm), tm), :], x_sc, x_sem)

    last = pl.num_programs(1) - 1

    def down_partial(rows):
        u = u_ref[rows, :]
        g = jnp.dot(u, wg_ref[...], preferred_element_type=F32)
        up = jnp.dot(u, wu_ref[...], preferred_element_type=F32)
        hdn = (g * (1.0 / (1.0 + jnp.exp(-g))) * up).astype(BF16)
        return jnp.dot(hdn, wd_ref[...], preferred_element_type=F32)

    @pl.when(f == 0)
    def _():
        x_copy.start()
        o_ref[...] = jnp.zeros_like(o_ref)

    @pl.when(f < last)
    def _():
        for r in range(0, tm, FFN_SUB):
            rows = slice(r, r + FFN_SUB)
            o_ref[rows, :] += down_partial(rows)

    @pl.when(f == last)
    def _():
        x_copy.wait()
        gate = 1.0 + mod_ref[5:6, :]
        for r in range(0, tm, FFN_SUB_LAST):
            rows = slice(r, r + FFN_SUB_LAST)
            acc = o_ref[rows, :] + down_partial(rows)
            o_ref[rows, :] = _layer_norm(DEEPNORM_ALPHA * x_sc[rows, :] + gate * acc, lg_ref[...], lb_ref[...])


def _ffn_call(l, u, x2, mod, w_gu, w_down, lg, lb):
    tm, tf = FFN_TM, FFN_TF
    nf = D_FF // tf
    return pl.pallas_call(
        _ffn_kernel,
        out_shape=jax.ShapeDtypeStruct((SEQ, D_MODEL), F32),
        grid=(SEQ // tm, nf),
        in_specs=[
            pl.BlockSpec((tm, D_MODEL), lambda i, f: (i, 0)),
            pl.BlockSpec(memory_space=pl.ANY),
            pl.BlockSpec((None, 6, D_MODEL), lambda i, f: (l, 0, 0)),
            pl.BlockSpec((D_MODEL, tf), lambda i, f: (0, f)),
            pl.BlockSpec((D_MODEL, tf), lambda i, f: (0, nf + f)),
            pl.BlockSpec((tf, D_MODEL), lambda i, f: (f, 0)),
            pl.BlockSpec((None, 1, D_MODEL), lambda i, f: (l, 0, 0)),
            pl.BlockSpec((None, 1, D_MODEL), lambda i, f: (l, 0, 0)),
        ],
        out_specs=pl.BlockSpec((tm, D_MODEL), lambda i, f: (i, 0)),
        scratch_shapes=[pltpu.VMEM((tm, D_MODEL), F32), pltpu.SemaphoreType.DMA(())],
        compiler_params=_cparams(("arbitrary", "arbitrary")),
        name="ffn_ln",
    )(u, x2, mod, w_gu, w_gu, w_down, lg, lb)


def _rot_half_cols(w):
    half = w.shape[-1] // 2
    return jnp.concatenate([-w[..., half:], w[..., :half]], axis=-1)


def _rope_tables():
    half = MLA_ROPE // 2
    inv = ROPE_THETA ** (-jnp.arange(half, dtype=F32) / half)
    ang = jnp.arange(SEQ, dtype=F32)[:, None] * inv[None, :]
    zeros = jnp.zeros((SEQ, MLA_ROPE), F32)
    cos = jnp.cos(ang)
    sin = jnp.sin(ang)
    return (jnp.concatenate([cos, cos, zeros], axis=1), jnp.concatenate([sin, sin, zeros], axis=1))


def _colscale():
    cs = np.ones((1, P_W), np.float32)
    cs[0, 0:D_A] = HEAD_DIM ** -0.5 * LOG2_E
    c0 = IN_A + PB_W
    cs[0, c0:c0 + D_C] = HEAD_DIM ** -0.5 * LOG2_E
    return jnp.asarray(cs)


def kernel(x, c, w_ada, b_ada, w_in, na_rpb, mla_q_norm, mla_kv_norm, mla_w_uq, mla_w_ukv,
           swa_sink, out_norm_g, w_o, ln1_g, ln1_b, w_gu, w_down, ln2_g, ln2_b):
    assert x.shape == (1, SEQ, D_MODEL)
    x2 = x.reshape(SEQ, D_MODEL)
    b_ada3 = b_ada.reshape(DEPTH, 1, -1)
    mod_head, cond_lanes = _ada_head_call(c.reshape(D_MODEL, 1), w_ada, b_ada3)
    mod = jnp.concatenate([mod_head, jnp.zeros((1, 6 * D_MODEL - ADA_HEAD), F32)], axis=1).reshape(1, 6, D_MODEL)
    cos_t, sin_t = _rope_tables()
    colscale = _colscale()
    w_p = _winprep_call(jnp.swapaxes(w_in, 1, 2))
    wq = mla_w_uq.reshape(DEPTH, MLA_Q_RANK, MLA_HEADS, MLA_NOPE + MLA_ROPE)
    wq = jnp.concatenate([wq, _rot_half_cols(wq[..., MLA_NOPE:])], axis=-1)
    wq = wq.reshape(DEPTH, MLA_Q_RANK, MLA_HEADS * MLA_QK).astype(BF16)
    wkv = mla_w_ukv.reshape(DEPTH, MLA_KV_RANK, MLA_HEADS, MLA_NOPE + MLA_V)
    wk = wkv[..., :MLA_NOPE].reshape(DEPTH, MLA_KV_RANK, MLA_HEADS * MLA_NOPE).astype(BF16)
    wvt = jnp.transpose(wkv[..., MLA_NOPE:], (0, 2, 3, 1)).reshape(DEPTH, MLA_HEADS * MLA_V, MLA_KV_RANK)
    wvt = wvt.astype(BF16)
    row3 = lambda a: a.reshape(DEPTH, 1, -1)
    gq, gkv, gn = row3(mla_q_norm), row3(mla_kv_norm), row3(out_norm_g)
    l1g, l1b, l2g, l2b = row3(ln1_g), row3(ln1_b), row3(ln2_g), row3(ln2_b)
    for l in range(DEPTH):
        pa, pc, q, k, vt = _inproj_call(l, x2, mod, l, w_p, colscale, gq, gkv, wq, wk, wvt, cos_t, sin_t)
        bias = _na_bias_call(na_rpb[l].reshape(-1))
        ya = _na_call(pa, bias)
        if l == 0:
            yb_even, yb_odd, w_o_b, w_gu_b, w_down_b, mod_rest = _mla_attn_call(
                l, q, k, vt, w_o, w_gu, w_down, ada=(cond_lanes, w_ada, b_ada3))
            mod = jnp.concatenate([mod_head, mod_rest], axis=1).reshape(DEPTH, 6, D_MODEL)
        else:
            yb_even, yb_odd, w_o_b, w_gu_b, w_down_b = _mla_attn_call(l, q, k, vt, w_o, w_gu, w_down)
        yc = _swa_call(swa_sink[l], pc)
        x2, u = _outproj_call(l, ya, yb_even, yb_odd, yc, x2, mod, gn, w_o_b, l1g, l1b)
        x2 = _ffn_call(l, u, x2, mod, w_gu_b, w_down_b, l2g, l2b)
    return x2.reshape(1, SEQ, D_MODEL)
```

```python
import functools

import numpy as np
import jax
import jax.numpy as jnp
from jax import lax
from jax.experimental import pallas as pl
from jax.experimental.pallas import tpu as pltpu

F32 = jnp.float32
BF16 = jnp.bfloat16

D_MODEL = 2048
SEQ = 8192
DEPTH = 2
GRID_W = 64
GRID_ROWS = SEQ // GRID_W
HEAD_DIM = 128
NA_HEADS = 4
NA_WIN_ROWS = 8
NA_WIN_COLS = 16
MLA_HEADS = 6
MLA_Q_RANK = 512
MLA_KV_RANK = 256
MLA_NOPE = 128
MLA_ROPE = 64
MLA_V = 128
ROPE_THETA = 10000.0
SWA_HEADS = 6
SWA_KV_HEADS = 2
SWA_GROUP = SWA_HEADS // SWA_KV_HEADS
SWA_WINDOW = 128
SWA_BLOCK = 128
D_A = NA_HEADS * HEAD_DIM
D_B = MLA_HEADS * MLA_V
D_C = SWA_HEADS * HEAD_DIM
IN_A = 3 * D_A
IN_B = MLA_Q_RANK + MLA_KV_RANK + MLA_ROPE
IN_C = (SWA_HEADS + 2 * SWA_KV_HEADS) * HEAD_DIM
D_FF = 5632
DEEPNORM_ALPHA = (2 * DEPTH) ** 0.25
LN_EPS = 1e-5
RMS_EPS = 1e-6
NEG_INF = -1e30
LOG2_E = 1.4426950408889634

PB_W = MLA_Q_RANK + MLA_KV_RANK + 2 * MLA_ROPE
P_W = IN_A + PB_W + IN_C
MLA_QK = 2 * HEAD_DIM
MLA_VT = MLA_V + 16

NA_QROWS = 4
NA_KROWS = 12
NA_TQ = NA_QROWS * GRID_W
NA_TK = NA_KROWS * GRID_W
NA_NBLK = GRID_ROWS // NA_QROWS

VMEM_LIMIT = 56 * 1024 * 1024


def _cparams(sem):
    return pltpu.CompilerParams(dimension_semantics=sem, vmem_limit_bytes=VMEM_LIMIT)


def _layer_norm(z, g, b):
    mu = jnp.mean(z, axis=-1, keepdims=True)
    zc = z - mu
    var = jnp.mean(zc * zc, axis=-1, keepdims=True)
    return zc * lax.rsqrt(var + LN_EPS) * g + b


def _rms_norm(x, g):
    ms = jnp.mean(x * x, axis=-1, keepdims=True)
    return x * lax.rsqrt(ms + RMS_EPS) * g


def _dot_nt(a, b):
    return lax.dot_general(a, b, (((1,), (1,)), ((), ())), preferred_element_type=F32)


_VEC_ORDER = (("out_norm", D_MODEL), ("ln1_g", D_MODEL), ("ln1_b", D_MODEL), ("ln2_g", D_MODEL),
              ("ln2_b", D_MODEL), ("q_norm", MLA_Q_RANK), ("kv_norm", MLA_KV_RANK))
_VEC_OFF = {}
_off = 0
for _name, _w in _VEC_ORDER:
    assert _off % _w == 0
    _VEC_OFF[_name] = (_off // _w, _w)
    _off += _w
VEC_W = _off


def _vec_spec(l, name):
    blk, w = _VEC_OFF[name]
    return pl.BlockSpec((None, 1, w), lambda *grid_idx: (l, 0, blk))


ADA_TN = 1024
ADA_RC = 256
ADA_HEAD = 2 * D_MODEL
ADA_SLAB = 256
ADA_NSLAB = (DEPTH * 6 * D_MODEL - ADA_HEAD) // ADA_SLAB


def _ada_slab(cb_ref, w_ref, b_ref, o_ref):
    tn = o_ref.shape[-1]
    acc = jnp.zeros((8, tn), F32)
    for r in range(0, D_MODEL, ADA_RC):
        cb = jnp.concatenate([cb_ref[r:r + ADA_RC, :]] * (tn // 128), axis=1)
        prod = w_ref[r:r + ADA_RC, :] * cb
        acc = acc + jnp.sum(prod.reshape(ADA_RC // 8, 8, tn), axis=0)
    o_ref[...] = jnp.sum(acc, axis=0, keepdims=True) + b_ref[...]


def _ada_head_kernel(c_ref, w_ref, b_ref, o_ref, cb_ref):
    for r in range(0, D_MODEL, ADA_RC):
        c = c_ref[r:r + ADA_RC, :]
        cond = c * (1.0 / (1.0 + jnp.exp(-c)))
        cb_ref[r:r + ADA_RC, :] = jnp.broadcast_to(cond, (ADA_RC, 128))
    _ada_slab(cb_ref, w_ref, b_ref, o_ref)


def _ada_head_call(c_col, w_ada, b_ada3):
    return pl.pallas_call(
        _ada_head_kernel,
        out_shape=(jax.ShapeDtypeStruct((1, ADA_HEAD), F32), jax.ShapeDtypeStruct((D_MODEL, 128), F32)),
        grid=(ADA_HEAD // ADA_TN,),
        in_specs=[
            pl.BlockSpec((D_MODEL, 1), lambda j: (0, 0)),
            pl.BlockSpec((None, D_MODEL, ADA_TN), lambda j: (0, 0, j)),
            pl.BlockSpec((None, 1, ADA_TN), lambda j: (0, 0, j)),
        ],
        out_specs=(pl.BlockSpec((1, ADA_TN), lambda j: (0, j)),
                   pl.BlockSpec((D_MODEL, 128), lambda j: (0, 0))),
        compiler_params=_cparams(("arbitrary",)),
        name="ada_head",
    )(c_col, w_ada, b_ada3)


def _ada_slab_index(j):
    s = jnp.minimum(j, ADA_NSLAB - 1)
    n0 = (6 * D_MODEL - ADA_HEAD) // ADA_SLAB
    return jnp.where(s < n0, 0, 1), jnp.where(s < n0, s + ADA_HEAD // ADA_SLAB, s - n0), s


WPREP_TN = 512
KR0 = IN_A + MLA_Q_RANK + MLA_KV_RANK


def _winprep_kernel(w_ref, o_ref):
    half = MLA_ROPE // 2
    kr1 = KR0 + MLA_ROPE
    o_ref[0:kr1, :] = w_ref[0:kr1, :].astype(BF16)
    o_ref[kr1:kr1 + half, :] = (-w_ref[KR0 + half:kr1, :]).astype(BF16)
    o_ref[kr1 + half:kr1 + MLA_ROPE, :] = w_ref[KR0:KR0 + half, :].astype(BF16)
    o_ref[kr1 + MLA_ROPE:, :] = w_ref[kr1:, :].astype(BF16)


def _winprep_call(w_in_t):
    tn = WPREP_TN
    return pl.pallas_call(
        _winprep_kernel,
        out_shape=jax.ShapeDtypeStruct((DEPTH, P_W, D_MODEL), BF16),
        grid=(DEPTH, D_MODEL // tn),
        in_specs=[pl.BlockSpec((None, w_in_t.shape[1], tn), lambda l, i: (l, 0, i))],
        out_specs=pl.BlockSpec((None, P_W, tn), lambda l, i: (l, 0, i)),
        compiler_params=_cparams(("parallel", "parallel")),
        name="w_in_prep",
    )(w_in_t)


INPROJ_TM = 512
_INPROJ_CHUNKS = (
    (0, 512, 0, 0), (512, 1024, 0, 512), (1024, 1536, 0, 1024),
    (2432, 2944, 1, 0), (2944, 3456, 1, 512), (3456, 3712, 1, 1024),
)
_PB0 = IN_A


def _inproj_kernel(x_ref, mod_ref, w_ref, cs_ref, gq_ref, gkv_ref, wq_ref, wk_ref, wvt_ref, cos_ref, sin_ref,
                   oa_ref, oc_ref, q_ref, k_ref, vt_ref):
    outs = (oa_ref, oc_ref)
    sh = mod_ref[0:1, :]
    sc = mod_ref[1:2, :]
    u = (x_ref[...] * (1.0 + sc) + sh).astype(BF16)
    for c0, c1, oi, off in _INPROJ_CHUNKS:
        acc = _dot_nt(u, w_ref[c0:c1, :]) * cs_ref[:, c0:c1]
        outs[oi][:, off:off + (c1 - c0)] = acc.astype(outs[oi].dtype)
    cq = _dot_nt(u, w_ref[_PB0:_PB0 + MLA_Q_RANK, :])
    ckv_kr = _dot_nt(u, w_ref[_PB0 + MLA_Q_RANK:_PB0 + PB_W, :])
    _mla_up(cq, ckv_kr[:, 0:MLA_KV_RANK], ckv_kr[:, MLA_KV_RANK:], gq_ref, gkv_ref, wq_ref, wk_ref, wvt_ref,
            cos_ref, sin_ref, q_ref, k_ref, vt_ref)


def _inproj_call(l, x2, mod, lmod, w_p, colscale, vec, wq, wk, wvt, cos_t, sin_t):
    tm = INPROJ_TM
    hsd = jax.ShapeDtypeStruct((MLA_HEADS, SEQ, MLA_QK), BF16)
    hspec = pl.BlockSpec((MLA_HEADS, tm, MLA_QK), lambda i: (0, i, 0))
    layer = lambda r, c: pl.BlockSpec((None, r, c), lambda i: (l, 0, 0))
    return pl.pallas_call(
        _inproj_kernel,
        out_shape=(jax.ShapeDtypeStruct((SEQ, IN_A), BF16),
                   jax.ShapeDtypeStruct((SEQ, IN_C), BF16),
                   hsd, hsd, jax.ShapeDtypeStruct((MLA_HEADS, MLA_VT, SEQ), BF16)),
        grid=(SEQ // tm,),
        in_specs=[
            pl.BlockSpec((tm, D_MODEL), lambda i: (i, 0)),
            pl.BlockSpec((None, 6, D_MODEL), lambda i: (lmod, 0, 0)),
            pl.BlockSpec((None, P_W, D_MODEL), lambda i: (l, 0, 0), pipeline_mode=pl.Buffered(1)),
            pl.BlockSpec((1, P_W), lambda i: (0, 0)),
            _vec_spec(l, "q_norm"), _vec_spec(l, "kv_norm"),
            layer(MLA_Q_RANK, MLA_HEADS * MLA_QK), layer(MLA_KV_RANK, MLA_HEADS * MLA_NOPE),
            layer(MLA_HEADS * MLA_V, MLA_KV_RANK),
            pl.BlockSpec((tm, MLA_ROPE // 2), lambda i: (i, 0)),
            pl.BlockSpec((tm, MLA_ROPE // 2), lambda i: (i, 0)),
        ],
        out_specs=(pl.BlockSpec((tm, IN_A), lambda i: (i, 0)),
                   pl.BlockSpec((tm, IN_C), lambda i: (i, 0)),
                   hspec, hspec, pl.BlockSpec((MLA_HEADS, MLA_VT, tm), lambda i: (0, 0, i))),
        compiler_params=_cparams(("parallel",)),
        name="in_proj",
    )(x2, mod, w_p, colscale, vec, vec, wq, wk, wvt, cos_t, sin_t)


def _na_block_rule(btype, i, j):
    if btype == 0:
        r0 = max(i - NA_WIN_ROWS // 2, 0)
        valid = r0 <= j < r0 + NA_WIN_ROWS
        ro = j - i + (NA_WIN_ROWS - 1)
    elif btype == 1:
        valid = i <= j < i + NA_WIN_ROWS
        ro = j - i + (NA_WIN_ROWS - 1) - NA_WIN_ROWS // 2
    else:
        r = GRID_ROWS - NA_QROWS + i
        ks = GRID_ROWS - NA_KROWS
        r0 = min(r - NA_WIN_ROWS // 2, GRID_ROWS - NA_WIN_ROWS)
        valid = r0 <= ks + j < r0 + NA_WIN_ROWS
        ro = ks + j - r + (NA_WIN_ROWS - 1)
    return ro if valid else None


def _na_bias_kernel(rpb_ref, o_ref):
    h = pl.program_id(0)
    n_ro = 2 * NA_WIN_ROWS - 1
    n_co = 2 * NA_WIN_COLS - 1
    cq = lax.broadcasted_iota(jnp.int32, (GRID_W, GRID_W), 0)
    ck = lax.broadcasted_iota(jnp.int32, (GRID_W, GRID_W), 1)
    c0 = jnp.clip(cq - NA_WIN_COLS // 2, 0, GRID_W - NA_WIN_COLS)
    coff = jnp.clip(ck - cq, -(NA_WIN_COLS - 1), NA_WIN_COLS - 1) + (NA_WIN_COLS - 1)
    neg = jnp.full((GRID_W, GRID_W), NEG_INF, F32)
    tblocks = []
    for ro in range(n_ro):
        tb = neg
        for j in range(n_co):
            tb = jnp.where(coff == j, LOG2_E * rpb_ref[h * (n_ro * n_co) + ro * n_co + j], tb)
        inside = jnp.where(ck >= c0, jnp.where(ck < c0 + NA_WIN_COLS, 1, 0), 0)
        tblocks.append(jnp.where(inside == 1, tb, neg))
    for btype in range(3):
        for i in range(NA_QROWS):
            for jp in range(NA_KROWS // 2):
                pair = []
                for j in (2 * jp, 2 * jp + 1):
                    ro = _na_block_rule(btype, i, j)
                    pair.append(neg if ro is None else tblocks[ro])
                o_ref[0, btype, i * GRID_W:(i + 1) * GRID_W, jp * 128:(jp + 1) * 128] = (
                    jnp.concatenate(pair, axis=1))


def _na_bias_call(rpb_flat):
    return pl.pallas_call(
        _na_bias_kernel,
        out_shape=jax.ShapeDtypeStruct((NA_HEADS, 3, NA_TQ, NA_TK), F32),
        grid=(NA_HEADS,),
        in_specs=[pl.BlockSpec(memory_space=pltpu.SMEM)],
        out_specs=pl.BlockSpec((1, 3, NA_TQ, NA_TK), lambda h: (h, 0, 0, 0)),
        compiler_params=_cparams(("parallel",)),
        name="na_bias",
    )(rpb_flat)


NA_BPS = 4


def _na_kernel(q_ref, k_ref, v_ref, *rest):
    bias_refs, o_ref = rest[:NA_BPS], rest[NA_BPS]
    j = pl.program_id(0)
    ones = jnp.ones((NA_TK, HEAD_DIM), BF16)
    for sub, bias_ref in enumerate(bias_refs):
        b = NA_BPS * j + sub
        ks = jnp.clip(NA_QROWS * b - NA_WIN_ROWS // 2, 0, GRID_ROWS - NA_KROWS) * GRID_W
        ks = pl.multiple_of(ks, GRID_W)
        rows = slice(sub * NA_TQ, (sub + 1) * NA_TQ)
        for h in range(NA_HEADS):
            cols = slice(h * HEAD_DIM, (h + 1) * HEAD_DIM)
            s = _dot_nt(q_ref[rows, cols], k_ref[pl.ds(ks, NA_TK), cols]) + bias_ref[h, 0]
            m = jnp.max(s, axis=1, keepdims=True)
            p = jnp.exp2(s - m).astype(BF16)
            vext = jnp.concatenate([v_ref[pl.ds(ks, NA_TK), cols], ones], axis=1)
            acc = jnp.dot(p, vext, preferred_element_type=F32)
            o_ref[rows, cols] = acc[:, 0:HEAD_DIM] * (1.0 / acc[:, HEAD_DIM:HEAD_DIM + 1])


def _na_call(pa, bias):
    def btype(b):
        return jnp.where(b == 0, 0, jnp.where(b == NA_NBLK - 1, 2, 1))
    bias_specs = [pl.BlockSpec((NA_HEADS, 1, NA_TQ, NA_TK),
                               functools.partial(lambda j, sub: (0, btype(NA_BPS * j + sub), 0, 0), sub=sub))
                  for sub in range(NA_BPS)]
    return pl.pallas_call(
        _na_kernel,
        out_shape=jax.ShapeDtypeStruct((SEQ, D_A), F32),
        grid=(NA_NBLK // NA_BPS,),
        in_specs=[
            pl.BlockSpec((NA_BPS * NA_TQ, D_A), lambda j: (j, 0)),
            pl.BlockSpec((SEQ, D_A), lambda j: (0, 1), pipeline_mode=pl.Buffered(1)),
            pl.BlockSpec((SEQ, D_A), lambda j: (0, 2), pipeline_mode=pl.Buffered(1)),
        ] + bias_specs,
        out_specs=pl.BlockSpec((NA_BPS * NA_TQ, D_A), lambda j: (j, 0)),
        compiler_params=_cparams(("arbitrary",)),
        name="na_attn",
    )(pa, pa, pa, *([bias] * NA_BPS))


def _mla_up(cq, ckv, krb, gq_ref, gkv_ref, wq_ref, wk_ref, wvt_ref, cos_ref, sin_ref, q_ref, k_ref, vt_ref):
    tm = cq.shape[0]
    cqn = _rms_norm(cq, gq_ref[...]).astype(BF16)
    ckvn = _rms_norm(ckv, gkv_ref[...]).astype(BF16)
    zeros = jnp.zeros((tm, MLA_ROPE), F32)
    cos = jnp.concatenate([cos_ref[...], cos_ref[...], zeros], axis=1)
    sin = jnp.concatenate([sin_ref[...], sin_ref[...], zeros], axis=1)

    def rotary(t):
        return t * cos + pltpu.roll(t, MLA_ROPE, 1) * sin

    kpe = rotary(krb).astype(BF16)
    ones = jnp.ones((MLA_VT - MLA_V, tm), BF16)
    scale = (MLA_NOPE + MLA_ROPE) ** -0.5 * LOG2_E
    k_all = jnp.dot(ckvn, wk_ref[...], preferred_element_type=F32).astype(BF16)
    for h in range(MLA_HEADS):
        qh = jnp.dot(cqn, wq_ref[:, h * MLA_QK:(h + 1) * MLA_QK], preferred_element_type=F32)
        q_ref[h, :, 0:HEAD_DIM] = (qh[:, 0:HEAD_DIM] * scale).astype(BF16)
        q_ref[h, :, HEAD_DIM:MLA_QK] = (rotary(qh[:, HEAD_DIM:MLA_QK]) * scale).astype(BF16)
        k_ref[h, :, 0:HEAD_DIM] = k_all[:, h * MLA_NOPE:(h + 1) * MLA_NOPE]
        k_ref[h, :, HEAD_DIM:MLA_QK] = kpe
        vt_ref[h, 0:MLA_V, :] = _dot_nt(wvt_ref[h * MLA_V:(h + 1) * MLA_V, :], ckvn).astype(BF16)
        vt_ref[h, MLA_V:MLA_VT, :] = ones


MLA_TQ = 256
MLA_TK = 1024


def _mla_attn_kernel(*refs, with_ada):
    qe_ref, qo_ref, k_ref, vt_ref, wo_ref, wgu_ref, wdn_ref = refs[:7]
    ada_in = refs[7:10] if with_ada else ()
    outs = refs[7 + len(ada_in):]
    oe_ref, oo_ref, wo_b_ref, wgu_b_ref, wdn_b_ref = outs[:5]
    ada_out = outs[5:6] if with_ada else ()
    s0_sc, s1_sc, s2_sc, s3_sc, m0_sc, m1_sc, m2_sc, m3_sc = outs[5 + len(ada_out):]
    cast_refs = ((wo_ref, wo_b_ref), (wgu_ref, wgu_b_ref), (wdn_ref, wdn_b_ref))
    j = pl.program_id(0)

    @pl.when(j == 0)
    def _():
        for ref in (s2_sc, s3_sc, m2_sc, m3_sc):
            ref[...] = jnp.zeros_like(ref)

    def stage(q_ref, sa_sc, ma_sc, sb_sc, mb_sc, vt_ref, o_ref):
        q = q_ref[0]
        tq = q.shape[0]
        m_prev = mb_sc[...]
        mx = jnp.full((8, tq), NEG_INF, F32)
        acc = jnp.zeros((MLA_VT, tq), F32)
        for c in range(SEQ // MLA_TK):
            keys = slice(c * MLA_TK, (c + 1) * MLA_TK)
            p = jnp.exp2(sb_sc[keys, :] - m_prev).astype(BF16)
            acc = acc + jnp.dot(vt_ref[0, :, keys], p, preferred_element_type=F32)
            s = _dot_nt(k_ref[0, keys, :], q)
            sa_sc[keys, :] = s
            mx = jnp.maximum(mx, jnp.max(s.reshape(MLA_TK // 8, 8, tq), axis=0))
        ma_sc[...] = jnp.max(mx, axis=0, keepdims=True)
        o = acc[0:MLA_V, :] * (1.0 / acc[MLA_V:MLA_V + 1, :])
        o_ref[...] = o.T

    def step(a_even, a_odd, b_even, b_odd):
        for w_ref, wb_ref in cast_refs:
            wb_ref[...] = w_ref[...].astype(BF16)
        if with_ada:
            _ada_slab(*ada_in, *ada_out)
        stage(qe_ref, *a_even, *b_even, vt_ref, oe_ref)
        stage(qo_ref, *a_odd, *b_odd, vt_ref, oo_ref)

    buf = ((s0_sc, m0_sc), (s1_sc, m1_sc), (s2_sc, m2_sc), (s3_sc, m3_sc))
    parity = lax.rem(j, 2)

    @pl.when(parity == 0)
    def _():
        step(buf[0], buf[1], buf[2], buf[3])

    @pl.when(parity == 1)
    def _():
        step(buf[2], buf[3], buf[0], buf[1])


WO_SLAB = (32, D_MODEL)
WGU_SLAB = (256, 1024)
WDN_SLAB = (64, D_MODEL)


def _mla_attn_call(l, q, k, vt, w_o, w_gu, w_down, ada=None):
    tq = MLA_TQ
    npair = SEQ // (2 * tq)
    last = MLA_HEADS * npair - 1
    nstep = last + 2
    pair_a = lambda j: jnp.minimum(j, last)
    pair_b = lambda j: jnp.maximum(j - 1, 0)
    half = jax.ShapeDtypeStruct((SEQ // 2, D_B), F32)
    out_spec = pl.BlockSpec((tq, MLA_V), lambda j: (pair_b(j) % npair, pair_b(j) // npair))

    def slab_specs(shape, slab):
        nr, nc = shape[0] // slab[0], shape[1] // slab[1]
        assert nr * slab[0] == shape[0] and nc * slab[1] == shape[1] and nr * nc <= nstep
        idx = lambda j: jnp.minimum(j, nr * nc - 1)
        return (pl.BlockSpec((None,) + slab, lambda j: (l, idx(j) // nc, idx(j) % nc)),
                pl.BlockSpec(slab, lambda j: (idx(j) // nc, idx(j) % nc)))

    wo_in, wo_out = slab_specs(w_o.shape[1:], WO_SLAB)
    wgu_in, wgu_out = slab_specs(w_gu.shape[1:], WGU_SLAB)
    wdn_in, wdn_out = slab_specs(w_down.shape[1:], WDN_SLAB)
    ada_args, ada_in, ada_out, ada_shape = (), [], (), ()
    if ada is not None:
        assert ADA_NSLAB <= nstep
        ada_args = ada
        ada_in = [
            pl.BlockSpec((D_MODEL, 128), lambda j: (0, 0)),
            pl.BlockSpec((None, D_MODEL, ADA_SLAB), lambda j: (_ada_slab_index(j)[0], 0, _ada_slab_index(j)[1])),
            pl.BlockSpec((None, 1, ADA_SLAB), lambda j: (_ada_slab_index(j)[0], 0, _ada_slab_index(j)[1])),
        ]
        ada_out = (pl.BlockSpec((1, ADA_SLAB), lambda j: (0, _ada_slab_index(j)[2])),)
        ada_shape = (jax.ShapeDtypeStruct((1, ADA_NSLAB * ADA_SLAB), F32),)
    return pl.pallas_call(
        functools.partial(_mla_attn_kernel, with_ada=ada is not None),
        out_shape=(half, half,
                   jax.ShapeDtypeStruct(w_o.shape[1:], BF16),
                   jax.ShapeDtypeStruct(w_gu.shape[1:], BF16),
                   jax.ShapeDtypeStruct(w_down.shape[1:], BF16)) + ada_shape,
        grid=(nstep,),
        in_specs=[
            pl.BlockSpec((1, tq, MLA_QK), lambda j: (pair_a(j) // npair, 2 * (pair_a(j) % npair), 0)),
            pl.BlockSpec((1, tq, MLA_QK), lambda j: (pair_a(j) // npair, 2 * (pair_a(j) % npair) + 1, 0)),
            pl.BlockSpec((1, SEQ, MLA_QK), lambda j: (pair_a(j) // npair, 0, 0)),
            pl.BlockSpec((1, MLA_VT, SEQ), lambda j: (pair_b(j) // npair, 0, 0), pipeline_mode=pl.Buffered(1)),
            wo_in, wgu_in, wdn_in,
        ] + ada_in,
        out_specs=(out_spec, out_spec, wo_out, wgu_out, wdn_out) + ada_out,
        scratch_shapes=[pltpu.VMEM((SEQ, tq), F32)] * 4 + [pltpu.VMEM((1, tq), F32)] * 4,
        compiler_params=_cparams(("arbitrary",)),
        name="mla_attn",
    )(q, q, k, vt, w_o, w_gu, w_down, *ada_args)


_SWA_SLOPES = tuple(2.0 ** (-8.0 * (i + 1) / SWA_HEADS) for i in range(SWA_HEADS))


SWA_NB = 16


def _swa_kernel(sink_ref, q_ref, kp_ref, kc_ref, kn_ref, vp_ref, vc_ref, vn_ref, o_ref):
    g_kv = pl.program_id(0)
    j = pl.program_id(1)
    t = SWA_BLOCK
    rows = SWA_GROUP * t
    kwin = jnp.concatenate([kp_ref[...], kc_ref[...], kn_ref[...]], axis=0)
    vwin = jnp.concatenate([vp_ref[...], vc_ref[...], vn_ref[...]], axis=0)
    ones = jnp.ones((3 * t, HEAD_DIM), BF16)
    ri = lax.broadcasted_iota(jnp.int32, (rows, 3 * t), 0)
    ci = lax.broadcasted_iota(jnp.int32, (rows, 3 * t), 1)
    grp = jnp.right_shift(ri, 7)
    dist = jnp.abs(jnp.bitwise_and(ri, t - 1) - (ci - t))
    slope_lo = jnp.where(grp == 0, _SWA_SLOPES[0], jnp.where(grp == 1, _SWA_SLOPES[1], _SWA_SLOPES[2]))
    slope_hi = jnp.where(grp == 0, _SWA_SLOPES[3], jnp.where(grp == 1, _SWA_SLOPES[4], _SWA_SLOPES[5]))
    slope = jnp.where(g_kv == 0, slope_lo, slope_hi)
    band = jnp.where(dist <= SWA_WINDOW, (-LOG2_E) * slope * dist.astype(F32), NEG_INF)
    rcol = jnp.right_shift(lax.broadcasted_iota(jnp.int32, (rows, 1), 0), 7)
    base = g_kv * SWA_GROUP
    sink = LOG2_E * jnp.where(rcol == 0, sink_ref[base],
                              jnp.where(rcol == 1, sink_ref[base + 1], sink_ref[base + 2]))
    first_cols = jnp.where(j == 0, t, 0)
    last_cols = jnp.where(j == pl.num_programs(1) - 1, 2 * t, 3 * t)
    for b in range(SWA_NB):
        q = jnp.concatenate([q_ref[b * t:(b + 1) * t, g * t:(g + 1) * t] for g in range(SWA_GROUP)], axis=0)
        s = _dot_nt(q, kwin[b * t:(b + 3) * t, :]) + band
        if b == 0:
            s = jnp.where(ci < first_cols, NEG_INF, s)
        if b == SWA_NB - 1:
            s = jnp.where(ci >= last_cols, NEG_INF, s)
        m = jnp.maximum(jnp.max(s, axis=1, keepdims=True), sink)
        p = jnp.exp2(s - m).astype(BF16)
        vext = jnp.concatenate([vwin[b * t:(b + 3) * t, :], ones], axis=1)
        acc = jnp.dot(p, vext, preferred_element_type=F32)
        l = acc[:, HEAD_DIM:HEAD_DIM + 1] + jnp.exp2(sink - m)
        o = acc[:, 0:HEAD_DIM] * (1.0 / l)
        for g in range(SWA_GROUP):
            o_ref[b * t:(b + 1) * t, g * t:(g + 1) * t] = o[g * t:(g + 1) * t, :]


def _swa_call(sink, pc):
    t = SWA_BLOCK
    nb = SEQ // t
    tq = SWA_NB * t
    kcol = SWA_HEADS
    vcol = SWA_HEADS + SWA_KV_HEADS
    prev = lambda j: jnp.maximum(SWA_NB * j - 1, 0)
    nxt = lambda j: jnp.minimum(SWA_NB * (j + 1), nb - 1)
    return pl.pallas_call(
        _swa_kernel,
        out_shape=jax.ShapeDtypeStruct((SEQ, D_C), F32),
        grid=(SWA_KV_HEADS, SEQ // tq),
        in_specs=[
            pl.BlockSpec(memory_space=pltpu.SMEM),
            pl.BlockSpec((tq, SWA_GROUP * t), lambda g, j: (j, g)),
            pl.BlockSpec((t, t), lambda g, j: (prev(j), kcol + g)),
            pl.BlockSpec((tq, t), lambda g, j: (j, kcol + g)),
            pl.BlockSpec((t, t), lambda g, j: (nxt(j), kcol + g)),
            pl.BlockSpec((t, t), lambda g, j: (prev(j), vcol + g)),
            pl.BlockSpec((tq, t), lambda g, j: (j, vcol + g)),
            pl.BlockSpec((t, t), lambda g, j: (nxt(j), vcol + g)),
        ],
        out_specs=pl.BlockSpec((tq, SWA_GROUP * t), lambda g, j: (j, g)),
        compiler_params=_cparams(("parallel", "arbitrary")),
        name="swa_attn",
    )(sink, pc, pc, pc, pc, pc, pc, pc)


OUTPROJ_TM = 2 * MLA_TQ


def _outproj_kernel(ya_ref, ybe_ref, ybo_ref, yc_ref, x_ref, mod_ref, gn_ref, w_ref, lg_ref, lb_ref,
                    o_ref, u_ref):
    hm = OUTPROJ_TM // 2
    gate = 1.0 + mod_ref[2:3, :]
    for half, yb_ref in enumerate((ybe_ref, ybo_ref)):
        rows = slice(half * hm, (half + 1) * hm)
        acc = None
        for y, c0 in ((ya_ref[rows, :], 0), (yb_ref[...], D_A), (yc_ref[rows, :], D_A + D_B)):
            c1 = c0 + y.shape[1]
            yn = _rms_norm(y, gn_ref[:, c0:c1]).astype(BF16)
            part = jnp.dot(yn, w_ref[c0:c1, :], preferred_element_type=F32)
            acc = part if acc is None else acc + part
        x1 = _layer_norm(DEEPNORM_ALPHA * x_ref[rows, :] + gate * acc, lg_ref[...], lb_ref[...])
        o_ref[rows, :] = x1
        u_ref[rows, :] = (x1 * (1.0 + mod_ref[4:5, :]) + mod_ref[3:4, :]).astype(BF16)


def _outproj_call(l, ya, yb_even, yb_odd, yc, x2, mod, vec, w_o):
    tm = OUTPROJ_TM
    row = lambda r, w: pl.BlockSpec((r, w), lambda i: (i, 0))
    return pl.pallas_call(
        _outproj_kernel,
        out_shape=(jax.ShapeDtypeStruct((SEQ, D_MODEL), F32), jax.ShapeDtypeStruct((SEQ, D_MODEL), BF16)),
        grid=(SEQ // tm,),
        in_specs=[row(tm, D_A), row(tm // 2, D_B), row(tm // 2, D_B), row(tm, D_C), row(tm, D_MODEL),
                  pl.BlockSpec((None, 6, D_MODEL), lambda i: (l, 0, 0)), _vec_spec(l, "out_norm"),
                  pl.BlockSpec((D_MODEL, D_MODEL), lambda i: (0, 0), pipeline_mode=pl.Buffered(1)),
                  _vec_spec(l, "ln1_g"), _vec_spec(l, "ln1_b")],
        out_specs=(row(tm, D_MODEL), row(tm, D_MODEL)),
        compiler_params=_cparams(("parallel",)),
        name="out_proj_ln",
    )(ya, yb_even, yb_odd, yc, x2, mod, vec, w_o, vec, vec)


FFN_TM = 1024
FFN_SUB = 512
FFN_SUB_LAST = 256
FFN_TF = 512


def _ffn_kernel(u_ref, x_hbm, mod_ref, wg_ref, wu_ref, wd_ref, lg_ref, lb_ref, o_ref, x_sc, x_sem):
    i = pl.program_id(0)
    f = pl.program_id(1)
    tm = o_ref.shape[0]
    x_copy = pltpu.make_async_copy(x_hbm.at[pl.ds(pl.multiple_of(i * tm, tm), tm), :], x_sc, x_sem)

    last = pl.num_programs(1) - 1

    def down_partial(rows):
        u = u_ref[rows, :]
        g = jnp.dot(u, wg_ref[...], preferred_element_type=F32)
        up = jnp.dot(u, wu_ref[...], preferred_element_type=F32)
        hdn = (g * (1.0 / (1.0 + jnp.exp(-g))) * up).astype(BF16)
        return jnp.dot(hdn, wd_ref[...], preferred_element_type=F32)

    @pl.when(f == 0)
    def _():
        x_copy.start()
        o_ref[...] = jnp.zeros_like(o_ref)

    @pl.when(f < last)
    def _():
        for r in range(0, tm, FFN_SUB):
            rows = slice(r, r + FFN_SUB)
            o_ref[rows, :] += down_partial(rows)

    @pl.when(f == last)
    def _():
        x_copy.wait()
        gate = 1.0 + mod_ref[5:6, :]
        for r in range(0, tm, FFN_SUB_LAST):
            rows = slice(r, r + FFN_SUB_LAST)
            acc = o_ref[rows, :] + down_partial(rows)
            o_ref[rows, :] = _layer_norm(DEEPNORM_ALPHA * x_sc[rows, :] + gate * acc, lg_ref[...], lb_ref[...])


def _ffn_call(l, u, x2, mod, w_gu, w_down, vec):
    tm, tf = FFN_TM, FFN_TF
    nf = D_FF // tf
    return pl.pallas_call(
        _ffn_kernel,
        out_shape=jax.ShapeDtypeStruct((SEQ, D_MODEL), F32),
        grid=(SEQ // tm, nf),
        in_specs=[
            pl.BlockSpec((tm, D_MODEL), lambda i, f: (i, 0)),
            pl.BlockSpec(memory_space=pl.ANY),
            pl.BlockSpec((None, 6, D_MODEL), lambda i, f: (l, 0, 0)),
            pl.BlockSpec((D_MODEL, tf), lambda i, f: (0, f)),
            pl.BlockSpec((D_MODEL, tf), lambda i, f: (0, nf + f)),
            pl.BlockSpec((tf, D_MODEL), lambda i, f: (f, 0)),
            _vec_spec(l, "ln2_g"), _vec_spec(l, "ln2_b"),
        ],
        out_specs=pl.BlockSpec((tm, D_MODEL), lambda i, f: (i, 0)),
        scratch_shapes=[pltpu.VMEM((tm, D_MODEL), F32), pltpu.SemaphoreType.DMA(())],
        compiler_params=_cparams(("arbitrary", "arbitrary")),
        name="ffn_ln",
    )(u, x2, mod, w_gu, w_gu, w_down, vec, vec)


def _rot_half_cols(w):
    half = w.shape[-1] // 2
    return jnp.concatenate([-w[..., half:], w[..., :half]], axis=-1)


def _rope_tables():
    half = MLA_ROPE // 2
    inv = ROPE_THETA ** (-jnp.arange(half, dtype=F32) / half)
    ang = jnp.arange(SEQ, dtype=F32)[:, None] * inv[None, :]
    return jnp.cos(ang), jnp.sin(ang)


def _colscale():
    cs = np.ones((1, P_W), np.float32)
    cs[0, 0:D_A] = HEAD_DIM ** -0.5 * LOG2_E
    c0 = IN_A + PB_W
    cs[0, c0:c0 + D_C] = HEAD_DIM ** -0.5 * LOG2_E
    return jnp.asarray(cs)


def kernel(x, c, w_ada, b_ada, w_in, na_rpb, mla_q_norm, mla_kv_norm, mla_w_uq, mla_w_ukv,
           swa_sink, out_norm_g, w_o, ln1_g, ln1_b, w_gu, w_down, ln2_g, ln2_b):
    assert x.shape == (1, SEQ, D_MODEL)
    x2 = x.reshape(SEQ, D_MODEL)
    b_ada3 = b_ada.reshape(DEPTH, 1, -1)
    mod_head, cond_lanes = _ada_head_call(c.reshape(D_MODEL, 1), w_ada, b_ada3)
    mod = jnp.concatenate([mod_head, jnp.zeros((1, 6 * D_MODEL - ADA_HEAD), F32)], axis=1).reshape(1, 6, D_MODEL)
    cos_t, sin_t = _rope_tables()
    colscale = _colscale()
    w_p = _winprep_call(jnp.swapaxes(w_in, 1, 2))
    wq = mla_w_uq.reshape(DEPTH, MLA_Q_RANK, MLA_HEADS, MLA_NOPE + MLA_ROPE)
    wq = jnp.concatenate([wq, _rot_half_cols(wq[..., MLA_NOPE:])], axis=-1)
    wq = wq.reshape(DEPTH, MLA_Q_RANK, MLA_HEADS * MLA_QK).astype(BF16)
    wkv = mla_w_ukv.reshape(DEPTH, MLA_KV_RANK, MLA_HEADS, MLA_NOPE + MLA_V)
    wk = wkv[..., :MLA_NOPE].reshape(DEPTH, MLA_KV_RANK, MLA_HEADS * MLA_NOPE).astype(BF16)
    wvt = jnp.transpose(wkv[..., MLA_NOPE:], (0, 2, 3, 1)).reshape(DEPTH, MLA_HEADS * MLA_V, MLA_KV_RANK)
    wvt = wvt.astype(BF16)
    vec_parts = dict(out_norm=out_norm_g, ln1_g=ln1_g, ln1_b=ln1_b, ln2_g=ln2_g, ln2_b=ln2_b,
                     q_norm=mla_q_norm, kv_norm=mla_kv_norm)
    vec = jnp.concatenate([vec_parts[name] for name, _ in _VEC_ORDER], axis=1).reshape(DEPTH, 1, VEC_W)
    for l in range(DEPTH):
        pa, pc, q, k, vt = _inproj_call(l, x2, mod, l, w_p, colscale, vec, wq, wk, wvt, cos_t, sin_t)
        bias = _na_bias_call(na_rpb[l].reshape(-1))
        ya = _na_call(pa, bias)
        if l == 0:
            yb_even, yb_odd, w_o_b, w_gu_b, w_down_b, mod_rest = _mla_attn_call(
                l, q, k, vt, w_o, w_gu, w_down, ada=(cond_lanes, w_ada, b_ada3))
            mod = jnp.concatenate([mod_head, mod_rest], axis=1).reshape(DEPTH, 6, D_MODEL)
        else:
            yb_even, yb_odd, w_o_b, w_gu_b, w_down_b = _mla_attn_call(l, q, k, vt, w_o, w_gu, w_down)
        yc = _swa_call(swa_sink[l], pc)
        x2, u = _outproj_call(l, ya, yb_even, yb_odd, yc, x2, mod, vec, w_o_b)
        x2 = _ffn_call(l, u, x2, mod, w_gu_b, w_down_b, vec)
    return x2.reshape(1, SEQ, D_MODEL)
```

```python
import functools

import numpy as np
import jax
import jax.numpy as jnp
from jax import lax
from jax.experimental import pallas as pl
from jax.experimental.pallas import tpu as pltpu

F32 = jnp.float32
BF16 = jnp.bfloat16

D_MODEL = 2048
SEQ = 8192
DEPTH = 2
GRID_W = 64
GRID_ROWS = SEQ // GRID_W
HEAD_DIM = 128
NA_HEADS = 4
NA_WIN_ROWS = 8
NA_WIN_COLS = 16
MLA_HEADS = 6
MLA_Q_RANK = 512
MLA_KV_RANK = 256
MLA_NOPE = 128
MLA_ROPE = 64
MLA_V = 128
ROPE_THETA = 10000.0
SWA_HEADS = 6
SWA_KV_HEADS = 2
SWA_GROUP = SWA_HEADS // SWA_KV_HEADS
SWA_WINDOW = 128
SWA_BLOCK = 128
D_A = NA_HEADS * HEAD_DIM
D_B = MLA_HEADS * MLA_V
D_C = SWA_HEADS * HEAD_DIM
IN_A = 3 * D_A
IN_B = MLA_Q_RANK + MLA_KV_RANK + MLA_ROPE
IN_C = (SWA_HEADS + 2 * SWA_KV_HEADS) * HEAD_DIM
D_FF = 5632
DEEPNORM_ALPHA = (2 * DEPTH) ** 0.25
LN_EPS = 1e-5
RMS_EPS = 1e-6
NEG_INF = -1e30
LOG2_E = 1.4426950408889634

PB_W = MLA_Q_RANK + MLA_KV_RANK + 2 * MLA_ROPE
P_W = IN_A + PB_W + IN_C
MLA_QK = 2 * HEAD_DIM
MLA_VT = MLA_V + 16

NA_QROWS = 4
NA_KROWS = 12
NA_TQ = NA_QROWS * GRID_W
NA_TK = NA_KROWS * GRID_W
NA_NBLK = GRID_ROWS // NA_QROWS

VMEM_LIMIT = 56 * 1024 * 1024


def _cparams(sem):
    return pltpu.CompilerParams(dimension_semantics=sem, vmem_limit_bytes=VMEM_LIMIT)


def _layer_norm(z, g, b):
    mu = jnp.mean(z, axis=-1, keepdims=True)
    zc = z - mu
    var = jnp.mean(zc * zc, axis=-1, keepdims=True)
    return zc * lax.rsqrt(var + LN_EPS) * g + b


def _rms_norm(x, g):
    ms = jnp.mean(x * x, axis=-1, keepdims=True)
    return x * lax.rsqrt(ms + RMS_EPS) * g


def _dot_nt(a, b):
    return lax.dot_general(a, b, (((1,), (1,)), ((), ())), preferred_element_type=F32)


_VEC_ORDER = (("out_norm", D_MODEL), ("ln1_g", D_MODEL), ("ln1_b", D_MODEL), ("ln2_g", D_MODEL),
              ("ln2_b", D_MODEL), ("q_norm", MLA_Q_RANK), ("kv_norm", MLA_KV_RANK))
_VEC_OFF = {}
_off = 0
for _name, _w in _VEC_ORDER:
    assert _off % _w == 0
    _VEC_OFF[_name] = (_off // _w, _w)
    _off += _w
VEC_W = _off


def _vec_spec(l, name):
    blk, w = _VEC_OFF[name]
    return pl.BlockSpec((None, 1, w), lambda *grid_idx: (l, 0, blk))


ADA_TN = 2048
ADA_RC = 256
ADA_HEAD = 2 * D_MODEL
ADA_SLAB = 256
ADA_NSLAB = (DEPTH * 6 * D_MODEL - ADA_HEAD) // ADA_SLAB


def _ada_slab(cb_ref, w_ref, b_ref, o_ref):
    tn = o_ref.shape[-1]
    acc = jnp.zeros((8, tn), F32)
    for r in range(0, D_MODEL, ADA_RC):
        cb = jnp.concatenate([cb_ref[r:r + ADA_RC, :]] * (tn // 128), axis=1)
        prod = w_ref[r:r + ADA_RC, :] * cb
        acc = acc + jnp.sum(prod.reshape(ADA_RC // 8, 8, tn), axis=0)
    o_ref[...] = jnp.sum(acc, axis=0, keepdims=True) + b_ref[...]


def _ada_head_kernel(c_ref, w_ref, b_ref, o_ref, cb_ref):
    for r in range(0, D_MODEL, ADA_RC):
        c = c_ref[r:r + ADA_RC, :]
        cond = c * (1.0 / (1.0 + jnp.exp(-c)))
        cb_ref[r:r + ADA_RC, :] = jnp.broadcast_to(cond, (ADA_RC, 128))
    _ada_slab(cb_ref, w_ref, b_ref, o_ref)


def _ada_head_call(c_col, w_ada, b_ada3):
    return pl.pallas_call(
        _ada_head_kernel,
        out_shape=(jax.ShapeDtypeStruct((1, ADA_HEAD), F32), jax.ShapeDtypeStruct((D_MODEL, 128), F32)),
        grid=(ADA_HEAD // ADA_TN,),
        in_specs=[
            pl.BlockSpec((D_MODEL, 1), lambda j: (0, 0)),
            pl.BlockSpec((None, D_MODEL, ADA_TN), lambda j: (0, 0, j)),
            pl.BlockSpec((None, 1, ADA_TN), lambda j: (0, 0, j)),
        ],
        out_specs=(pl.BlockSpec((1, ADA_TN), lambda j: (0, j)),
                   pl.BlockSpec((D_MODEL, 128), lambda j: (0, 0))),
        compiler_params=_cparams(("arbitrary",)),
        name="ada_head",
    )(c_col, w_ada, b_ada3)


def _ada_slab_index(j):
    s = jnp.minimum(j, ADA_NSLAB - 1)
    n0 = (6 * D_MODEL - ADA_HEAD) // ADA_SLAB
    return jnp.where(s < n0, 0, 1), jnp.where(s < n0, s + ADA_HEAD // ADA_SLAB, s - n0), s


WPREP_TN = 1024
KR0 = IN_A + MLA_Q_RANK + MLA_KV_RANK


def _winprep_kernel(w_ref, o_ref):
    half = MLA_ROPE // 2
    kr1 = KR0 + MLA_ROPE
    o_ref[0:kr1, :] = w_ref[0:kr1, :].astype(BF16)
    o_ref[kr1:kr1 + half, :] = (-w_ref[KR0 + half:kr1, :]).astype(BF16)
    o_ref[kr1 + half:kr1 + MLA_ROPE, :] = w_ref[KR0:KR0 + half, :].astype(BF16)
    o_ref[kr1 + MLA_ROPE:, :] = w_ref[kr1:, :].astype(BF16)


def _winprep_call(w_in_t):
    tn = WPREP_TN
    return pl.pallas_call(
        _winprep_kernel,
        out_shape=jax.ShapeDtypeStruct((DEPTH, P_W, D_MODEL), BF16),
        grid=(DEPTH, D_MODEL // tn),
        in_specs=[pl.BlockSpec((None, w_in_t.shape[1], tn), lambda l, i: (l, 0, i))],
        out_specs=pl.BlockSpec((None, P_W, tn), lambda l, i: (l, 0, i)),
        compiler_params=_cparams(("parallel", "parallel")),
        name="w_in_prep",
    )(w_in_t)


INPROJ_TM = 512
_INPROJ_CHUNKS = (
    (0, 512, 0, 0), (512, 1024, 0, 512), (1024, 1536, 0, 1024),
    (2432, 2944, 1, 0), (2944, 3456, 1, 512), (3456, 3712, 1, 1024),
)
_PB0 = IN_A


def _inproj_kernel(x_ref, mod_ref, w_ref, cs_ref, gq_ref, gkv_ref, wq_ref, wk_ref, wvt_ref, cos_ref, sin_ref,
                   oa_ref, oc_ref, q_ref, k_ref, vt_ref):
    outs = (oa_ref, oc_ref)
    sh = mod_ref[0:1, :]
    sc = mod_ref[1:2, :]
    u = (x_ref[...] * (1.0 + sc) + sh).astype(BF16)
    for c0, c1, oi, off in _INPROJ_CHUNKS:
        acc = _dot_nt(u, w_ref[c0:c1, :]) * cs_ref[:, c0:c1]
        outs[oi][:, off:off + (c1 - c0)] = acc.astype(outs[oi].dtype)
    cq = _dot_nt(u, w_ref[_PB0:_PB0 + MLA_Q_RANK, :])
    ckv_kr = _dot_nt(u, w_ref[_PB0 + MLA_Q_RANK:_PB0 + PB_W, :])
    _mla_up(cq, ckv_kr[:, 0:MLA_KV_RANK], ckv_kr[:, MLA_KV_RANK:], gq_ref, gkv_ref, wq_ref, wk_ref, wvt_ref,
            cos_ref, sin_ref, q_ref, k_ref, vt_ref)


def _inproj_call(l, x2, mod, lmod, w_p, colscale, vec, wq, wk, wvt, cos_t, sin_t):
    tm = INPROJ_TM
    hsd = jax.ShapeDtypeStruct((MLA_HEADS, SEQ, MLA_QK), BF16)
    hspec = pl.BlockSpec((MLA_HEADS, tm, MLA_QK), lambda i: (0, i, 0))
    layer = lambda r, c: pl.BlockSpec((None, r, c), lambda i: (l, 0, 0))
    return pl.pallas_call(
        _inproj_kernel,
        out_shape=(jax.ShapeDtypeStruct((SEQ, IN_A), BF16),
                   jax.ShapeDtypeStruct((SEQ, IN_C), BF16),
                   hsd, hsd, jax.ShapeDtypeStruct((MLA_HEADS, MLA_VT, SEQ), BF16)),
        grid=(SEQ // tm,),
        in_specs=[
            pl.BlockSpec((tm, D_MODEL), lambda i: (i, 0)),
            pl.BlockSpec((None, 6, D_MODEL), lambda i: (lmod, 0, 0)),
            pl.BlockSpec((None, P_W, D_MODEL), lambda i: (l, 0, 0), pipeline_mode=pl.Buffered(1)),
            pl.BlockSpec((1, P_W), lambda i: (0, 0)),
            _vec_spec(l, "q_norm"), _vec_spec(l, "kv_norm"),
            layer(MLA_Q_RANK, MLA_HEADS * MLA_QK), layer(MLA_KV_RANK, MLA_HEADS * MLA_NOPE),
            layer(MLA_HEADS * MLA_V, MLA_KV_RANK),
            pl.BlockSpec((tm, MLA_ROPE // 2), lambda i: (i, 0)),
            pl.BlockSpec((tm, MLA_ROPE // 2), lambda i: (i, 0)),
        ],
        out_specs=(pl.BlockSpec((tm, IN_A), lambda i: (i, 0)),
                   pl.BlockSpec((tm, IN_C), lambda i: (i, 0)),
                   hspec, hspec, pl.BlockSpec((MLA_HEADS, MLA_VT, tm), lambda i: (0, 0, i))),
        compiler_params=_cparams(("parallel",)),
        name="in_proj",
    )(x2, mod, w_p, colscale, vec, vec, wq, wk, wvt, cos_t, sin_t)


def _na_block_rule(btype, i, j):
    if btype == 0:
        r0 = max(i - NA_WIN_ROWS // 2, 0)
        valid = r0 <= j < r0 + NA_WIN_ROWS
        ro = j - i + (NA_WIN_ROWS - 1)
    elif btype == 1:
        valid = i <= j < i + NA_WIN_ROWS
        ro = j - i + (NA_WIN_ROWS - 1) - NA_WIN_ROWS // 2
    else:
        r = GRID_ROWS - NA_QROWS + i
        ks = GRID_ROWS - NA_KROWS
        r0 = min(r - NA_WIN_ROWS // 2, GRID_ROWS - NA_WIN_ROWS)
        valid = r0 <= ks + j < r0 + NA_WIN_ROWS
        ro = ks + j - r + (NA_WIN_ROWS - 1)
    return ro if valid else None


def _na_bias_kernel(rpb_ref, o_ref):
    h = pl.program_id(0)
    n_ro = 2 * NA_WIN_ROWS - 1
    n_co = 2 * NA_WIN_COLS - 1
    cq = lax.broadcasted_iota(jnp.int32, (GRID_W, GRID_W), 0)
    ck = lax.broadcasted_iota(jnp.int32, (GRID_W, GRID_W), 1)
    c0 = jnp.clip(cq - NA_WIN_COLS // 2, 0, GRID_W - NA_WIN_COLS)
    coff = jnp.clip(ck - cq, -(NA_WIN_COLS - 1), NA_WIN_COLS - 1) + (NA_WIN_COLS - 1)
    neg = jnp.full((GRID_W, GRID_W), NEG_INF, F32)
    tblocks = []
    for ro in range(n_ro):
        tb = neg
        for j in range(n_co):
            tb = jnp.where(coff == j, LOG2_E * rpb_ref[h * (n_ro * n_co) + ro * n_co + j], tb)
        inside = jnp.where(ck >= c0, jnp.where(ck < c0 + NA_WIN_COLS, 1, 0), 0)
        tblocks.append(jnp.where(inside == 1, tb, neg))
    for btype in range(3):
        for i in range(NA_QROWS):
            for jp in range(NA_KROWS // 2):
                pair = []
                for j in (2 * jp, 2 * jp + 1):
                    ro = _na_block_rule(btype, i, j)
                    pair.append(neg if ro is None else tblocks[ro])
                o_ref[0, btype, i * GRID_W:(i + 1) * GRID_W, jp * 128:(jp + 1) * 128] = (
                    jnp.concatenate(pair, axis=1))


def _na_bias_call(rpb_flat):
    return pl.pallas_call(
        _na_bias_kernel,
        out_shape=jax.ShapeDtypeStruct((NA_HEADS, 3, NA_TQ, NA_TK), F32),
        grid=(NA_HEADS,),
        in_specs=[pl.BlockSpec(memory_space=pltpu.SMEM)],
        out_specs=pl.BlockSpec((1, 3, NA_TQ, NA_TK), lambda h: (h, 0, 0, 0)),
        compiler_params=_cparams(("parallel",)),
        name="na_bias",
    )(rpb_flat)


NA_BPS = 4


def _na_kernel(q_ref, k_ref, v_ref, *rest):
    bias_refs, o_ref = rest[:NA_BPS], rest[NA_BPS]
    j = pl.program_id(0)
    ones = jnp.ones((NA_TK, HEAD_DIM), BF16)
    for sub, bias_ref in enumerate(bias_refs):
        b = NA_BPS * j + sub
        ks = jnp.clip(NA_QROWS * b - NA_WIN_ROWS // 2, 0, GRID_ROWS - NA_KROWS) * GRID_W
        ks = pl.multiple_of(ks, GRID_W)
        rows = slice(sub * NA_TQ, (sub + 1) * NA_TQ)
        for h in range(NA_HEADS):
            cols = slice(h * HEAD_DIM, (h + 1) * HEAD_DIM)
            s = _dot_nt(q_ref[rows, cols], k_ref[pl.ds(ks, NA_TK), cols]) + bias_ref[h, 0]
            m = jnp.max(s, axis=1, keepdims=True)
            p = jnp.exp2(s - m).astype(BF16)
            vext = jnp.concatenate([v_ref[pl.ds(ks, NA_TK), cols], ones], axis=1)
            acc = jnp.dot(p, vext, preferred_element_type=F32)
            o_ref[rows, cols] = acc[:, 0:HEAD_DIM] * (1.0 / acc[:, HEAD_DIM:HEAD_DIM + 1])


def _na_call(pa, bias):
    def btype(b):
        return jnp.where(b == 0, 0, jnp.where(b == NA_NBLK - 1, 2, 1))
    bias_specs = [pl.BlockSpec((NA_HEADS, 1, NA_TQ, NA_TK),
                               functools.partial(lambda j, sub: (0, btype(NA_BPS * j + sub), 0, 0), sub=sub))
                  for sub in range(NA_BPS)]
    return pl.pallas_call(
        _na_kernel,
        out_shape=jax.ShapeDtypeStruct((SEQ, D_A), F32),
        grid=(NA_NBLK // NA_BPS,),
        in_specs=[
            pl.BlockSpec((NA_BPS * NA_TQ, D_A), lambda j: (j, 0)),
            pl.BlockSpec((SEQ, D_A), lambda j: (0, 1), pipeline_mode=pl.Buffered(1)),
            pl.BlockSpec((SEQ, D_A), lambda j: (0, 2), pipeline_mode=pl.Buffered(1)),
        ] + bias_specs,
        out_specs=pl.BlockSpec((NA_BPS * NA_TQ, D_A), lambda j: (j, 0)),
        compiler_params=_cparams(("arbitrary",)),
        name="na_attn",
    )(pa, pa, pa, *([bias] * NA_BPS))


def _mla_up(cq, ckv, krb, gq_ref, gkv_ref, wq_ref, wk_ref, wvt_ref, cos_ref, sin_ref, q_ref, k_ref, vt_ref):
    tm = cq.shape[0]
    cqn = _rms_norm(cq, gq_ref[...]).astype(BF16)
    ckvn = _rms_norm(ckv, gkv_ref[...]).astype(BF16)
    zeros = jnp.zeros((tm, MLA_ROPE), F32)
    cos = jnp.concatenate([cos_ref[...], cos_ref[...], zeros], axis=1)
    sin = jnp.concatenate([sin_ref[...], sin_ref[...], zeros], axis=1)

    def rotary(t):
        return t * cos + pltpu.roll(t, MLA_ROPE, 1) * sin

    kpe = rotary(krb).astype(BF16)
    ones = jnp.ones((MLA_VT - MLA_V, tm), BF16)
    scale = (MLA_NOPE + MLA_ROPE) ** -0.5 * LOG2_E
    k_all = jnp.dot(ckvn, wk_ref[...], preferred_element_type=F32).astype(BF16)
    for h in range(MLA_HEADS):
        qh = jnp.dot(cqn, wq_ref[:, h * MLA_QK:(h + 1) * MLA_QK], preferred_element_type=F32)
        q_ref[h, :, 0:HEAD_DIM] = (qh[:, 0:HEAD_DIM] * scale).astype(BF16)
        q_ref[h, :, HEAD_DIM:MLA_QK] = (rotary(qh[:, HEAD_DIM:MLA_QK]) * scale).astype(BF16)
        k_ref[h, :, 0:HEAD_DIM] = k_all[:, h * MLA_NOPE:(h + 1) * MLA_NOPE]
        k_ref[h, :, HEAD_DIM:MLA_QK] = kpe
        vt_ref[h, 0:MLA_V, :] = _dot_nt(wvt_ref[h * MLA_V:(h + 1) * MLA_V, :], ckvn).astype(BF16)
        vt_ref[h, MLA_V:MLA_VT, :] = ones


MLA_TQ = 256
MLA_TK = 1024


def _mla_attn_kernel(*refs, with_ada):
    qe_ref, qo_ref, k_ref, vt_ref, wo_ref, wgu_ref, wdn_ref = refs[:7]
    ada_in = refs[7:10] if with_ada else ()
    outs = refs[7 + len(ada_in):]
    oe_ref, oo_ref, wo_b_ref, wgu_b_ref, wdn_b_ref = outs[:5]
    ada_out = outs[5:6] if with_ada else ()
    s0_sc, s1_sc, s2_sc, s3_sc, m0_sc, m1_sc, m2_sc, m3_sc = outs[5 + len(ada_out):]
    cast_refs = ((wo_ref, wo_b_ref), (wgu_ref, wgu_b_ref), (wdn_ref, wdn_b_ref))
    j = pl.program_id(0)

    @pl.when(j == 0)
    def _():
        for ref in (s2_sc, s3_sc, m2_sc, m3_sc):
            ref[...] = jnp.zeros_like(ref)

    def stage(q_ref, sa_sc, ma_sc, sb_sc, mb_sc, vt_ref, o_ref):
        q = q_ref[0]
        tq = q.shape[0]
        m_prev = mb_sc[...]
        mx = jnp.full((8, tq), NEG_INF, F32)
        acc = jnp.zeros((MLA_VT, tq), F32)
        for c in range(SEQ // MLA_TK):
            keys = slice(c * MLA_TK, (c + 1) * MLA_TK)
            p = jnp.exp2(sb_sc[keys, :] - m_prev).astype(BF16)
            acc = acc + jnp.dot(vt_ref[0, :, keys], p, preferred_element_type=F32)
            s = _dot_nt(k_ref[0, keys, :], q)
            sa_sc[keys, :] = s
            mx = jnp.maximum(mx, jnp.max(s.reshape(MLA_TK // 8, 8, tq), axis=0))
        ma_sc[...] = jnp.max(mx, axis=0, keepdims=True)
        o = acc[0:MLA_V, :] * (1.0 / acc[MLA_V:MLA_V + 1, :])
        o_ref[...] = o.T

    def step(a_even, a_odd, b_even, b_odd):
        for w_ref, wb_ref in cast_refs:
            wb_ref[...] = w_ref[...].astype(BF16)
        if with_ada:
            _ada_slab(*ada_in, *ada_out)
        stage(qe_ref, *a_even, *b_even, vt_ref, oe_ref)
        stage(qo_ref, *a_odd, *b_odd, vt_ref, oo_ref)

    buf = ((s0_sc, m0_sc), (s1_sc, m1_sc), (s2_sc, m2_sc), (s3_sc, m3_sc))
    parity = lax.rem(j, 2)

    @pl.when(parity == 0)
    def _():
        step(buf[0], buf[1], buf[2], buf[3])

    @pl.when(parity == 1)
    def _():
        step(buf[2], buf[3], buf[0], buf[1])


WO_SLAB = (32, D_MODEL)
WGU_SLAB = (256, 1024)
WDN_SLAB = (64, D_MODEL)


def _mla_attn_call(l, q, k, vt, w_o, w_gu, w_down, ada=None):
    tq = MLA_TQ
    npair = SEQ // (2 * tq)
    last = MLA_HEADS * npair - 1
    nstep = last + 2
    pair_a = lambda j: jnp.minimum(j, last)
    pair_b = lambda j: jnp.maximum(j - 1, 0)
    half = jax.ShapeDtypeStruct((SEQ // 2, D_B), F32)
    out_spec = pl.BlockSpec((tq, MLA_V), lambda j: (pair_b(j) % npair, pair_b(j) // npair))

    def slab_specs(shape, slab):
        nr, nc = shape[0] // slab[0], shape[1] // slab[1]
        assert nr * slab[0] == shape[0] and nc * slab[1] == shape[1] and nr * nc <= nstep
        idx = lambda j: jnp.minimum(j, nr * nc - 1)
        return (pl.BlockSpec((None,) + slab, lambda j: (l, idx(j) // nc, idx(j) % nc)),
                pl.BlockSpec(slab, lambda j: (idx(j) // nc, idx(j) % nc)))

    wo_in, wo_out = slab_specs(w_o.shape[1:], WO_SLAB)
    wgu_in, wgu_out = slab_specs(w_gu.shape[1:], WGU_SLAB)
    wdn_in, wdn_out = slab_specs(w_down.shape[1:], WDN_SLAB)
    ada_args, ada_in, ada_out, ada_shape = (), [], (), ()
    if ada is not None:
        assert ADA_NSLAB <= nstep
        ada_args = ada
        ada_in = [
            pl.BlockSpec((D_MODEL, 128), lambda j: (0, 0)),
            pl.BlockSpec((None, D_MODEL, ADA_SLAB), lambda j: (_ada_slab_index(j)[0], 0, _ada_slab_index(j)[1])),
            pl.BlockSpec((None, 1, ADA_SLAB), lambda j: (_ada_slab_index(j)[0], 0, _ada_slab_index(j)[1])),
        ]
        ada_out = (pl.BlockSpec((1, ADA_SLAB), lambda j: (0, _ada_slab_index(j)[2])),)
        ada_shape = (jax.ShapeDtypeStruct((1, ADA_NSLAB * ADA_SLAB), F32),)
    return pl.pallas_call(
        functools.partial(_mla_attn_kernel, with_ada=ada is not None),
        out_shape=(half, half,
                   jax.ShapeDtypeStruct(w_o.shape[1:], BF16),
                   jax.ShapeDtypeStruct(w_gu.shape[1:], BF16),
                   jax.ShapeDtypeStruct(w_down.shape[1:], BF16)) + ada_shape,
        grid=(nstep,),
        in_specs=[
            pl.BlockSpec((1, tq, MLA_QK), lambda j: (pair_a(j) // npair, 2 * (pair_a(j) % npair), 0)),
            pl.BlockSpec((1, tq, MLA_QK), lambda j: (pair_a(j) // npair, 2 * (pair_a(j) % npair) + 1, 0)),
            pl.BlockSpec((1, SEQ, MLA_QK), lambda j: (pair_a(j) // npair, 0, 0)),
            pl.BlockSpec((1, MLA_VT, SEQ), lambda j: (pair_b(j) // npair, 0, 0),
                         pipeline_mode=pl.Buffered(1 if ada is not None else 2)),
            wo_in, wgu_in, wdn_in,
        ] + ada_in,
        out_specs=(out_spec, out_spec, wo_out, wgu_out, wdn_out) + ada_out,
        scratch_shapes=[pltpu.VMEM((SEQ, tq), F32)] * 4 + [pltpu.VMEM((1, tq), F32)] * 4,
        compiler_params=_cparams(("arbitrary",)),
        name="mla_attn",
    )(q, q, k, vt, w_o, w_gu, w_down, *ada_args)


_SWA_SLOPES = tuple(2.0 ** (-8.0 * (i + 1) / SWA_HEADS) for i in range(SWA_HEADS))


SWA_NB = 16


def _swa_kernel(sink_ref, q_ref, kp_ref, kc_ref, kn_ref, vp_ref, vc_ref, vn_ref, o_ref):
    g_kv = pl.program_id(0)
    j = pl.program_id(1)
    t = SWA_BLOCK
    rows = SWA_GROUP * t
    kwin = jnp.concatenate([kp_ref[...], kc_ref[...], kn_ref[...]], axis=0)
    vwin = jnp.concatenate([vp_ref[...], vc_ref[...], vn_ref[...]], axis=0)
    ones = jnp.ones((3 * t, HEAD_DIM), BF16)
    ri = lax.broadcasted_iota(jnp.int32, (rows, 3 * t), 0)
    ci = lax.broadcasted_iota(jnp.int32, (rows, 3 * t), 1)
    grp = jnp.right_shift(ri, 7)
    dist = jnp.abs(jnp.bitwise_and(ri, t - 1) - (ci - t))
    slope_lo = jnp.where(grp == 0, _SWA_SLOPES[0], jnp.where(grp == 1, _SWA_SLOPES[1], _SWA_SLOPES[2]))
    slope_hi = jnp.where(grp == 0, _SWA_SLOPES[3], jnp.where(grp == 1, _SWA_SLOPES[4], _SWA_SLOPES[5]))
    slope = jnp.where(g_kv == 0, slope_lo, slope_hi)
    band = jnp.where(dist <= SWA_WINDOW, (-LOG2_E) * slope * dist.astype(F32), NEG_INF)
    rcol = jnp.right_shift(lax.broadcasted_iota(jnp.int32, (rows, 1), 0), 7)
    base = g_kv * SWA_GROUP
    sink = LOG2_E * jnp.where(rcol == 0, sink_ref[base],
                              jnp.where(rcol == 1, sink_ref[base + 1], sink_ref[base + 2]))
    first_cols = jnp.where(j == 0, t, 0)
    last_cols = jnp.where(j == pl.num_programs(1) - 1, 2 * t, 3 * t)
    for b in range(SWA_NB):
        q = jnp.concatenate([q_ref[b * t:(b + 1) * t, g * t:(g + 1) * t] for g in range(SWA_GROUP)], axis=0)
        s = _dot_nt(q, kwin[b * t:(b + 3) * t, :]) + band
        if b == 0:
            s = jnp.where(ci < first_cols, NEG_INF, s)
        if b == SWA_NB - 1:
            s = jnp.where(ci >= last_cols, NEG_INF, s)
        m = jnp.maximum(jnp.max(s, axis=1, keepdims=True), sink)
        p = jnp.exp2(s - m).astype(BF16)
        vext = jnp.concatenate([vwin[b * t:(b + 3) * t, :], ones], axis=1)
        acc = jnp.dot(p, vext, preferred_element_type=F32)
        l = acc[:, HEAD_DIM:HEAD_DIM + 1] + jnp.exp2(sink - m)
        o = acc[:, 0:HEAD_DIM] * (1.0 / l)
        for g in range(SWA_GROUP):
            o_ref[b * t:(b + 1) * t, g * t:(g + 1) * t] = o[g * t:(g + 1) * t, :]


def _swa_call(sink, pc):
    t = SWA_BLOCK
    nb = SEQ // t
    tq = SWA_NB * t
    kcol = SWA_HEADS
    vcol = SWA_HEADS + SWA_KV_HEADS
    prev = lambda j: jnp.maximum(SWA_NB * j - 1, 0)
    nxt = lambda j: jnp.minimum(SWA_NB * (j + 1), nb - 1)
    return pl.pallas_call(
        _swa_kernel,
        out_shape=jax.ShapeDtypeStruct((SEQ, D_C), F32),
        grid=(SWA_KV_HEADS, SEQ // tq),
        in_specs=[
            pl.BlockSpec(memory_space=pltpu.SMEM),
            pl.BlockSpec((tq, SWA_GROUP * t), lambda g, j: (j, g)),
            pl.BlockSpec((t, t), lambda g, j: (prev(j), kcol + g)),
            pl.BlockSpec((tq, t), lambda g, j: (j, kcol + g)),
            pl.BlockSpec((t, t), lambda g, j: (nxt(j), kcol + g)),
            pl.BlockSpec((t, t), lambda g, j: (prev(j), vcol + g)),
            pl.BlockSpec((tq, t), lambda g, j: (j, vcol + g)),
            pl.BlockSpec((t, t), lambda g, j: (nxt(j), vcol + g)),
        ],
        out_specs=pl.BlockSpec((tq, SWA_GROUP * t), lambda g, j: (j, g)),
        compiler_params=_cparams(("parallel", "arbitrary")),
        name="swa_attn",
    )(sink, pc, pc, pc, pc, pc, pc, pc)


OUTPROJ_TM = 2 * MLA_TQ


def _outproj_kernel(ya_ref, ybe_ref, ybo_ref, yc_ref, x_ref, mod_ref, gn_ref, w_ref, lg_ref, lb_ref,
                    o_ref, u_ref):
    hm = OUTPROJ_TM // 2
    gate = 1.0 + mod_ref[2:3, :]
    for half, yb_ref in enumerate((ybe_ref, ybo_ref)):
        rows = slice(half * hm, (half + 1) * hm)
        acc = None
        for y, c0 in ((ya_ref[rows, :], 0), (yb_ref[...], D_A), (yc_ref[rows, :], D_A + D_B)):
            c1 = c0 + y.shape[1]
            yn = _rms_norm(y, gn_ref[:, c0:c1]).astype(BF16)
            part = jnp.dot(yn, w_ref[c0:c1, :], preferred_element_type=F32)
            acc = part if acc is None else acc + part
        x1 = _layer_norm(DEEPNORM_ALPHA * x_ref[rows, :] + gate * acc, lg_ref[...], lb_ref[...])
        o_ref[rows, :] = x1
        u_ref[rows, :] = (x1 * (1.0 + mod_ref[4:5, :]) + mod_ref[3:4, :]).astype(BF16)


def _outproj_call(l, ya, yb_even, yb_odd, yc, x2, mod, vec, w_o):
    tm = OUTPROJ_TM
    row = lambda r, w: pl.BlockSpec((r, w), lambda i: (i, 0))
    return pl.pallas_call(
        _outproj_kernel,
        out_shape=(jax.ShapeDtypeStruct((SEQ, D_MODEL), F32), jax.ShapeDtypeStruct((SEQ, D_MODEL), BF16)),
        grid=(SEQ // tm,),
        in_specs=[row(tm, D_A), row(tm // 2, D_B), row(tm // 2, D_B), row(tm, D_C), row(tm, D_MODEL),
                  pl.BlockSpec((None, 6, D_MODEL), lambda i: (l, 0, 0)), _vec_spec(l, "out_norm"),
                  pl.BlockSpec((D_MODEL, D_MODEL), lambda i: (0, 0), pipeline_mode=pl.Buffered(1)),
                  _vec_spec(l, "ln1_g"), _vec_spec(l, "ln1_b")],
        out_specs=(row(tm, D_MODEL), row(tm, D_MODEL)),
        compiler_params=_cparams(("parallel",)),
        name="out_proj_ln",
    )(ya, yb_even, yb_odd, yc, x2, mod, vec, w_o, vec, vec)


FFN_TM = 1024
FFN_SUB = 512
FFN_SUB_LAST = 256
FFN_TF = 512


def _ffn_kernel(u_ref, x_hbm, mod_ref, wg_ref, wu_ref, wd_ref, lg_ref, lb_ref, o_ref, x_sc, x_sem):
    i = pl.program_id(0)
    f = pl.program_id(1)
    tm = o_ref.shape[0]
    x_copy = pltpu.make_async_copy(x_hbm.at[pl.ds(pl.multiple_of(i * tm, tm), tm), :], x_sc, x_sem)

    last = pl.num_programs(1) - 1

    def down_partial(rows):
        u = u_ref[rows, :]
        g = jnp.dot(u, wg_ref[...], preferred_element_type=F32)
        up = jnp.dot(u, wu_ref[...], preferred_element_type=F32)
        hdn = (g * (1.0 / (1.0 + jnp.exp(-g))) * up).astype(BF16)
        return jnp.dot(hdn, wd_ref[...], preferred_element_type=F32)

    @pl.when(f == 0)
    def _():
        x_copy.start()
        o_ref[...] = jnp.zeros_like(o_ref)

    @pl.when(f < last)
    def _():
        for r in range(0, tm, FFN_SUB):
            rows = slice(r, r + FFN_SUB)
            o_ref[rows, :] += down_partial(rows)

    @pl.when(f == last)
    def _():
        x_copy.wait()
        gate = 1.0 + mod_ref[5:6, :]
        for r in range(0, tm, FFN_SUB_LAST):
            rows = slice(r, r + FFN_SUB_LAST)
            acc = o_ref[rows, :] + down_partial(rows)
            o_ref[rows, :] = _layer_norm(DEEPNORM_ALPHA * x_sc[rows, :] + gate * acc, lg_ref[...], lb_ref[...])


def _ffn_call(l, u, x2, mod, w_gu, w_down, vec):
    tm, tf = FFN_TM, FFN_TF
    nf = D_FF // tf
    return pl.pallas_call(
        _ffn_kernel,
        out_shape=jax.ShapeDtypeStruct((SEQ, D_MODEL), F32),
        grid=(SEQ // tm, nf),
        in_specs=[
            pl.BlockSpec((tm, D_MODEL), lambda i, f: (i, 0)),
            pl.BlockSpec(memory_space=pl.ANY),
            pl.BlockSpec((None, 6, D_MODEL), lambda i, f: (l, 0, 0)),
            pl.BlockSpec((D_MODEL, tf), lambda i, f: (0, f)),
            pl.BlockSpec((D_MODEL, tf), lambda i, f: (0, nf + f)),
            pl.BlockSpec((tf, D_MODEL), lambda i, f: (f, 0)),
            _vec_spec(l, "ln2_g"), _vec_spec(l, "ln2_b"),
        ],
        out_specs=pl.BlockSpec((tm, D_MODEL), lambda i, f: (i, 0)),
        scratch_shapes=[pltpu.VMEM((tm, D_MODEL), F32), pltpu.SemaphoreType.DMA(())],
        compiler_params=_cparams(("arbitrary", "arbitrary")),
        name="ffn_ln",
    )(u, x2, mod, w_gu, w_gu, w_down, vec, vec)


def _rot_half_cols(w):
    half = w.shape[-1] // 2
    return jnp.concatenate([-w[..., half:], w[..., :half]], axis=-1)


def _rope_tables():
    half = MLA_ROPE // 2
    inv = ROPE_THETA ** (-jnp.arange(half, dtype=F32) / half)
    ang = jnp.arange(SEQ, dtype=F32)[:, None] * inv[None, :]
    return jnp.cos(ang), jnp.sin(ang)


def _colscale():
    cs = np.ones((1, P_W), np.float32)
    cs[0, 0:D_A] = HEAD_DIM ** -0.5 * LOG2_E
    c0 = IN_A + PB_W
    cs[0, c0:c0 + D_C] = HEAD_DIM ** -0.5 * LOG2_E
    return jnp.asarray(cs)


def kernel(x, c, w_ada, b_ada, w_in, na_rpb, mla_q_norm, mla_kv_norm, mla_w_uq, mla_w_ukv,
           swa_sink, out_norm_g, w_o, ln1_g, ln1_b, w_gu, w_down, ln2_g, ln2_b):
    assert x.shape == (1, SEQ, D_MODEL)
    x2 = x.reshape(SEQ, D_MODEL)
    b_ada3 = b_ada.reshape(DEPTH, 1, -1)
    mod_head, cond_lanes = _ada_head_call(c.reshape(D_MODEL, 1), w_ada, b_ada3)
    mod = jnp.concatenate([mod_head, jnp.zeros((1, 6 * D_MODEL - ADA_HEAD), F32)], axis=1).reshape(1, 6, D_MODEL)
    cos_t, sin_t = _rope_tables()
    colscale = _colscale()
    w_p = _winprep_call(jnp.swapaxes(w_in, 1, 2))
    wq = mla_w_uq.reshape(DEPTH, MLA_Q_RANK, MLA_HEADS, MLA_NOPE + MLA_ROPE)
    wq = jnp.concatenate([wq, _rot_half_cols(wq[..., MLA_NOPE:])], axis=-1)
    wq = wq.reshape(DEPTH, MLA_Q_RANK, MLA_HEADS * MLA_QK).astype(BF16)
    wkv = mla_w_ukv.reshape(DEPTH, MLA_KV_RANK, MLA_HEADS, MLA_NOPE + MLA_V)
    wk = wkv[..., :MLA_NOPE].reshape(DEPTH, MLA_KV_RANK, MLA_HEADS * MLA_NOPE).astype(BF16)
    wvt = jnp.transpose(wkv[..., MLA_NOPE:], (0, 2, 3, 1)).reshape(DEPTH, MLA_HEADS * MLA_V, MLA_KV_RANK)
    wvt = wvt.astype(BF16)
    vec_parts = dict(out_norm=out_norm_g, ln1_g=ln1_g, ln1_b=ln1_b, ln2_g=ln2_g, ln2_b=ln2_b,
                     q_norm=mla_q_norm, kv_norm=mla_kv_norm)
    vec = jnp.concatenate([vec_parts[name] for name, _ in _VEC_ORDER], axis=1).reshape(DEPTH, 1, VEC_W)
    for l in range(DEPTH):
        pa, pc, q, k, vt = _inproj_call(l, x2, mod, l, w_p, colscale, vec, wq, wk, wvt, cos_t, sin_t)
        bias = _na_bias_call(na_rpb[l].reshape(-1))
        ya = _na_call(pa, bias)
        if l == 0:
            yb_even, yb_odd, w_o_b, w_gu_b, w_down_b, mod_rest = _mla_attn_call(
                l, q, k, vt, w_o, w_gu, w_down, ada=(cond_lanes, w_ada, b_ada3))
            mod = jnp.concatenate([mod_head, mod_rest], axis=1).reshape(DEPTH, 6, D_MODEL)
        else:
            yb_even, yb_odd, w_o_b, w_gu_b, w_down_b = _mla_attn_call(l, q, k, vt, w_o, w_gu, w_down)
        yc = _swa_call(swa_sink[l], pc)
        x2, u = _outproj_call(l, ya, yb_even, yb_odd, yc, x2, mod, vec, w_o_b)
        x2 = _ffn_call(l, u, x2, mod, w_gu_b, w_down_b, vec)
    return x2.reshape(1, SEQ, D_MODEL)
```

```python
import functools

import numpy as np
import jax
import jax.numpy as jnp
from jax import lax
from jax.experimental import pallas as pl
from jax.experimental.pallas import tpu as pltpu

F32 = jnp.float32
BF16 = jnp.bfloat16

D_MODEL = 2048
SEQ = 8192
DEPTH = 2
GRID_W = 64
GRID_ROWS = SEQ // GRID_W
HEAD_DIM = 128
NA_HEADS = 4
NA_WIN_ROWS = 8
NA_WIN_COLS = 16
MLA_HEADS = 6
MLA_Q_RANK = 512
MLA_KV_RANK = 256
MLA_NOPE = 128
MLA_ROPE = 64
MLA_V = 128
ROPE_THETA = 10000.0
SWA_HEADS = 6
SWA_KV_HEADS = 2
SWA_GROUP = SWA_HEADS // SWA_KV_HEADS
SWA_WINDOW = 128
SWA_BLOCK = 128
D_A = NA_HEADS * HEAD_DIM
D_B = MLA_HEADS * MLA_V
D_C = SWA_HEADS * HEAD_DIM
IN_A = 3 * D_A
IN_B = MLA_Q_RANK + MLA_KV_RANK + MLA_ROPE
IN_C = (SWA_HEADS + 2 * SWA_KV_HEADS) * HEAD_DIM
D_FF = 5632
DEEPNORM_ALPHA = (2 * DEPTH) ** 0.25
LN_EPS = 1e-5
RMS_EPS = 1e-6
NEG_INF = -1e30
LOG2_E = 1.4426950408889634

PB_W = MLA_Q_RANK + MLA_KV_RANK + 2 * MLA_ROPE
P_W = IN_A + PB_W + IN_C
MLA_QK = 2 * HEAD_DIM
MLA_VT = MLA_V + 16

NA_QROWS = 4
NA_KROWS = 12
NA_TQ = NA_QROWS * GRID_W
NA_TK = NA_KROWS * GRID_W
NA_NBLK = GRID_ROWS // NA_QROWS

VMEM_LIMIT = 56 * 1024 * 1024
VMEM_LIMIT_ATTN = 58 * 1024 * 1024


def _cparams(sem, vmem_limit=VMEM_LIMIT):
    return pltpu.CompilerParams(dimension_semantics=sem, vmem_limit_bytes=vmem_limit)


def _layer_norm(z, g, b):
    mu = jnp.mean(z, axis=-1, keepdims=True)
    zc = z - mu
    var = jnp.mean(zc * zc, axis=-1, keepdims=True)
    return zc * lax.rsqrt(var + LN_EPS) * g + b


def _rms_norm(x, g):
    ms = jnp.mean(x * x, axis=-1, keepdims=True)
    return x * lax.rsqrt(ms + RMS_EPS) * g


def _dot_nt(a, b):
    return lax.dot_general(a, b, (((1,), (1,)), ((), ())), preferred_element_type=F32)


_VEC_ORDER = (("out_norm", D_MODEL), ("ln1_g", D_MODEL), ("ln1_b", D_MODEL), ("ln2_g", D_MODEL),
              ("ln2_b", D_MODEL), ("q_norm", MLA_Q_RANK), ("kv_norm", MLA_KV_RANK))
_VEC_OFF = {}
_off = 0
for _name, _w in _VEC_ORDER:
    assert _off % _w == 0
    _VEC_OFF[_name] = (_off // _w, _w)
    _off += _w
VEC_W = _off


def _vec_spec(l, name):
    blk, w = _VEC_OFF[name]
    return pl.BlockSpec((None, 1, w), lambda *grid_idx: (l, 0, blk))


ADA_TN = 1024
ADA_RC = 256
ADA_HEAD = 2 * D_MODEL
ADA_SLAB = 256
ADA_NSLAB = (DEPTH * 6 * D_MODEL - ADA_HEAD) // ADA_SLAB


def _ada_slab(cb_ref, w_ref, b_ref, o_ref):
    tn = o_ref.shape[-1]
    acc = jnp.zeros((8, tn), F32)
    for r in range(0, D_MODEL, ADA_RC):
        cb = jnp.concatenate([cb_ref[r:r + ADA_RC, :]] * (tn // 128), axis=1)
        prod = w_ref[r:r + ADA_RC, :] * cb
        acc = acc + jnp.sum(prod.reshape(ADA_RC // 8, 8, tn), axis=0)
    o_ref[...] = jnp.sum(acc, axis=0, keepdims=True) + b_ref[...]


def _ada_head_kernel(c_ref, w_ref, b_ref, o_ref, cb_ref):
    for r in range(0, D_MODEL, ADA_RC):
        c = c_ref[r:r + ADA_RC, :]
        cond = c * (1.0 / (1.0 + jnp.exp(-c)))
        cb_ref[r:r + ADA_RC, :] = jnp.broadcast_to(cond, (ADA_RC, 128))
    _ada_slab(cb_ref, w_ref, b_ref, o_ref)


def _ada_head_call(c_col, w_ada, b_ada3):
    return pl.pallas_call(
        _ada_head_kernel,
        out_shape=(jax.ShapeDtypeStruct((1, ADA_HEAD), F32), jax.ShapeDtypeStruct((D_MODEL, 128), F32)),
        grid=(ADA_HEAD // ADA_TN,),
        in_specs=[
            pl.BlockSpec((D_MODEL, 1), lambda j: (0, 0)),
            pl.BlockSpec((None, D_MODEL, ADA_TN), lambda j: (0, 0, j)),
            pl.BlockSpec((None, 1, ADA_TN), lambda j: (0, 0, j)),
        ],
        out_specs=(pl.BlockSpec((1, ADA_TN), lambda j: (0, j)),
                   pl.BlockSpec((D_MODEL, 128), lambda j: (0, 0))),
        compiler_params=_cparams(("arbitrary",)),
        name="ada_head",
    )(c_col, w_ada, b_ada3)


def _ada_slab_index(j):
    s = jnp.minimum(j, ADA_NSLAB - 1)
    n0 = (6 * D_MODEL - ADA_HEAD) // ADA_SLAB
    return jnp.where(s < n0, 0, 1), jnp.where(s < n0, s + ADA_HEAD // ADA_SLAB, s - n0), s


WPREP_TN = 512
KR0 = IN_A + MLA_Q_RANK + MLA_KV_RANK


def _winprep_kernel(w_ref, o_ref):
    half = MLA_ROPE // 2
    kr1 = KR0 + MLA_ROPE
    o_ref[0:kr1, :] = w_ref[0:kr1, :].astype(BF16)
    o_ref[kr1:kr1 + half, :] = (-w_ref[KR0 + half:kr1, :]).astype(BF16)
    o_ref[kr1 + half:kr1 + MLA_ROPE, :] = w_ref[KR0:KR0 + half, :].astype(BF16)
    o_ref[kr1 + MLA_ROPE:, :] = w_ref[kr1:, :].astype(BF16)


def _winprep_call(w_in_t):
    tn = WPREP_TN
    return pl.pallas_call(
        _winprep_kernel,
        out_shape=jax.ShapeDtypeStruct((DEPTH, P_W, D_MODEL), BF16),
        grid=(DEPTH, D_MODEL // tn),
        in_specs=[pl.BlockSpec((None, w_in_t.shape[1], tn), lambda l, i: (l, 0, i))],
        out_specs=pl.BlockSpec((None, P_W, tn), lambda l, i: (l, 0, i)),
        compiler_params=_cparams(("parallel", "parallel")),
        name="w_in_prep",
    )(w_in_t)


INPROJ_TM = 512
_INPROJ_CHUNKS = (
    (0, 512, 0, 0), (512, 1024, 0, 512), (1024, 1536, 0, 1024),
    (2432, 2944, 1, 0), (2944, 3456, 1, 512), (3456, 3712, 1, 1024),
)
_PB0 = IN_A


def _inproj_kernel(x_ref, mod_ref, w_ref, cs_ref, gq_ref, gkv_ref, wq_ref, wk_ref, wvt_ref, cos_ref, sin_ref,
                   oa_ref, oc_ref, q_ref, k_ref, vt_ref):
    outs = (oa_ref, oc_ref)
    sh = mod_ref[0:1, :]
    sc = mod_ref[1:2, :]
    u = (x_ref[...] * (1.0 + sc) + sh).astype(BF16)
    for c0, c1, oi, off in _INPROJ_CHUNKS:
        acc = _dot_nt(u, w_ref[c0:c1, :]) * cs_ref[:, c0:c1]
        outs[oi][:, off:off + (c1 - c0)] = acc.astype(outs[oi].dtype)
    cq = _dot_nt(u, w_ref[_PB0:_PB0 + MLA_Q_RANK, :])
    ckv_kr = _dot_nt(u, w_ref[_PB0 + MLA_Q_RANK:_PB0 + PB_W, :])
    _mla_up(cq, ckv_kr[:, 0:MLA_KV_RANK], ckv_kr[:, MLA_KV_RANK:], gq_ref, gkv_ref, wq_ref, wk_ref, wvt_ref,
            cos_ref, sin_ref, q_ref, k_ref, vt_ref)


def _inproj_call(l, x2, mod, lmod, w_p, colscale, vec, wq, wk, wvt, cos_t, sin_t):
    tm = INPROJ_TM
    hsd = jax.ShapeDtypeStruct((MLA_HEADS, SEQ, MLA_QK), BF16)
    hspec = pl.BlockSpec((MLA_HEADS, tm, MLA_QK), lambda i: (0, i, 0))
    layer = lambda r, c: pl.BlockSpec((None, r, c), lambda i: (l, 0, 0))
    return pl.pallas_call(
        _inproj_kernel,
        out_shape=(jax.ShapeDtypeStruct((SEQ, IN_A), BF16),
                   jax.ShapeDtypeStruct((SEQ, IN_C), BF16),
                   hsd, hsd, jax.ShapeDtypeStruct((MLA_HEADS, MLA_VT, SEQ), BF16)),
        grid=(SEQ // tm,),
        in_specs=[
            pl.BlockSpec((tm, D_MODEL), lambda i: (i, 0)),
            pl.BlockSpec((None, 6, D_MODEL), lambda i: (lmod, 0, 0)),
            pl.BlockSpec((None, P_W, D_MODEL), lambda i: (l, 0, 0), pipeline_mode=pl.Buffered(1)),
            pl.BlockSpec((1, P_W), lambda i: (0, 0)),
            _vec_spec(l, "q_norm"), _vec_spec(l, "kv_norm"),
            layer(MLA_Q_RANK, MLA_HEADS * MLA_QK), layer(MLA_KV_RANK, MLA_HEADS * MLA_NOPE),
            layer(MLA_HEADS * MLA_V, MLA_KV_RANK),
            pl.BlockSpec((tm, MLA_ROPE // 2), lambda i: (i, 0)),
            pl.BlockSpec((tm, MLA_ROPE // 2), lambda i: (i, 0)),
        ],
        out_specs=(pl.BlockSpec((tm, IN_A), lambda i: (i, 0)),
                   pl.BlockSpec((tm, IN_C), lambda i: (i, 0)),
                   hspec, hspec, pl.BlockSpec((MLA_HEADS, MLA_VT, tm), lambda i: (0, 0, i))),
        compiler_params=_cparams(("parallel",)),
        name="in_proj",
    )(x2, mod, w_p, colscale, vec, vec, wq, wk, wvt, cos_t, sin_t)


def _na_block_rule(btype, i, j):
    if btype == 0:
        r0 = max(i - NA_WIN_ROWS // 2, 0)
        valid = r0 <= j < r0 + NA_WIN_ROWS
        ro = j - i + (NA_WIN_ROWS - 1)
    elif btype == 1:
        valid = i <= j < i + NA_WIN_ROWS
        ro = j - i + (NA_WIN_ROWS - 1) - NA_WIN_ROWS // 2
    else:
        r = GRID_ROWS - NA_QROWS + i
        ks = GRID_ROWS - NA_KROWS
        r0 = min(r - NA_WIN_ROWS // 2, GRID_ROWS - NA_WIN_ROWS)
        valid = r0 <= ks + j < r0 + NA_WIN_ROWS
        ro = ks + j - r + (NA_WIN_ROWS - 1)
    return ro if valid else None


def _na_bias_kernel(rpb_ref, o_ref):
    h = pl.program_id(0)
    n_ro = 2 * NA_WIN_ROWS - 1
    n_co = 2 * NA_WIN_COLS - 1
    cq = lax.broadcasted_iota(jnp.int32, (GRID_W, GRID_W), 0)
    ck = lax.broadcasted_iota(jnp.int32, (GRID_W, GRID_W), 1)
    c0 = jnp.clip(cq - NA_WIN_COLS // 2, 0, GRID_W - NA_WIN_COLS)
    coff = jnp.clip(ck - cq, -(NA_WIN_COLS - 1), NA_WIN_COLS - 1) + (NA_WIN_COLS - 1)
    neg = jnp.full((GRID_W, GRID_W), NEG_INF, F32)
    tblocks = []
    for ro in range(n_ro):
        tb = neg
        for j in range(n_co):
            tb = jnp.where(coff == j, LOG2_E * rpb_ref[h * (n_ro * n_co) + ro * n_co + j], tb)
        inside = jnp.where(ck >= c0, jnp.where(ck < c0 + NA_WIN_COLS, 1, 0), 0)
        tblocks.append(jnp.where(inside == 1, tb, neg))
    for btype in range(3):
        for i in range(NA_QROWS):
            for jp in range(NA_KROWS // 2):
                pair = []
                for j in (2 * jp, 2 * jp + 1):
                    ro = _na_block_rule(btype, i, j)
                    pair.append(neg if ro is None else tblocks[ro])
                o_ref[0, btype, i * GRID_W:(i + 1) * GRID_W, jp * 128:(jp + 1) * 128] = (
                    jnp.concatenate(pair, axis=1))


def _na_bias_call(rpb_flat):
    return pl.pallas_call(
        _na_bias_kernel,
        out_shape=jax.ShapeDtypeStruct((NA_HEADS, 3, NA_TQ, NA_TK), F32),
        grid=(NA_HEADS,),
        in_specs=[pl.BlockSpec(memory_space=pltpu.SMEM)],
        out_specs=pl.BlockSpec((1, 3, NA_TQ, NA_TK), lambda h: (h, 0, 0, 0)),
        compiler_params=_cparams(("parallel",)),
        name="na_bias",
    )(rpb_flat)


NA_BPS = 4


def _na_kernel(q_ref, k_ref, v_ref, *rest):
    bias_refs, o_ref = rest[:NA_BPS], rest[NA_BPS]
    j = pl.program_id(0)
    ones = jnp.ones((NA_TK, HEAD_DIM), BF16)
    for sub, bias_ref in enumerate(bias_refs):
        b = NA_BPS * j + sub
        ks = jnp.clip(NA_QROWS * b - NA_WIN_ROWS // 2, 0, GRID_ROWS - NA_KROWS) * GRID_W
        ks = pl.multiple_of(ks, GRID_W)
        rows = slice(sub * NA_TQ, (sub + 1) * NA_TQ)
        for h in range(NA_HEADS):
            cols = slice(h * HEAD_DIM, (h + 1) * HEAD_DIM)
            s = _dot_nt(q_ref[rows, cols], k_ref[pl.ds(ks, NA_TK), cols]) + bias_ref[h, 0]
            m = jnp.max(s, axis=1, keepdims=True)
            p = jnp.exp2(s - m).astype(BF16)
            vext = jnp.concatenate([v_ref[pl.ds(ks, NA_TK), cols], ones], axis=1)
            acc = jnp.dot(p, vext, preferred_element_type=F32)
            o_ref[rows, cols] = acc[:, 0:HEAD_DIM] * (1.0 / acc[:, HEAD_DIM:HEAD_DIM + 1])


def _na_call(pa, bias):
    def btype(b):
        return jnp.where(b == 0, 0, jnp.where(b == NA_NBLK - 1, 2, 1))
    bias_specs = [pl.BlockSpec((NA_HEADS, 1, NA_TQ, NA_TK),
                               functools.partial(lambda j, sub: (0, btype(NA_BPS * j + sub), 0, 0), sub=sub))
                  for sub in range(NA_BPS)]
    return pl.pallas_call(
        _na_kernel,
        out_shape=jax.ShapeDtypeStruct((SEQ, D_A), F32),
        grid=(NA_NBLK // NA_BPS,),
        in_specs=[
            pl.BlockSpec((NA_BPS * NA_TQ, D_A), lambda j: (j, 0)),
            pl.BlockSpec((SEQ, D_A), lambda j: (0, 1), pipeline_mode=pl.Buffered(1)),
            pl.BlockSpec((SEQ, D_A), lambda j: (0, 2), pipeline_mode=pl.Buffered(1)),
        ] + bias_specs,
        out_specs=pl.BlockSpec((NA_BPS * NA_TQ, D_A), lambda j: (j, 0)),
        compiler_params=_cparams(("arbitrary",)),
        name="na_attn",
    )(pa, pa, pa, *([bias] * NA_BPS))


def _mla_up(cq, ckv, krb, gq_ref, gkv_ref, wq_ref, wk_ref, wvt_ref, cos_ref, sin_ref, q_ref, k_ref, vt_ref):
    tm = cq.shape[0]
    cqn = _rms_norm(cq, gq_ref[...]).astype(BF16)
    ckvn = _rms_norm(ckv, gkv_ref[...]).astype(BF16)
    zeros = jnp.zeros((tm, MLA_ROPE), F32)
    cos = jnp.concatenate([cos_ref[...], cos_ref[...], zeros], axis=1)
    sin = jnp.concatenate([sin_ref[...], sin_ref[...], zeros], axis=1)

    def rotary(t):
        return t * cos + pltpu.roll(t, MLA_ROPE, 1) * sin

    kpe = rotary(krb).astype(BF16)
    ones = jnp.ones((MLA_VT - MLA_V, tm), BF16)
    scale = (MLA_NOPE + MLA_ROPE) ** -0.5 * LOG2_E
    k_all = jnp.dot(ckvn, wk_ref[...], preferred_element_type=F32).astype(BF16)
    for h in range(MLA_HEADS):
        qh = jnp.dot(cqn, wq_ref[:, h * MLA_QK:(h + 1) * MLA_QK], preferred_element_type=F32)
        q_ref[h, :, 0:HEAD_DIM] = (qh[:, 0:HEAD_DIM] * scale).astype(BF16)
        q_ref[h, :, HEAD_DIM:MLA_QK] = (rotary(qh[:, HEAD_DIM:MLA_QK]) * scale).astype(BF16)
        k_ref[h, :, 0:HEAD_DIM] = k_all[:, h * MLA_NOPE:(h + 1) * MLA_NOPE]
        k_ref[h, :, HEAD_DIM:MLA_QK] = kpe
        vt_ref[h, 0:MLA_V, :] = _dot_nt(wvt_ref[h * MLA_V:(h + 1) * MLA_V, :], ckvn).astype(BF16)
        vt_ref[h, MLA_V:MLA_VT, :] = ones


MLA_TQ = 256
MLA_TK = 1024


def _mla_attn_kernel(*refs, with_ada):
    qe_ref, qo_ref, k_ref, vt_ref, wo_ref, wgu_ref, wdn_ref = refs[:7]
    ada_in = refs[7:10] if with_ada else ()
    outs = refs[7 + len(ada_in):]
    oe_ref, oo_ref, wo_b_ref, wgu_b_ref, wdn_b_ref = outs[:5]
    ada_out = outs[5:6] if with_ada else ()
    s0_sc, s1_sc, s2_sc, s3_sc, m0_sc, m1_sc, m2_sc, m3_sc = outs[5 + len(ada_out):]
    cast_refs = ((wo_ref, wo_b_ref), (wgu_ref, wgu_b_ref), (wdn_ref, wdn_b_ref))
    j = pl.program_id(0)

    @pl.when(j == 0)
    def _():
        for ref in (s2_sc, s3_sc, m2_sc, m3_sc):
            ref[...] = jnp.zeros_like(ref)

    def stage(q_ref, sa_sc, ma_sc, sb_sc, mb_sc, vt_ref, o_ref):
        q = q_ref[0]
        tq = q.shape[0]
        m_prev = mb_sc[...]
        mx = jnp.full((8, tq), NEG_INF, F32)
        acc = jnp.zeros((MLA_VT, tq), F32)
        for c in range(SEQ // MLA_TK):
            keys = slice(c * MLA_TK, (c + 1) * MLA_TK)
            p = jnp.exp2(sb_sc[keys, :] - m_prev).astype(BF16)
            acc = acc + jnp.dot(vt_ref[0, :, keys], p, preferred_element_type=F32)
            s = _dot_nt(k_ref[0, keys, :], q)
            sa_sc[keys, :] = s
            mx = jnp.maximum(mx, jnp.max(s.reshape(MLA_TK // 8, 8, tq), axis=0))
        ma_sc[...] = jnp.max(mx, axis=0, keepdims=True)
        o = acc[0:MLA_V, :] * (1.0 / acc[MLA_V:MLA_V + 1, :])
        o_ref[...] = o.T

    def step(a_even, a_odd, b_even, b_odd):
        for w_ref, wb_ref in cast_refs:
            wb_ref[...] = w_ref[...].astype(BF16)
        if with_ada:
            _ada_slab(*ada_in, *ada_out)
        stage(qe_ref, *a_even, *b_even, vt_ref, oe_ref)
        stage(qo_ref, *a_odd, *b_odd, vt_ref, oo_ref)

    buf = ((s0_sc, m0_sc), (s1_sc, m1_sc), (s2_sc, m2_sc), (s3_sc, m3_sc))
    parity = lax.rem(j, 2)

    @pl.when(parity == 0)
    def _():
        step(buf[0], buf[1], buf[2], buf[3])

    @pl.when(parity == 1)
    def _():
        step(buf[2], buf[3], buf[0], buf[1])


WO_SLAB = (32, D_MODEL)
WGU_SLAB = (256, 1024)
WDN_SLAB = (64, D_MODEL)


def _mla_attn_call(l, q, k, vt, w_o, w_gu, w_down, ada=None):
    tq = MLA_TQ
    npair = SEQ // (2 * tq)
    last = MLA_HEADS * npair - 1
    nstep = last + 2
    pair_a = lambda j: jnp.minimum(j, last)
    pair_b = lambda j: jnp.maximum(j - 1, 0)
    half = jax.ShapeDtypeStruct((SEQ // 2, D_B), F32)
    out_spec = pl.BlockSpec((tq, MLA_V), lambda j: (pair_b(j) % npair, pair_b(j) // npair))

    def slab_specs(shape, slab):
        nr, nc = shape[0] // slab[0], shape[1] // slab[1]
        assert nr * slab[0] == shape[0] and nc * slab[1] == shape[1] and nr * nc <= nstep
        idx = lambda j: jnp.minimum(j, nr * nc - 1)
        return (pl.BlockSpec((None,) + slab, lambda j: (l, idx(j) // nc, idx(j) % nc)),
                pl.BlockSpec(slab, lambda j: (idx(j) // nc, idx(j) % nc)))

    wo_in, wo_out = slab_specs(w_o.shape[1:], WO_SLAB)
    wgu_in, wgu_out = slab_specs(w_gu.shape[1:], WGU_SLAB)
    wdn_in, wdn_out = slab_specs(w_down.shape[1:], WDN_SLAB)
    ada_args, ada_in, ada_out, ada_shape = (), [], (), ()
    if ada is not None:
        assert ADA_NSLAB <= nstep
        ada_args = ada
        ada_in = [
            pl.BlockSpec((D_MODEL, 128), lambda j: (0, 0)),
            pl.BlockSpec((None, D_MODEL, ADA_SLAB), lambda j: (_ada_slab_index(j)[0], 0, _ada_slab_index(j)[1])),
            pl.BlockSpec((None, 1, ADA_SLAB), lambda j: (_ada_slab_index(j)[0], 0, _ada_slab_index(j)[1])),
        ]
        ada_out = (pl.BlockSpec((1, ADA_SLAB), lambda j: (0, _ada_slab_index(j)[2])),)
        ada_shape = (jax.ShapeDtypeStruct((1, ADA_NSLAB * ADA_SLAB), F32),)
    return pl.pallas_call(
        functools.partial(_mla_attn_kernel, with_ada=ada is not None),
        out_shape=(half, half,
                   jax.ShapeDtypeStruct(w_o.shape[1:], BF16),
                   jax.ShapeDtypeStruct(w_gu.shape[1:], BF16),
                   jax.ShapeDtypeStruct(w_down.shape[1:], BF16)) + ada_shape,
        grid=(nstep,),
        in_specs=[
            pl.BlockSpec((1, tq, MLA_QK), lambda j: (pair_a(j) // npair, 2 * (pair_a(j) % npair), 0)),
            pl.BlockSpec((1, tq, MLA_QK), lambda j: (pair_a(j) // npair, 2 * (pair_a(j) % npair) + 1, 0)),
            pl.BlockSpec((1, SEQ, MLA_QK), lambda j: (pair_a(j) // npair, 0, 0)),
            pl.BlockSpec((1, MLA_VT, SEQ), lambda j: (pair_b(j) // npair, 0, 0)),
            wo_in, wgu_in, wdn_in,
        ] + ada_in,
        out_specs=(out_spec, out_spec, wo_out, wgu_out, wdn_out) + ada_out,
        scratch_shapes=[pltpu.VMEM((SEQ, tq), F32)] * 4 + [pltpu.VMEM((1, tq), F32)] * 4,
        compiler_params=_cparams(("arbitrary",), VMEM_LIMIT_ATTN),
        name="mla_attn",
    )(q, q, k, vt, w_o, w_gu, w_down, *ada_args)


_SWA_SLOPES = tuple(2.0 ** (-8.0 * (i + 1) / SWA_HEADS) for i in range(SWA_HEADS))


SWA_NB = 16


def _swa_kernel(sink_ref, q_ref, kp_ref, kc_ref, kn_ref, vp_ref, vc_ref, vn_ref, o_ref):
    g_kv = pl.program_id(0)
    j = pl.program_id(1)
    t = SWA_BLOCK
    rows = SWA_GROUP * t
    kwin = jnp.concatenate([kp_ref[...], kc_ref[...], kn_ref[...]], axis=0)
    vwin = jnp.concatenate([vp_ref[...], vc_ref[...], vn_ref[...]], axis=0)
    ones = jnp.ones((3 * t, HEAD_DIM), BF16)
    ri = lax.broadcasted_iota(jnp.int32, (rows, 3 * t), 0)
    ci = lax.broadcasted_iota(jnp.int32, (rows, 3 * t), 1)
    grp = jnp.right_shift(ri, 7)
    dist = jnp.abs(jnp.bitwise_and(ri, t - 1) - (ci - t))
    slope_lo = jnp.where(grp == 0, _SWA_SLOPES[0], jnp.where(grp == 1, _SWA_SLOPES[1], _SWA_SLOPES[2]))
    slope_hi = jnp.where(grp == 0, _SWA_SLOPES[3], jnp.where(grp == 1, _SWA_SLOPES[4], _SWA_SLOPES[5]))
    slope = jnp.where(g_kv == 0, slope_lo, slope_hi)
    band = jnp.where(dist <= SWA_WINDOW, (-LOG2_E) * slope * dist.astype(F32), NEG_INF)
    rcol = jnp.right_shift(lax.broadcasted_iota(jnp.int32, (rows, 1), 0), 7)
    base = g_kv * SWA_GROUP
    sink = LOG2_E * jnp.where(rcol == 0, sink_ref[base],
                              jnp.where(rcol == 1, sink_ref[base + 1], sink_ref[base + 2]))
    first_cols = jnp.where(j == 0, t, 0)
    last_cols = jnp.where(j == pl.num_programs(1) - 1, 2 * t, 3 * t)
    for b in range(SWA_NB):
        q = jnp.concatenate([q_ref[b * t:(b + 1) * t, g * t:(g + 1) * t] for g in range(SWA_GROUP)], axis=0)
        s = _dot_nt(q, kwin[b * t:(b + 3) * t, :]) + band
        if b == 0:
            s = jnp.where(ci < first_cols, NEG_INF, s)
        if b == SWA_NB - 1:
            s = jnp.where(ci >= last_cols, NEG_INF, s)
        m = jnp.maximum(jnp.max(s, axis=1, keepdims=True), sink)
        p = jnp.exp2(s - m).astype(BF16)
        vext = jnp.concatenate([vwin[b * t:(b + 3) * t, :], ones], axis=1)
        acc = jnp.dot(p, vext, preferred_element_type=F32)
        l = acc[:, HEAD_DIM:HEAD_DIM + 1] + jnp.exp2(sink - m)
        o = acc[:, 0:HEAD_DIM] * (1.0 / l)
        for g in range(SWA_GROUP):
            o_ref[b * t:(b + 1) * t, g * t:(g + 1) * t] = o[g * t:(g + 1) * t, :]


def _swa_call(sink, pc):
    t = SWA_BLOCK
    nb = SEQ // t
    tq = SWA_NB * t
    kcol = SWA_HEADS
    vcol = SWA_HEADS + SWA_KV_HEADS
    prev = lambda j: jnp.maximum(SWA_NB * j - 1, 0)
    nxt = lambda j: jnp.minimum(SWA_NB * (j + 1), nb - 1)
    return pl.pallas_call(
        _swa_kernel,
        out_shape=jax.ShapeDtypeStruct((SEQ, D_C), F32),
        grid=(SWA_KV_HEADS, SEQ // tq),
        in_specs=[
            pl.BlockSpec(memory_space=pltpu.SMEM),
            pl.BlockSpec((tq, SWA_GROUP * t), lambda g, j: (j, g)),
            pl.BlockSpec((t, t), lambda g, j: (prev(j), kcol + g)),
            pl.BlockSpec((tq, t), lambda g, j: (j, kcol + g)),
            pl.BlockSpec((t, t), lambda g, j: (nxt(j), kcol + g)),
            pl.BlockSpec((t, t), lambda g, j: (prev(j), vcol + g)),
            pl.BlockSpec((tq, t), lambda g, j: (j, vcol + g)),
            pl.BlockSpec((t, t), lambda g, j: (nxt(j), vcol + g)),
        ],
        out_specs=pl.BlockSpec((tq, SWA_GROUP * t), lambda g, j: (j, g)),
        compiler_params=_cparams(("parallel", "arbitrary")),
        name="swa_attn",
    )(sink, pc, pc, pc, pc, pc, pc, pc)


OUTPROJ_TM = 2 * MLA_TQ


def _outproj_kernel(ya_ref, ybe_ref, ybo_ref, yc_ref, x_ref, mod_ref, gn_ref, w_ref, lg_ref, lb_ref,
                    o_ref, u_ref):
    hm = OUTPROJ_TM // 2
    gate = 1.0 + mod_ref[2:3, :]
    for half, yb_ref in enumerate((ybe_ref, ybo_ref)):
        rows = slice(half * hm, (half + 1) * hm)
        acc = None
        for y, c0 in ((ya_ref[rows, :], 0), (yb_ref[...], D_A), (yc_ref[rows, :], D_A + D_B)):
            c1 = c0 + y.shape[1]
            yn = _rms_norm(y, gn_ref[:, c0:c1]).astype(BF16)
            part = jnp.dot(yn, w_ref[c0:c1, :], preferred_element_type=F32)
            acc = part if acc is None else acc + part
        x1 = _layer_norm(DEEPNORM_ALPHA * x_ref[rows, :] + gate * acc, lg_ref[...], lb_ref[...])
        o_ref[rows, :] = x1
        u_ref[rows, :] = (x1 * (1.0 + mod_ref[4:5, :]) + mod_ref[3:4, :]).astype(BF16)


def _outproj_call(l, ya, yb_even, yb_odd, yc, x2, mod, vec, w_o):
    tm = OUTPROJ_TM
    row = lambda r, w: pl.BlockSpec((r, w), lambda i: (i, 0))
    return pl.pallas_call(
        _outproj_kernel,
        out_shape=(jax.ShapeDtypeStruct((SEQ, D_MODEL), F32), jax.ShapeDtypeStruct((SEQ, D_MODEL), BF16)),
        grid=(SEQ // tm,),
        in_specs=[row(tm, D_A), row(tm // 2, D_B), row(tm // 2, D_B), row(tm, D_C), row(tm, D_MODEL),
                  pl.BlockSpec((None, 6, D_MODEL), lambda i: (l, 0, 0)), _vec_spec(l, "out_norm"),
                  pl.BlockSpec((D_MODEL, D_MODEL), lambda i: (0, 0), pipeline_mode=pl.Buffered(1)),
                  _vec_spec(l, "ln1_g"), _vec_spec(l, "ln1_b")],
        out_specs=(row(tm, D_MODEL), row(tm, D_MODEL)),
        compiler_params=_cparams(("parallel",)),
        name="out_proj_ln",
    )(ya, yb_even, yb_odd, yc, x2, mod, vec, w_o, vec, vec)


FFN_TM = 1024
FFN_SUB = 512
FFN_SUB_LAST = 256
FFN_TF = 512


def _ffn_kernel(u_ref, x_hbm, mod_ref, wg_ref, wu_ref, wd_ref, lg_ref, lb_ref, o_ref, x_sc, x_sem):
    i = pl.program_id(0)
    f = pl.program_id(1)
    tm = o_ref.shape[0]
    x_copy = pltpu.make_async_copy(x_hbm.at[pl.ds(pl.multiple_of(i * tm, tm), tm), :], x_sc, x_sem)

    last = pl.num_programs(1) - 1

    def down_partial(rows):
        u = u_ref[rows, :]
        g = jnp.dot(u, wg_ref[...], preferred_element_type=F32)
        up = jnp.dot(u, wu_ref[...], preferred_element_type=F32)
        hdn = (g * (1.0 / (1.0 + jnp.exp(-g))) * up).astype(BF16)
        return jnp.dot(hdn, wd_ref[...], preferred_element_type=F32)

    @pl.when(f == 0)
    def _():
        x_copy.start()
        o_ref[...] = jnp.zeros_like(o_ref)

    @pl.when(f < last)
    def _():
        for r in range(0, tm, FFN_SUB):
            rows = slice(r, r + FFN_SUB)
            o_ref[rows, :] += down_partial(rows)

    @pl.when(f == last)
    def _():
        x_copy.wait()
        gate = 1.0 + mod_ref[5:6, :]
        for r in range(0, tm, FFN_SUB_LAST):
            rows = slice(r, r + FFN_SUB_LAST)
            acc = o_ref[rows, :] + down_partial(rows)
            o_ref[rows, :] = _layer_norm(DEEPNORM_ALPHA * x_sc[rows, :] + gate * acc, lg_ref[...], lb_ref[...])


def _ffn_call(l, u, x2, mod, w_gu, w_down, vec):
    tm, tf = FFN_TM, FFN_TF
    nf = D_FF // tf
    return pl.pallas_call(
        _ffn_kernel,
        out_shape=jax.ShapeDtypeStruct((SEQ, D_MODEL), F32),
        grid=(SEQ // tm, nf),
        in_specs=[
            pl.BlockSpec((tm, D_MODEL), lambda i, f: (i, 0)),
            pl.BlockSpec(memory_space=pl.ANY),
            pl.BlockSpec((None, 6, D_MODEL), lambda i, f: (l, 0, 0)),
            pl.BlockSpec((D_MODEL, tf), lambda i, f: (0, f)),
            pl.BlockSpec((D_MODEL, tf), lambda i, f: (0, nf + f)),
            pl.BlockSpec((tf, D_MODEL), lambda i, f: (f, 0)),
            _vec_spec(l, "ln2_g"), _vec_spec(l, "ln2_b"),
        ],
        out_specs=pl.BlockSpec((tm, D_MODEL), lambda i, f: (i, 0)),
        scratch_shapes=[pltpu.VMEM((tm, D_MODEL), F32), pltpu.SemaphoreType.DMA(())],
        compiler_params=_cparams(("arbitrary", "arbitrary")),
        name="ffn_ln",
    )(u, x2, mod, w_gu, w_gu, w_down, vec, vec)


def _rot_half_cols(w):
    half = w.shape[-1] // 2
    return jnp.concatenate([-w[..., half:], w[..., :half]], axis=-1)


def _rope_tables():
    half = MLA_ROPE // 2
    inv = ROPE_THETA ** (-jnp.arange(half, dtype=F32) / half)
    ang = jnp.arange(SEQ, dtype=F32)[:, None] * inv[None, :]
    return jnp.cos(ang), jnp.sin(ang)


def _colscale():
    cs = np.ones((1, P_W), np.float32)
    cs[0, 0:D_A] = HEAD_DIM ** -0.5 * LOG2_E
    c0 = IN_A + PB_W
    cs[0, c0:c0 + D_C] = HEAD_DIM ** -0.5 * LOG2_E
    return jnp.asarray(cs)


def kernel(x, c, w_ada, b_ada, w_in, na_rpb, mla_q_norm, mla_kv_norm, mla_w_uq, mla_w_ukv,
           swa_sink, out_norm_g, w_o, ln1_g, ln1_b, w_gu, w_down, ln2_g, ln2_b):
    assert x.shape == (1, SEQ, D_MODEL)
    x2 = x.reshape(SEQ, D_MODEL)
    b_ada3 = b_ada.reshape(DEPTH, 1, -1)
    mod_head, cond_lanes = _ada_head_call(c.reshape(D_MODEL, 1), w_ada, b_ada3)
    mod = jnp.concatenate([mod_head, jnp.zeros((1, 6 * D_MODEL - ADA_HEAD), F32)], axis=1).reshape(1, 6, D_MODEL)
    cos_t, sin_t = _rope_tables()
    colscale = _colscale()
    w_p = _winprep_call(jnp.swapaxes(w_in, 1, 2))
    wq = mla_w_uq.reshape(DEPTH, MLA_Q_RANK, MLA_HEADS, MLA_NOPE + MLA_ROPE)
    wq = jnp.concatenate([wq, _rot_half_cols(wq[..., MLA_NOPE:])], axis=-1)
    wq = wq.reshape(DEPTH, MLA_Q_RANK, MLA_HEADS * MLA_QK).astype(BF16)
    wkv = mla_w_ukv.reshape(DEPTH, MLA_KV_RANK, MLA_HEADS, MLA_NOPE + MLA_V)
    wk = wkv[..., :MLA_NOPE].reshape(DEPTH, MLA_KV_RANK, MLA_HEADS * MLA_NOPE).astype(BF16)
    wvt = jnp.transpose(wkv[..., MLA_NOPE:], (0, 2, 3, 1)).reshape(DEPTH, MLA_HEADS * MLA_V, MLA_KV_RANK)
    wvt = wvt.astype(BF16)
    vec_parts = dict(out_norm=out_norm_g, ln1_g=ln1_g, ln1_b=ln1_b, ln2_g=ln2_g, ln2_b=ln2_b,
                     q_norm=mla_q_norm, kv_norm=mla_kv_norm)
    vec = jnp.concatenate([vec_parts[name] for name, _ in _VEC_ORDER], axis=1).reshape(DEPTH, 1, VEC_W)
    for l in range(DEPTH):
        pa, pc, q, k, vt = _inproj_call(l, x2, mod, l, w_p, colscale, vec, wq, wk, wvt, cos_t, sin_t)
        bias = _na_bias_call(na_rpb[l].reshape(-1))
        ya = _na_call(pa, bias)
        if l == 0:
            yb_even, yb_odd, w_o_b, w_gu_b, w_down_b, mod_rest = _mla_attn_call(
                l, q, k, vt, w_o, w_gu, w_down, ada=(cond_lanes, w_ada, b_ada3))
            mod = jnp.concatenate([mod_head, mod_rest], axis=1).reshape(DEPTH, 6, D_MODEL)
        else:
            yb_even, yb_odd, w_o_b, w_gu_b, w_down_b = _mla_attn_call(l, q, k, vt, w_o, w_gu, w_down)
        yc = _swa_call(swa_sink[l], pc)
        x2, u = _outproj_call(l, ya, yb_even, yb_odd, yc, x2, mod, vec, w_o_b)
        x2 = _ffn_call(l, u, x2, mod, w_gu_b, w_down_b, vec)
    return x2.reshape(1, SEQ, D_MODEL)
```

```python
import functools

import numpy as np
import jax
import jax.numpy as jnp
from jax import lax
from jax.experimental import pallas as pl
from jax.experimental.pallas import tpu as pltpu

F32 = jnp.float32
BF16 = jnp.bfloat16

D_MODEL = 2048
SEQ = 8192
DEPTH = 2
GRID_W = 64
GRID_ROWS = SEQ // GRID_W
HEAD_DIM = 128
NA_HEADS = 4
NA_WIN_ROWS = 8
NA_WIN_COLS = 16
MLA_HEADS = 6
MLA_Q_RANK = 512
MLA_KV_RANK = 256
MLA_NOPE = 128
MLA_ROPE = 64
MLA_V = 128
ROPE_THETA = 10000.0
SWA_HEADS = 6
SWA_KV_HEADS = 2
SWA_GROUP = SWA_HEADS // SWA_KV_HEADS
SWA_WINDOW = 128
SWA_BLOCK = 128
D_A = NA_HEADS * HEAD_DIM
D_B = MLA_HEADS * MLA_V
D_C = SWA_HEADS * HEAD_DIM
IN_A = 3 * D_A
IN_B = MLA_Q_RANK + MLA_KV_RANK + MLA_ROPE
IN_C = (SWA_HEADS + 2 * SWA_KV_HEADS) * HEAD_DIM
D_FF = 5632
DEEPNORM_ALPHA = (2 * DEPTH) ** 0.25
LN_EPS = 1e-5
RMS_EPS = 1e-6
NEG_INF = -1e30
LOG2_E = 1.4426950408889634

PB_W = MLA_Q_RANK + MLA_KV_RANK + 2 * MLA_ROPE
P_W = IN_A + PB_W + IN_C
MLA_QK = 2 * HEAD_DIM
MLA_VT = MLA_V + 16

NA_QROWS = 4
NA_KROWS = 12
NA_TQ = NA_QROWS * GRID_W
NA_TK = NA_KROWS * GRID_W
NA_NBLK = GRID_ROWS // NA_QROWS

VMEM_LIMIT = 56 * 1024 * 1024
VMEM_LIMIT_ATTN = 58 * 1024 * 1024


def _cparams(sem, vmem_limit=VMEM_LIMIT):
    return pltpu.CompilerParams(dimension_semantics=sem, vmem_limit_bytes=vmem_limit)


def _layer_norm(z, g, b):
    mu = jnp.mean(z, axis=-1, keepdims=True)
    zc = z - mu
    var = jnp.mean(zc * zc, axis=-1, keepdims=True)
    return zc * lax.rsqrt(var + LN_EPS) * g + b


def _rms_norm(x, g):
    ms = jnp.mean(x * x, axis=-1, keepdims=True)
    return x * lax.rsqrt(ms + RMS_EPS) * g


def _dot_nt(a, b):
    return lax.dot_general(a, b, (((1,), (1,)), ((), ())), preferred_element_type=F32)


_VEC_ORDER = (("out_norm", D_MODEL), ("ln1_g", D_MODEL), ("ln1_b", D_MODEL), ("ln2_g", D_MODEL),
              ("ln2_b", D_MODEL), ("q_norm", MLA_Q_RANK), ("kv_norm", MLA_KV_RANK))
_VEC_OFF = {}
_off = 0
for _name, _w in _VEC_ORDER:
    assert _off % _w == 0
    _VEC_OFF[_name] = (_off // _w, _w)
    _off += _w
VEC_W = _off


def _vec_spec(l, name):
    blk, w = _VEC_OFF[name]
    return pl.BlockSpec((None, 1, w), lambda *grid_idx: (l, 0, blk))


ADA_TN = 1024
ADA_RC = 256
ADA_HEAD = 2 * D_MODEL
ADA_SLAB = 256
ADA_NSLAB = (DEPTH * 6 * D_MODEL - ADA_HEAD) // ADA_SLAB


def _ada_slab(cb_ref, w_ref, b_ref, o_ref):
    tn = o_ref.shape[-1]
    acc = jnp.zeros((8, tn), F32)
    for r in range(0, D_MODEL, ADA_RC):
        cb = jnp.concatenate([cb_ref[r:r + ADA_RC, :]] * (tn // 128), axis=1)
        prod = w_ref[r:r + ADA_RC, :] * cb
        acc = acc + jnp.sum(prod.reshape(ADA_RC // 8, 8, tn), axis=0)
    o_ref[...] = jnp.sum(acc, axis=0, keepdims=True) + b_ref[...]


def _ada_head_kernel(c_ref, w_ref, b_ref, o_ref, cb_ref):
    for r in range(0, D_MODEL, ADA_RC):
        c = c_ref[r:r + ADA_RC, :]
        cond = c * (1.0 / (1.0 + jnp.exp(-c)))
        cb_ref[r:r + ADA_RC, :] = jnp.broadcast_to(cond, (ADA_RC, 128))
    _ada_slab(cb_ref, w_ref, b_ref, o_ref)


def _ada_head_call(c_col, w_ada, b_ada3):
    return pl.pallas_call(
        _ada_head_kernel,
        out_shape=(jax.ShapeDtypeStruct((1, ADA_HEAD), F32), jax.ShapeDtypeStruct((D_MODEL, 128), F32)),
        grid=(ADA_HEAD // ADA_TN,),
        in_specs=[
            pl.BlockSpec((D_MODEL, 1), lambda j: (0, 0)),
            pl.BlockSpec((None, D_MODEL, ADA_TN), lambda j: (0, 0, j)),
            pl.BlockSpec((None, 1, ADA_TN), lambda j: (0, 0, j)),
        ],
        out_specs=(pl.BlockSpec((1, ADA_TN), lambda j: (0, j)),
                   pl.BlockSpec((D_MODEL, 128), lambda j: (0, 0))),
        compiler_params=_cparams(("arbitrary",)),
        name="ada_head",
    )(c_col, w_ada, b_ada3)


def _ada_slab_index(j):
    s = jnp.minimum(j, ADA_NSLAB - 1)
    n0 = (6 * D_MODEL - ADA_HEAD) // ADA_SLAB
    return jnp.where(s < n0, 0, 1), jnp.where(s < n0, s + ADA_HEAD // ADA_SLAB, s - n0), s


WPREP_TN = 512
KR0 = IN_A + MLA_Q_RANK + MLA_KV_RANK


def _winprep_kernel(w_ref, o_ref):
    half = MLA_ROPE // 2
    kr1 = KR0 + MLA_ROPE
    o_ref[0:kr1, :] = w_ref[0:kr1, :].astype(BF16)
    o_ref[kr1:kr1 + half, :] = (-w_ref[KR0 + half:kr1, :]).astype(BF16)
    o_ref[kr1 + half:kr1 + MLA_ROPE, :] = w_ref[KR0:KR0 + half, :].astype(BF16)
    o_ref[kr1 + MLA_ROPE:, :] = w_ref[kr1:, :].astype(BF16)


def _winprep_call(w_in_t):
    tn = WPREP_TN
    return pl.pallas_call(
        _winprep_kernel,
        out_shape=jax.ShapeDtypeStruct((DEPTH, P_W, D_MODEL), BF16),
        grid=(DEPTH, D_MODEL // tn),
        in_specs=[pl.BlockSpec((None, w_in_t.shape[1], tn), lambda l, i: (l, 0, i))],
        out_specs=pl.BlockSpec((None, P_W, tn), lambda l, i: (l, 0, i)),
        compiler_params=_cparams(("parallel", "parallel")),
        name="w_in_prep",
    )(w_in_t)


INPROJ_TM = 512
_INPROJ_CHUNKS = (
    (0, 512, 0, 0), (512, 1024, 0, 512), (1024, 1536, 0, 1024),
    (2432, 2944, 1, 0), (2944, 3456, 1, 512), (3456, 3712, 1, 1024),
)
_PB0 = IN_A


def _inproj_kernel(x_ref, mod_ref, w_ref, cs_ref, gq_ref, gkv_ref, wq_ref, wk_ref, wvt_ref, cos_ref, sin_ref,
                   oa_ref, oc_ref, q_ref, k_ref, vt_ref):
    outs = (oa_ref, oc_ref)
    sh = mod_ref[0:1, :]
    sc = mod_ref[1:2, :]
    u = (x_ref[...] * (1.0 + sc) + sh).astype(BF16)
    for c0, c1, oi, off in _INPROJ_CHUNKS:
        acc = _dot_nt(u, w_ref[c0:c1, :]) * cs_ref[:, c0:c1]
        outs[oi][:, off:off + (c1 - c0)] = acc.astype(outs[oi].dtype)
    cq = _dot_nt(u, w_ref[_PB0:_PB0 + MLA_Q_RANK, :])
    ckv_kr = _dot_nt(u, w_ref[_PB0 + MLA_Q_RANK:_PB0 + PB_W, :])
    _mla_up(cq, ckv_kr[:, 0:MLA_KV_RANK], ckv_kr[:, MLA_KV_RANK:], gq_ref, gkv_ref, wq_ref, wk_ref, wvt_ref,
            cos_ref, sin_ref, q_ref, k_ref, vt_ref)


def _inproj_call(l, x2, mod, lmod, w_p, colscale, vec, wq, wk, wvt, cos_t, sin_t):
    tm = INPROJ_TM
    hsd = jax.ShapeDtypeStruct((MLA_HEADS, SEQ, MLA_QK), BF16)
    hspec = pl.BlockSpec((MLA_HEADS, tm, MLA_QK), lambda i: (0, i, 0))
    layer = lambda r, c: pl.BlockSpec((None, r, c), lambda i: (l, 0, 0))
    return pl.pallas_call(
        _inproj_kernel,
        out_shape=(jax.ShapeDtypeStruct((SEQ, IN_A), BF16),
                   jax.ShapeDtypeStruct((SEQ, IN_C), BF16),
                   hsd, hsd, jax.ShapeDtypeStruct((MLA_HEADS, MLA_VT, SEQ), BF16)),
        grid=(SEQ // tm,),
        in_specs=[
            pl.BlockSpec((tm, D_MODEL), lambda i: (i, 0)),
            pl.BlockSpec((None, 6, D_MODEL), lambda i: (lmod, 0, 0)),
            pl.BlockSpec((None, P_W, D_MODEL), lambda i: (l, 0, 0), pipeline_mode=pl.Buffered(1)),
            pl.BlockSpec((1, P_W), lambda i: (0, 0)),
            _vec_spec(l, "q_norm"), _vec_spec(l, "kv_norm"),
            layer(MLA_Q_RANK, MLA_HEADS * MLA_QK), layer(MLA_KV_RANK, MLA_HEADS * MLA_NOPE),
            layer(MLA_HEADS * MLA_V, MLA_KV_RANK),
            pl.BlockSpec((tm, MLA_ROPE // 2), lambda i: (i, 0)),
            pl.BlockSpec((tm, MLA_ROPE // 2), lambda i: (i, 0)),
        ],
        out_specs=(pl.BlockSpec((tm, IN_A), lambda i: (i, 0)),
                   pl.BlockSpec((tm, IN_C), lambda i: (i, 0)),
                   hspec, hspec, pl.BlockSpec((MLA_HEADS, MLA_VT, tm), lambda i: (0, 0, i))),
        compiler_params=_cparams(("parallel",)),
        name="in_proj",
    )(x2, mod, w_p, colscale, vec, vec, wq, wk, wvt, cos_t, sin_t)


def _na_block_rule(btype, i, j):
    if btype == 0:
        r0 = max(i - NA_WIN_ROWS // 2, 0)
        valid = r0 <= j < r0 + NA_WIN_ROWS
        ro = j - i + (NA_WIN_ROWS - 1)
    elif btype == 1:
        valid = i <= j < i + NA_WIN_ROWS
        ro = j - i + (NA_WIN_ROWS - 1) - NA_WIN_ROWS // 2
    else:
        r = GRID_ROWS - NA_QROWS + i
        ks = GRID_ROWS - NA_KROWS
        r0 = min(r - NA_WIN_ROWS // 2, GRID_ROWS - NA_WIN_ROWS)
        valid = r0 <= ks + j < r0 + NA_WIN_ROWS
        ro = ks + j - r + (NA_WIN_ROWS - 1)
    return ro if valid else None


def _na_bias_kernel(rpb_ref, o_ref):
    h = pl.program_id(0)
    n_ro = 2 * NA_WIN_ROWS - 1
    n_co = 2 * NA_WIN_COLS - 1
    cq = lax.broadcasted_iota(jnp.int32, (GRID_W, GRID_W), 0)
    ck = lax.broadcasted_iota(jnp.int32, (GRID_W, GRID_W), 1)
    c0 = jnp.clip(cq - NA_WIN_COLS // 2, 0, GRID_W - NA_WIN_COLS)
    coff = jnp.clip(ck - cq, -(NA_WIN_COLS - 1), NA_WIN_COLS - 1) + (NA_WIN_COLS - 1)
    neg = jnp.full((GRID_W, GRID_W), NEG_INF, F32)
    tblocks = []
    for ro in range(n_ro):
        tb = neg
        for j in range(n_co):
            tb = jnp.where(coff == j, LOG2_E * rpb_ref[h * (n_ro * n_co) + ro * n_co + j], tb)
        inside = jnp.where(ck >= c0, jnp.where(ck < c0 + NA_WIN_COLS, 1, 0), 0)
        tblocks.append(jnp.where(inside == 1, tb, neg))
    for btype in range(3):
        for i in range(NA_QROWS):
            for jp in range(NA_KROWS // 2):
                pair = []
                for j in (2 * jp, 2 * jp + 1):
                    ro = _na_block_rule(btype, i, j)
                    pair.append(neg if ro is None else tblocks[ro])
                o_ref[0, btype, i * GRID_W:(i + 1) * GRID_W, jp * 128:(jp + 1) * 128] = (
                    jnp.concatenate(pair, axis=1))


def _na_bias_call(rpb_flat):
    return pl.pallas_call(
        _na_bias_kernel,
        out_shape=jax.ShapeDtypeStruct((NA_HEADS, 3, NA_TQ, NA_TK), F32),
        grid=(NA_HEADS,),
        in_specs=[pl.BlockSpec(memory_space=pltpu.SMEM)],
        out_specs=pl.BlockSpec((1, 3, NA_TQ, NA_TK), lambda h: (h, 0, 0, 0)),
        compiler_params=_cparams(("parallel",)),
        name="na_bias",
    )(rpb_flat)


NA_BPS = 4


def _na_kernel(q_ref, k_ref, v_ref, *rest):
    bias_refs, o_ref = rest[:NA_BPS], rest[NA_BPS]
    j = pl.program_id(0)
    ones = jnp.ones((NA_TK, HEAD_DIM), BF16)
    for sub, bias_ref in enumerate(bias_refs):
        b = NA_BPS * j + sub
        ks = jnp.clip(NA_QROWS * b - NA_WIN_ROWS // 2, 0, GRID_ROWS - NA_KROWS) * GRID_W
        ks = pl.multiple_of(ks, GRID_W)
        rows = slice(sub * NA_TQ, (sub + 1) * NA_TQ)
        for h in range(NA_HEADS):
            cols = slice(h * HEAD_DIM, (h + 1) * HEAD_DIM)
            s = _dot_nt(q_ref[rows, cols], k_ref[pl.ds(ks, NA_TK), cols]) + bias_ref[h, 0]
            m = jnp.max(s, axis=1, keepdims=True)
            p = jnp.exp2(s - m).astype(BF16)
            vext = jnp.concatenate([v_ref[pl.ds(ks, NA_TK), cols], ones], axis=1)
            acc = jnp.dot(p, vext, preferred_element_type=F32)
            o_ref[rows, cols] = acc[:, 0:HEAD_DIM] * (1.0 / acc[:, HEAD_DIM:HEAD_DIM + 1])


def _na_call(pa, bias):
    def btype(b):
        return jnp.where(b == 0, 0, jnp.where(b == NA_NBLK - 1, 2, 1))
    bias_specs = [pl.BlockSpec((NA_HEADS, 1, NA_TQ, NA_TK),
                               functools.partial(lambda j, sub: (0, btype(NA_BPS * j + sub), 0, 0), sub=sub))
                  for sub in range(NA_BPS)]
    return pl.pallas_call(
        _na_kernel,
        out_shape=jax.ShapeDtypeStruct((SEQ, D_A), F32),
        grid=(NA_NBLK // NA_BPS,),
        in_specs=[
            pl.BlockSpec((NA_BPS * NA_TQ, D_A), lambda j: (j, 0)),
            pl.BlockSpec((SEQ, D_A), lambda j: (0, 1), pipeline_mode=pl.Buffered(1)),
            pl.BlockSpec((SEQ, D_A), lambda j: (0, 2), pipeline_mode=pl.Buffered(1)),
        ] + bias_specs,
        out_specs=pl.BlockSpec((NA_BPS * NA_TQ, D_A), lambda j: (j, 0)),
        compiler_params=_cparams(("arbitrary",)),
        name="na_attn",
    )(pa, pa, pa, *([bias] * NA_BPS))


def _mla_up(cq, ckv, krb, gq_ref, gkv_ref, wq_ref, wk_ref, wvt_ref, cos_ref, sin_ref, q_ref, k_ref, vt_ref):
    tm = cq.shape[0]
    cqn = _rms_norm(cq, gq_ref[...]).astype(BF16)
    ckvn = _rms_norm(ckv, gkv_ref[...]).astype(BF16)
    zeros = jnp.zeros((tm, MLA_ROPE), F32)
    cos = jnp.concatenate([cos_ref[...], cos_ref[...], zeros], axis=1)
    sin = jnp.concatenate([sin_ref[...], sin_ref[...], zeros], axis=1)

    def rotary(t):
        return t * cos + pltpu.roll(t, MLA_ROPE, 1) * sin

    kpe = rotary(krb).astype(BF16)
    ones = jnp.ones((MLA_VT - MLA_V, tm), BF16)
    scale = (MLA_NOPE + MLA_ROPE) ** -0.5 * LOG2_E
    k_all = jnp.dot(ckvn, wk_ref[...], preferred_element_type=F32).astype(BF16)
    for h in range(MLA_HEADS):
        qh = jnp.dot(cqn, wq_ref[:, h * MLA_QK:(h + 1) * MLA_QK], preferred_element_type=F32)
        q_ref[h, :, 0:HEAD_DIM] = (qh[:, 0:HEAD_DIM] * scale).astype(BF16)
        q_ref[h, :, HEAD_DIM:MLA_QK] = (rotary(qh[:, HEAD_DIM:MLA_QK]) * scale).astype(BF16)
        k_ref[h, :, 0:HEAD_DIM] = k_all[:, h * MLA_NOPE:(h + 1) * MLA_NOPE]
        k_ref[h, :, HEAD_DIM:MLA_QK] = kpe
        vt_ref[h, 0:MLA_V, :] = _dot_nt(wvt_ref[h * MLA_V:(h + 1) * MLA_V, :], ckvn).astype(BF16)
        vt_ref[h, MLA_V:MLA_VT, :] = ones


MLA_TQ = 256
MLA_TK = 1024


def _mla_attn_kernel(*refs, with_ada):
    qe_ref, qo_ref, k_ref, vt_ref, wo_ref, wgu_ref, wdn_ref = refs[:7]
    ada_in = refs[7:10] if with_ada else ()
    outs = refs[7 + len(ada_in):]
    oe_ref, oo_ref, wo_b_ref, wgu_b_ref, wdn_b_ref = outs[:5]
    ada_out = outs[5:6] if with_ada else ()
    s0_sc, s1_sc, s2_sc, s3_sc, m0_sc, m1_sc, m2_sc, m3_sc = outs[5 + len(ada_out):]
    cast_refs = ((wo_ref, wo_b_ref), (wgu_ref, wgu_b_ref), (wdn_ref, wdn_b_ref))
    j = pl.program_id(0)

    @pl.when(j == 0)
    def _():
        for ref in (s2_sc, s3_sc, m2_sc, m3_sc):
            ref[...] = jnp.zeros_like(ref)

    def stage(q_ref, sa_sc, ma_sc, sb_sc, mb_sc, vt_ref, o_ref):
        q = q_ref[0]
        tq = q.shape[0]
        m_prev = mb_sc[...]
        mx = jnp.full((8, tq), NEG_INF, F32)
        acc = jnp.zeros((MLA_VT, tq), F32)
        for c in range(SEQ // MLA_TK):
            keys = slice(c * MLA_TK, (c + 1) * MLA_TK)
            p = jnp.exp2(sb_sc[keys, :] - m_prev).astype(BF16)
            acc = acc + jnp.dot(vt_ref[0, :, keys], p, preferred_element_type=F32)
            s = _dot_nt(k_ref[0, keys, :], q)
            sa_sc[keys, :] = s
            mx = jnp.maximum(mx, jnp.max(s.reshape(MLA_TK // 8, 8, tq), axis=0))
        ma_sc[...] = jnp.max(mx, axis=0, keepdims=True)
        o = acc[0:MLA_V, :] * (1.0 / acc[MLA_V:MLA_V + 1, :])
        o_ref[...] = o.T

    def step(a_even, a_odd, b_even, b_odd):
        for w_ref, wb_ref in cast_refs:
            wb_ref[...] = w_ref[...].astype(BF16)
        if with_ada:
            _ada_slab(*ada_in, *ada_out)
        stage(qe_ref, *a_even, *b_even, vt_ref, oe_ref)
        stage(qo_ref, *a_odd, *b_odd, vt_ref, oo_ref)

    buf = ((s0_sc, m0_sc), (s1_sc, m1_sc), (s2_sc, m2_sc), (s3_sc, m3_sc))
    parity = lax.rem(j, 2)

    @pl.when(parity == 0)
    def _():
        step(buf[0], buf[1], buf[2], buf[3])

    @pl.when(parity == 1)
    def _():
        step(buf[2], buf[3], buf[0], buf[1])


WO_SLAB = (32, D_MODEL)
WGU_SLAB = (256, 1024)
WDN_SLAB = (64, D_MODEL)


def _mla_attn_call(l, q, k, vt, w_o, w_gu, w_down, ada=None):
    tq = MLA_TQ
    npair = SEQ // (2 * tq)
    last = MLA_HEADS * npair - 1
    nstep = last + 2
    pair_a = lambda j: jnp.minimum(j, last)
    pair_b = lambda j: jnp.maximum(j - 1, 0)
    half = jax.ShapeDtypeStruct((SEQ // 2, D_B), F32)
    out_spec = pl.BlockSpec((tq, MLA_V), lambda j: (pair_b(j) % npair, pair_b(j) // npair))

    def slab_specs(shape, slab):
        nr, nc = shape[0] // slab[0], shape[1] // slab[1]
        assert nr * slab[0] == shape[0] and nc * slab[1] == shape[1] and nr * nc <= nstep
        idx = lambda j: jnp.minimum(j, nr * nc - 1)
        return (pl.BlockSpec((None,) + slab, lambda j: (l, idx(j) // nc, idx(j) % nc)),
                pl.BlockSpec(slab, lambda j: (idx(j) // nc, idx(j) % nc)))

    wo_in, wo_out = slab_specs(w_o.shape[1:], WO_SLAB)
    wgu_in, wgu_out = slab_specs(w_gu.shape[1:], WGU_SLAB)
    wdn_in, wdn_out = slab_specs(w_down.shape[1:], WDN_SLAB)
    ada_args, ada_in, ada_out, ada_shape = (), [], (), ()
    if ada is not None:
        assert ADA_NSLAB <= nstep
        ada_args = ada
        ada_in = [
            pl.BlockSpec((D_MODEL, 128), lambda j: (0, 0)),
            pl.BlockSpec((None, D_MODEL, ADA_SLAB), lambda j: (_ada_slab_index(j)[0], 0, _ada_slab_index(j)[1])),
            pl.BlockSpec((None, 1, ADA_SLAB), lambda j: (_ada_slab_index(j)[0], 0, _ada_slab_index(j)[1])),
        ]
        ada_out = (pl.BlockSpec((1, ADA_SLAB), lambda j: (0, _ada_slab_index(j)[2])),)
        ada_shape = (jax.ShapeDtypeStruct((1, ADA_NSLAB * ADA_SLAB), F32),)
    return pl.pallas_call(
        functools.partial(_mla_attn_kernel, with_ada=ada is not None),
        out_shape=(half, half,
                   jax.ShapeDtypeStruct(w_o.shape[1:], BF16),
                   jax.ShapeDtypeStruct(w_gu.shape[1:], BF16),
                   jax.ShapeDtypeStruct(w_down.shape[1:], BF16)) + ada_shape,
        grid=(nstep,),
        in_specs=[
            pl.BlockSpec((1, tq, MLA_QK), lambda j: (pair_a(j) // npair, 2 * (pair_a(j) % npair), 0)),
            pl.BlockSpec((1, tq, MLA_QK), lambda j: (pair_a(j) // npair, 2 * (pair_a(j) % npair) + 1, 0)),
            pl.BlockSpec((1, SEQ, MLA_QK), lambda j: (pair_a(j) // npair, 0, 0)),
            pl.BlockSpec((1, MLA_VT, SEQ), lambda j: (pair_b(j) // npair, 0, 0)),
            wo_in, wgu_in, wdn_in,
        ] + ada_in,
        out_specs=(out_spec, out_spec, wo_out, wgu_out, wdn_out) + ada_out,
        scratch_shapes=[pltpu.VMEM((SEQ, tq), F32)] * 4 + [pltpu.VMEM((1, tq), F32)] * 4,
        compiler_params=_cparams(("arbitrary",), VMEM_LIMIT_ATTN),
        name="mla_attn",
    )(q, q, k, vt, w_o, w_gu, w_down, *ada_args)


_SWA_SLOPES = tuple(2.0 ** (-8.0 * (i + 1) / SWA_HEADS) for i in range(SWA_HEADS))


SWA_NB = 16


def _swa_kernel(sink_ref, q_ref, kp_ref, kc_ref, kn_ref, vp_ref, vc_ref, vn_ref, o_ref):
    g_kv = pl.program_id(0)
    j = pl.program_id(1)
    t = SWA_BLOCK
    rows = SWA_GROUP * t
    kwin = jnp.concatenate([kp_ref[...], kc_ref[...], kn_ref[...]], axis=0)
    vwin = jnp.concatenate([vp_ref[...], vc_ref[...], vn_ref[...]], axis=0)
    ones = jnp.ones((3 * t, HEAD_DIM), BF16)
    ri = lax.broadcasted_iota(jnp.int32, (rows, 3 * t), 0)
    ci = lax.broadcasted_iota(jnp.int32, (rows, 3 * t), 1)
    grp = jnp.right_shift(ri, 7)
    dist = jnp.abs(jnp.bitwise_and(ri, t - 1) - (ci - t))
    slope_lo = jnp.where(grp == 0, _SWA_SLOPES[0], jnp.where(grp == 1, _SWA_SLOPES[1], _SWA_SLOPES[2]))
    slope_hi = jnp.where(grp == 0, _SWA_SLOPES[3], jnp.where(grp == 1, _SWA_SLOPES[4], _SWA_SLOPES[5]))
    slope = jnp.where(g_kv == 0, slope_lo, slope_hi)
    band = jnp.where(dist <= SWA_WINDOW, (-LOG2_E) * slope * dist.astype(F32), NEG_INF)
    rcol = jnp.right_shift(lax.broadcasted_iota(jnp.int32, (rows, 1), 0), 7)
    base = g_kv * SWA_GROUP
    sink = LOG2_E * jnp.where(rcol == 0, sink_ref[base],
                              jnp.where(rcol == 1, sink_ref[base + 1], sink_ref[base + 2]))
    first_cols = jnp.where(j == 0, t, 0)
    last_cols = jnp.where(j == pl.num_programs(1) - 1, 2 * t, 3 * t)
    for b in range(SWA_NB):
        q = jnp.concatenate([q_ref[b * t:(b + 1) * t, g * t:(g + 1) * t] for g in range(SWA_GROUP)], axis=0)
        s = _dot_nt(q, kwin[b * t:(b + 3) * t, :]) + band
        if b == 0:
            s = jnp.where(ci < first_cols, NEG_INF, s)
        if b == SWA_NB - 1:
            s = jnp.where(ci >= last_cols, NEG_INF, s)
        m = jnp.maximum(jnp.max(s, axis=1, keepdims=True), sink)
        p = jnp.exp2(s - m).astype(BF16)
        vext = jnp.concatenate([vwin[b * t:(b + 3) * t, :], ones], axis=1)
        acc = jnp.dot(p, vext, preferred_element_type=F32)
        l = acc[:, HEAD_DIM:HEAD_DIM + 1] + jnp.exp2(sink - m)
        o = acc[:, 0:HEAD_DIM] * (1.0 / l)
        for g in range(SWA_GROUP):
            o_ref[b * t:(b + 1) * t, g * t:(g + 1) * t] = o[g * t:(g + 1) * t, :]


def _swa_call(sink, pc):
    t = SWA_BLOCK
    nb = SEQ // t
    tq = SWA_NB * t
    kcol = SWA_HEADS
    vcol = SWA_HEADS + SWA_KV_HEADS
    prev = lambda j: jnp.maximum(SWA_NB * j - 1, 0)
    nxt = lambda j: jnp.minimum(SWA_NB * (j + 1), nb - 1)
    return pl.pallas_call(
        _swa_kernel,
        out_shape=jax.ShapeDtypeStruct((SEQ, D_C), F32),
        grid=(SWA_KV_HEADS, SEQ // tq),
        in_specs=[
            pl.BlockSpec(memory_space=pltpu.SMEM),
            pl.BlockSpec((tq, SWA_GROUP * t), lambda g, j: (j, g)),
            pl.BlockSpec((t, t), lambda g, j: (prev(j), kcol + g)),
            pl.BlockSpec((tq, t), lambda g, j: (j, kcol + g)),
            pl.BlockSpec((t, t), lambda g, j: (nxt(j), kcol + g)),
            pl.BlockSpec((t, t), lambda g, j: (prev(j), vcol + g)),
            pl.BlockSpec((tq, t), lambda g, j: (j, vcol + g)),
            pl.BlockSpec((t, t), lambda g, j: (nxt(j), vcol + g)),
        ],
        out_specs=pl.BlockSpec((tq, SWA_GROUP * t), lambda g, j: (j, g)),
        compiler_params=_cparams(("parallel", "arbitrary")),
        name="swa_attn",
    )(sink, pc, pc, pc, pc, pc, pc, pc)


OUTPROJ_TM = 2 * MLA_TQ


def _outproj_kernel(ya_ref, ybe_ref, ybo_ref, yc_ref, x_ref, mod_ref, gn_ref, w_ref, lg_ref, lb_ref,
                    o_ref, u_ref):
    hm = OUTPROJ_TM // 2
    gate = 1.0 + mod_ref[2:3, :]
    for half, yb_ref in enumerate((ybe_ref, ybo_ref)):
        rows = slice(half * hm, (half + 1) * hm)
        acc = None
        for y, c0 in ((ya_ref[rows, :], 0), (yb_ref[...], D_A), (yc_ref[rows, :], D_A + D_B)):
            c1 = c0 + y.shape[1]
            yn = _rms_norm(y, gn_ref[:, c0:c1]).astype(BF16)
            part = jnp.dot(yn, w_ref[c0:c1, :], preferred_element_type=F32)
            acc = part if acc is None else acc + part
        x1 = _layer_norm(DEEPNORM_ALPHA * x_ref[rows, :] + gate * acc, lg_ref[...], lb_ref[...])
        o_ref[rows, :] = x1
        u_ref[rows, :] = (x1 * (1.0 + mod_ref[4:5, :]) + mod_ref[3:4, :]).astype(BF16)


def _outproj_call(l, ya, yb_even, yb_odd, yc, x2, mod, vec, w_o):
    tm = OUTPROJ_TM
    row = lambda r, w: pl.BlockSpec((r, w), lambda i: (i, 0))
    return pl.pallas_call(
        _outproj_kernel,
        out_shape=(jax.ShapeDtypeStruct((SEQ, D_MODEL), F32), jax.ShapeDtypeStruct((SEQ, D_MODEL), BF16)),
        grid=(SEQ // tm,),
        in_specs=[row(tm, D_A), row(tm // 2, D_B), row(tm // 2, D_B), row(tm, D_C), row(tm, D_MODEL),
                  pl.BlockSpec((None, 6, D_MODEL), lambda i: (l, 0, 0)), _vec_spec(l, "out_norm"),
                  pl.BlockSpec((D_MODEL, D_MODEL), lambda i: (0, 0), pipeline_mode=pl.Buffered(1)),
                  _vec_spec(l, "ln1_g"), _vec_spec(l, "ln1_b")],
        out_specs=(row(tm, D_MODEL), row(tm, D_MODEL)),
        compiler_params=_cparams(("parallel",)),
        name="out_proj_ln",
    )(ya, yb_even, yb_odd, yc, x2, mod, vec, w_o, vec, vec)


FFN_TM = 1024
FFN_SUB = 512
FFN_SUB_LAST = 256
FFN_TF = 512


def _ffn_kernel(u_ref, x_hbm, mod_ref, wg_ref, wu_ref, wd_ref, lg_ref, lb_ref, o_ref, x_sc, x_sem):
    i = pl.program_id(0)
    f = pl.program_id(1)
    tm = o_ref.shape[0]
    x_copy = pltpu.make_async_copy(x_hbm.at[pl.ds(pl.multiple_of(i * tm, tm), tm), :], x_sc, x_sem)

    last = pl.num_programs(1) - 1

    def down_partial(rows):
        u = u_ref[rows, :]
        g = jnp.dot(u, wg_ref[...], preferred_element_type=F32)
        up = jnp.dot(u, wu_ref[...], preferred_element_type=F32)
        hdn = (g * (1.0 / (1.0 + jnp.exp(-g))) * up).astype(BF16)
        return jnp.dot(hdn, wd_ref[...], preferred_element_type=F32)

    @pl.when(f == 0)
    def _():
        x_copy.start()
        for r in range(0, tm, FFN_SUB):
            rows = slice(r, r + FFN_SUB)
            o_ref[rows, :] = down_partial(rows)

    @pl.when(jnp.logical_and(f > 0, f < last))
    def _():
        for r in range(0, tm, FFN_SUB):
            rows = slice(r, r + FFN_SUB)
            o_ref[rows, :] += down_partial(rows)

    @pl.when(f == last)
    def _():
        x_copy.wait()
        gate = 1.0 + mod_ref[5:6, :]
        for r in range(0, tm, FFN_SUB_LAST):
            rows = slice(r, r + FFN_SUB_LAST)
            acc = o_ref[rows, :] + down_partial(rows)
            o_ref[rows, :] = _layer_norm(DEEPNORM_ALPHA * x_sc[rows, :] + gate * acc, lg_ref[...], lb_ref[...])


def _ffn_call(l, u, x2, mod, w_gu, w_down, vec):
    tm, tf = FFN_TM, FFN_TF
    nf = D_FF // tf
    return pl.pallas_call(
        _ffn_kernel,
        out_shape=jax.ShapeDtypeStruct((SEQ, D_MODEL), F32),
        grid=(SEQ // tm, nf),
        in_specs=[
            pl.BlockSpec((tm, D_MODEL), lambda i, f: (i, 0)),
            pl.BlockSpec(memory_space=pl.ANY),
            pl.BlockSpec((None, 6, D_MODEL), lambda i, f: (l, 0, 0)),
            pl.BlockSpec((D_MODEL, tf), lambda i, f: (0, f)),
            pl.BlockSpec((D_MODEL, tf), lambda i, f: (0, nf + f)),
            pl.BlockSpec((tf, D_MODEL), lambda i, f: (f, 0)),
            _vec_spec(l, "ln2_g"), _vec_spec(l, "ln2_b"),
        ],
        out_specs=pl.BlockSpec((tm, D_MODEL), lambda i, f: (i, 0)),
        scratch_shapes=[pltpu.VMEM((tm, D_MODEL), F32), pltpu.SemaphoreType.DMA(())],
        compiler_params=_cparams(("arbitrary", "arbitrary")),
        name="ffn_ln",
    )(u, x2, mod, w_gu, w_gu, w_down, vec, vec)


def _rot_half_cols(w):
    half = w.shape[-1] // 2
    return jnp.concatenate([-w[..., half:], w[..., :half]], axis=-1)


def _rope_tables():
    half = MLA_ROPE // 2
    inv = ROPE_THETA ** (-jnp.arange(half, dtype=F32) / half)
    ang = jnp.arange(SEQ, dtype=F32)[:, None] * inv[None, :]
    return jnp.cos(ang), jnp.sin(ang)


def _colscale():
    cs = np.ones((1, P_W), np.float32)
    cs[0, 0:D_A] = HEAD_DIM ** -0.5 * LOG2_E
    c0 = IN_A + PB_W
    cs[0, c0:c0 + D_C] = HEAD_DIM ** -0.5 * LOG2_E
    return jnp.asarray(cs)


def kernel(x, c, w_ada, b_ada, w_in, na_rpb, mla_q_norm, mla_kv_norm, mla_w_uq, mla_w_ukv,
           swa_sink, out_norm_g, w_o, ln1_g, ln1_b, w_gu, w_down, ln2_g, ln2_b):
    assert x.shape == (1, SEQ, D_MODEL)
    x2 = x.reshape(SEQ, D_MODEL)
    b_ada3 = b_ada.reshape(DEPTH, 1, -1)
    mod_head, cond_lanes = _ada_head_call(c.reshape(D_MODEL, 1), w_ada, b_ada3)
    mod = jnp.concatenate([mod_head, jnp.zeros((1, 6 * D_MODEL - ADA_HEAD), F32)], axis=1).reshape(1, 6, D_MODEL)
    cos_t, sin_t = _rope_tables()
    colscale = _colscale()
    w_p = _winprep_call(jnp.swapaxes(w_in, 1, 2))
    wq = mla_w_uq.reshape(DEPTH, MLA_Q_RANK, MLA_HEADS, MLA_NOPE + MLA_ROPE)
    wq = jnp.concatenate([wq, _rot_half_cols(wq[..., MLA_NOPE:])], axis=-1)
    wq = wq.reshape(DEPTH, MLA_Q_RANK, MLA_HEADS * MLA_QK).astype(BF16)
    wkv = mla_w_ukv.reshape(DEPTH, MLA_KV_RANK, MLA_HEADS, MLA_NOPE + MLA_V)
    wk = wkv[..., :MLA_NOPE].reshape(DEPTH, MLA_KV_RANK, MLA_HEADS * MLA_NOPE).astype(BF16)
    wvt = jnp.transpose(wkv[..., MLA_NOPE:], (0, 2, 3, 1)).reshape(DEPTH, MLA_HEADS * MLA_V, MLA_KV_RANK)
    wvt = wvt.astype(BF16)
    vec_parts = dict(out_norm=out_norm_g, ln1_g=ln1_g, ln1_b=ln1_b, ln2_g=ln2_g, ln2_b=ln2_b,
                     q_norm=mla_q_norm, kv_norm=mla_kv_norm)
    vec = jnp.concatenate([vec_parts[name] for name, _ in _VEC_ORDER], axis=1).reshape(DEPTH, 1, VEC_W)
    for l in range(DEPTH):
        pa, pc, q, k, vt = _inproj_call(l, x2, mod, l, w_p, colscale, vec, wq, wk, wvt, cos_t, sin_t)
        bias = _na_bias_call(na_rpb[l].reshape(-1))
        ya = _na_call(pa, bias)
        if l == 0:
            yb_even, yb_odd, w_o_b, w_gu_b, w_down_b, mod_rest = _mla_attn_call(
                l, q, k, vt, w_o, w_gu, w_down, ada=(cond_lanes, w_ada, b_ada3))
            mod = jnp.concatenate([mod_head, mod_rest], axis=1).reshape(DEPTH, 6, D_MODEL)
        else:
            yb_even, yb_odd, w_o_b, w_gu_b, w_down_b = _mla_attn_call(l, q, k, vt, w_o, w_gu, w_down)
        yc = _swa_call(swa_sink[l], pc)
        x2, u = _outproj_call(l, ya, yb_even, yb_odd, yc, x2, mod, vec, w_o_b)
        x2 = _ffn_call(l, u, x2, mod, w_gu_b, w_down_b, vec)
    return x2.reshape(1, SEQ, D_MODEL)
```

```python
import functools

import numpy as np
import jax
import jax.numpy as jnp
from jax import lax
from jax.experimental import pallas as pl
from jax.experimental.pallas import tpu as pltpu

F32 = jnp.float32
BF16 = jnp.bfloat16

D_MODEL = 2048
SEQ = 8192
DEPTH = 2
GRID_W = 64
GRID_ROWS = SEQ // GRID_W
HEAD_DIM = 128
NA_HEADS = 4
NA_WIN_ROWS = 8
NA_WIN_COLS = 16
MLA_HEADS = 6
MLA_Q_RANK = 512
MLA_KV_RANK = 256
MLA_NOPE = 128
MLA_ROPE = 64
MLA_V = 128
ROPE_THETA = 10000.0
SWA_HEADS = 6
SWA_KV_HEADS = 2
SWA_GROUP = SWA_HEADS // SWA_KV_HEADS
SWA_WINDOW = 128
SWA_BLOCK = 128
D_A = NA_HEADS * HEAD_DIM
D_B = MLA_HEADS * MLA_V
D_C = SWA_HEADS * HEAD_DIM
IN_A = 3 * D_A
IN_B = MLA_Q_RANK + MLA_KV_RANK + MLA_ROPE
IN_C = (SWA_HEADS + 2 * SWA_KV_HEADS) * HEAD_DIM
D_FF = 5632
DEEPNORM_ALPHA = (2 * DEPTH) ** 0.25
LN_EPS = 1e-5
RMS_EPS = 1e-6
NEG_INF = -1e30
LOG2_E = 1.4426950408889634

PB_W = MLA_Q_RANK + MLA_KV_RANK + 2 * MLA_ROPE
P_W = IN_A + PB_W + IN_C
MLA_QK = 2 * HEAD_DIM
MLA_VT = MLA_V + 16

NA_QROWS = 4
NA_KROWS = 12
NA_TQ = NA_QROWS * GRID_W
NA_TK = NA_KROWS * GRID_W
NA_NBLK = GRID_ROWS // NA_QROWS

VMEM_LIMIT = 56 * 1024 * 1024
VMEM_LIMIT_ATTN = 58 * 1024 * 1024


def _cparams(sem, vmem_limit=VMEM_LIMIT):
    return pltpu.CompilerParams(dimension_semantics=sem, vmem_limit_bytes=vmem_limit)


def _layer_norm(z, g, b):
    mu = jnp.mean(z, axis=-1, keepdims=True)
    zc = z - mu
    var = jnp.mean(zc * zc, axis=-1, keepdims=True)
    return zc * lax.rsqrt(var + LN_EPS) * g + b


def _rms_norm(x, g):
    ms = jnp.mean(x * x, axis=-1, keepdims=True)
    return x * lax.rsqrt(ms + RMS_EPS) * g


def _dot_nt(a, b):
    return lax.dot_general(a, b, (((1,), (1,)), ((), ())), preferred_element_type=F32)


_VEC_ORDER = (("out_norm", D_MODEL), ("ln1_g", D_MODEL), ("ln1_b", D_MODEL), ("ln2_g", D_MODEL),
              ("ln2_b", D_MODEL), ("q_norm", MLA_Q_RANK), ("kv_norm", MLA_KV_RANK))
_VEC_OFF = {}
_off = 0
for _name, _w in _VEC_ORDER:
    assert _off % _w == 0
    _VEC_OFF[_name] = (_off // _w, _w)
    _off += _w
VEC_W = _off


def _vec_spec(l, name):
    blk, w = _VEC_OFF[name]
    return pl.BlockSpec((None, 1, w), lambda *grid_idx: (l, 0, blk))


ADA_TN = 1024
ADA_RC = 256
ADA_HEAD = 2 * D_MODEL
ADA_SLAB = 256
ADA_NSLAB = (DEPTH * 6 * D_MODEL - ADA_HEAD) // ADA_SLAB


def _ada_slab(cb_ref, w_ref, b_ref, o_ref):
    tn = o_ref.shape[-1]
    acc = jnp.zeros((8, tn), F32)
    for r in range(0, D_MODEL, ADA_RC):
        cb = jnp.concatenate([cb_ref[r:r + ADA_RC, :]] * (tn // 128), axis=1)
        prod = w_ref[r:r + ADA_RC, :] * cb
        acc = acc + jnp.sum(prod.reshape(ADA_RC // 8, 8, tn), axis=0)
    o_ref[...] = jnp.sum(acc, axis=0, keepdims=True) + b_ref[...]


def _ada_head_kernel(c_ref, w_ref, b_ref, o_ref, cb_ref):
    for r in range(0, D_MODEL, ADA_RC):
        c = c_ref[r:r + ADA_RC, :]
        cond = c * (1.0 / (1.0 + jnp.exp(-c)))
        cb_ref[r:r + ADA_RC, :] = jnp.broadcast_to(cond, (ADA_RC, 128))
    _ada_slab(cb_ref, w_ref, b_ref, o_ref)


def _ada_head_call(c_col, w_ada, b_ada3):
    return pl.pallas_call(
        _ada_head_kernel,
        out_shape=(jax.ShapeDtypeStruct((1, ADA_HEAD), F32), jax.ShapeDtypeStruct((D_MODEL, 128), F32)),
        grid=(ADA_HEAD // ADA_TN,),
        in_specs=[
            pl.BlockSpec((D_MODEL, 1), lambda j: (0, 0)),
            pl.BlockSpec((None, D_MODEL, ADA_TN), lambda j: (0, 0, j)),
            pl.BlockSpec((None, 1, ADA_TN), lambda j: (0, 0, j)),
        ],
        out_specs=(pl.BlockSpec((1, ADA_TN), lambda j: (0, j)),
                   pl.BlockSpec((D_MODEL, 128), lambda j: (0, 0))),
        compiler_params=_cparams(("arbitrary",)),
        name="ada_head",
    )(c_col, w_ada, b_ada3)


def _ada_slab_index(j):
    s = jnp.minimum(j, ADA_NSLAB - 1)
    n0 = (6 * D_MODEL - ADA_HEAD) // ADA_SLAB
    return jnp.where(s < n0, 0, 1), jnp.where(s < n0, s + ADA_HEAD // ADA_SLAB, s - n0), s


WPREP_TN = 512
KR0 = IN_A + MLA_Q_RANK + MLA_KV_RANK


def _winprep_kernel(w_ref, o_ref):
    half = MLA_ROPE // 2
    kr1 = KR0 + MLA_ROPE
    o_ref[0:kr1, :] = w_ref[0:kr1, :].astype(BF16)
    o_ref[kr1:kr1 + half, :] = (-w_ref[KR0 + half:kr1, :]).astype(BF16)
    o_ref[kr1 + half:kr1 + MLA_ROPE, :] = w_ref[KR0:KR0 + half, :].astype(BF16)
    o_ref[kr1 + MLA_ROPE:, :] = w_ref[kr1:, :].astype(BF16)


def _winprep_call(w_in_t):
    tn = WPREP_TN
    return pl.pallas_call(
        _winprep_kernel,
        out_shape=jax.ShapeDtypeStruct((DEPTH, P_W, D_MODEL), BF16),
        grid=(DEPTH, D_MODEL // tn),
        in_specs=[pl.BlockSpec((None, w_in_t.shape[1], tn), lambda l, i: (l, 0, i))],
        out_specs=pl.BlockSpec((None, P_W, tn), lambda l, i: (l, 0, i)),
        compiler_params=_cparams(("parallel", "parallel")),
        name="w_in_prep",
    )(w_in_t)


INPROJ_TM = 512
_INPROJ_CHUNKS = (
    (0, 512, 0, 0), (512, 1024, 0, 512), (1024, 1536, 0, 1024),
    (2432, 2944, 1, 0), (2944, 3456, 1, 512), (3456, 3712, 1, 1024),
)
_PB0 = IN_A


_INPROJ_WROWS = tuple((c0, c1) for c0, c1, _, _ in _INPROJ_CHUNKS) + (
    (_PB0, _PB0 + MLA_Q_RANK), (_PB0 + MLA_Q_RANK, _PB0 + PB_W))


def _inproj_kernel(x_ref, mod_ref, w_hbm, cs_ref, gq_ref, gkv_ref, wq_ref, wk_ref, wvt_ref, cos_ref, sin_ref,
                   oa_ref, oc_ref, q_ref, k_ref, vt_ref, w_ref, w_sem, *, layer):
    copies = [pltpu.make_async_copy(w_hbm.at[layer, r0:r1, :], w_ref.at[r0:r1, :], w_sem.at[n])
              for n, (r0, r1) in enumerate(_INPROJ_WROWS)]
    outs = (oa_ref, oc_ref)

    def body(first):
        if first:
            for cp in copies:
                cp.start()
        sh = mod_ref[0:1, :]
        sc = mod_ref[1:2, :]
        u = (x_ref[...] * (1.0 + sc) + sh).astype(BF16)
        for n, (c0, c1, oi, off) in enumerate(_INPROJ_CHUNKS):
            if first:
                copies[n].wait()
            acc = _dot_nt(u, w_ref[c0:c1, :]) * cs_ref[:, c0:c1]
            outs[oi][:, off:off + (c1 - c0)] = acc.astype(outs[oi].dtype)
        if first:
            copies[-2].wait()
        cq = _dot_nt(u, w_ref[_PB0:_PB0 + MLA_Q_RANK, :])
        if first:
            copies[-1].wait()
        ckv_kr = _dot_nt(u, w_ref[_PB0 + MLA_Q_RANK:_PB0 + PB_W, :])
        _mla_up(cq, ckv_kr[:, 0:MLA_KV_RANK], ckv_kr[:, MLA_KV_RANK:], gq_ref, gkv_ref, wq_ref, wk_ref, wvt_ref,
                cos_ref, sin_ref, q_ref, k_ref, vt_ref)

    i = pl.program_id(0)

    @pl.when(i == 0)
    def _():
        body(True)

    @pl.when(i > 0)
    def _():
        body(False)


def _inproj_call(l, x2, mod, lmod, w_p, colscale, vec, wq, wk, wvt, cos_t, sin_t):
    tm = INPROJ_TM
    hsd = jax.ShapeDtypeStruct((MLA_HEADS, SEQ, MLA_QK), BF16)
    hspec = pl.BlockSpec((MLA_HEADS, tm, MLA_QK), lambda i: (0, i, 0))
    layer = lambda r, c: pl.BlockSpec((None, r, c), lambda i: (l, 0, 0))
    return pl.pallas_call(
        functools.partial(_inproj_kernel, layer=l),
        out_shape=(jax.ShapeDtypeStruct((SEQ, IN_A), BF16),
                   jax.ShapeDtypeStruct((SEQ, IN_C), BF16),
                   hsd, hsd, jax.ShapeDtypeStruct((MLA_HEADS, MLA_VT, SEQ), BF16)),
        grid=(SEQ // tm,),
        in_specs=[
            pl.BlockSpec((tm, D_MODEL), lambda i: (i, 0)),
            pl.BlockSpec((None, 6, D_MODEL), lambda i: (lmod, 0, 0)),
            pl.BlockSpec(memory_space=pl.ANY),
            pl.BlockSpec((1, P_W), lambda i: (0, 0)),
            _vec_spec(l, "q_norm"), _vec_spec(l, "kv_norm"),
            layer(MLA_Q_RANK, MLA_HEADS * MLA_QK), layer(MLA_KV_RANK, MLA_HEADS * MLA_NOPE),
            layer(MLA_HEADS * MLA_V, MLA_KV_RANK),
            pl.BlockSpec((tm, MLA_ROPE // 2), lambda i: (i, 0)),
            pl.BlockSpec((tm, MLA_ROPE // 2), lambda i: (i, 0)),
        ],
        out_specs=(pl.BlockSpec((tm, IN_A), lambda i: (i, 0)),
                   pl.BlockSpec((tm, IN_C), lambda i: (i, 0)),
                   hspec, hspec, pl.BlockSpec((MLA_HEADS, MLA_VT, tm), lambda i: (0, 0, i))),
        scratch_shapes=[pltpu.VMEM((P_W, D_MODEL), BF16), pltpu.SemaphoreType.DMA((len(_INPROJ_WROWS),))],
        compiler_params=_cparams(("arbitrary",)),
        name="in_proj",
    )(x2, mod, w_p, colscale, vec, vec, wq, wk, wvt, cos_t, sin_t)


def _na_block_rule(btype, i, j):
    if btype == 0:
        r0 = max(i - NA_WIN_ROWS // 2, 0)
        valid = r0 <= j < r0 + NA_WIN_ROWS
        ro = j - i + (NA_WIN_ROWS - 1)
    elif btype == 1:
        valid = i <= j < i + NA_WIN_ROWS
        ro = j - i + (NA_WIN_ROWS - 1) - NA_WIN_ROWS // 2
    else:
        r = GRID_ROWS - NA_QROWS + i
        ks = GRID_ROWS - NA_KROWS
        r0 = min(r - NA_WIN_ROWS // 2, GRID_ROWS - NA_WIN_ROWS)
        valid = r0 <= ks + j < r0 + NA_WIN_ROWS
        ro = ks + j - r + (NA_WIN_ROWS - 1)
    return ro if valid else None


def _na_bias_kernel(rpb_ref, o_ref):
    h = pl.program_id(0)
    n_ro = 2 * NA_WIN_ROWS - 1
    n_co = 2 * NA_WIN_COLS - 1
    cq = lax.broadcasted_iota(jnp.int32, (GRID_W, GRID_W), 0)
    ck = lax.broadcasted_iota(jnp.int32, (GRID_W, GRID_W), 1)
    c0 = jnp.clip(cq - NA_WIN_COLS // 2, 0, GRID_W - NA_WIN_COLS)
    coff = jnp.clip(ck - cq, -(NA_WIN_COLS - 1), NA_WIN_COLS - 1) + (NA_WIN_COLS - 1)
    neg = jnp.full((GRID_W, GRID_W), NEG_INF, F32)
    tblocks = []
    for ro in range(n_ro):
        tb = neg
        for j in range(n_co):
            tb = jnp.where(coff == j, LOG2_E * rpb_ref[h * (n_ro * n_co) + ro * n_co + j], tb)
        inside = jnp.where(ck >= c0, jnp.where(ck < c0 + NA_WIN_COLS, 1, 0), 0)
        tblocks.append(jnp.where(inside == 1, tb, neg))
    for btype in range(3):
        for i in range(NA_QROWS):
            for jp in range(NA_KROWS // 2):
                pair = []
                for j in (2 * jp, 2 * jp + 1):
                    ro = _na_block_rule(btype, i, j)
                    pair.append(neg if ro is None else tblocks[ro])
                o_ref[0, btype, i * GRID_W:(i + 1) * GRID_W, jp * 128:(jp + 1) * 128] = (
                    jnp.concatenate(pair, axis=1))


def _na_bias_call(rpb_flat):
    return pl.pallas_call(
        _na_bias_kernel,
        out_shape=jax.ShapeDtypeStruct((NA_HEADS, 3, NA_TQ, NA_TK), F32),
        grid=(NA_HEADS,),
        in_specs=[pl.BlockSpec(memory_space=pltpu.SMEM)],
        out_specs=pl.BlockSpec((1, 3, NA_TQ, NA_TK), lambda h: (h, 0, 0, 0)),
        compiler_params=_cparams(("parallel",)),
        name="na_bias",
    )(rpb_flat)


NA_BPS = 4


def _na_kernel(q_ref, k_ref, v_ref, *rest):
    bias_refs, o_ref = rest[:NA_BPS], rest[NA_BPS]
    j = pl.program_id(0)
    ones = jnp.ones((NA_TK, HEAD_DIM), BF16)
    for sub, bias_ref in enumerate(bias_refs):
        b = NA_BPS * j + sub
        ks = jnp.clip(NA_QROWS * b - NA_WIN_ROWS // 2, 0, GRID_ROWS - NA_KROWS) * GRID_W
        ks = pl.multiple_of(ks, GRID_W)
        rows = slice(sub * NA_TQ, (sub + 1) * NA_TQ)
        for h in range(NA_HEADS):
            cols = slice(h * HEAD_DIM, (h + 1) * HEAD_DIM)
            s = _dot_nt(q_ref[rows, cols], k_ref[pl.ds(ks, NA_TK), cols]) + bias_ref[h, 0]
            m = jnp.max(s, axis=1, keepdims=True)
            p = jnp.exp2(s - m).astype(BF16)
            vext = jnp.concatenate([v_ref[pl.ds(ks, NA_TK), cols], ones], axis=1)
            acc = jnp.dot(p, vext, preferred_element_type=F32)
            o_ref[rows, cols] = acc[:, 0:HEAD_DIM] * (1.0 / acc[:, HEAD_DIM:HEAD_DIM + 1])


def _na_call(pa, bias):
    def btype(b):
        return jnp.where(b == 0, 0, jnp.where(b == NA_NBLK - 1, 2, 1))
    bias_specs = [pl.BlockSpec((NA_HEADS, 1, NA_TQ, NA_TK),
                               functools.partial(lambda j, sub: (0, btype(NA_BPS * j + sub), 0, 0), sub=sub))
                  for sub in range(NA_BPS)]
    return pl.pallas_call(
        _na_kernel,
        out_shape=jax.ShapeDtypeStruct((SEQ, D_A), F32),
        grid=(NA_NBLK // NA_BPS,),
        in_specs=[
            pl.BlockSpec((NA_BPS * NA_TQ, D_A), lambda j: (j, 0)),
            pl.BlockSpec((SEQ, D_A), lambda j: (0, 1), pipeline_mode=pl.Buffered(1)),
            pl.BlockSpec((SEQ, D_A), lambda j: (0, 2), pipeline_mode=pl.Buffered(1)),
        ] + bias_specs,
        out_specs=pl.BlockSpec((NA_BPS * NA_TQ, D_A), lambda j: (j, 0)),
        compiler_params=_cparams(("arbitrary",)),
        name="na_attn",
    )(pa, pa, pa, *([bias] * NA_BPS))


def _mla_up(cq, ckv, krb, gq_ref, gkv_ref, wq_ref, wk_ref, wvt_ref, cos_ref, sin_ref, q_ref, k_ref, vt_ref):
    tm = cq.shape[0]
    cqn = _rms_norm(cq, gq_ref[...]).astype(BF16)
    ckvn = _rms_norm(ckv, gkv_ref[...]).astype(BF16)
    zeros = jnp.zeros((tm, MLA_ROPE), F32)
    cos = jnp.concatenate([cos_ref[...], cos_ref[...], zeros], axis=1)
    sin = jnp.concatenate([sin_ref[...], sin_ref[...], zeros], axis=1)

    def rotary(t):
        return t * cos + pltpu.roll(t, MLA_ROPE, 1) * sin

    kpe = rotary(krb).astype(BF16)
    ones = jnp.ones((MLA_VT - MLA_V, tm), BF16)
    scale = (MLA_NOPE + MLA_ROPE) ** -0.5 * LOG2_E
    k_all = jnp.dot(ckvn, wk_ref[...], preferred_element_type=F32).astype(BF16)
    for h in range(MLA_HEADS):
        qh = jnp.dot(cqn, wq_ref[:, h * MLA_QK:(h + 1) * MLA_QK], preferred_element_type=F32)
        q_ref[h, :, 0:HEAD_DIM] = (qh[:, 0:HEAD_DIM] * scale).astype(BF16)
        q_ref[h, :, HEAD_DIM:MLA_QK] = (rotary(qh[:, HEAD_DIM:MLA_QK]) * scale).astype(BF16)
        k_ref[h, :, 0:HEAD_DIM] = k_all[:, h * MLA_NOPE:(h + 1) * MLA_NOPE]
        k_ref[h, :, HEAD_DIM:MLA_QK] = kpe
        vt_ref[h, 0:MLA_V, :] = _dot_nt(wvt_ref[h * MLA_V:(h + 1) * MLA_V, :], ckvn).astype(BF16)
        vt_ref[h, MLA_V:MLA_VT, :] = ones


MLA_TQ = 256
MLA_TK = 1024


def _mla_attn_kernel(*refs, with_ada):
    qe_ref, qo_ref, k_ref, vt_ref, wo_ref, wgu_ref, wdn_ref = refs[:7]
    ada_in = refs[7:10] if with_ada else ()
    outs = refs[7 + len(ada_in):]
    oe_ref, oo_ref, wo_b_ref, wgu_b_ref, wdn_b_ref = outs[:5]
    ada_out = outs[5:6] if with_ada else ()
    s0_sc, s1_sc, s2_sc, s3_sc, m0_sc, m1_sc, m2_sc, m3_sc = outs[5 + len(ada_out):]
    cast_refs = ((wo_ref, wo_b_ref), (wgu_ref, wgu_b_ref), (wdn_ref, wdn_b_ref))
    j = pl.program_id(0)

    @pl.when(j == 0)
    def _():
        for ref in (s2_sc, s3_sc, m2_sc, m3_sc):
            ref[...] = jnp.zeros_like(ref)

    def stage(q_ref, sa_sc, ma_sc, sb_sc, mb_sc, vt_ref, o_ref):
        q = q_ref[0]
        tq = q.shape[0]
        m_prev = mb_sc[...]
        mx = jnp.full((8, tq), NEG_INF, F32)
        acc = jnp.zeros((MLA_VT, tq), F32)
        for c in range(SEQ // MLA_TK):
            keys = slice(c * MLA_TK, (c + 1) * MLA_TK)
            p = jnp.exp2(sb_sc[keys, :] - m_prev).astype(BF16)
            acc = acc + jnp.dot(vt_ref[0, :, keys], p, preferred_element_type=F32)
            s = _dot_nt(k_ref[0, keys, :], q)
            sa_sc[keys, :] = s
            mx = jnp.maximum(mx, jnp.max(s.reshape(MLA_TK // 8, 8, tq), axis=0))
        ma_sc[...] = jnp.max(mx, axis=0, keepdims=True)
        o = acc[0:MLA_V, :] * (1.0 / acc[MLA_V:MLA_V + 1, :])
        o_ref[...] = o.T

    def step(a_even, a_odd, b_even, b_odd):
        for w_ref, wb_ref in cast_refs:
            wb_ref[...] = w_ref[...].astype(BF16)
        if with_ada:
            _ada_slab(*ada_in, *ada_out)
        stage(qe_ref, *a_even, *b_even, vt_ref, oe_ref)
        stage(qo_ref, *a_odd, *b_odd, vt_ref, oo_ref)

    buf = ((s0_sc, m0_sc), (s1_sc, m1_sc), (s2_sc, m2_sc), (s3_sc, m3_sc))
    parity = lax.rem(j, 2)

    @pl.when(parity == 0)
    def _():
        step(buf[0], buf[1], buf[2], buf[3])

    @pl.when(parity == 1)
    def _():
        step(buf[2], buf[3], buf[0], buf[1])


WO_SLAB = (32, D_MODEL)
WGU_SLAB = (256, 1024)
WDN_SLAB = (64, D_MODEL)


def _mla_attn_call(l, q, k, vt, w_o, w_gu, w_down, ada=None):
    tq = MLA_TQ
    npair = SEQ // (2 * tq)
    last = MLA_HEADS * npair - 1
    nstep = last + 2
    pair_a = lambda j: jnp.minimum(j, last)
    pair_b = lambda j: jnp.maximum(j - 1, 0)
    half = jax.ShapeDtypeStruct((SEQ // 2, D_B), F32)
    out_spec = pl.BlockSpec((tq, MLA_V), lambda j: (pair_b(j) % npair, pair_b(j) // npair))

    def slab_specs(shape, slab):
        nr, nc = shape[0] // slab[0], shape[1] // slab[1]
        assert nr * slab[0] == shape[0] and nc * slab[1] == shape[1] and nr * nc <= nstep
        idx = lambda j: jnp.minimum(j, nr * nc - 1)
        return (pl.BlockSpec((None,) + slab, lambda j: (l, idx(j) // nc, idx(j) % nc)),
                pl.BlockSpec(slab, lambda j: (idx(j) // nc, idx(j) % nc)))

    wo_in, wo_out = slab_specs(w_o.shape[1:], WO_SLAB)
    wgu_in, wgu_out = slab_specs(w_gu.shape[1:], WGU_SLAB)
    wdn_in, wdn_out = slab_specs(w_down.shape[1:], WDN_SLAB)
    ada_args, ada_in, ada_out, ada_shape = (), [], (), ()
    if ada is not None:
        assert ADA_NSLAB <= nstep
        ada_args = ada
        ada_in = [
            pl.BlockSpec((D_MODEL, 128), lambda j: (0, 0)),
            pl.BlockSpec((None, D_MODEL, ADA_SLAB), lambda j: (_ada_slab_index(j)[0], 0, _ada_slab_index(j)[1])),
            pl.BlockSpec((None, 1, ADA_SLAB), lambda j: (_ada_slab_index(j)[0], 0, _ada_slab_index(j)[1])),
        ]
        ada_out = (pl.BlockSpec((1, ADA_SLAB), lambda j: (0, _ada_slab_index(j)[2])),)
        ada_shape = (jax.ShapeDtypeStruct((1, ADA_NSLAB * ADA_SLAB), F32),)
    return pl.pallas_call(
        functools.partial(_mla_attn_kernel, with_ada=ada is not None),
        out_shape=(half, half,
                   jax.ShapeDtypeStruct(w_o.shape[1:], BF16),
                   jax.ShapeDtypeStruct(w_gu.shape[1:], BF16),
                   jax.ShapeDtypeStruct(w_down.shape[1:], BF16)) + ada_shape,
        grid=(nstep,),
        in_specs=[
            pl.BlockSpec((1, tq, MLA_QK), lambda j: (pair_a(j) // npair, 2 * (pair_a(j) % npair), 0)),
            pl.BlockSpec((1, tq, MLA_QK), lambda j: (pair_a(j) // npair, 2 * (pair_a(j) % npair) + 1, 0)),
            pl.BlockSpec((1, SEQ, MLA_QK), lambda j: (pair_a(j) // npair, 0, 0)),
            pl.BlockSpec((1, MLA_VT, SEQ), lambda j: (pair_b(j) // npair, 0, 0)),
            wo_in, wgu_in, wdn_in,
        ] + ada_in,
        out_specs=(out_spec, out_spec, wo_out, wgu_out, wdn_out) + ada_out,
        scratch_shapes=[pltpu.VMEM((SEQ, tq), F32)] * 4 + [pltpu.VMEM((1, tq), F32)] * 4,
        compiler_params=_cparams(("arbitrary",), VMEM_LIMIT_ATTN),
        name="mla_attn",
    )(q, q, k, vt, w_o, w_gu, w_down, *ada_args)


_SWA_SLOPES = tuple(2.0 ** (-8.0 * (i + 1) / SWA_HEADS) for i in range(SWA_HEADS))


SWA_NB = 16


def _swa_kernel(sink_ref, q_ref, kp_ref, kc_ref, kn_ref, vp_ref, vc_ref, vn_ref, o_ref):
    g_kv = pl.program_id(0)
    j = pl.program_id(1)
    t = SWA_BLOCK
    rows = SWA_GROUP * t
    kwin = jnp.concatenate([kp_ref[...], kc_ref[...], kn_ref[...]], axis=0)
    vwin = jnp.concatenate([vp_ref[...], vc_ref[...], vn_ref[...]], axis=0)
    ones = jnp.ones((3 * t, HEAD_DIM), BF16)
    ri = lax.broadcasted_iota(jnp.int32, (rows, 3 * t), 0)
    ci = lax.broadcasted_iota(jnp.int32, (rows, 3 * t), 1)
    grp = jnp.right_shift(ri, 7)
    dist = jnp.abs(jnp.bitwise_and(ri, t - 1) - (ci - t))
    slope_lo = jnp.where(grp == 0, _SWA_SLOPES[0], jnp.where(grp == 1, _SWA_SLOPES[1], _SWA_SLOPES[2]))
    slope_hi = jnp.where(grp == 0, _SWA_SLOPES[3], jnp.where(grp == 1, _SWA_SLOPES[4], _SWA_SLOPES[5]))
    slope = jnp.where(g_kv == 0, slope_lo, slope_hi)
    band = jnp.where(dist <= SWA_WINDOW, (-LOG2_E) * slope * dist.astype(F32), NEG_INF)
    rcol = jnp.right_shift(lax.broadcasted_iota(jnp.int32, (rows, 1), 0), 7)
    base = g_kv * SWA_GROUP
    sink = LOG2_E * jnp.where(rcol == 0, sink_ref[base],
                              jnp.where(rcol == 1, sink_ref[base + 1], sink_ref[base + 2]))
    first_cols = jnp.where(j == 0, t, 0)
    last_cols = jnp.where(j == pl.num_programs(1) - 1, 2 * t, 3 * t)
    for b in range(SWA_NB):
        q = jnp.concatenate([q_ref[b * t:(b + 1) * t, g * t:(g + 1) * t] for g in range(SWA_GROUP)], axis=0)
        s = _dot_nt(q, kwin[b * t:(b + 3) * t, :]) + band
        if b == 0:
            s = jnp.where(ci < first_cols, NEG_INF, s)
        if b == SWA_NB - 1:
            s = jnp.where(ci >= last_cols, NEG_INF, s)
        m = jnp.maximum(jnp.max(s, axis=1, keepdims=True), sink)
        p = jnp.exp2(s - m).astype(BF16)
        vext = jnp.concatenate([vwin[b * t:(b + 3) * t, :], ones], axis=1)
        acc = jnp.dot(p, vext, preferred_element_type=F32)
        l = acc[:, HEAD_DIM:HEAD_DIM + 1] + jnp.exp2(sink - m)
        o = acc[:, 0:HEAD_DIM] * (1.0 / l)
        for g in range(SWA_GROUP):
            o_ref[b * t:(b + 1) * t, g * t:(g + 1) * t] = o[g * t:(g + 1) * t, :]


def _swa_call(sink, pc):
    t = SWA_BLOCK
    nb = SEQ // t
    tq = SWA_NB * t
    kcol = SWA_HEADS
    vcol = SWA_HEADS + SWA_KV_HEADS
    prev = lambda j: jnp.maximum(SWA_NB * j - 1, 0)
    nxt = lambda j: jnp.minimum(SWA_NB * (j + 1), nb - 1)
    return pl.pallas_call(
        _swa_kernel,
        out_shape=jax.ShapeDtypeStruct((SEQ, D_C), F32),
        grid=(SWA_KV_HEADS, SEQ // tq),
        in_specs=[
            pl.BlockSpec(memory_space=pltpu.SMEM),
            pl.BlockSpec((tq, SWA_GROUP * t), lambda g, j: (j, g)),
            pl.BlockSpec((t, t), lambda g, j: (prev(j), kcol + g)),
            pl.BlockSpec((tq, t), lambda g, j: (j, kcol + g)),
            pl.BlockSpec((t, t), lambda g, j: (nxt(j), kcol + g)),
            pl.BlockSpec((t, t), lambda g, j: (prev(j), vcol + g)),
            pl.BlockSpec((tq, t), lambda g, j: (j, vcol + g)),
            pl.BlockSpec((t, t), lambda g, j: (nxt(j), vcol + g)),
        ],
        out_specs=pl.BlockSpec((tq, SWA_GROUP * t), lambda g, j: (j, g)),
        compiler_params=_cparams(("parallel", "arbitrary")),
        name="swa_attn",
    )(sink, pc, pc, pc, pc, pc, pc, pc)


OUTPROJ_TM = 2 * MLA_TQ


def _outproj_kernel(ya_ref, ybe_ref, ybo_ref, yc_ref, x_ref, mod_ref, gn_ref, w_ref, lg_ref, lb_ref,
                    o_ref, u_ref):
    hm = OUTPROJ_TM // 2
    gate = 1.0 + mod_ref[2:3, :]
    for half, yb_ref in enumerate((ybe_ref, ybo_ref)):
        rows = slice(half * hm, (half + 1) * hm)
        acc = None
        for y, c0 in ((ya_ref[rows, :], 0), (yb_ref[...], D_A), (yc_ref[rows, :], D_A + D_B)):
            c1 = c0 + y.shape[1]
            yn = _rms_norm(y, gn_ref[:, c0:c1]).astype(BF16)
            part = jnp.dot(yn, w_ref[c0:c1, :], preferred_element_type=F32)
            acc = part if acc is None else acc + part
        x1 = _layer_norm(DEEPNORM_ALPHA * x_ref[rows, :] + gate * acc, lg_ref[...], lb_ref[...])
        o_ref[rows, :] = x1
        u_ref[rows, :] = (x1 * (1.0 + mod_ref[4:5, :]) + mod_ref[3:4, :]).astype(BF16)


def _outproj_call(l, ya, yb_even, yb_odd, yc, x2, mod, vec, w_o):
    tm = OUTPROJ_TM
    row = lambda r, w: pl.BlockSpec((r, w), lambda i: (i, 0))
    return pl.pallas_call(
        _outproj_kernel,
        out_shape=(jax.ShapeDtypeStruct((SEQ, D_MODEL), F32), jax.ShapeDtypeStruct((SEQ, D_MODEL), BF16)),
        grid=(SEQ // tm,),
        in_specs=[row(tm, D_A), row(tm // 2, D_B), row(tm // 2, D_B), row(tm, D_C), row(tm, D_MODEL),
                  pl.BlockSpec((None, 6, D_MODEL), lambda i: (l, 0, 0)), _vec_spec(l, "out_norm"),
                  pl.BlockSpec((D_MODEL, D_MODEL), lambda i: (0, 0), pipeline_mode=pl.Buffered(1)),
                  _vec_spec(l, "ln1_g"), _vec_spec(l, "ln1_b")],
        out_specs=(row(tm, D_MODEL), row(tm, D_MODEL)),
        compiler_params=_cparams(("parallel",)),
        name="out_proj_ln",
    )(ya, yb_even, yb_odd, yc, x2, mod, vec, w_o, vec, vec)


FFN_TM = 1024
FFN_SUB = 512
FFN_SUB_LAST = 256
FFN_TF = 512


def _ffn_kernel(u_ref, x_hbm, mod_ref, wg_ref, wu_ref, wd_ref, lg_ref, lb_ref, o_ref, x_sc, x_sem):
    i = pl.program_id(0)
    f = pl.program_id(1)
    tm = o_ref.shape[0]
    x_copy = pltpu.make_async_copy(x_hbm.at[pl.ds(pl.multiple_of(i * tm, tm), tm), :], x_sc, x_sem)

    last = pl.num_programs(1) - 1

    def down_partial(rows):
        u = u_ref[rows, :]
        g = jnp.dot(u, wg_ref[...], preferred_element_type=F32)
        up = jnp.dot(u, wu_ref[...], preferred_element_type=F32)
        hdn = (g * (1.0 / (1.0 + jnp.exp(-g))) * up).astype(BF16)
        return jnp.dot(hdn, wd_ref[...], preferred_element_type=F32)

    @pl.when(f == 0)
    def _():
        x_copy.start()
        for r in range(0, tm, FFN_SUB):
            rows = slice(r, r + FFN_SUB)
            o_ref[rows, :] = down_partial(rows)

    @pl.when(jnp.logical_and(f > 0, f < last))
    def _():
        for r in range(0, tm, FFN_SUB):
            rows = slice(r, r + FFN_SUB)
            o_ref[rows, :] += down_partial(rows)

    @pl.when(f == last)
    def _():
        x_copy.wait()
        gate = 1.0 + mod_ref[5:6, :]
        for r in range(0, tm, FFN_SUB_LAST):
            rows = slice(r, r + FFN_SUB_LAST)
            acc = o_ref[rows, :] + down_partial(rows)
            o_ref[rows, :] = _layer_norm(DEEPNORM_ALPHA * x_sc[rows, :] + gate * acc, lg_ref[...], lb_ref[...])


def _ffn_call(l, u, x2, mod, w_gu, w_down, vec):
    tm, tf = FFN_TM, FFN_TF
    nf = D_FF // tf
    return pl.pallas_call(
        _ffn_kernel,
        out_shape=jax.ShapeDtypeStruct((SEQ, D_MODEL), F32),
        grid=(SEQ // tm, nf),
        in_specs=[
            pl.BlockSpec((tm, D_MODEL), lambda i, f: (i, 0)),
            pl.BlockSpec(memory_space=pl.ANY),
            pl.BlockSpec((None, 6, D_MODEL), lambda i, f: (l, 0, 0)),
            pl.BlockSpec((D_MODEL, tf), lambda i, f: (0, f)),
            pl.BlockSpec((D_MODEL, tf), lambda i, f: (0, nf + f)),
            pl.BlockSpec((tf, D_MODEL), lambda i, f: (f, 0)),
            _vec_spec(l, "ln2_g"), _vec_spec(l, "ln2_b"),
        ],
        out_specs=pl.BlockSpec((tm, D_MODEL), lambda i, f: (i, 0)),
        scratch_shapes=[pltpu.VMEM((tm, D_MODEL), F32), pltpu.SemaphoreType.DMA(())],
        compiler_params=_cparams(("arbitrary", "arbitrary")),
        name="ffn_ln",
    )(u, x2, mod, w_gu, w_gu, w_down, vec, vec)


def _rot_half_cols(w):
    half = w.shape[-1] // 2
    return jnp.concatenate([-w[..., half:], w[..., :half]], axis=-1)


def _rope_tables():
    half = MLA_ROPE // 2
    inv = ROPE_THETA ** (-jnp.arange(half, dtype=F32) / half)
    ang = jnp.arange(SEQ, dtype=F32)[:, None] * inv[None, :]
    return jnp.cos(ang), jnp.sin(ang)


def _colscale():
    cs = np.ones((1, P_W), np.float32)
    cs[0, 0:D_A] = HEAD_DIM ** -0.5 * LOG2_E
    c0 = IN_A + PB_W
    cs[0, c0:c0 + D_C] = HEAD_DIM ** -0.5 * LOG2_E
    return jnp.asarray(cs)


def kernel(x, c, w_ada, b_ada, w_in, na_rpb, mla_q_norm, mla_kv_norm, mla_w_uq, mla_w_ukv,
           swa_sink, out_norm_g, w_o, ln1_g, ln1_b, w_gu, w_down, ln2_g, ln2_b):
    assert x.shape == (1, SEQ, D_MODEL)
    x2 = x.reshape(SEQ, D_MODEL)
    b_ada3 = b_ada.reshape(DEPTH, 1, -1)
    mod_head, cond_lanes = _ada_head_call(c.reshape(D_MODEL, 1), w_ada, b_ada3)
    mod = jnp.concatenate([mod_head, jnp.zeros((1, 6 * D_MODEL - ADA_HEAD), F32)], axis=1).reshape(1, 6, D_MODEL)
    cos_t, sin_t = _rope_tables()
    colscale = _colscale()
    w_p = _winprep_call(jnp.swapaxes(w_in, 1, 2))
    wq = mla_w_uq.reshape(DEPTH, MLA_Q_RANK, MLA_HEADS, MLA_NOPE + MLA_ROPE)
    wq = jnp.concatenate([wq, _rot_half_cols(wq[..., MLA_NOPE:])], axis=-1)
    wq = wq.reshape(DEPTH, MLA_Q_RANK, MLA_HEADS * MLA_QK).astype(BF16)
    wkv = mla_w_ukv.reshape(DEPTH, MLA_KV_RANK, MLA_HEADS, MLA_NOPE + MLA_V)
    wk = wkv[..., :MLA_NOPE].reshape(DEPTH, MLA_KV_RANK, MLA_HEADS * MLA_NOPE).astype(BF16)
    wvt = jnp.transpose(wkv[..., MLA_NOPE:], (0, 2, 3, 1)).reshape(DEPTH, MLA_HEADS * MLA_V, MLA_KV_RANK)
    wvt = wvt.astype(BF16)
    vec_parts = dict(out_norm=out_norm_g, ln1_g=ln1_g, ln1_b=ln1_b, ln2_g=ln2_g, ln2_b=ln2_b,
                     q_norm=mla_q_norm, kv_norm=mla_kv_norm)
    vec = jnp.concatenate([vec_parts[name] for name, _ in _VEC_ORDER], axis=1).reshape(DEPTH, 1, VEC_W)
    for l in range(DEPTH):
        pa, pc, q, k, vt = _inproj_call(l, x2, mod, l, w_p, colscale, vec, wq, wk, wvt, cos_t, sin_t)
        bias = _na_bias_call(na_rpb[l].reshape(-1))
        ya = _na_call(pa, bias)
        if l == 0:
            yb_even, yb_odd, w_o_b, w_gu_b, w_down_b, mod_rest = _mla_attn_call(
                l, q, k, vt, w_o, w_gu, w_down, ada=(cond_lanes, w_ada, b_ada3))
            mod = jnp.concatenate([mod_head, mod_rest], axis=1).reshape(DEPTH, 6, D_MODEL)
        else:
            yb_even, yb_odd, w_o_b, w_gu_b, w_down_b = _mla_attn_call(l, q, k, vt, w_o, w_gu, w_down)
        yc = _swa_call(swa_sink[l], pc)
        x2, u = _outproj_call(l, ya, yb_even, yb_odd, yc, x2, mod, vec, w_o_b)
        x2 = _ffn_call(l, u, x2, mod, w_gu_b, w_down_b, vec)
    return x2.reshape(1, SEQ, D_MODEL)
```

```python
import functools

import numpy as np
import jax
import jax.numpy as jnp
from jax import lax
from jax.experimental import pallas as pl
from jax.experimental.pallas import tpu as pltpu

F32 = jnp.float32
BF16 = jnp.bfloat16

D_MODEL = 2048
SEQ = 8192
DEPTH = 2
GRID_W = 64
GRID_ROWS = SEQ // GRID_W
HEAD_DIM = 128
NA_HEADS = 4
NA_WIN_ROWS = 8
NA_WIN_COLS = 16
MLA_HEADS = 6
MLA_Q_RANK = 512
MLA_KV_RANK = 256
MLA_NOPE = 128
MLA_ROPE = 64
MLA_V = 128
ROPE_THETA = 10000.0
SWA_HEADS = 6
SWA_KV_HEADS = 2
SWA_GROUP = SWA_HEADS // SWA_KV_HEADS
SWA_WINDOW = 128
SWA_BLOCK = 128
D_A = NA_HEADS * HEAD_DIM
D_B = MLA_HEADS * MLA_V
D_C = SWA_HEADS * HEAD_DIM
IN_A = 3 * D_A
IN_B = MLA_Q_RANK + MLA_KV_RANK + MLA_ROPE
IN_C = (SWA_HEADS + 2 * SWA_KV_HEADS) * HEAD_DIM
D_FF = 5632
DEEPNORM_ALPHA = (2 * DEPTH) ** 0.25
LN_EPS = 1e-5
RMS_EPS = 1e-6
NEG_INF = -1e30
LOG2_E = 1.4426950408889634

PB_W = MLA_Q_RANK + MLA_KV_RANK + 2 * MLA_ROPE
P_W = IN_A + PB_W + IN_C
MLA_QK = 2 * HEAD_DIM
MLA_VT = MLA_V + 16

NA_QROWS = 4
NA_KROWS = 12
NA_TQ = NA_QROWS * GRID_W
NA_TK = NA_KROWS * GRID_W
NA_NBLK = GRID_ROWS // NA_QROWS

VMEM_LIMIT = 56 * 1024 * 1024
VMEM_LIMIT_ATTN = 58 * 1024 * 1024


def _cparams(sem, vmem_limit=VMEM_LIMIT):
    return pltpu.CompilerParams(dimension_semantics=sem, vmem_limit_bytes=vmem_limit)


def _layer_norm(z, g, b):
    mu = jnp.mean(z, axis=-1, keepdims=True)
    zc = z - mu
    var = jnp.mean(zc * zc, axis=-1, keepdims=True)
    return zc * lax.rsqrt(var + LN_EPS) * g + b


def _rms_norm(x, g):
    ms = jnp.mean(x * x, axis=-1, keepdims=True)
    return x * lax.rsqrt(ms + RMS_EPS) * g


def _dot_nt(a, b):
    return lax.dot_general(a, b, (((1,), (1,)), ((), ())), preferred_element_type=F32)


_VEC_ORDER = (("out_norm", D_MODEL), ("ln1_g", D_MODEL), ("ln1_b", D_MODEL), ("ln2_g", D_MODEL),
              ("ln2_b", D_MODEL), ("q_norm", MLA_Q_RANK), ("kv_norm", MLA_KV_RANK))
_VEC_OFF = {}
_off = 0
for _name, _w in _VEC_ORDER:
    assert _off % _w == 0
    _VEC_OFF[_name] = (_off // _w, _w)
    _off += _w
VEC_W = _off


def _vec_spec(l, name):
    blk, w = _VEC_OFF[name]
    return pl.BlockSpec((None, 1, w), lambda *grid_idx: (l, 0, blk))


ADA_TN = 1024
ADA_RC = 256
ADA_HEAD = 2 * D_MODEL
ADA_SLAB = 256
ADA_NSLAB = (DEPTH * 6 * D_MODEL - ADA_HEAD) // ADA_SLAB


def _ada_slab(cb_ref, w_ref, b_ref, o_ref):
    tn = o_ref.shape[-1]
    acc = jnp.zeros((8, tn), F32)
    for r in range(0, D_MODEL, ADA_RC):
        cb = jnp.concatenate([cb_ref[r:r + ADA_RC, :]] * (tn // 128), axis=1)
        prod = w_ref[r:r + ADA_RC, :] * cb
        acc = acc + jnp.sum(prod.reshape(ADA_RC // 8, 8, tn), axis=0)
    o_ref[...] = jnp.sum(acc, axis=0, keepdims=True) + b_ref[...]


def _ada_head_kernel(c_ref, w_ref, b_ref, o_ref, cb_ref):
    for r in range(0, D_MODEL, ADA_RC):
        c = c_ref[r:r + ADA_RC, :]
        cond = c * (1.0 / (1.0 + jnp.exp(-c)))
        cb_ref[r:r + ADA_RC, :] = jnp.broadcast_to(cond, (ADA_RC, 128))
    _ada_slab(cb_ref, w_ref, b_ref, o_ref)


def _ada_head_call(c_col, w_ada, b_ada3):
    return pl.pallas_call(
        _ada_head_kernel,
        out_shape=(jax.ShapeDtypeStruct((1, ADA_HEAD), F32), jax.ShapeDtypeStruct((D_MODEL, 128), F32)),
        grid=(ADA_HEAD // ADA_TN,),
        in_specs=[
            pl.BlockSpec((D_MODEL, 1), lambda j: (0, 0)),
            pl.BlockSpec((None, D_MODEL, ADA_TN), lambda j: (0, 0, j)),
            pl.BlockSpec((None, 1, ADA_TN), lambda j: (0, 0, j)),
        ],
        out_specs=(pl.BlockSpec((1, ADA_TN), lambda j: (0, j)),
                   pl.BlockSpec((D_MODEL, 128), lambda j: (0, 0))),
        compiler_params=_cparams(("arbitrary",)),
        name="ada_head",
    )(c_col, w_ada, b_ada3)


def _ada_slab_index(j):
    s = jnp.minimum(j, ADA_NSLAB - 1)
    n0 = (6 * D_MODEL - ADA_HEAD) // ADA_SLAB
    return jnp.where(s < n0, 0, 1), jnp.where(s < n0, s + ADA_HEAD // ADA_SLAB, s - n0), s


WPREP_TN = 512
KR0 = IN_A + MLA_Q_RANK + MLA_KV_RANK


def _winprep_kernel(w_ref, o_ref):
    half = MLA_ROPE // 2
    kr1 = KR0 + MLA_ROPE
    o_ref[0:kr1, :] = w_ref[0:kr1, :].astype(BF16)
    o_ref[kr1:kr1 + half, :] = (-w_ref[KR0 + half:kr1, :]).astype(BF16)
    o_ref[kr1 + half:kr1 + MLA_ROPE, :] = w_ref[KR0:KR0 + half, :].astype(BF16)
    o_ref[kr1 + MLA_ROPE:, :] = w_ref[kr1:, :].astype(BF16)


def _winprep_call(w_in_t):
    tn = WPREP_TN
    return pl.pallas_call(
        _winprep_kernel,
        out_shape=jax.ShapeDtypeStruct((DEPTH, P_W, D_MODEL), BF16),
        grid=(DEPTH, D_MODEL // tn),
        in_specs=[pl.BlockSpec((None, w_in_t.shape[1], tn), lambda l, i: (l, 0, i))],
        out_specs=pl.BlockSpec((None, P_W, tn), lambda l, i: (l, 0, i)),
        compiler_params=_cparams(("parallel", "parallel")),
        name="w_in_prep",
    )(w_in_t)


INPROJ_TM = 512
_INPROJ_CHUNKS = (
    (0, 512, 0, 0), (512, 1024, 0, 512), (1024, 1536, 0, 1024),
    (2432, 2944, 1, 0), (2944, 3456, 1, 512), (3456, 3712, 1, 1024),
)
_PB0 = IN_A


def _inproj_kernel(x_ref, mod_ref, w_ref, cs_ref, gq_ref, gkv_ref, wq_ref, wk_ref, wvt_ref, cos_ref, sin_ref,
                   oa_ref, oc_ref, q_ref, k_ref, vt_ref):
    outs = (oa_ref, oc_ref)
    sh = mod_ref[0:1, :]
    sc = mod_ref[1:2, :]
    u = (x_ref[...] * (1.0 + sc) + sh).astype(BF16)
    for c0, c1, oi, off in _INPROJ_CHUNKS:
        acc = _dot_nt(u, w_ref[c0:c1, :]) * cs_ref[:, c0:c1]
        outs[oi][:, off:off + (c1 - c0)] = acc.astype(outs[oi].dtype)
    cq = _dot_nt(u, w_ref[_PB0:_PB0 + MLA_Q_RANK, :])
    ckv_kr = _dot_nt(u, w_ref[_PB0 + MLA_Q_RANK:_PB0 + PB_W, :])
    _mla_up(cq, ckv_kr[:, 0:MLA_KV_RANK], ckv_kr[:, MLA_KV_RANK:], gq_ref, gkv_ref, wq_ref, wk_ref, wvt_ref,
            cos_ref, sin_ref, q_ref, k_ref, vt_ref)


def _inproj_call(l, x2, mod, lmod, w_p, colscale, vec, wq, wk, wvt, cos_t, sin_t):
    tm = INPROJ_TM
    hsd = jax.ShapeDtypeStruct((MLA_HEADS, SEQ, MLA_QK), BF16)
    hspec = pl.BlockSpec((MLA_HEADS, tm, MLA_QK), lambda i: (0, i, 0))
    layer = lambda r, c: pl.BlockSpec((None, r, c), lambda i: (l, 0, 0))
    return pl.pallas_call(
        _inproj_kernel,
        out_shape=(jax.ShapeDtypeStruct((SEQ, IN_A), BF16),
                   jax.ShapeDtypeStruct((SEQ, IN_C), BF16),
                   hsd, hsd, jax.ShapeDtypeStruct((MLA_HEADS, MLA_VT, SEQ), BF16)),
        grid=(SEQ // tm,),
        in_specs=[
            pl.BlockSpec((tm, D_MODEL), lambda i: (i, 0)),
            pl.BlockSpec((None, 6, D_MODEL), lambda i: (lmod, 0, 0)),
            pl.BlockSpec((None, P_W, D_MODEL), lambda i: (l, 0, 0), pipeline_mode=pl.Buffered(1)),
            pl.BlockSpec((1, P_W), lambda i: (0, 0)),
            _vec_spec(l, "q_norm"), _vec_spec(l, "kv_norm"),
            layer(MLA_Q_RANK, MLA_HEADS * MLA_QK), layer(MLA_KV_RANK, MLA_HEADS * MLA_NOPE),
            layer(MLA_HEADS * MLA_V, MLA_KV_RANK),
            pl.BlockSpec((tm, MLA_ROPE // 2), lambda i: (i, 0)),
            pl.BlockSpec((tm, MLA_ROPE // 2), lambda i: (i, 0)),
        ],
        out_specs=(pl.BlockSpec((tm, IN_A), lambda i: (i, 0)),
                   pl.BlockSpec((tm, IN_C), lambda i: (i, 0)),
                   hspec, hspec, pl.BlockSpec((MLA_HEADS, MLA_VT, tm), lambda i: (0, 0, i))),
        compiler_params=_cparams(("parallel",)),
        name="in_proj",
    )(x2, mod, w_p, colscale, vec, vec, wq, wk, wvt, cos_t, sin_t)


def _na_block_rule(btype, i, j):
    if btype == 0:
        r0 = max(i - NA_WIN_ROWS // 2, 0)
        valid = r0 <= j < r0 + NA_WIN_ROWS
        ro = j - i + (NA_WIN_ROWS - 1)
    elif btype == 1:
        valid = i <= j < i + NA_WIN_ROWS
        ro = j - i + (NA_WIN_ROWS - 1) - NA_WIN_ROWS // 2
    else:
        r = GRID_ROWS - NA_QROWS + i
        ks = GRID_ROWS - NA_KROWS
        r0 = min(r - NA_WIN_ROWS // 2, GRID_ROWS - NA_WIN_ROWS)
        valid = r0 <= ks + j < r0 + NA_WIN_ROWS
        ro = ks + j - r + (NA_WIN_ROWS - 1)
    return ro if valid else None


def _na_bias_kernel(rpb_ref, o_ref):
    h = pl.program_id(0)
    n_ro = 2 * NA_WIN_ROWS - 1
    n_co = 2 * NA_WIN_COLS - 1
    cq = lax.broadcasted_iota(jnp.int32, (GRID_W, GRID_W), 0)
    ck = lax.broadcasted_iota(jnp.int32, (GRID_W, GRID_W), 1)
    c0 = jnp.clip(cq - NA_WIN_COLS // 2, 0, GRID_W - NA_WIN_COLS)
    coff = jnp.clip(ck - cq, -(NA_WIN_COLS - 1), NA_WIN_COLS - 1) + (NA_WIN_COLS - 1)
    neg = jnp.full((GRID_W, GRID_W), NEG_INF, F32)
    tblocks = []
    for ro in range(n_ro):
        tb = neg
        for j in range(n_co):
            tb = jnp.where(coff == j, LOG2_E * rpb_ref[h * (n_ro * n_co) + ro * n_co + j], tb)
        inside = jnp.where(ck >= c0, jnp.where(ck < c0 + NA_WIN_COLS, 1, 0), 0)
        tblocks.append(jnp.where(inside == 1, tb, neg))
    for btype in range(3):
        for i in range(NA_QROWS):
            for jp in range(NA_KROWS // 2):
                pair = []
                for j in (2 * jp, 2 * jp + 1):
                    ro = _na_block_rule(btype, i, j)
                    pair.append(neg if ro is None else tblocks[ro])
                o_ref[0, btype, i * GRID_W:(i + 1) * GRID_W, jp * 128:(jp + 1) * 128] = (
                    jnp.concatenate(pair, axis=1))


def _na_bias_call(rpb_flat):
    return pl.pallas_call(
        _na_bias_kernel,
        out_shape=jax.ShapeDtypeStruct((DEPTH * NA_HEADS, 3, NA_TQ, NA_TK), F32),
        grid=(DEPTH * NA_HEADS,),
        in_specs=[pl.BlockSpec(memory_space=pltpu.SMEM)],
        out_specs=pl.BlockSpec((1, 3, NA_TQ, NA_TK), lambda h: (h, 0, 0, 0)),
        compiler_params=_cparams(("parallel",)),
        name="na_bias",
    )(rpb_flat)


NA_BPS = 4


def _na_kernel(q_ref, k_ref, v_ref, *rest):
    bias_refs, o_ref = rest[:NA_BPS], rest[NA_BPS]
    j = pl.program_id(0)
    ones = jnp.ones((NA_TK, HEAD_DIM), BF16)
    for sub, bias_ref in enumerate(bias_refs):
        b = NA_BPS * j + sub
        ks = jnp.clip(NA_QROWS * b - NA_WIN_ROWS // 2, 0, GRID_ROWS - NA_KROWS) * GRID_W
        ks = pl.multiple_of(ks, GRID_W)
        rows = slice(sub * NA_TQ, (sub + 1) * NA_TQ)
        for h in range(NA_HEADS):
            cols = slice(h * HEAD_DIM, (h + 1) * HEAD_DIM)
            s = _dot_nt(q_ref[rows, cols], k_ref[pl.ds(ks, NA_TK), cols]) + bias_ref[h, 0]
            m = jnp.max(s, axis=1, keepdims=True)
            p = jnp.exp2(s - m).astype(BF16)
            vext = jnp.concatenate([v_ref[pl.ds(ks, NA_TK), cols], ones], axis=1)
            acc = jnp.dot(p, vext, preferred_element_type=F32)
            o_ref[rows, cols] = acc[:, 0:HEAD_DIM] * (1.0 / acc[:, HEAD_DIM:HEAD_DIM + 1])


def _na_call(l, pa, bias):
    def btype(b):
        return jnp.where(b == 0, 0, jnp.where(b == NA_NBLK - 1, 2, 1))
    bias_specs = [pl.BlockSpec((NA_HEADS, 1, NA_TQ, NA_TK),
                               functools.partial(lambda j, sub: (l, btype(NA_BPS * j + sub), 0, 0), sub=sub))
                  for sub in range(NA_BPS)]
    return pl.pallas_call(
        _na_kernel,
        out_shape=jax.ShapeDtypeStruct((SEQ, D_A), F32),
        grid=(NA_NBLK // NA_BPS,),
        in_specs=[
            pl.BlockSpec((NA_BPS * NA_TQ, D_A), lambda j: (j, 0)),
            pl.BlockSpec((SEQ, D_A), lambda j: (0, 1), pipeline_mode=pl.Buffered(1)),
            pl.BlockSpec((SEQ, D_A), lambda j: (0, 2), pipeline_mode=pl.Buffered(1)),
        ] + bias_specs,
        out_specs=pl.BlockSpec((NA_BPS * NA_TQ, D_A), lambda j: (j, 0)),
        compiler_params=_cparams(("arbitrary",)),
        name="na_attn",
    )(pa, pa, pa, *([bias] * NA_BPS))


def _mla_up(cq, ckv, krb, gq_ref, gkv_ref, wq_ref, wk_ref, wvt_ref, cos_ref, sin_ref, q_ref, k_ref, vt_ref):
    tm = cq.shape[0]
    cqn = _rms_norm(cq, gq_ref[...]).astype(BF16)
    ckvn = _rms_norm(ckv, gkv_ref[...]).astype(BF16)
    zeros = jnp.zeros((tm, MLA_ROPE), F32)
    cos = jnp.concatenate([cos_ref[...], cos_ref[...], zeros], axis=1)
    sin = jnp.concatenate([sin_ref[...], sin_ref[...], zeros], axis=1)

    def rotary(t):
        return t * cos + pltpu.roll(t, MLA_ROPE, 1) * sin

    kpe = rotary(krb).astype(BF16)
    ones = jnp.ones((MLA_VT - MLA_V, tm), BF16)
    scale = (MLA_NOPE + MLA_ROPE) ** -0.5 * LOG2_E
    k_all = jnp.dot(ckvn, wk_ref[...], preferred_element_type=F32).astype(BF16)
    for h in range(MLA_HEADS):
        qh = jnp.dot(cqn, wq_ref[:, h * MLA_QK:(h + 1) * MLA_QK], preferred_element_type=F32)
        q_ref[h, :, 0:HEAD_DIM] = (qh[:, 0:HEAD_DIM] * scale).astype(BF16)
        q_ref[h, :, HEAD_DIM:MLA_QK] = (rotary(qh[:, HEAD_DIM:MLA_QK]) * scale).astype(BF16)
        k_ref[h, :, 0:HEAD_DIM] = k_all[:, h * MLA_NOPE:(h + 1) * MLA_NOPE]
        k_ref[h, :, HEAD_DIM:MLA_QK] = kpe
        vt_ref[h, 0:MLA_V, :] = _dot_nt(wvt_ref[h * MLA_V:(h + 1) * MLA_V, :], ckvn).astype(BF16)
        vt_ref[h, MLA_V:MLA_VT, :] = ones


MLA_TQ = 256
MLA_TK = 1024


def _mla_attn_kernel(*refs, with_ada):
    qe_ref, qo_ref, k_ref, vt_ref, wo_ref, wgu_ref, wdn_ref = refs[:7]
    ada_in = refs[7:10] if with_ada else ()
    outs = refs[7 + len(ada_in):]
    oe_ref, oo_ref, wo_b_ref, wgu_b_ref, wdn_b_ref = outs[:5]
    ada_out = outs[5:6] if with_ada else ()
    s0_sc, s1_sc, s2_sc, s3_sc, m0_sc, m1_sc, m2_sc, m3_sc = outs[5 + len(ada_out):]
    cast_refs = ((wo_ref, wo_b_ref), (wgu_ref, wgu_b_ref), (wdn_ref, wdn_b_ref))
    j = pl.program_id(0)

    @pl.when(j == 0)
    def _():
        for ref in (s2_sc, s3_sc, m2_sc, m3_sc):
            ref[...] = jnp.zeros_like(ref)

    def stage(q_ref, sa_sc, ma_sc, sb_sc, mb_sc, vt_ref, o_ref):
        q = q_ref[0]
        tq = q.shape[0]
        m_prev = mb_sc[...]
        mx = jnp.full((8, tq), NEG_INF, F32)
        acc = jnp.zeros((MLA_VT, tq), F32)
        for c in range(SEQ // MLA_TK):
            keys = slice(c * MLA_TK, (c + 1) * MLA_TK)
            p = jnp.exp2(sb_sc[keys, :] - m_prev).astype(BF16)
            acc = acc + jnp.dot(vt_ref[0, :, keys], p, preferred_element_type=F32)
            s = _dot_nt(k_ref[0, keys, :], q)
            sa_sc[keys, :] = s
            mx = jnp.maximum(mx, jnp.max(s.reshape(MLA_TK // 8, 8, tq), axis=0))
        ma_sc[...] = jnp.max(mx, axis=0, keepdims=True)
        o = acc[0:MLA_V, :] * (1.0 / acc[MLA_V:MLA_V + 1, :])
        o_ref[...] = o.T

    def step(a_even, a_odd, b_even, b_odd):
        for w_ref, wb_ref in cast_refs:
            wb_ref[...] = w_ref[...].astype(BF16)
        if with_ada:
            _ada_slab(*ada_in, *ada_out)
        stage(qe_ref, *a_even, *b_even, vt_ref, oe_ref)
        stage(qo_ref, *a_odd, *b_odd, vt_ref, oo_ref)

    buf = ((s0_sc, m0_sc), (s1_sc, m1_sc), (s2_sc, m2_sc), (s3_sc, m3_sc))
    parity = lax.rem(j, 2)

    @pl.when(parity == 0)
    def _():
        step(buf[0], buf[1], buf[2], buf[3])

    @pl.when(parity == 1)
    def _():
        step(buf[2], buf[3], buf[0], buf[1])


WO_SLAB = (32, D_MODEL)
WGU_SLAB = (256, 1024)
WDN_SLAB = (64, D_MODEL)


def _mla_attn_call(l, q, k, vt, w_o, w_gu, w_down, ada=None):
    tq = MLA_TQ
    npair = SEQ // (2 * tq)
    last = MLA_HEADS * npair - 1
    nstep = last + 2
    pair_a = lambda j: jnp.minimum(j, last)
    pair_b = lambda j: jnp.maximum(j - 1, 0)
    half = jax.ShapeDtypeStruct((SEQ // 2, D_B), F32)
    out_spec = pl.BlockSpec((tq, MLA_V), lambda j: (pair_b(j) % npair, pair_b(j) // npair))

    def slab_specs(shape, slab):
        nr, nc = shape[0] // slab[0], shape[1] // slab[1]
        assert nr * slab[0] == shape[0] and nc * slab[1] == shape[1] and nr * nc <= nstep
        idx = lambda j: jnp.minimum(j, nr * nc - 1)
        return (pl.BlockSpec((None,) + slab, lambda j: (l, idx(j) // nc, idx(j) % nc)),
                pl.BlockSpec(slab, lambda j: (idx(j) // nc, idx(j) % nc)))

    wo_in, wo_out = slab_specs(w_o.shape[1:], WO_SLAB)
    wgu_in, wgu_out = slab_specs(w_gu.shape[1:], WGU_SLAB)
    wdn_in, wdn_out = slab_specs(w_down.shape[1:], WDN_SLAB)
    ada_args, ada_in, ada_out, ada_shape = (), [], (), ()
    if ada is not None:
        assert ADA_NSLAB <= nstep
        ada_args = ada
        ada_in = [
            pl.BlockSpec((D_MODEL, 128), lambda j: (0, 0)),
            pl.BlockSpec((None, D_MODEL, ADA_SLAB), lambda j: (_ada_slab_index(j)[0], 0, _ada_slab_index(j)[1])),
            pl.BlockSpec((None, 1, ADA_SLAB), lambda j: (_ada_slab_index(j)[0], 0, _ada_slab_index(j)[1])),
        ]
        ada_out = (pl.BlockSpec((1, ADA_SLAB), lambda j: (0, _ada_slab_index(j)[2])),)
        ada_shape = (jax.ShapeDtypeStruct((1, ADA_NSLAB * ADA_SLAB), F32),)
    return pl.pallas_call(
        functools.partial(_mla_attn_kernel, with_ada=ada is not None),
        out_shape=(half, half,
                   jax.ShapeDtypeStruct(w_o.shape[1:], BF16),
                   jax.ShapeDtypeStruct(w_gu.shape[1:], BF16),
                   jax.ShapeDtypeStruct(w_down.shape[1:], BF16)) + ada_shape,
        grid=(nstep,),
        in_specs=[
            pl.BlockSpec((1, tq, MLA_QK), lambda j: (pair_a(j) // npair, 2 * (pair_a(j) % npair), 0)),
            pl.BlockSpec((1, tq, MLA_QK), lambda j: (pair_a(j) // npair, 2 * (pair_a(j) % npair) + 1, 0)),
            pl.BlockSpec((1, SEQ, MLA_QK), lambda j: (pair_a(j) // npair, 0, 0)),
            pl.BlockSpec((1, MLA_VT, SEQ), lambda j: (pair_b(j) // npair, 0, 0)),
            wo_in, wgu_in, wdn_in,
        ] + ada_in,
        out_specs=(out_spec, out_spec, wo_out, wgu_out, wdn_out) + ada_out,
        scratch_shapes=[pltpu.VMEM((SEQ, tq), F32)] * 4 + [pltpu.VMEM((1, tq), F32)] * 4,
        compiler_params=_cparams(("arbitrary",), VMEM_LIMIT_ATTN),
        name="mla_attn",
    )(q, q, k, vt, w_o, w_gu, w_down, *ada_args)


_SWA_SLOPES = tuple(2.0 ** (-8.0 * (i + 1) / SWA_HEADS) for i in range(SWA_HEADS))


SWA_NB = 16


def _swa_kernel(sink_ref, q_ref, kp_ref, kc_ref, kn_ref, vp_ref, vc_ref, vn_ref, o_ref):
    g_kv = pl.program_id(0)
    j = pl.program_id(1)
    t = SWA_BLOCK
    rows = SWA_GROUP * t
    kwin = jnp.concatenate([kp_ref[...], kc_ref[...], kn_ref[...]], axis=0)
    vwin = jnp.concatenate([vp_ref[...], vc_ref[...], vn_ref[...]], axis=0)
    ones = jnp.ones((3 * t, HEAD_DIM), BF16)
    ri = lax.broadcasted_iota(jnp.int32, (rows, 3 * t), 0)
    ci = lax.broadcasted_iota(jnp.int32, (rows, 3 * t), 1)
    grp = jnp.right_shift(ri, 7)
    dist = jnp.abs(jnp.bitwise_and(ri, t - 1) - (ci - t))
    slope_lo = jnp.where(grp == 0, _SWA_SLOPES[0], jnp.where(grp == 1, _SWA_SLOPES[1], _SWA_SLOPES[2]))
    slope_hi = jnp.where(grp == 0, _SWA_SLOPES[3], jnp.where(grp == 1, _SWA_SLOPES[4], _SWA_SLOPES[5]))
    slope = jnp.where(g_kv == 0, slope_lo, slope_hi)
    band = jnp.where(dist <= SWA_WINDOW, (-LOG2_E) * slope * dist.astype(F32), NEG_INF)
    rcol = jnp.right_shift(lax.broadcasted_iota(jnp.int32, (rows, 1), 0), 7)
    base = g_kv * SWA_GROUP
    sink = LOG2_E * jnp.where(rcol == 0, sink_ref[base],
                              jnp.where(rcol == 1, sink_ref[base + 1], sink_ref[base + 2]))
    first_cols = jnp.where(j == 0, t, 0)
    last_cols = jnp.where(j == pl.num_programs(1) - 1, 2 * t, 3 * t)
    for b in range(SWA_NB):
        q = jnp.concatenate([q_ref[b * t:(b + 1) * t, g * t:(g + 1) * t] for g in range(SWA_GROUP)], axis=0)
        s = _dot_nt(q, kwin[b * t:(b + 3) * t, :]) + band
        if b == 0:
            s = jnp.where(ci < first_cols, NEG_INF, s)
        if b == SWA_NB - 1:
            s = jnp.where(ci >= last_cols, NEG_INF, s)
        m = jnp.maximum(jnp.max(s, axis=1, keepdims=True), sink)
        p = jnp.exp2(s - m).astype(BF16)
        vext = jnp.concatenate([vwin[b * t:(b + 3) * t, :], ones], axis=1)
        acc = jnp.dot(p, vext, preferred_element_type=F32)
        l = acc[:, HEAD_DIM:HEAD_DIM + 1] + jnp.exp2(sink - m)
        o = acc[:, 0:HEAD_DIM] * (1.0 / l)
        for g in range(SWA_GROUP):
            o_ref[b * t:(b + 1) * t, g * t:(g + 1) * t] = o[g * t:(g + 1) * t, :]


def _swa_call(sink, pc):
    t = SWA_BLOCK
    nb = SEQ // t
    tq = SWA_NB * t
    kcol = SWA_HEADS
    vcol = SWA_HEADS + SWA_KV_HEADS
    prev = lambda j: jnp.maximum(SWA_NB * j - 1, 0)
    nxt = lambda j: jnp.minimum(SWA_NB * (j + 1), nb - 1)
    return pl.pallas_call(
        _swa_kernel,
        out_shape=jax.ShapeDtypeStruct((SEQ, D_C), F32),
        grid=(SWA_KV_HEADS, SEQ // tq),
        in_specs=[
            pl.BlockSpec(memory_space=pltpu.SMEM),
            pl.BlockSpec((tq, SWA_GROUP * t), lambda g, j: (j, g)),
            pl.BlockSpec((t, t), lambda g, j: (prev(j), kcol + g)),
            pl.BlockSpec((tq, t), lambda g, j: (j, kcol + g)),
            pl.BlockSpec((t, t), lambda g, j: (nxt(j), kcol + g)),
            pl.BlockSpec((t, t), lambda g, j: (prev(j), vcol + g)),
            pl.BlockSpec((tq, t), lambda g, j: (j, vcol + g)),
            pl.BlockSpec((t, t), lambda g, j: (nxt(j), vcol + g)),
        ],
        out_specs=pl.BlockSpec((tq, SWA_GROUP * t), lambda g, j: (j, g)),
        compiler_params=_cparams(("parallel", "arbitrary")),
        name="swa_attn",
    )(sink, pc, pc, pc, pc, pc, pc, pc)


OUTPROJ_TM = 2 * MLA_TQ


def _outproj_kernel(ya_ref, ybe_ref, ybo_ref, yc_ref, x_ref, mod_ref, gn_ref, w_ref, lg_ref, lb_ref,
                    o_ref, u_ref):
    hm = OUTPROJ_TM // 2
    gate = 1.0 + mod_ref[2:3, :]
    for half, yb_ref in enumerate((ybe_ref, ybo_ref)):
        rows = slice(half * hm, (half + 1) * hm)
        acc = None
        for y, c0 in ((ya_ref[rows, :], 0), (yb_ref[...], D_A), (yc_ref[rows, :], D_A + D_B)):
            c1 = c0 + y.shape[1]
            yn = _rms_norm(y, gn_ref[:, c0:c1]).astype(BF16)
            part = jnp.dot(yn, w_ref[c0:c1, :], preferred_element_type=F32)
            acc = part if acc is None else acc + part
        x1 = _layer_norm(DEEPNORM_ALPHA * x_ref[rows, :] + gate * acc, lg_ref[...], lb_ref[...])
        o_ref[rows, :] = x1
        u_ref[rows, :] = (x1 * (1.0 + mod_ref[4:5, :]) + mod_ref[3:4, :]).astype(BF16)


def _outproj_call(l, ya, yb_even, yb_odd, yc, x2, mod, vec, w_o):
    tm = OUTPROJ_TM
    row = lambda r, w: pl.BlockSpec((r, w), lambda i: (i, 0))
    return pl.pallas_call(
        _outproj_kernel,
        out_shape=(jax.ShapeDtypeStruct((SEQ, D_MODEL), F32), jax.ShapeDtypeStruct((SEQ, D_MODEL), BF16)),
        grid=(SEQ // tm,),
        in_specs=[row(tm, D_A), row(tm // 2, D_B), row(tm // 2, D_B), row(tm, D_C), row(tm, D_MODEL),
                  pl.BlockSpec((None, 6, D_MODEL), lambda i: (l, 0, 0)), _vec_spec(l, "out_norm"),
                  pl.BlockSpec((D_MODEL, D_MODEL), lambda i: (0, 0), pipeline_mode=pl.Buffered(1)),
                  _vec_spec(l, "ln1_g"), _vec_spec(l, "ln1_b")],
        out_specs=(row(tm, D_MODEL), row(tm, D_MODEL)),
        compiler_params=_cparams(("parallel",)),
        name="out_proj_ln",
    )(ya, yb_even, yb_odd, yc, x2, mod, vec, w_o, vec, vec)


FFN_TM = 1024
FFN_SUB = 512
FFN_SUB_LAST = 256
FFN_TF = 512


def _ffn_kernel(u_ref, x_hbm, mod_ref, wg_ref, wu_ref, wd_ref, lg_ref, lb_ref, o_ref, x_sc, x_sem):
    i = pl.program_id(0)
    f = pl.program_id(1)
    tm = o_ref.shape[0]
    x_copy = pltpu.make_async_copy(x_hbm.at[pl.ds(pl.multiple_of(i * tm, tm), tm), :], x_sc, x_sem)

    last = pl.num_programs(1) - 1

    def down_partial(rows):
        u = u_ref[rows, :]
        g = jnp.dot(u, wg_ref[...], preferred_element_type=F32)
        up = jnp.dot(u, wu_ref[...], preferred_element_type=F32)
        hdn = (g * (1.0 / (1.0 + jnp.exp(-g))) * up).astype(BF16)
        return jnp.dot(hdn, wd_ref[...], preferred_element_type=F32)

    @pl.when(f == 0)
    def _():
        x_copy.start()
        for r in range(0, tm, FFN_SUB):
            rows = slice(r, r + FFN_SUB)
            o_ref[rows, :] = down_partial(rows)

    @pl.when(jnp.logical_and(f > 0, f < last))
    def _():
        for r in range(0, tm, FFN_SUB):
            rows = slice(r, r + FFN_SUB)
            o_ref[rows, :] += down_partial(rows)

    @pl.when(f == last)
    def _():
        x_copy.wait()
        gate = 1.0 + mod_ref[5:6, :]
        for r in range(0, tm, FFN_SUB_LAST):
            rows = slice(r, r + FFN_SUB_LAST)
            acc = o_ref[rows, :] + down_partial(rows)
            o_ref[rows, :] = _layer_norm(DEEPNORM_ALPHA * x_sc[rows, :] + gate * acc, lg_ref[...], lb_ref[...])


def _ffn_call(l, u, x2, mod, w_gu, w_down, vec):
    tm, tf = FFN_TM, FFN_TF
    nf = D_FF // tf
    return pl.pallas_call(
        _ffn_kernel,
        out_shape=jax.ShapeDtypeStruct((SEQ, D_MODEL), F32),
        grid=(SEQ // tm, nf),
        in_specs=[
            pl.BlockSpec((tm, D_MODEL), lambda i, f: (i, 0)),
            pl.BlockSpec(memory_space=pl.ANY),
            pl.BlockSpec((None, 6, D_MODEL), lambda i, f: (l, 0, 0)),
            pl.BlockSpec((D_MODEL, tf), lambda i, f: (0, f)),
            pl.BlockSpec((D_MODEL, tf), lambda i, f: (0, nf + f)),
            pl.BlockSpec((tf, D_MODEL), lambda i, f: (f, 0)),
            _vec_spec(l, "ln2_g"), _vec_spec(l, "ln2_b"),
        ],
        out_specs=pl.BlockSpec((tm, D_MODEL), lambda i, f: (i, 0)),
        scratch_shapes=[pltpu.VMEM((tm, D_MODEL), F32), pltpu.SemaphoreType.DMA(())],
        compiler_params=_cparams(("arbitrary", "arbitrary")),
        name="ffn_ln",
    )(u, x2, mod, w_gu, w_gu, w_down, vec, vec)


def _rot_half_cols(w):
    half = w.shape[-1] // 2
    return jnp.concatenate([-w[..., half:], w[..., :half]], axis=-1)


def _rope_tables():
    half = MLA_ROPE // 2
    inv = ROPE_THETA ** (-jnp.arange(half, dtype=F32) / half)
    ang = jnp.arange(SEQ, dtype=F32)[:, None] * inv[None, :]
    return jnp.cos(ang), jnp.sin(ang)


def _colscale():
    cs = np.ones((1, P_W), np.float32)
    cs[0, 0:D_A] = HEAD_DIM ** -0.5 * LOG2_E
    c0 = IN_A + PB_W
    cs[0, c0:c0 + D_C] = HEAD_DIM ** -0.5 * LOG2_E
    return jnp.asarray(cs)


def kernel(x, c, w_ada, b_ada, w_in, na_rpb, mla_q_norm, mla_kv_norm, mla_w_uq, mla_w_ukv,
           swa_sink, out_norm_g, w_o, ln1_g, ln1_b, w_gu, w_down, ln2_g, ln2_b):
    assert x.shape == (1, SEQ, D_MODEL)
    x2 = x.reshape(SEQ, D_MODEL)
    b_ada3 = b_ada.reshape(DEPTH, 1, -1)
    mod_head, cond_lanes = _ada_head_call(c.reshape(D_MODEL, 1), w_ada, b_ada3)
    mod = jnp.concatenate([mod_head, jnp.zeros((1, 6 * D_MODEL - ADA_HEAD), F32)], axis=1).reshape(1, 6, D_MODEL)
    cos_t, sin_t = _rope_tables()
    colscale = _colscale()
    w_p = _winprep_call(jnp.swapaxes(w_in, 1, 2))
    wq = mla_w_uq.reshape(DEPTH, MLA_Q_RANK, MLA_HEADS, MLA_NOPE + MLA_ROPE)
    wq = jnp.concatenate([wq, _rot_half_cols(wq[..., MLA_NOPE:])], axis=-1)
    wq = wq.reshape(DEPTH, MLA_Q_RANK, MLA_HEADS * MLA_QK).astype(BF16)
    wkv = mla_w_ukv.reshape(DEPTH, MLA_KV_RANK, MLA_HEADS, MLA_NOPE + MLA_V)
    wk = wkv[..., :MLA_NOPE].reshape(DEPTH, MLA_KV_RANK, MLA_HEADS * MLA_NOPE).astype(BF16)
    wvt = jnp.transpose(wkv[..., MLA_NOPE:], (0, 2, 3, 1)).reshape(DEPTH, MLA_HEADS * MLA_V, MLA_KV_RANK)
    wvt = wvt.astype(BF16)
    vec_parts = dict(out_norm=out_norm_g, ln1_g=ln1_g, ln1_b=ln1_b, ln2_g=ln2_g, ln2_b=ln2_b,
                     q_norm=mla_q_norm, kv_norm=mla_kv_norm)
    vec = jnp.concatenate([vec_parts[name] for name, _ in _VEC_ORDER], axis=1).reshape(DEPTH, 1, VEC_W)
    bias = _na_bias_call(na_rpb.reshape(-1))
    for l in range(DEPTH):
        pa, pc, q, k, vt = _inproj_call(l, x2, mod, l, w_p, colscale, vec, wq, wk, wvt, cos_t, sin_t)
        ya = _na_call(l, pa, bias)
        if l == 0:
            yb_even, yb_odd, w_o_b, w_gu_b, w_down_b, mod_rest = _mla_attn_call(
                l, q, k, vt, w_o, w_gu, w_down, ada=(cond_lanes, w_ada, b_ada3))
            mod = jnp.concatenate([mod_head, mod_rest], axis=1).reshape(DEPTH, 6, D_MODEL)
        else:
            yb_even, yb_odd, w_o_b, w_gu_b, w_down_b = _mla_attn_call(l, q, k, vt, w_o, w_gu, w_down)
        yc = _swa_call(swa_sink[l], pc)
        x2, u = _outproj_call(l, ya, yb_even, yb_odd, yc, x2, mod, vec, w_o_b)
        x2 = _ffn_call(l, u, x2, mod, w_gu_b, w_down_b, vec)
    return x2.reshape(1, SEQ, D_MODEL)
```
